```python
import jax, jax.numpy as jnp
from jax import lax
import numpy as np

D_MODEL = 1024
BATCH = 2
SEQ = 8192
DEPTH = 4

N_MIXERS = 3
HEAD_DIM = 64
MEM_TOKENS = 256
MEM_HEADS = 4
MEM_HEAD_DIM = D_MODEL // 16
MEM_WIDTH = MEM_HEADS * MEM_HEAD_DIM
MIX_WIDTH = D_MODEL - MEM_WIDTH
RMS_EPS = 1e-6
NEG_INF = -1e30

SWA_Q_HEADS = MIX_WIDTH // HEAD_DIM
SWA_KV_HEADS = 4
SWA_GROUP = SWA_Q_HEADS // SWA_KV_HEADS
SWA_WINDOW = 128
SWA_BLOCK = 128
SWA_IN_SIZES = (SWA_Q_HEADS * HEAD_DIM, SWA_KV_HEADS * HEAD_DIM, SWA_KV_HEADS * HEAD_DIM, MEM_WIDTH)

RET_HEADS = 6
RET_QK_DIM = 64
RET_V_DIM = MIX_WIDTH // RET_HEADS
RET_CHUNK = 128
RET_IN_SIZES = (RET_HEADS * RET_QK_DIM, RET_HEADS * RET_QK_DIM, MIX_WIDTH, MIX_WIDTH, MEM_WIDTH)

MLSTM_HEADS = 4
MLSTM_V_DIM = MIX_WIDTH // MLSTM_HEADS
MLSTM_QK_DIM = MLSTM_V_DIM // 2
MLSTM_CHUNK = 128
MLSTM_CONV = 4
MLSTM_IN_SIZES = (2 * MLSTM_HEADS * MLSTM_QK_DIM, MIX_WIDTH, MIX_WIDTH, MLSTM_HEADS, MLSTM_HEADS, MEM_WIDTH)

FFN_DIM = 7 * D_MODEL // 2
N_EXPERTS = 8
TOP_K = 2

N_SWA_LAYERS = (DEPTH + 2) // 3
N_RET_LAYERS = (DEPTH + 1) // 3
N_MLSTM_LAYERS = DEPTH // 3
N_DENSE_LAYERS = (DEPTH + 1) // 2
N_MOE_LAYERS = DEPTH // 2

kernel_name = "hybrid_swa_retention_mlstm_moe_trunk"


def rmsnorm(x, gain=None, eps=RMS_EPS):
    xf = x.astype(jnp.float32)
    y = xf * lax.rsqrt(jnp.mean(xf * xf, axis=-1, keepdims=True) + eps)
    if gain is not None:
        y = y * gain.astype(jnp.float32)
    return y.astype(x.dtype)


def split_cols(p, sizes):
    out, start = [], 0
    for s in sizes:
        out.append(p[..., start:start + s])
        start += s
    return out


def to_chunks(t, L):
    B, S, H, d = t.shape
    return t.reshape(B, S // L, L, H, d).transpose(1, 0, 3, 2, 4)


def from_chunks(t):
    NC, B, H, L, d = t.shape
    return t.transpose(1, 0, 3, 2, 4).reshape(B, NC * L, H, d)


def alibi_slopes(n):
    return jnp.exp2(-8.0 * jnp.arange(1, n + 1, dtype=jnp.float32) / n)


def swiglu(x, w_gate, w_up, w_down):
    return (jax.nn.silu(x @ w_gate) * (x @ w_up)) @ w_down


def causal_depthwise_conv(x, w, b):
    K, C = w.shape
    y = lax.conv_general_dilated(x, w[:, None, :].astype(x.dtype), window_strides=(1,),
                                 padding=[(K - 1, 0)], dimension_numbers=('NWC', 'WIO', 'NWC'),
                                 feature_group_count=C)
    return y + b.astype(x.dtype)


def sliding_window_attention(q, k, v, sink):
    B, S = q.shape[0], q.shape[1]
    L = SWA_BLOCK
    nb = S // L
    qb = q.reshape(B, nb, L, SWA_KV_HEADS, SWA_GROUP, HEAD_DIM)
    kb = k.reshape(B, nb, L, SWA_KV_HEADS, HEAD_DIM)
    vb = v.reshape(B, nb, L, SWA_KV_HEADS, HEAD_DIM)

    def band(t):
        prev = jnp.pad(t[:, :-1], ((0, 0), (1, 0), (0, 0), (0, 0), (0, 0)))
        return jnp.concatenate([prev, t], axis=2)

    kw, vw = band(kb), band(vb)
    scores = jnp.einsum('bnqhgd,bnkhd->bnhgqk', qb, kw).astype(jnp.float32) * (HEAD_DIM ** -0.5)
    qpos = jnp.arange(L) + L
    kpos = jnp.arange(2 * L)
    dist = qpos[:, None] - kpos[None, :]
    in_window = (dist >= 0) & (dist < SWA_WINDOW)
    has_prev = (jnp.arange(nb)[:, None, None] > 0) | (kpos[None, None, :] >= L)
    mask = in_window[None] & has_prev
    slopes = alibi_slopes(SWA_Q_HEADS).reshape(SWA_KV_HEADS, SWA_GROUP)
    scores = scores - slopes[:, :, None, None] * jnp.abs(dist).astype(jnp.float32)
    scores = jnp.where(mask[None, :, None, None], scores, NEG_INF)
    sink_logit = jnp.broadcast_to(sink.astype(jnp.float32).reshape(SWA_KV_HEADS, SWA_GROUP, 1, 1),
                                  scores.shape[:-1] + (1,))
    probs = jax.nn.softmax(jnp.concatenate([scores, sink_logit], axis=-1), axis=-1)[..., :-1]
    out = jnp.einsum('bnhgqk,bnkhd->bnqhgd', probs.astype(v.dtype), vw)
    return out.reshape(B, S, SWA_Q_HEADS * HEAD_DIM)


def memory_attention(qm, mk, mv):
    s = jnp.einsum('bshd,bmhd->bhsm', qm, mk).astype(jnp.float32) * (MEM_HEAD_DIM ** -0.5)
    p = jax.nn.softmax(s, axis=-1).astype(mv.dtype)
    out = jnp.einsum('bhsm,bmhd->bshd', p, mv)
    return out.reshape(qm.shape[0], qm.shape[1], MEM_WIDTH)


def retention_chunkwise(q, k, v):
    B, S, H, dk = q.shape
    dv = v.shape[-1]
    L = RET_CHUNK
    f32 = jnp.float32
    log_gamma = jnp.log1p(-jnp.exp2(-5.0 - jnp.arange(H, dtype=f32)))
    pos = jnp.arange(L, dtype=f32)
    rel = pos[:, None] - pos[None, :]
    intra_decay = jnp.exp(jnp.where(rel >= 0, log_gamma[:, None, None] * rel, -jnp.inf))
    q_decay = jnp.exp(log_gamma[:, None] * (pos + 1.0))[..., None]
    k_decay = jnp.exp(log_gamma[:, None] * (L - 1.0 - pos))[..., None]
    chunk_decay = jnp.exp(log_gamma * L)[:, None, None]
    qc = to_chunks(q.astype(f32), L)
    kc = to_chunks(k.astype(f32) * (dk ** -0.5), L)
    vc = to_chunks(v.astype(f32), L)

    def step(state, xs):
        qi, ki, vi = xs
        s = jnp.einsum('bhid,bhjd->bhij', qi, ki) * intra_decay
        o = jnp.einsum('bhij,bhjv->bhiv', s, vi) + jnp.einsum('bhid,bhdv->bhiv', qi, state) * q_decay
        state = state * chunk_decay + jnp.einsum('bhjd,bhjv->bhdv', ki * k_decay, vi)
        return state, o

    _, o = lax.scan(step, jnp.zeros((B, H, dk, dv), f32), (qc, kc, vc))
    return from_chunks(o)


def mlstm_chunkwise(q, k, v, log_i, log_f):
    B, S, H, dk = q.shape
    dv = v.shape[-1]
    L = MLSTM_CHUNK
    f32 = jnp.float32
    qc = to_chunks(q.astype(f32), L)
    kc = to_chunks(k.astype(f32) * (dk ** -0.5), L)
    vc = to_chunks(v.astype(f32), L)
    lic = to_chunks(log_i[..., None], L)[..., 0]
    lfc = to_chunks(log_f[..., None], L)[..., 0]
    pos = jnp.arange(L)
    causal = pos[:, None] >= pos[None, :]

    def step(carry, xs):
        C, n, m = carry
        qi, ki, vi, li, lf = xs
        b = jnp.cumsum(lf, axis=-1)
        g = b[..., -1]
        D = jnp.where(causal, b[..., :, None] - b[..., None, :] + li[..., None, :], -jnp.inf)
        inter_log = b + m[..., None]
        m_t = jnp.maximum(inter_log, jnp.max(D, axis=-1))
        w = jnp.exp(D - m_t[..., None])
        a = jnp.exp(inter_log - m_t)
        s = jnp.einsum('bhid,bhjd->bhij', qi, ki) * w
        num = jnp.einsum('bhij,bhjv->bhiv', s, vi) + a[..., None] * jnp.einsum('bhid,bhdv->bhiv', qi, C)
        den = jnp.sum(s, axis=-1) + a * jnp.einsum('bhid,bhd->bhi', qi, n)
        h = num / jnp.maximum(jnp.abs(den), jnp.exp(-m_t))[..., None]
        u = g[..., None] - b + li
        m_new = jnp.maximum(g + m, jnp.max(u, axis=-1))
        wk = jnp.exp(u - m_new[..., None])[..., None]
        decay = jnp.exp(g + m - m_new)
        C = decay[..., None, None] * C + jnp.einsum('bhjd,bhjv->bhdv', ki * wk, vi)
        n = decay[..., None] * n + jnp.sum(ki * wk, axis=2)
        return (C, n, m_new), h

    init = (jnp.zeros((B, H, dk, dv), f32), jnp.zeros((B, H, dk), f32), jnp.zeros((B, H), f32))
    _, h = lax.scan(step, init, (qc, kc, vc, lic, lfc))
    return from_chunks(h)


def moe_ffn(x, router, w_gate, w_up, w_down):
    B, S, D = x.shape
    xt = x.reshape(B * S, D)
    logits = (xt @ router).astype(jnp.float32)
    top_v, top_i = lax.top_k(logits, TOP_K)
    top_w = jax.nn.softmax(top_v, axis=-1)
    gates = jnp.sum(jax.nn.one_hot(top_i, N_EXPERTS, dtype=jnp.float32) * top_w[..., None], axis=1)
    gates = gates.astype(x.dtype)
    y = jnp.zeros_like(xt)
    for e in range(N_EXPERTS):
        y = y + gates[:, e:e + 1] * swiglu(xt, w_gate[e], w_up[e], w_down[e])
    return y.reshape(B, S, D)


def setup_inputs(seed: int = 0) -> dict:
    key = jax.random.key(seed)
    keys = iter(jax.random.split(key, 64))
    f32 = jnp.float32

    def nrm(shape, scale):
        return scale * jax.random.normal(next(keys), shape, f32)

    def gain(shape):
        return 1.0 + 0.02 * jax.random.normal(next(keys), shape, f32)

    D = D_MODEL
    swa_in = sum(SWA_IN_SIZES)
    ret_in = sum(RET_IN_SIZES)
    mlstm_in = sum(MLSTM_IN_SIZES)
    f_bias = jnp.broadcast_to(jnp.linspace(3.0, 6.0, MLSTM_HEADS, dtype=f32), (N_MLSTM_LAYERS, MLSTM_HEADS))
    return {
        "x": nrm((BATCH, SEQ, D), 1.0),
        "mem": nrm((BATCH, MEM_TOKENS, D), 1.0),
        "ln_mix": gain((DEPTH, D)),
        "ln_mem": gain((DEPTH, D)),
        "w_mem_kv": nrm((DEPTH, D, 2 * MEM_WIDTH), D ** -0.5),
        "mem_q_gain": gain((DEPTH, MEM_HEAD_DIM)),
        "mem_k_gain": gain((DEPTH, MEM_HEAD_DIM)),
        "w_out": nrm((DEPTH, D, D), 0.5 * D ** -0.5),
        "ln_ffn": gain((DEPTH, D)),
        "swa_w_in": nrm((N_SWA_LAYERS, D, swa_in), D ** -0.5),
        "swa_q_gain": gain((N_SWA_LAYERS, HEAD_DIM)),
        "swa_k_gain": gain((N_SWA_LAYERS, HEAD_DIM)),
        "swa_sink": nrm((N_SWA_LAYERS, SWA_Q_HEADS), 0.5),
        "ret_w_in": nrm((N_RET_LAYERS, D, ret_in), D ** -0.5),
        "mlstm_w_in": nrm((N_MLSTM_LAYERS, D, mlstm_in), D ** -0.5),
        "mlstm_conv_w": nrm((N_MLSTM_LAYERS, MLSTM_CONV, 2 * MLSTM_HEADS * MLSTM_QK_DIM), MLSTM_CONV ** -0.5),
        "mlstm_conv_b": nrm((N_MLSTM_LAYERS, 2 * MLSTM_HEADS * MLSTM_QK_DIM), 0.02),
        "mlstm_i_bias": nrm((N_MLSTM_LAYERS, MLSTM_HEADS), 0.1),
        "mlstm_f_bias": f_bias + nrm((N_MLSTM_LAYERS, MLSTM_HEADS), 0.1),
        "mlstm_out_gain": gain((N_MLSTM_LAYERS, MIX_WIDTH)),
        "ffn_w_gate": nrm((N_DENSE_LAYERS, D, FFN_DIM), D ** -0.5),
        "ffn_w_up": nrm((N_DENSE_LAYERS, D, FFN_DIM), D ** -0.5),
        "ffn_w_down": nrm((N_DENSE_LAYERS, FFN_DIM, D), 0.5 * FFN_DIM ** -0.5),
        "moe_router": nrm((N_MOE_LAYERS, D, N_EXPERTS), D ** -0.5),
        "moe_w_gate": nrm((N_MOE_LAYERS, N_EXPERTS, D, FFN_DIM), D ** -0.5),
        "moe_w_up": nrm((N_MOE_LAYERS, N_EXPERTS, D, FFN_DIM), D ** -0.5),
        "moe_w_down": nrm((N_MOE_LAYERS, N_EXPERTS, FFN_DIM, D), 0.5 * FFN_DIM ** -0.5),
    }


def reference(x, mem, ln_mix, ln_mem, w_mem_kv, mem_q_gain, mem_k_gain, w_out, ln_ffn,
              swa_w_in, swa_q_gain, swa_k_gain, swa_sink,
              ret_w_in,
              mlstm_w_in, mlstm_conv_w, mlstm_conv_b, mlstm_i_bias, mlstm_f_bias, mlstm_out_gain,
              ffn_w_gate, ffn_w_up, ffn_w_down,
              moe_router, moe_w_gate, moe_w_up, moe_w_down):
    B, S, _ = x.shape
    M = mem.shape[1]
    h = x
    for layer in range(DEPTH):
        kind = layer % N_MIXERS
        idx = layer // N_MIXERS
        xn = rmsnorm(h, ln_mix[layer])

        mkv = rmsnorm(mem, ln_mem[layer]) @ w_mem_kv[layer]
        mk = rmsnorm(mkv[..., :MEM_WIDTH].reshape(B, M, MEM_HEADS, MEM_HEAD_DIM), mem_k_gain[layer])
        mv = mkv[..., MEM_WIDTH:].reshape(B, M, MEM_HEADS, MEM_HEAD_DIM)

        if kind == 0:
            q, k, v, qm = split_cols(xn @ swa_w_in[idx], SWA_IN_SIZES)
            q = rmsnorm(q.reshape(B, S, SWA_Q_HEADS, HEAD_DIM), swa_q_gain[idx])
            k = rmsnorm(k.reshape(B, S, SWA_KV_HEADS, HEAD_DIM), swa_k_gain[idx])
            v = v.reshape(B, S, SWA_KV_HEADS, HEAD_DIM)
            mix = sliding_window_attention(q, k, v, swa_sink[idx])
        elif kind == 1:
            q, k, v, g, qm = split_cols(xn @ ret_w_in[idx], RET_IN_SIZES)
            o = retention_chunkwise(q.reshape(B, S, RET_HEADS, RET_QK_DIM),
                                    k.reshape(B, S, RET_HEADS, RET_QK_DIM),
                                    v.reshape(B, S, RET_HEADS, RET_V_DIM))
            o = rmsnorm(o.astype(x.dtype)).reshape(B, S, MIX_WIDTH)
            mix = jax.nn.silu(g) * o
        else:
            qk, v, og, ig, fg, qm = split_cols(xn @ mlstm_w_in[idx], MLSTM_IN_SIZES)
            qk = jax.nn.silu(causal_depthwise_conv(qk, mlstm_conv_w[idx], mlstm_conv_b[idx]))
            q, k = split_cols(qk, (MLSTM_HEADS * MLSTM_QK_DIM, MLSTM_HEADS * MLSTM_QK_DIM))
            log_i = (ig + mlstm_i_bias[idx]).astype(jnp.float32)
            log_f = jax.nn.log_sigmoid((fg + mlstm_f_bias[idx]).astype(jnp.float32))
            hc = mlstm_chunkwise(q.reshape(B, S, MLSTM_HEADS, MLSTM_QK_DIM),
                                 k.reshape(B, S, MLSTM_HEADS, MLSTM_QK_DIM),
                                 v.reshape(B, S, MLSTM_HEADS, MLSTM_V_DIM), log_i, log_f)
            hc = rmsnorm(hc.astype(x.dtype), mlstm_out_gain[idx].reshape(MLSTM_HEADS, MLSTM_V_DIM))
            mix = jax.nn.sigmoid(og) * hc.reshape(B, S, MIX_WIDTH)

        qm = rmsnorm(qm.reshape(B, S, MEM_HEADS, MEM_HEAD_DIM), mem_q_gain[layer])
        mem_out = memory_attention(qm, mk, mv)
        h = h + jnp.concatenate([mix, mem_out], axis=-1) @ w_out[layer]

        hn = rmsnorm(h, ln_ffn[layer])
        j = layer // 2
        if layer % 2 == 0:
            h = h + swiglu(hn, ffn_w_gate[j], ffn_w_up[j], ffn_w_down[j])
        else:
            h = h + moe_ffn(hn, moe_router[j], moe_w_gate[j], moe_w_up[j], moe_w_down[j])
    return h
```

```python
import functools
import math

import jax
import jax.numpy as jnp
from jax import lax
from jax.experimental import pallas as pl
from jax.experimental.pallas import tpu as pltpu

F32 = jnp.float32
BF16 = jnp.bfloat16
I32 = jnp.int32

D_MODEL = 1024
N_MIXERS = 3
HEAD_DIM = 64
MEM_HEADS = 4
MEM_HEAD_DIM = 64
MEM_WIDTH = MEM_HEADS * MEM_HEAD_DIM
MIX_WIDTH = D_MODEL - MEM_WIDTH
RMS_EPS = 1e-6
NEG_INF = -1e30

SWA_Q_HEADS = MIX_WIDTH // HEAD_DIM
SWA_KV_HEADS = 4
SWA_GROUP = SWA_Q_HEADS // SWA_KV_HEADS
SWA_WINDOW = 128
SWA_BLOCK = 128

RET_HEADS = 6
RET_QK_DIM = 64
RET_V_DIM = MIX_WIDTH // RET_HEADS
RET_CHUNK = 128

MLSTM_HEADS = 4
MLSTM_V_DIM = MIX_WIDTH // MLSTM_HEADS
MLSTM_QK_DIM = MLSTM_V_DIM // 2
MLSTM_CHUNK = 128
MLSTM_CONV = 4
MLSTM_QK_PAD = 128
MLSTM_V_PAD = 256

N_EXPERTS = 8
TOP_K = 2

LANES = 128
SUBLANES = 8
VMEM_LIMIT = 56 * 1024 * 1024


def _cparams(n_axes):
    return pltpu.CompilerParams(dimension_semantics=("arbitrary",) * n_axes,
                                vmem_limit_bytes=VMEM_LIMIT)


def _rms(x, eps=RMS_EPS):
    return x * lax.rsqrt(jnp.mean(x * x, axis=-1, keepdims=True) + eps)


def _dot(a, b):
    return jnp.dot(a, b, preferred_element_type=F32)


def _dot_nt(a, b):
    return lax.dot_general(a, b, (((1,), (1,)), ((), ())), preferred_element_type=F32)


def _dot_tn(a, b):
    return lax.dot_general(a, b, (((0,), (0,)), ((), ())), preferred_element_type=F32)


def _silu(x):
    return x * jax.nn.sigmoid(x)


def _rms_proj_kernel(x_ref, g_ref, w_ref, *o_refs, widths):
    xn = (_rms(x_ref[...]) * g_ref[...]).astype(BF16)
    off = 0
    for o_ref, wd in zip(o_refs, widths):
        o_ref[...] = _dot(xn, w_ref[:, off:off + wd]).astype(o_ref.dtype)
        off += wd


def rms_proj(x, gain, w, widths, dtypes, tm, name):
    T, D = x.shape
    N = w.shape[1]
    assert sum(widths) == N and T % tm == 0
    return pl.pallas_call(
        functools.partial(_rms_proj_kernel, widths=tuple(widths)),
        grid=(T // tm,),
        in_specs=[pl.BlockSpec((tm, D), lambda i: (i, 0)),
                  pl.BlockSpec((1, D), lambda i: (0, 0)),
                  pl.BlockSpec((D, N), lambda i: (0, 0))],
        out_specs=[pl.BlockSpec((tm, wd), lambda i: (i, 0)) for wd in widths],
        out_shape=[jax.ShapeDtypeStruct((T, wd), dt) for wd, dt in zip(widths, dtypes)],
        compiler_params=_cparams(1),
        name=name,
    )(x, gain.reshape(1, D), w)


def _swa_kernel(sink_ref, q_ref, kc_ref, kp_ref, vc_ref, vp_ref, qg_ref, kg_ref, o_ref):
    n = pl.program_id(1)
    L = SWA_BLOCK
    row = lax.broadcasted_iota(I32, (L, 2 * L), 0)
    col = lax.broadcasted_iota(I32, (L, 2 * L), 1)
    dist = row + L - col
    mask = (dist >= 0) & (dist < SWA_WINDOW) & ((col >= L) | (n > 0))
    adist = jnp.abs(dist).astype(F32)
    qg = qg_ref[...]
    kg = kg_ref[...]
    outs = []
    for g in range(SWA_KV_HEADS):
        ksl = slice(g * HEAD_DIM, (g + 1) * HEAD_DIM)
        k = jnp.concatenate([kp_ref[:, ksl], kc_ref[:, ksl]], axis=0)
        kn = (_rms(k) * kg).astype(BF16)
        v = jnp.concatenate([vp_ref[:, ksl], vc_ref[:, ksl]], axis=0)
        for j in range(SWA_GROUP):
            h = g * SWA_GROUP + j
            slope = 2.0 ** (-8.0 * (h + 1) / SWA_Q_HEADS)
            qn = (_rms(q_ref[:, h * HEAD_DIM:(h + 1) * HEAD_DIM]) * qg).astype(BF16)
            s = _dot_nt(qn, kn) * (HEAD_DIM ** -0.5) - slope * adist
            s = jnp.where(mask, s, NEG_INF)
            sink = sink_ref[h]
            m = jnp.maximum(jnp.max(s, axis=-1, keepdims=True), sink)
            e = jnp.exp(s - m)
            denom = jnp.sum(e, axis=-1, keepdims=True) + jnp.exp(sink - m)
            outs.append(_dot(e.astype(BF16), v) / denom)
    o_ref[...] = jnp.concatenate(outs, axis=-1).astype(o_ref.dtype)


def swa_attention(q, k, v, q_gain, k_gain, sink, batch):
    T = q.shape[0]
    L = SWA_BLOCK
    nb = T // batch // L
    kvw = SWA_KV_HEADS * HEAD_DIM
    cur = lambda b, n: (b * nb + n, 0)
    prev = lambda b, n: (b * nb + jnp.maximum(n - 1, 0), 0)
    return pl.pallas_call(
        _swa_kernel,
        grid=(batch, nb),
        in_specs=[pl.BlockSpec(memory_space=pltpu.SMEM),
                  pl.BlockSpec((L, MIX_WIDTH), cur),
                  pl.BlockSpec((L, kvw), cur), pl.BlockSpec((L, kvw), prev),
                  pl.BlockSpec((L, kvw), cur), pl.BlockSpec((L, kvw), prev),
                  pl.BlockSpec((1, HEAD_DIM), lambda b, n: (0, 0)),
                  pl.BlockSpec((1, HEAD_DIM), lambda b, n: (0, 0))],
        out_specs=pl.BlockSpec((L, MIX_WIDTH), cur),
        out_shape=jax.ShapeDtypeStruct((T, MIX_WIDTH), BF16),
        compiler_params=_cparams(2),
        name="swa_attention",
    )(sink, q, k, k, v, v, q_gain.reshape(1, HEAD_DIM), k_gain.reshape(1, HEAD_DIM))


def _retention_kernel(q_ref, k_ref, v_ref, g_ref, dec_ref, qd_ref, kd_ref, cd_ref, o_ref, state_ref):
    @pl.when(pl.program_id(1) == 0)
    def _():
        state_ref[...] = jnp.zeros_like(state_ref)

    outs = []
    for h in range(RET_HEADS):
        qsl = slice(h * RET_QK_DIM, (h + 1) * RET_QK_DIM)
        vsl = slice(h * RET_V_DIM, (h + 1) * RET_V_DIM)
        q = q_ref[:, qsl].astype(BF16)
        k = k_ref[:, qsl] * (RET_QK_DIM ** -0.5)
        v = v_ref[:, vsl]
        state = state_ref[h]
        s = _dot_nt(q, k.astype(BF16)) * dec_ref[h]
        o = _dot(s.astype(BF16), v) + _dot(q, state.astype(BF16)) * qd_ref[h]
        state_ref[h] = state * cd_ref[h] + _dot_tn((k * kd_ref[h]).astype(BF16), v)
        outs.append(_silu(g_ref[:, vsl]) * _rms(o))
    o_ref[...] = jnp.concatenate(outs, axis=-1).astype(o_ref.dtype)


def _retention_consts():
    H, L = RET_HEADS, RET_CHUNK
    log_gamma = jnp.log1p(-jnp.exp2(-5.0 - jnp.arange(H, dtype=F32)))
    pos = jnp.arange(L, dtype=F32)
    rel = pos[:, None] - pos[None, :]
    intra = jnp.exp(jnp.where(rel >= 0, log_gamma[:, None, None] * rel, -jnp.inf))
    q_decay = jnp.exp(log_gamma[:, None] * (pos + 1.0))[..., None]
    k_decay = jnp.exp(log_gamma[:, None] * (L - 1.0 - pos))[..., None]
    chunk_decay = jnp.exp(log_gamma * L)[:, None, None]
    return (intra,
            jnp.broadcast_to(q_decay, (H, L, RET_V_DIM)),
            jnp.broadcast_to(k_decay, (H, L, RET_QK_DIM)),
            jnp.broadcast_to(chunk_decay, (H, RET_QK_DIM, RET_V_DIM)))


def retention(q, k, v, g, batch):
    T = q.shape[0]
    L = RET_CHUNK
    nc = T // batch // L
    H = RET_HEADS
    intra, qd, kd, cd = _retention_consts()
    cur = lambda b, c: (b * nc + c, 0)
    const = lambda b, c: (0, 0, 0)
    return pl.pallas_call(
        _retention_kernel,
        grid=(batch, nc),
        in_specs=[pl.BlockSpec((L, H * RET_QK_DIM), cur), pl.BlockSpec((L, H * RET_QK_DIM), cur),
                  pl.BlockSpec((L, MIX_WIDTH), cur), pl.BlockSpec((L, MIX_WIDTH), cur),
                  pl.BlockSpec((H, L, L), const), pl.BlockSpec((H, L, RET_V_DIM), const),
                  pl.BlockSpec((H, L, RET_QK_DIM), const), pl.BlockSpec((H, RET_QK_DIM, RET_V_DIM), const)],
        out_specs=pl.BlockSpec((L, MIX_WIDTH), cur),
        out_shape=jax.ShapeDtypeStruct((T, MIX_WIDTH), BF16),
        scratch_shapes=[pltpu.VMEM((H, RET_QK_DIM, RET_V_DIM), F32)],
        compiler_params=_cparams(2),
        name="retention",
    )(q, k, v, g, intra, qd, kd, cd)


def _split3(x):
    x1 = x.astype(BF16)
    r1 = x - x1.astype(F32)
    x2 = r1.astype(BF16)
    x3 = (r1 - x2.astype(F32)).astype(BF16)
    return x1, x2, x3


def _mlstm_kernel(qk_ref, v_ref, og_ref, gc_ref, gr_ref, cw_ref, cb_ref, bc_ref, br_ref, gain_ref,
                  tri_ref, trit_ref, o_ref, xbuf_ref, c_ref, n_ref, m_ref):
    L = MLSTM_CHUNK
    H = MLSTM_HEADS
    P = MLSTM_QK_PAD
    VP = MLSTM_V_PAD
    KT = SUBLANES

    @pl.when(pl.program_id(1) == 0)
    def _():
        xbuf_ref[0:KT, :] = jnp.zeros((KT, xbuf_ref.shape[1]), F32)
        c_ref[...] = jnp.zeros_like(c_ref)
        n_ref[...] = jnp.zeros_like(n_ref)
        m_ref[...] = jnp.zeros_like(m_ref)

    xbuf_ref[KT:KT + L, :] = qk_ref[...]
    acc = cb_ref[...] + cw_ref[MLSTM_CONV - 1:MLSTM_CONV, :] * xbuf_ref[KT:KT + L, :]
    for j in range(MLSTM_CONV - 1):
        sh = MLSTM_CONV - 1 - j
        acc = acc + cw_ref[j:j + 1, :] * xbuf_ref[KT - sh:KT - sh + L, :]
    xbuf_ref[0:KT, :] = qk_ref[L - KT:L, :]
    qk = _silu(acc)

    xc = gc_ref[...] + bc_ref[...]
    xr = gr_ref[...] + br_ref[...]
    lfc = jax.nn.log_sigmoid(xc)
    lfr = jax.nn.log_sigmoid(xr)
    tri = tri_ref[...]
    trit = trit_ref[...]
    bc = sum(_dot(tri, t) for t in _split3(lfc))
    br = sum(_dot(t, trit) for t in _split3(lfr))

    rowi = lax.broadcasted_iota(I32, (L, L), 0)
    coli = lax.broadcasted_iota(I32, (L, L), 1)
    causal = rowi >= coli
    outs = []
    for h in range(H):
        q = qk[:, h * P:(h + 1) * P].astype(BF16)
        k = qk[:, (H + h) * P:(H + h + 1) * P] * (MLSTM_QK_DIM ** -0.5)
        v = v_ref[:, h * VP:(h + 1) * VP]
        li_c = xc[:, h:h + 1]
        b_c = bc[:, H + h:H + h + 1]
        li_r = xr[h:h + 1, :]
        b_r = br[H + h:H + h + 1, :]
        g = b_c[L - 1:L, :]
        m = m_ref[h:h + 1, 0:1]
        C = c_ref[h]
        nvec = n_ref[h:h + 1, :]

        dmat = jnp.where(causal, b_c - b_r + li_r, -jnp.inf)
        inter = b_c + m
        m_t = jnp.maximum(inter, jnp.max(dmat, axis=-1, keepdims=True))
        w = jnp.exp(dmat - m_t)
        a = jnp.exp(inter - m_t)
        s = _dot_nt(q, k.astype(BF16)) * w
        num = _dot(s.astype(BF16), v) + a * _dot(q, C.astype(BF16))
        qf = qk[:, h * P:(h + 1) * P]
        den = jnp.sum(s, axis=-1, keepdims=True) + a * jnp.sum(qf * nvec, axis=-1, keepdims=True)
        hh = num / jnp.maximum(jnp.abs(den), jnp.exp(-m_t))

        u_c = g - b_c + li_c
        u_r = g - b_r + li_r
        m_new = jnp.maximum(g + m, jnp.max(u_r, axis=-1, keepdims=True))
        wk = jnp.exp(u_c - m_new)
        decay = jnp.exp(g + m - m_new)
        kw = k * wk
        c_ref[h] = decay * C + _dot_tn(kw.astype(BF16), v)
        n_ref[h:h + 1, :] = decay * nvec + jnp.sum(kw, axis=0, keepdims=True)
        m_ref[h:h + 1, :] = jnp.broadcast_to(m_new, (1, m_ref.shape[1]))

        ms = jnp.sum(hh * hh, axis=-1, keepdims=True) * (1.0 / MLSTM_V_DIM)
        hc = hh * lax.rsqrt(ms + RMS_EPS) * gain_ref[:, h * VP:(h + 1) * VP]
        outs.append(jax.nn.sigmoid(og_ref[:, h * VP:(h + 1) * VP]) * hc)
    o_ref[...] = jnp.concatenate(outs, axis=-1).astype(o_ref.dtype)


def mlstm(qk, v, og, gates, gates_t, conv_w, conv_b, bias_c, bias_r, gain, batch):
    T = qk.shape[0]
    L = MLSTM_CHUNK
    nc = T // batch // L
    H, P, VP = MLSTM_HEADS, MLSTM_QK_PAD, MLSTM_V_PAD
    tri = jnp.tril(jnp.ones((L, L), BF16))
    cur = lambda b, c: (b * nc + c, 0)
    c2 = lambda b, c: (0, 0)
    return pl.pallas_call(
        _mlstm_kernel,
        grid=(batch, nc),
        in_specs=[pl.BlockSpec((L, 2 * H * P), cur), pl.BlockSpec((L, H * VP), cur),
                  pl.BlockSpec((L, H * VP), cur), pl.BlockSpec((L, LANES), cur),
                  pl.BlockSpec((SUBLANES, L), lambda b, c: (0, b * nc + c)),
                  pl.BlockSpec((MLSTM_CONV, 2 * H * P), c2), pl.BlockSpec((1, 2 * H * P), c2),
                  pl.BlockSpec((1, LANES), c2), pl.BlockSpec((SUBLANES, L), c2),
                  pl.BlockSpec((1, H * VP), c2),
                  pl.BlockSpec((L, L), c2), pl.BlockSpec((L, L), c2)],
        out_specs=pl.BlockSpec((L, H * VP), cur),
        out_shape=jax.ShapeDtypeStruct((T, H * VP), BF16),
        scratch_shapes=[pltpu.VMEM((SUBLANES + L, 2 * H * P), F32),
                        pltpu.VMEM((H, P, VP), F32),
                        pltpu.VMEM((SUBLANES, P), F32),
                        pltpu.VMEM((SUBLANES, LANES), F32)],
        compiler_params=_cparams(2),
        name="mlstm",
    )(qk, v, og, gates, gates_t, conv_w, conv_b, bias_c, bias_r, gain, tri, tri.T)


def _mem_attn_kernel(q_ref, k_ref, v_ref, qg_ref, kg_ref, o_ref):
    qg = qg_ref[...]
    kg = kg_ref[...]
    outs = []
    for h in range(MEM_HEADS):
        sl = slice(h * MEM_HEAD_DIM, (h + 1) * MEM_HEAD_DIM)
        qn = (_rms(q_ref[:, sl]) * qg).astype(BF16)
        kn = (_rms(k_ref[:, sl]) * kg).astype(BF16)
        s = _dot_nt(qn, kn) * (MEM_HEAD_DIM ** -0.5)
        e = jnp.exp(s - jnp.max(s, axis=-1, keepdims=True))
        outs.append(_dot(e.astype(BF16), v_ref[:, sl]) / jnp.sum(e, axis=-1, keepdims=True))
    o_ref[...] = jnp.concatenate(outs, axis=-1).astype(o_ref.dtype)


def mem_attention(qm, mk, mv, q_gain, k_gain, batch, tm):
    T = qm.shape[0]
    M = mk.shape[0] // batch
    nt = T // batch // tm
    return pl.pallas_call(
        _mem_attn_kernel,
        grid=(batch, nt),
        in_specs=[pl.BlockSpec((tm, MEM_WIDTH), lambda b, i: (b * nt + i, 0)),
                  pl.BlockSpec((M, MEM_WIDTH), lambda b, i: (b, 0)),
                  pl.BlockSpec((M, MEM_WIDTH), lambda b, i: (b, 0)),
                  pl.BlockSpec((1, MEM_HEAD_DIM), lambda b, i: (0, 0)),
                  pl.BlockSpec((1, MEM_HEAD_DIM), lambda b, i: (0, 0))],
        out_specs=pl.BlockSpec((tm, MEM_WIDTH), lambda b, i: (b * nt + i, 0)),
        out_shape=jax.ShapeDtypeStruct((T, MEM_WIDTH), BF16),
        compiler_params=_cparams(2),
        name="mem_attention",
    )(qm, mk, mv, q_gain.reshape(1, MEM_HEAD_DIM), k_gain.reshape(1, MEM_HEAD_DIM))


def _out_proj_kernel(h_ref, mix_ref, mem_ref, w1_ref, w2_ref, o_ref):
    o_ref[...] = h_ref[...] + _dot(mix_ref[...], w1_ref[...]) + _dot(mem_ref[...], w2_ref[...])


def out_proj(h, mix, mem_out, w_mix, w_mem, tm):
    T, D = h.shape
    Wm = mix.shape[1]
    return pl.pallas_call(
        _out_proj_kernel,
        grid=(T // tm,),
        in_specs=[pl.BlockSpec((tm, D), lambda i: (i, 0)),
                  pl.BlockSpec((tm, Wm), lambda i: (i, 0)),
                  pl.BlockSpec((tm, MEM_WIDTH), lambda i: (i, 0)),
                  pl.BlockSpec((Wm, D), lambda i: (0, 0)),
                  pl.BlockSpec((MEM_WIDTH, D), lambda i: (0, 0))],
        out_specs=pl.BlockSpec((tm, D), lambda i: (i, 0)),
        out_shape=jax.ShapeDtypeStruct((T, D), F32),
        compiler_params=_cparams(1),
        name="out_proj",
    )(h, mix, mem_out, w_mix, w_mem)


def _dense_ffn_kernel(h_ref, g_ref, wg_ref, wu_ref, wd_ref, o_ref, xn_ref, acc_ref):
    j = pl.program_id(1)

    @pl.when(j == 0)
    def _():
        h = h_ref[...]
        xn_ref[...] = (_rms(h) * g_ref[...]).astype(BF16)
        acc_ref[...] = h

    xn = xn_ref[...]
    a = _silu(_dot(xn, wg_ref[...])) * _dot(xn, wu_ref[...])
    acc_ref[...] += _dot(a.astype(BF16), wd_ref[...])

    @pl.when(j == pl.num_programs(1) - 1)
    def _():
        o_ref[...] = acc_ref[...]


def dense_ffn(h, gain, w_gate, w_up, w_down, tm, tf):
    T, D = h.shape
    Fd = w_gate.shape[1]
    return pl.pallas_call(
        _dense_ffn_kernel,
        grid=(T // tm, Fd // tf),
        in_specs=[pl.BlockSpec((tm, D), lambda i, j: (i, 0)),
                  pl.BlockSpec((1, D), lambda i, j: (0, 0)),
                  pl.BlockSpec((D, tf), lambda i, j: (0, j)),
                  pl.BlockSpec((D, tf), lambda i, j: (0, j)),
                  pl.BlockSpec((tf, D), lambda i, j: (j, 0))],
        out_specs=pl.BlockSpec((tm, D), lambda i, j: (i, 0)),
        out_shape=jax.ShapeDtypeStruct((T, D), F32),
        scratch_shapes=[pltpu.VMEM((tm, D), BF16), pltpu.VMEM((tm, D), F32)],
        compiler_params=_cparams(2),
        name="dense_ffn",
    )(h, gain.reshape(1, D), w_gate, w_up, w_down)


def _router_kernel(h_ref, g_ref, wr_ref, tri_ref, o_ref, cnt_ref, carry_ref):
    i = pl.program_id(0)

    @pl.when(i == 0)
    def _():
        carry_ref[...] = jnp.zeros_like(carry_ref)

    tm = h_ref.shape[0]
    xn = _rms(h_ref[...]) * g_ref[...]
    logits = jnp.dot(xn, wr_ref[...], preferred_element_type=F32, precision=lax.Precision.HIGHEST)
    lane = lax.broadcasted_iota(I32, (tm, LANES), 1)
    logits = jnp.where(lane < N_EXPERTS, logits, -jnp.inf)
    t1 = jnp.max(logits, axis=-1, keepdims=True)
    e1 = jnp.min(jnp.where(logits == t1, lane, LANES), axis=-1, keepdims=True)
    rest = jnp.where(lane == e1, -jnp.inf, logits)
    t2 = jnp.max(rest, axis=-1, keepdims=True)
    e2 = jnp.min(jnp.where(rest == t2, lane, LANES), axis=-1, keepdims=True)
    x2 = jnp.exp(t2 - t1)
    w1 = 1.0 / (1.0 + x2)
    w2 = x2 / (1.0 + x2)

    oh1 = lane == e1
    oh2 = lane == e2
    cnt = jnp.where(oh1 | oh2, 1.0, 0.0)
    before = _dot(tri_ref[...], cnt.astype(BF16)) + carry_ref[...]
    r1 = jnp.sum(jnp.where(oh1, before, 0.0), axis=-1, keepdims=True)
    r2 = jnp.sum(jnp.where(oh2, before, 0.0), axis=-1, keepdims=True)
    carry_ref[...] += jnp.sum(cnt, axis=0, keepdims=True)

    cols = (e1.astype(F32), e2.astype(F32), w1, w2, r1, r2)
    out = jnp.zeros((tm, LANES), F32)
    for c, val in enumerate(cols):
        out = jnp.where(lane == c, val, out)
    o_ref[...] = out
    cnt_ref[...] = jnp.broadcast_to(carry_ref[...], cnt_ref.shape)


def moe_router(h, gain, w_router, tm):
    T, D = h.shape
    wr = jnp.zeros((D, LANES), F32).at[:, :N_EXPERTS].set(w_router)
    tri = jnp.tril(jnp.ones((tm, tm), BF16), k=-1)
    return pl.pallas_call(
        _router_kernel,
        grid=(T // tm,),
        in_specs=[pl.BlockSpec((tm, D), lambda i: (i, 0)),
                  pl.BlockSpec((1, D), lambda i: (0, 0)),
                  pl.BlockSpec((D, LANES), lambda i: (0, 0)),
                  pl.BlockSpec((tm, tm), lambda i: (0, 0))],
        out_specs=[pl.BlockSpec((tm, LANES), lambda i: (i, 0)),
                   pl.BlockSpec((SUBLANES, LANES), lambda i: (0, 0))],
        out_shape=[jax.ShapeDtypeStruct((T, LANES), F32),
                   jax.ShapeDtypeStruct((SUBLANES, LANES), F32)],
        scratch_shapes=[pltpu.VMEM((1, LANES), F32)],
        compiler_params=_cparams(1),
        name="moe_router",
    )(h, gain.reshape(1, D), wr, tri)


def _dispatch_kernel(p1_ref, p2_ref, h_ref, xs_in_ref, xs_ref, sem):
    del xs_in_ref
    tb = p1_ref.shape[0]
    base = pl.program_id(0) * tb

    def copy(t, pos):
        return pltpu.make_async_copy(h_ref.at[base + t], xs_ref.at[pos], sem)

    def issue(t, c):
        copy(t, p1_ref[t]).start()
        copy(t, p2_ref[t]).start()
        return c

    lax.fori_loop(0, tb, issue, 0)

    def drain(t, c):
        copy(t, p1_ref[t]).wait()
        copy(t, p2_ref[t]).wait()
        return c

    lax.fori_loop(0, tb, drain, 0)


def moe_dispatch(h3, pos1, pos2, n_rows, tb):
    T = h3.shape[0]
    xs0 = jnp.zeros((n_rows,) + h3.shape[1:], h3.dtype)
    return pl.pallas_call(
        _dispatch_kernel,
        grid=(T // tb,),
        in_specs=[pl.BlockSpec((tb,), lambda i: (i,), memory_space=pltpu.SMEM),
                  pl.BlockSpec((tb,), lambda i: (i,), memory_space=pltpu.SMEM),
                  pl.BlockSpec(memory_space=pl.ANY),
                  pl.BlockSpec(memory_space=pl.ANY)],
        out_specs=pl.BlockSpec(memory_space=pl.ANY),
        out_shape=jax.ShapeDtypeStruct(xs0.shape, xs0.dtype),
        scratch_shapes=[pltpu.SemaphoreType.DMA(())],
        input_output_aliases={3: 0},
        compiler_params=_cparams(1),
        name="moe_dispatch",
    )(pos1, pos2, h3, xs0)


def _expert_ffn_kernel(te_ref, na_ref, x_ref, g_ref, wg_ref, wu_ref, wd_ref, o_ref, xn_ref, acc_ref):
    i = pl.program_id(0)
    j = pl.program_id(1)
    active = i < na_ref[0]

    @pl.when(active & (j == 0))
    def _():
        x = jnp.concatenate([x_ref[:, s, :] for s in range(SUBLANES)], axis=-1)
        xn_ref[...] = (_rms(x) * g_ref[...]).astype(BF16)
        acc_ref[...] = jnp.zeros_like(acc_ref)

    @pl.when(active)
    def _():
        xn = xn_ref[...]
        a = _silu(_dot(xn, wg_ref[...])) * _dot(xn, wu_ref[...])
        acc_ref[...] += _dot(a.astype(BF16), wd_ref[...])

    @pl.when(j == pl.num_programs(1) - 1)
    def _():
        @pl.when(active)
        def _():
            for s in range(SUBLANES):
                o_ref[:, s, :] = acc_ref[:, s * LANES:(s + 1) * LANES]

        @pl.when(jnp.logical_not(active))
        def _():
            o_ref[...] = jnp.zeros_like(o_ref)


def expert_ffn(xs3, gain, w_gate, w_up, w_down, tile_expert, n_active, tm, tf):
    R = xs3.shape[0]
    D = D_MODEL
    Fd = w_gate.shape[2]
    nf = Fd // tf

    def jj(i, j, na):
        return jnp.where(i < na[0], j, nf - 1)

    grid_spec = pltpu.PrefetchScalarGridSpec(
        num_scalar_prefetch=2,
        grid=(R // tm, nf),
        in_specs=[pl.BlockSpec((tm, SUBLANES, LANES), lambda i, j, te, na: (i, 0, 0)),
                  pl.BlockSpec((1, D), lambda i, j, te, na: (0, 0)),
                  pl.BlockSpec((None, D, tf), lambda i, j, te, na: (te[i], 0, jj(i, j, na))),
                  pl.BlockSpec((None, D, tf), lambda i, j, te, na: (te[i], 0, jj(i, j, na))),
                  pl.BlockSpec((None, tf, D), lambda i, j, te, na: (te[i], jj(i, j, na), 0))],
        out_specs=pl.BlockSpec((tm, SUBLANES, LANES), lambda i, j, te, na: (i, 0, 0)),
        scratch_shapes=[pltpu.VMEM((tm, D), BF16), pltpu.VMEM((tm, D), F32)],
    )
    return pl.pallas_call(
        _expert_ffn_kernel,
        grid_spec=grid_spec,
        out_shape=jax.ShapeDtypeStruct(xs3.shape, F32),
        compiler_params=_cparams(2),
        name="expert_ffn",
    )(tile_expert, n_active, xs3, gain.reshape(1, D), w_gate, w_up, w_down)


def _combine_kernel(p1_ref, p2_ref, w1_ref, w2_ref, h_ref, y_ref, o_ref, b1_ref, b2_ref, sem):
    tb = p1_ref.shape[0]

    def copies(t):
        return (pltpu.make_async_copy(y_ref.at[p1_ref[t]], b1_ref.at[t], sem),
                pltpu.make_async_copy(y_ref.at[p2_ref[t]], b2_ref.at[t], sem))

    def issue(t, c):
        for cp in copies(t):
            cp.start()
        return c

    lax.fori_loop(0, tb, issue, 0)

    def drain(t, c):
        for cp in copies(t):
            cp.wait()
        return c

    lax.fori_loop(0, tb, drain, 0)

    def mix(t, c):
        b1_ref[t] = h_ref[t] + w1_ref[t] * b1_ref[t] + w2_ref[t] * b2_ref[t]
        return c

    lax.fori_loop(0, tb, mix, 0)
    for s in range(SUBLANES):
        o_ref[:, s * LANES:(s + 1) * LANES] = b1_ref[:, s, :]


def moe_combine(h3, y3, pos1, pos2, w1, w2, tb):
    T = h3.shape[0]
    smem = lambda: pl.BlockSpec((tb,), lambda i: (i,), memory_space=pltpu.SMEM)
    return pl.pallas_call(
        _combine_kernel,
        grid=(T // tb,),
        in_specs=[smem(), smem(), smem(), smem(),
                  pl.BlockSpec((tb, SUBLANES, LANES), lambda i: (i, 0, 0)),
                  pl.BlockSpec(memory_space=pl.ANY)],
        out_specs=pl.BlockSpec((tb, D_MODEL), lambda i: (i, 0)),
        out_shape=jax.ShapeDtypeStruct((T, D_MODEL), F32),
        scratch_shapes=[pltpu.VMEM((tb, SUBLANES, LANES), F32),
                        pltpu.VMEM((tb, SUBLANES, LANES), F32),
                        pltpu.SemaphoreType.DMA(())],
        compiler_params=_cparams(1),
        name="moe_combine",
    )(pos1, pos2, w1, w2, h3, y3)


def _to_tiles_kernel(x_ref, o_ref):
    for s in range(SUBLANES):
        o_ref[:, s, :] = x_ref[:, s * LANES:(s + 1) * LANES]


def to_tiles(x, tm):
    T, D = x.shape
    return pl.pallas_call(
        _to_tiles_kernel,
        grid=(T // tm,),
        in_specs=[pl.BlockSpec((tm, D), lambda i: (i, 0))],
        out_specs=pl.BlockSpec((tm, SUBLANES, LANES), lambda i: (i, 0, 0)),
        out_shape=jax.ShapeDtypeStruct((T, SUBLANES, D // SUBLANES), x.dtype),
        compiler_params=_cparams(1),
        name="to_tiles",
    )(x)


def moe_ffn(h, gain, w_router, w_gate, w_up, w_down, tm=512, tf=896, tb=512):
    T = h.shape[0]
    route, counts = moe_router(h, gain, w_router, tm=512)
    e1 = route[:, 0].astype(I32)
    e2 = route[:, 1].astype(I32)
    w1, w2 = route[:, 2], route[:, 3]
    counts = counts[0, :N_EXPERTS].astype(I32)
    tiles_per = (counts + tm - 1) // tm
    tile_end = jnp.cumsum(tiles_per)
    row_start = (tile_end - tiles_per) * tm
    pos1 = row_start[e1] + route[:, 4].astype(I32)
    pos2 = row_start[e2] + route[:, 5].astype(I32)
    n_tiles = (TOP_K * T) // tm + N_EXPERTS
    n_active = tile_end[-1:]
    tile_ids = jnp.minimum(jnp.arange(n_tiles, dtype=I32), n_active[0] - 1)
    tile_expert = jnp.sum(tile_ids[:, None] >= tile_end[None, :], axis=1).astype(I32)

    h3 = to_tiles(h, tm=512)
    xs3 = moe_dispatch(h3, pos1, pos2, n_tiles * tm, tb)
    y3 = expert_ffn(xs3, gain, w_gate, w_up, w_down, tile_expert, n_active, tm, tf)
    return moe_combine(h3, y3, pos1, pos2, w1, w2, tb)


def _pad_heads(a, n_heads, width, pad_to, axis):
    shape = list(a.shape)
    a = a.reshape(shape[:axis] + [n_heads, width] + shape[axis + 1:])
    pads = [(0, 0)] * a.ndim
    pads[axis + 1] = (0, pad_to - width)
    a = jnp.pad(a, pads)
    return a.reshape(shape[:axis] + [n_heads * pad_to] + shape[axis + 1:])


def kernel(x, mem, ln_mix, ln_mem, w_mem_kv, mem_q_gain, mem_k_gain, w_out, ln_ffn, swa_w_in, swa_q_gain, swa_k_gain, swa_sink, ret_w_in, mlstm_w_in, mlstm_conv_w, mlstm_conv_b, mlstm_i_bias, mlstm_f_bias, mlstm_out_gain, ffn_w_gate, ffn_w_up, ffn_w_down, moe_router, moe_w_gate, moe_w_up, moe_w_down):
    B, S, D = x.shape
    M = mem.shape[1]
    T = B * S
    depth = ln_mix.shape[0]
    h = x.reshape(T, D)
    mem2 = mem.reshape(B * M, D)

    for layer in range(depth):
        kind = layer % N_MIXERS
        idx = layer // N_MIXERS
        mk, mv = rms_proj(mem2, ln_mem[layer], w_mem_kv[layer].astype(BF16),
                          (MEM_WIDTH, MEM_WIDTH), (F32, BF16), tm=B * M, name="mem_kv_proj")
        w_o = w_out[layer].astype(BF16)
        w_o_mix, w_o_mem = w_o[:MIX_WIDTH], w_o[MIX_WIDTH:]

        if kind == 0:
            kvw = SWA_KV_HEADS * HEAD_DIM
            q, k, v, qm = rms_proj(h, ln_mix[layer], swa_w_in[idx].astype(BF16),
                                   (MIX_WIDTH, kvw, kvw, MEM_WIDTH), (F32, F32, BF16, F32),
                                   tm=512, name="swa_in_proj")
            mix = swa_attention(q, k, v, swa_q_gain[idx], swa_k_gain[idx], swa_sink[idx], B)
        elif kind == 1:
            qkw = RET_HEADS * RET_QK_DIM
            q, k, v, g, qm = rms_proj(h, ln_mix[layer], ret_w_in[idx].astype(BF16),
                                      (qkw, qkw, MIX_WIDTH, MIX_WIDTH, MEM_WIDTH),
                                      (F32, F32, BF16, F32, F32), tm=512, name="ret_in_proj")
            mix = retention(q, k, v, g, B)
        else:
            H, P, VP = MLSTM_HEADS, MLSTM_QK_PAD, MLSTM_V_PAD
            w = mlstm_w_in[idx]
            qkw = 2 * H * MLSTM_QK_DIM
            o_v, o_og, o_ig = qkw, qkw + MIX_WIDTH, qkw + 2 * MIX_WIDTH
            o_fg, o_qm = o_ig + H, o_ig + 2 * H
            w_gates = jnp.zeros((D, LANES), F32).at[:, :2 * H].set(w[:, o_ig:o_qm])
            w_pad = jnp.concatenate([
                _pad_heads(w[:, :qkw], 2 * H, MLSTM_QK_DIM, P, 1),
                _pad_heads(w[:, o_v:o_og], H, MLSTM_V_DIM, VP, 1),
                _pad_heads(w[:, o_og:o_ig], H, MLSTM_V_DIM, VP, 1),
                w_gates, w[:, o_qm:]], axis=1).astype(BF16)
            qk, v, og, gates, qm = rms_proj(h, ln_mix[layer], w_pad,
                                            (2 * H * P, H * VP, H * VP, LANES, MEM_WIDTH),
                                            (F32, BF16, F32, F32, F32), tm=512, name="mlstm_in_proj")
            gates_t = gates[:, :SUBLANES].T
            bias = jnp.concatenate([mlstm_i_bias[idx], mlstm_f_bias[idx]])
            bias_c = jnp.zeros((1, LANES), F32).at[0, :2 * H].set(bias)
            bias_r = jnp.broadcast_to(bias[:, None], (SUBLANES, MLSTM_CHUNK))
            mix = mlstm(qk, v, og, gates, gates_t,
                        _pad_heads(mlstm_conv_w[idx], 2 * H, MLSTM_QK_DIM, P, 1),
                        _pad_heads(mlstm_conv_b[idx][None], 2 * H, MLSTM_QK_DIM, P, 1),
                        bias_c, bias_r,
                        _pad_heads(mlstm_out_gain[idx][None], H, MLSTM_V_DIM, VP, 1), B)
            w_o_mix = _pad_heads(w_o_mix, H, MLSTM_V_DIM, VP, 0)

        mem_out = mem_attention(qm, mk, mv, mem_q_gain[layer], mem_k_gain[layer], B, tm=512)
        h = out_proj(h, mix, mem_out, w_o_mix, w_o_mem, tm=512)

        j = layer // 2
        if layer % 2 == 0:
            h = dense_ffn(h, ln_ffn[layer], ffn_w_gate[j].astype(BF16), ffn_w_up[j].astype(BF16),
                          ffn_w_down[j].astype(BF16), tm=1024, tf=896)
        else:
            h = moe_ffn(h, ln_ffn[layer], moe_router[j], moe_w_gate[j].astype(BF16),
                        moe_w_up[j].astype(BF16), moe_w_down[j].astype(BF16))
    return h.reshape(B, S, D)
```

```python
import functools
import math

import jax
import jax.numpy as jnp
from jax import lax
from jax.experimental import pallas as pl
from jax.experimental.pallas import tpu as pltpu

F32 = jnp.float32
BF16 = jnp.bfloat16
I32 = jnp.int32

D_MODEL = 1024
N_MIXERS = 3
HEAD_DIM = 64
MEM_HEADS = 4
MEM_HEAD_DIM = 64
MEM_WIDTH = MEM_HEADS * MEM_HEAD_DIM
MIX_WIDTH = D_MODEL - MEM_WIDTH
RMS_EPS = 1e-6
NEG_INF = -1e30

SWA_Q_HEADS = MIX_WIDTH // HEAD_DIM
SWA_KV_HEADS = 4
SWA_GROUP = SWA_Q_HEADS // SWA_KV_HEADS
SWA_WINDOW = 128
SWA_BLOCK = 128

RET_HEADS = 6
RET_QK_DIM = 64
RET_V_DIM = MIX_WIDTH // RET_HEADS
RET_CHUNK = 128

MLSTM_HEADS = 4
MLSTM_V_DIM = MIX_WIDTH // MLSTM_HEADS
MLSTM_QK_DIM = MLSTM_V_DIM // 2
MLSTM_CHUNK = 128
MLSTM_CONV = 4
MLSTM_QK_PAD = 128
MLSTM_V_PAD = 256

N_EXPERTS = 8
TOP_K = 2

LANES = 128
SUBLANES = 8
VMEM_LIMIT = 56 * 1024 * 1024


def _cparams(n_axes):
    return pltpu.CompilerParams(dimension_semantics=("arbitrary",) * n_axes,
                                vmem_limit_bytes=VMEM_LIMIT)


def _rms(x, eps=RMS_EPS):
    return x * lax.rsqrt(jnp.mean(x * x, axis=-1, keepdims=True) + eps)


def _dot(a, b):
    return jnp.dot(a, b, preferred_element_type=F32)


def _dot_nt(a, b):
    return lax.dot_general(a, b, (((1,), (1,)), ((), ())), preferred_element_type=F32)


def _dot_tn(a, b):
    return lax.dot_general(a, b, (((0,), (0,)), ((), ())), preferred_element_type=F32)


def _silu(x):
    return x * jax.nn.sigmoid(x)


NORM_CHUNK = 256


def _head_rms(y, seg_ref, hg):
    sq = y * y
    hi = sq.astype(BF16)
    lo = (sq - hi.astype(F32)).astype(BF16)
    ms = _dot(hi, seg_ref[...]) + _dot(lo, seg_ref[...])
    return y * lax.rsqrt(ms + RMS_EPS) * hg


def _rms_proj_kernel(x_ref, g_ref, w_ref, seg_ref, *refs, widths, normed):
    n_gain = sum(normed)
    gain_refs, o_refs = refs[:n_gain], refs[n_gain:]
    xn = (_rms(x_ref[...]) * g_ref[...]).astype(BF16)
    off = 0
    gi = 0
    for o_ref, wd, nrm in zip(o_refs, widths, normed):
        if nrm:
            hg = gain_refs[gi][...]
            gi += 1
            for c in range(0, wd, NORM_CHUNK):
                y = _dot(xn, w_ref[:, off + c:off + c + NORM_CHUNK])
                o_ref[:, c:c + NORM_CHUNK] = _head_rms(y, seg_ref, hg).astype(o_ref.dtype)
        else:
            o_ref[...] = _dot(xn, w_ref[:, off:off + wd]).astype(o_ref.dtype)
        off += wd


def rms_proj(x, gain, w, outs, tm, name):
    T, D = x.shape
    N = w.shape[1]
    widths = tuple(o[0] for o in outs)
    normed = tuple(o[2] is not None for o in outs)
    head_gains = [jnp.tile(o[2], NORM_CHUNK // HEAD_DIM).reshape(1, NORM_CHUNK) for o in outs if o[2] is not None]
    assert sum(widths) == N and T % tm == 0
    assert all(wd % NORM_CHUNK == 0 for wd, nrm in zip(widths, normed) if nrm)
    head_of = jnp.arange(NORM_CHUNK) // HEAD_DIM
    seg = jnp.where(head_of[:, None] == head_of[None, :], 1.0 / HEAD_DIM, 0.0).astype(BF16)
    return pl.pallas_call(
        functools.partial(_rms_proj_kernel, widths=widths, normed=normed),
        grid=(T // tm,),
        in_specs=[pl.BlockSpec((tm, D), lambda i: (i, 0)),
                  pl.BlockSpec((1, D), lambda i: (0, 0)),
                  pl.BlockSpec((D, N), lambda i: (0, 0)),
                  pl.BlockSpec((NORM_CHUNK, NORM_CHUNK), lambda i: (0, 0))]
                 + [pl.BlockSpec((1, NORM_CHUNK), lambda i: (0, 0))] * len(head_gains),
        out_specs=[pl.BlockSpec((tm, wd), lambda i: (i, 0)) for wd in widths],
        out_shape=[jax.ShapeDtypeStruct((T, o[0]), o[1]) for o in outs],
        compiler_params=_cparams(1),
        name=name,
    )(x, gain.reshape(1, D), w, seg, *head_gains)


def _swa_kernel(sink_ref, q_ref, kc_ref, kp_ref, vc_ref, vp_ref, bias_ref, o_ref):
    n = pl.program_id(1)
    L = SWA_BLOCK
    G = SWA_GROUP
    row = lax.broadcasted_iota(I32, (G * L, 2 * L), 0) & (L - 1)
    col = lax.broadcasted_iota(I32, (G * L, 2 * L), 1)
    dist = row + L - col
    mask = (dist >= 0) & (dist < SWA_WINDOW) & ((col >= L) | (n > 0))
    band = lax.broadcasted_iota(I32, (G * L, 1), 0)
    outs = []
    for g in range(SWA_KV_HEADS):
        ksl = slice(g * HEAD_DIM, (g + 1) * HEAD_DIM)
        k = jnp.concatenate([kp_ref[:, ksl], kc_ref[:, ksl]], axis=0)
        v = jnp.concatenate([vp_ref[:, ksl], vc_ref[:, ksl]], axis=0)
        heads = range(g * G, (g + 1) * G)
        q = jnp.concatenate([q_ref[:, h * HEAD_DIM:(h + 1) * HEAD_DIM] for h in heads], axis=0)
        sink = jnp.full((G * L, 1), sink_ref[g * G + G - 1], F32)
        for j in range(G - 2, -1, -1):
            sink = jnp.where(band < (j + 1) * L, sink_ref[g * G + j], sink)
        s = _dot_nt(q, k) * (HEAD_DIM ** -0.5) - bias_ref[g]
        s = jnp.where(mask, s, NEG_INF)
        m = jnp.maximum(jnp.max(s, axis=-1, keepdims=True), sink)
        e = jnp.exp(s - m)
        denom = jnp.sum(e, axis=-1, keepdims=True) + jnp.exp(sink - m)
        o = _dot(e.astype(BF16), v) / denom
        outs.extend(o[j * L:(j + 1) * L] for j in range(G))
    o_ref[...] = jnp.concatenate(outs, axis=-1).astype(o_ref.dtype)


def swa_attention(q, k, v, sink, batch):
    T = q.shape[0]
    L = SWA_BLOCK
    G = SWA_GROUP
    nb = T // batch // L
    kvw = SWA_KV_HEADS * HEAD_DIM
    slopes = jnp.exp2(-8.0 * jnp.arange(1, SWA_Q_HEADS + 1, dtype=F32) / SWA_Q_HEADS)
    adist = jnp.abs(jnp.arange(L)[:, None] + L - jnp.arange(2 * L)[None, :]).astype(F32)
    bias = (slopes[:, None, None] * adist).reshape(SWA_KV_HEADS, G * L, 2 * L)
    cur = lambda b, n: (b * nb + n, 0)
    prev = lambda b, n: (b * nb + jnp.maximum(n - 1, 0), 0)
    return pl.pallas_call(
        _swa_kernel,
        grid=(batch, nb),
        in_specs=[pl.BlockSpec(memory_space=pltpu.SMEM),
                  pl.BlockSpec((L, MIX_WIDTH), cur),
                  pl.BlockSpec((L, kvw), cur), pl.BlockSpec((L, kvw), prev),
                  pl.BlockSpec((L, kvw), cur), pl.BlockSpec((L, kvw), prev),
                  pl.BlockSpec((SWA_KV_HEADS, G * L, 2 * L), lambda b, n: (0, 0, 0))],
        out_specs=pl.BlockSpec((L, MIX_WIDTH), cur),
        out_shape=jax.ShapeDtypeStruct((T, MIX_WIDTH), BF16),
        compiler_params=_cparams(2),
        name="swa_attention",
    )(sink, q, k, k, v, v, bias)


def _retention_kernel(q_ref, k_ref, v_ref, g_ref, dec_ref, qd_ref, kd_ref, cd_ref, o_ref, state_ref):
    @pl.when(pl.program_id(1) == 0)
    def _():
        state_ref[...] = jnp.zeros_like(state_ref)

    outs = []
    for h in range(RET_HEADS):
        qsl = slice(h * RET_QK_DIM, (h + 1) * RET_QK_DIM)
        vsl = slice(h * RET_V_DIM, (h + 1) * RET_V_DIM)
        q = q_ref[:, qsl].astype(BF16)
        k = k_ref[:, qsl] * (RET_QK_DIM ** -0.5)
        v = v_ref[:, vsl]
        state = state_ref[h]
        s = _dot_nt(q, k.astype(BF16)) * dec_ref[h]
        o = _dot(s.astype(BF16), v) + _dot(q, state.astype(BF16)) * qd_ref[h]
        state_ref[h] = state * cd_ref[h] + _dot_tn((k * kd_ref[h]).astype(BF16), v)
        outs.append(_silu(g_ref[:, vsl]) * _rms(o))
    o_ref[...] = jnp.concatenate(outs, axis=-1).astype(o_ref.dtype)


def _retention_consts():
    H, L = RET_HEADS, RET_CHUNK
    log_gamma = jnp.log1p(-jnp.exp2(-5.0 - jnp.arange(H, dtype=F32)))
    pos = jnp.arange(L, dtype=F32)
    rel = pos[:, None] - pos[None, :]
    intra = jnp.exp(jnp.where(rel >= 0, log_gamma[:, None, None] * rel, -jnp.inf))
    q_decay = jnp.exp(log_gamma[:, None] * (pos + 1.0))[..., None]
    k_decay = jnp.exp(log_gamma[:, None] * (L - 1.0 - pos))[..., None]
    chunk_decay = jnp.exp(log_gamma * L)[:, None, None]
    return (intra,
            jnp.broadcast_to(q_decay, (H, L, RET_V_DIM)),
            jnp.broadcast_to(k_decay, (H, L, RET_QK_DIM)),
            jnp.broadcast_to(chunk_decay, (H, RET_QK_DIM, RET_V_DIM)))


def retention(q, k, v, g, batch):
    T = q.shape[0]
    L = RET_CHUNK
    nc = T // batch // L
    H = RET_HEADS
    intra, qd, kd, cd = _retention_consts()
    cur = lambda b, c: (b * nc + c, 0)
    const = lambda b, c: (0, 0, 0)
    return pl.pallas_call(
        _retention_kernel,
        grid=(batch, nc),
        in_specs=[pl.BlockSpec((L, H * RET_QK_DIM), cur), pl.BlockSpec((L, H * RET_QK_DIM), cur),
                  pl.BlockSpec((L, MIX_WIDTH), cur), pl.BlockSpec((L, MIX_WIDTH), cur),
                  pl.BlockSpec((H, L, L), const), pl.BlockSpec((H, L, RET_V_DIM), const),
                  pl.BlockSpec((H, L, RET_QK_DIM), const), pl.BlockSpec((H, RET_QK_DIM, RET_V_DIM), const)],
        out_specs=pl.BlockSpec((L, MIX_WIDTH), cur),
        out_shape=jax.ShapeDtypeStruct((T, MIX_WIDTH), BF16),
        scratch_shapes=[pltpu.VMEM((H, RET_QK_DIM, RET_V_DIM), F32)],
        compiler_params=_cparams(2),
        name="retention",
    )(q, k, v, g, intra, qd, kd, cd)


def _split3(x):
    x1 = x.astype(BF16)
    r1 = x - x1.astype(F32)
    x2 = r1.astype(BF16)
    x3 = (r1 - x2.astype(F32)).astype(BF16)
    return x1, x2, x3


def _mlstm_kernel(qk_ref, v_ref, og_ref, gc_ref, gr_ref, cw_ref, cb_ref, bc_ref, br_ref, gain_ref,
                  tri_ref, trit_ref, o_ref, xbuf_ref, c_ref, n_ref, m_ref):
    L = MLSTM_CHUNK
    H = MLSTM_HEADS
    P = MLSTM_QK_PAD
    VP = MLSTM_V_PAD
    KT = SUBLANES

    @pl.when(pl.program_id(1) == 0)
    def _():
        xbuf_ref[0:KT, :] = jnp.zeros((KT, xbuf_ref.shape[1]), F32)
        c_ref[...] = jnp.zeros_like(c_ref)
        n_ref[...] = jnp.zeros_like(n_ref)
        m_ref[...] = jnp.zeros_like(m_ref)

    xbuf_ref[KT:KT + L, :] = qk_ref[...]
    acc = cb_ref[...] + cw_ref[MLSTM_CONV - 1:MLSTM_CONV, :] * xbuf_ref[KT:KT + L, :]
    for j in range(MLSTM_CONV - 1):
        sh = MLSTM_CONV - 1 - j
        acc = acc + cw_ref[j:j + 1, :] * xbuf_ref[KT - sh:KT - sh + L, :]
    xbuf_ref[0:KT, :] = qk_ref[L - KT:L, :]
    qk = _silu(acc)

    xc = gc_ref[...] + bc_ref[...]
    xr = gr_ref[...] + br_ref[...]
    lfc = jax.nn.log_sigmoid(xc)
    lfr = jax.nn.log_sigmoid(xr)
    tri = tri_ref[...]
    trit = trit_ref[...]
    bc = sum(_dot(tri, t) for t in _split3(lfc))
    br = sum(_dot(t, trit) for t in _split3(lfr))

    rowi = lax.broadcasted_iota(I32, (L, L), 0)
    coli = lax.broadcasted_iota(I32, (L, L), 1)
    causal = rowi >= coli
    outs = []
    for h in range(H):
        q = qk[:, h * P:(h + 1) * P].astype(BF16)
        k = qk[:, (H + h) * P:(H + h + 1) * P] * (MLSTM_QK_DIM ** -0.5)
        v = v_ref[:, h * VP:(h + 1) * VP]
        li_c = xc[:, h:h + 1]
        b_c = bc[:, H + h:H + h + 1]
        li_r = xr[h:h + 1, :]
        b_r = br[H + h:H + h + 1, :]
        g = b_c[L - 1:L, :]
        m = m_ref[h:h + 1, 0:1]
        C = c_ref[h]
        nvec = n_ref[h:h + 1, :]

        dmat = jnp.where(causal, b_c - b_r + li_r, -jnp.inf)
        inter = b_c + m
        m_t = jnp.maximum(inter, jnp.max(dmat, axis=-1, keepdims=True))
        w = jnp.exp(dmat - m_t)
        a = jnp.exp(inter - m_t)
        s = _dot_nt(q, k.astype(BF16)) * w
        num = _dot(s.astype(BF16), v) + a * _dot(q, C.astype(BF16))
        qf = qk[:, h * P:(h + 1) * P]
        den = jnp.sum(s, axis=-1, keepdims=True) + a * jnp.sum(qf * nvec, axis=-1, keepdims=True)
        hh = num / jnp.maximum(jnp.abs(den), jnp.exp(-m_t))

        u_c = g - b_c + li_c
        u_r = g - b_r + li_r
        m_new = jnp.maximum(g + m, jnp.max(u_r, axis=-1, keepdims=True))
        wk = jnp.exp(u_c - m_new)
        decay = jnp.exp(g + m - m_new)
        kw = k * wk
        c_ref[h] = decay * C + _dot_tn(kw.astype(BF16), v)
        n_ref[h:h + 1, :] = decay * nvec + jnp.sum(kw, axis=0, keepdims=True)
        m_ref[h:h + 1, :] = jnp.broadcast_to(m_new, (1, m_ref.shape[1]))

        ms = jnp.sum(hh * hh, axis=-1, keepdims=True) * (1.0 / MLSTM_V_DIM)
        hc = hh * lax.rsqrt(ms + RMS_EPS) * gain_ref[:, h * VP:(h + 1) * VP]
        outs.append(jax.nn.sigmoid(og_ref[:, h * VP:(h + 1) * VP]) * hc)
    o_ref[...] = jnp.concatenate(outs, axis=-1).astype(o_ref.dtype)


def mlstm(qk, v, og, gates, gates_t, conv_w, conv_b, bias_c, bias_r, gain, batch):
    T = qk.shape[0]
    L = MLSTM_CHUNK
    nc = T // batch // L
    H, P, VP = MLSTM_HEADS, MLSTM_QK_PAD, MLSTM_V_PAD
    tri = jnp.tril(jnp.ones((L, L), BF16))
    cur = lambda b, c: (b * nc + c, 0)
    c2 = lambda b, c: (0, 0)
    return pl.pallas_call(
        _mlstm_kernel,
        grid=(batch, nc),
        in_specs=[pl.BlockSpec((L, 2 * H * P), cur), pl.BlockSpec((L, H * VP), cur),
                  pl.BlockSpec((L, H * VP), cur), pl.BlockSpec((L, LANES), cur),
                  pl.BlockSpec((SUBLANES, L), lambda b, c: (0, b * nc + c)),
                  pl.BlockSpec((MLSTM_CONV, 2 * H * P), c2), pl.BlockSpec((1, 2 * H * P), c2),
                  pl.BlockSpec((1, LANES), c2), pl.BlockSpec((SUBLANES, L), c2),
                  pl.BlockSpec((1, H * VP), c2),
                  pl.BlockSpec((L, L), c2), pl.BlockSpec((L, L), c2)],
        out_specs=pl.BlockSpec((L, H * VP), cur),
        out_shape=jax.ShapeDtypeStruct((T, H * VP), BF16),
        scratch_shapes=[pltpu.VMEM((SUBLANES + L, 2 * H * P), F32),
                        pltpu.VMEM((H, P, VP), F32),
                        pltpu.VMEM((SUBLANES, P), F32),
                        pltpu.VMEM((SUBLANES, LANES), F32)],
        compiler_params=_cparams(2),
        name="mlstm",
    )(qk, v, og, gates, gates_t, conv_w, conv_b, bias_c, bias_r, gain, tri, tri.T)


def _mem_attn_kernel(q_ref, k_ref, v_ref, o_ref):
    outs = []
    for h in range(MEM_HEADS):
        sl = slice(h * MEM_HEAD_DIM, (h + 1) * MEM_HEAD_DIM)
        s = _dot_nt(q_ref[:, sl], k_ref[:, sl]) * (MEM_HEAD_DIM ** -0.5)
        e = jnp.exp(s - jnp.max(s, axis=-1, keepdims=True))
        outs.append(_dot(e.astype(BF16), v_ref[:, sl]) / jnp.sum(e, axis=-1, keepdims=True))
    o_ref[...] = jnp.concatenate(outs, axis=-1).astype(o_ref.dtype)


def mem_attention(qm, mk, mv, batch, tm):
    T = qm.shape[0]
    M = mk.shape[0] // batch
    nt = T // batch // tm
    return pl.pallas_call(
        _mem_attn_kernel,
        grid=(batch, nt),
        in_specs=[pl.BlockSpec((tm, MEM_WIDTH), lambda b, i: (b * nt + i, 0)),
                  pl.BlockSpec((M, MEM_WIDTH), lambda b, i: (b, 0)),
                  pl.BlockSpec((M, MEM_WIDTH), lambda b, i: (b, 0))],
        out_specs=pl.BlockSpec((tm, MEM_WIDTH), lambda b, i: (b * nt + i, 0)),
        out_shape=jax.ShapeDtypeStruct((T, MEM_WIDTH), BF16),
        compiler_params=_cparams(2),
        name="mem_attention",
    )(qm, mk, mv)


def _out_proj_kernel(h_ref, mix_ref, mem_ref, w1_ref, w2_ref, o_ref):
    o_ref[...] = h_ref[...] + _dot(mix_ref[...], w1_ref[...]) + _dot(mem_ref[...], w2_ref[...])


def out_proj(h, mix, mem_out, w_mix, w_mem, tm):
    T, D = h.shape
    Wm = mix.shape[1]
    return pl.pallas_call(
        _out_proj_kernel,
        grid=(T // tm,),
        in_specs=[pl.BlockSpec((tm, D), lambda i: (i, 0)),
                  pl.BlockSpec((tm, Wm), lambda i: (i, 0)),
                  pl.BlockSpec((tm, MEM_WIDTH), lambda i: (i, 0)),
                  pl.BlockSpec((Wm, D), lambda i: (0, 0)),
                  pl.BlockSpec((MEM_WIDTH, D), lambda i: (0, 0))],
        out_specs=pl.BlockSpec((tm, D), lambda i: (i, 0)),
        out_shape=jax.ShapeDtypeStruct((T, D), F32),
        compiler_params=_cparams(1),
        name="out_proj",
    )(h, mix, mem_out, w_mix, w_mem)


def _dense_ffn_kernel(h_ref, g_ref, wg_ref, wu_ref, wd_ref, o_ref, xn_ref, acc_ref):
    j = pl.program_id(1)

    @pl.when(j == 0)
    def _():
        h = h_ref[...]
        xn_ref[...] = (_rms(h) * g_ref[...]).astype(BF16)
        acc_ref[...] = h

    xn = xn_ref[...]
    a = _silu(_dot(xn, wg_ref[...])) * _dot(xn, wu_ref[...])
    acc_ref[...] += _dot(a.astype(BF16), wd_ref[...])

    @pl.when(j == pl.num_programs(1) - 1)
    def _():
        o_ref[...] = acc_ref[...]


def dense_ffn(h, gain, w_gate, w_up, w_down, tm, tf):
    T, D = h.shape
    Fd = w_gate.shape[1]
    return pl.pallas_call(
        _dense_ffn_kernel,
        grid=(T // tm, Fd // tf),
        in_specs=[pl.BlockSpec((tm, D), lambda i, j: (i, 0)),
                  pl.BlockSpec((1, D), lambda i, j: (0, 0)),
                  pl.BlockSpec((D, tf), lambda i, j: (0, j)),
                  pl.BlockSpec((D, tf), lambda i, j: (0, j)),
                  pl.BlockSpec((tf, D), lambda i, j: (j, 0))],
        out_specs=pl.BlockSpec((tm, D), lambda i, j: (i, 0)),
        out_shape=jax.ShapeDtypeStruct((T, D), F32),
        scratch_shapes=[pltpu.VMEM((tm, D), BF16), pltpu.VMEM((tm, D), F32)],
        compiler_params=_cparams(2),
        name="dense_ffn",
    )(h, gain.reshape(1, D), w_gate, w_up, w_down)


def _router_kernel(h_ref, g_ref, wr_ref, tri_ref, o_ref, cnt_ref, carry_ref):
    i = pl.program_id(0)

    @pl.when(i == 0)
    def _():
        carry_ref[...] = jnp.zeros_like(carry_ref)

    tm = h_ref.shape[0]
    xn = _rms(h_ref[...]) * g_ref[...]
    logits = jnp.dot(xn, wr_ref[...], preferred_element_type=F32, precision=lax.Precision.HIGHEST)
    lane = lax.broadcasted_iota(I32, (tm, LANES), 1)
    logits = jnp.where(lane < N_EXPERTS, logits, -jnp.inf)
    t1 = jnp.max(logits, axis=-1, keepdims=True)
    e1 = jnp.min(jnp.where(logits == t1, lane, LANES), axis=-1, keepdims=True)
    rest = jnp.where(lane == e1, -jnp.inf, logits)
    t2 = jnp.max(rest, axis=-1, keepdims=True)
    e2 = jnp.min(jnp.where(rest == t2, lane, LANES), axis=-1, keepdims=True)
    x2 = jnp.exp(t2 - t1)
    w1 = 1.0 / (1.0 + x2)
    w2 = x2 / (1.0 + x2)

    oh1 = lane == e1
    oh2 = lane == e2
    cnt = jnp.where(oh1 | oh2, 1.0, 0.0)
    before = _dot(tri_ref[...], cnt.astype(BF16)) + carry_ref[...]
    r1 = jnp.sum(jnp.where(oh1, before, 0.0), axis=-1, keepdims=True)
    r2 = jnp.sum(jnp.where(oh2, before, 0.0), axis=-1, keepdims=True)
    carry_ref[...] += jnp.sum(cnt, axis=0, keepdims=True)

    cols = (e1.astype(F32), e2.astype(F32), w1, w2, r1, r2)
    out = jnp.zeros((tm, LANES), F32)
    for c, val in enumerate(cols):
        out = jnp.where(lane == c, val, out)
    o_ref[...] = out
    cnt_ref[...] = jnp.broadcast_to(carry_ref[...], cnt_ref.shape)


def moe_router(h, gain, w_router, tm):
    T, D = h.shape
    wr = jnp.zeros((D, LANES), F32).at[:, :N_EXPERTS].set(w_router)
    tri = jnp.tril(jnp.ones((tm, tm), BF16), k=-1)
    return pl.pallas_call(
        _router_kernel,
        grid=(T // tm,),
        in_specs=[pl.BlockSpec((tm, D), lambda i: (i, 0)),
                  pl.BlockSpec((1, D), lambda i: (0, 0)),
                  pl.BlockSpec((D, LANES), lambda i: (0, 0)),
                  pl.BlockSpec((tm, tm), lambda i: (0, 0))],
        out_specs=[pl.BlockSpec((tm, LANES), lambda i: (i, 0)),
                   pl.BlockSpec((SUBLANES, LANES), lambda i: (0, 0))],
        out_shape=[jax.ShapeDtypeStruct((T, LANES), F32),
                   jax.ShapeDtypeStruct((SUBLANES, LANES), F32)],
        scratch_shapes=[pltpu.VMEM((1, LANES), F32)],
        compiler_params=_cparams(1),
        name="moe_router",
    )(h, gain.reshape(1, D), wr, tri)


def _dispatch_kernel(p1_ref, p2_ref, h_ref, xs_in_ref, xs_ref, sem):
    del xs_in_ref
    tb = p1_ref.shape[0]

    def copy(t, pos):
        return pltpu.make_async_copy(h_ref.at[pl.ds(t, 1), :], xs_ref.at[pl.ds(pos, 1), :], sem)

    def issue(t, c):
        copy(t, p1_ref[t]).start()
        copy(t, p2_ref[t]).start()
        return c

    lax.fori_loop(0, tb, issue, 0)

    def drain(t, c):
        copy(t, p1_ref[t]).wait()
        copy(t, p2_ref[t]).wait()
        return c

    lax.fori_loop(0, tb, drain, 0)


def moe_dispatch(h, pos1, pos2, n_rows, tb):
    T, D = h.shape
    xs0 = jnp.zeros((n_rows, D), h.dtype)
    return pl.pallas_call(
        _dispatch_kernel,
        grid=(T // tb,),
        in_specs=[pl.BlockSpec((tb,), lambda i: (i,), memory_space=pltpu.SMEM),
                  pl.BlockSpec((tb,), lambda i: (i,), memory_space=pltpu.SMEM),
                  pl.BlockSpec((tb, D), lambda i: (i, 0)),
                  pl.BlockSpec(memory_space=pl.ANY)],
        out_specs=pl.BlockSpec(memory_space=pl.ANY),
        out_shape=jax.ShapeDtypeStruct(xs0.shape, xs0.dtype),
        scratch_shapes=[pltpu.SemaphoreType.DMA(())],
        input_output_aliases={3: 0},
        compiler_params=_cparams(1),
        name="moe_dispatch",
    )(pos1, pos2, h, xs0)


def _expert_ffn_kernel(te_ref, tv_ref, x_ref, g_ref, wg_ref, wu_ref, wd_ref, o_ref,
                       xn_ref, wgb_ref, wub_ref, wdb_ref, *, ts):
    i = pl.program_id(0)
    j = pl.program_id(1)
    valid = tv_ref[i]
    n_sub = x_ref.shape[0] // ts

    @pl.when(valid > 0)
    def _():
        wgb_ref[...] = wg_ref[...].astype(BF16)
        wub_ref[...] = wu_ref[...].astype(BF16)
        wdb_ref[...] = wd_ref[...].astype(BF16)

    for s in range(n_sub):
        rows = pl.ds(s * ts, ts)

        @pl.when(s * ts < valid)
        def _():
            @pl.when(j == 0)
            def _():
                xn_ref[rows, :] = (_rms(x_ref[rows, :]) * g_ref[...]).astype(BF16)

            xn = xn_ref[rows, :]
            a = _silu(_dot(xn, wgb_ref[...])) * _dot(xn, wub_ref[...])
            y = _dot(a.astype(BF16), wdb_ref[...])

            @pl.when(j == 0)
            def _():
                o_ref[rows, :] = y

            @pl.when(j > 0)
            def _():
                o_ref[rows, :] += y

        @pl.when((s * ts >= valid) & (j == 0))
        def _():
            o_ref[rows, :] = jnp.zeros((ts, o_ref.shape[1]), F32)


def expert_ffn(xs, gain, w_gate, w_up, w_down, tile_expert, tile_valid, tm, ts, tf):
    R, D = xs.shape
    Fd = w_gate.shape[2]
    nf = Fd // tf

    def jj(i, j, tv):
        return jnp.where(tv[i] > 0, j, nf - 1)

    grid_spec = pltpu.PrefetchScalarGridSpec(
        num_scalar_prefetch=2,
        grid=(R // tm, nf),
        in_specs=[pl.BlockSpec((tm, D), lambda i, j, te, tv: (i, 0)),
                  pl.BlockSpec((1, D), lambda i, j, te, tv: (0, 0)),
                  pl.BlockSpec((None, D, tf), lambda i, j, te, tv: (te[i], 0, jj(i, j, tv))),
                  pl.BlockSpec((None, D, tf), lambda i, j, te, tv: (te[i], 0, jj(i, j, tv))),
                  pl.BlockSpec((None, tf, D), lambda i, j, te, tv: (te[i], jj(i, j, tv), 0))],
        out_specs=pl.BlockSpec((tm, D), lambda i, j, te, tv: (i, 0)),
        scratch_shapes=[pltpu.VMEM((tm, D), BF16),
                        pltpu.VMEM((D, tf), BF16), pltpu.VMEM((D, tf), BF16), pltpu.VMEM((tf, D), BF16)],
    )
    return pl.pallas_call(
        functools.partial(_expert_ffn_kernel, ts=ts),
        grid_spec=grid_spec,
        out_shape=jax.ShapeDtypeStruct((R, D), F32),
        compiler_params=_cparams(2),
        name="expert_ffn",
    )(tile_expert, tile_valid, xs, gain.reshape(1, D), w_gate, w_up, w_down)


def _combine_kernel(p1_ref, p2_ref, route_ref, h_ref, y_ref, o_ref, b1_ref, b2_ref, sem):
    tb = p1_ref.shape[0]

    def copies(t):
        dst = pl.ds(t, 1)
        return (pltpu.make_async_copy(y_ref.at[pl.ds(p1_ref[t], 1), :], b1_ref.at[dst, :], sem),
                pltpu.make_async_copy(y_ref.at[pl.ds(p2_ref[t], 1), :], b2_ref.at[dst, :], sem))

    def issue(t, c):
        for cp in copies(t):
            cp.start()
        return c

    lax.fori_loop(0, tb, issue, 0)

    def drain(t, c):
        for cp in copies(t):
            cp.wait()
        return c

    lax.fori_loop(0, tb, drain, 0)
    w1 = route_ref[:, 2:3]
    w2 = route_ref[:, 3:4]
    o_ref[...] = h_ref[...] + w1 * b1_ref[...] + w2 * b2_ref[...]


def moe_combine(h, y, route, pos1, pos2, tb):
    T, D = h.shape
    smem = lambda: pl.BlockSpec((tb,), lambda i: (i,), memory_space=pltpu.SMEM)
    return pl.pallas_call(
        _combine_kernel,
        grid=(T // tb,),
        in_specs=[smem(), smem(),
                  pl.BlockSpec((tb, LANES), lambda i: (i, 0)),
                  pl.BlockSpec((tb, D), lambda i: (i, 0)),
                  pl.BlockSpec(memory_space=pl.ANY)],
        out_specs=pl.BlockSpec((tb, D), lambda i: (i, 0)),
        out_shape=jax.ShapeDtypeStruct((T, D), F32),
        scratch_shapes=[pltpu.VMEM((tb, D), F32), pltpu.VMEM((tb, D), F32),
                        pltpu.SemaphoreType.DMA(())],
        compiler_params=_cparams(1),
        name="moe_combine",
    )(pos1, pos2, route, h, y)


def moe_ffn(h, gain, w_router, w_gate, w_up, w_down, tm=1024, ts=256, tf=512, tb=512):
    T = h.shape[0]
    route, counts = moe_router(h, gain, w_router, tm=512)
    e1 = route[:, 0].astype(I32)
    e2 = route[:, 1].astype(I32)
    counts = counts[0, :N_EXPERTS].astype(I32)
    tiles_per = (counts + tm - 1) // tm
    tile_end = jnp.cumsum(tiles_per)
    tile_start = tile_end - tiles_per
    row_start = tile_start * tm
    pos1 = row_start[e1] + route[:, 4].astype(I32)
    pos2 = row_start[e2] + route[:, 5].astype(I32)
    n_tiles = (TOP_K * T) // tm + N_EXPERTS
    tile_ids = jnp.arange(n_tiles, dtype=I32)
    tile_expert = jnp.sum(jnp.minimum(tile_ids, tile_end[-1] - 1)[:, None] >= tile_end[None, :], axis=1).astype(I32)
    tile_valid = jnp.clip(counts[tile_expert] - (tile_ids - tile_start[tile_expert]) * tm, 0, tm)
    tile_valid = jnp.where(tile_ids < tile_end[-1], tile_valid, 0).astype(I32)

    xs = moe_dispatch(h, pos1, pos2, n_tiles * tm, tb)
    y = expert_ffn(xs, gain, w_gate, w_up, w_down, tile_expert, tile_valid, tm, ts, tf)
    return moe_combine(h, y, route, pos1, pos2, tb)


def _pad_heads(a, n_heads, width, pad_to, axis):
    shape = list(a.shape)
    a = a.reshape(shape[:axis] + [n_heads, width] + shape[axis + 1:])
    pads = [(0, 0)] * a.ndim
    pads[axis + 1] = (0, pad_to - width)
    a = jnp.pad(a, pads)
    return a.reshape(shape[:axis] + [n_heads * pad_to] + shape[axis + 1:])


def kernel(x, mem, ln_mix, ln_mem, w_mem_kv, mem_q_gain, mem_k_gain, w_out, ln_ffn, swa_w_in, swa_q_gain, swa_k_gain, swa_sink, ret_w_in, mlstm_w_in, mlstm_conv_w, mlstm_conv_b, mlstm_i_bias, mlstm_f_bias, mlstm_out_gain, ffn_w_gate, ffn_w_up, ffn_w_down, moe_router, moe_w_gate, moe_w_up, moe_w_down):
    B, S, D = x.shape
    M = mem.shape[1]
    T = B * S
    depth = ln_mix.shape[0]
    h = x.reshape(T, D)
    mem2 = mem.reshape(B * M, D)

    for layer in range(depth):
        kind = layer % N_MIXERS
        idx = layer // N_MIXERS
        mk, mv = rms_proj(mem2, ln_mem[layer], w_mem_kv[layer].astype(BF16),
                          ((MEM_WIDTH, BF16, mem_k_gain[layer]), (MEM_WIDTH, BF16, None)),
                          tm=B * M, name="mem_kv_proj")
        w_o = w_out[layer].astype(BF16)
        w_o_mix, w_o_mem = w_o[:MIX_WIDTH], w_o[MIX_WIDTH:]
        qm_out = (MEM_WIDTH, BF16, mem_q_gain[layer])

        if kind == 0:
            kvw = SWA_KV_HEADS * HEAD_DIM
            q, k, v, qm = rms_proj(h, ln_mix[layer], swa_w_in[idx].astype(BF16),
                                   ((MIX_WIDTH, BF16, swa_q_gain[idx]), (kvw, BF16, swa_k_gain[idx]),
                                    (kvw, BF16, None), qm_out), tm=512, name="swa_in_proj")
            mix = swa_attention(q, k, v, swa_sink[idx], B)
        elif kind == 1:
            qkw = RET_HEADS * RET_QK_DIM
            q, k, v, g, qm = rms_proj(h, ln_mix[layer], ret_w_in[idx].astype(BF16),
                                      ((qkw, BF16, None), (qkw, F32, None), (MIX_WIDTH, BF16, None),
                                       (MIX_WIDTH, F32, None), qm_out), tm=512, name="ret_in_proj")
            mix = retention(q, k, v, g, B)
        else:
            H, P, VP = MLSTM_HEADS, MLSTM_QK_PAD, MLSTM_V_PAD
            w = mlstm_w_in[idx]
            qkw = 2 * H * MLSTM_QK_DIM
            o_v, o_og, o_ig = qkw, qkw + MIX_WIDTH, qkw + 2 * MIX_WIDTH
            o_fg, o_qm = o_ig + H, o_ig + 2 * H
            w_gates = jnp.zeros((D, LANES), F32).at[:, :2 * H].set(w[:, o_ig:o_qm])
            w_pad = jnp.concatenate([
                _pad_heads(w[:, :qkw], 2 * H, MLSTM_QK_DIM, P, 1),
                _pad_heads(w[:, o_v:o_og], H, MLSTM_V_DIM, VP, 1),
                _pad_heads(w[:, o_og:o_ig], H, MLSTM_V_DIM, VP, 1),
                w_gates, w[:, o_qm:]], axis=1).astype(BF16)
            qk, v, og, gates, qm = rms_proj(h, ln_mix[layer], w_pad,
                                            ((2 * H * P, F32, None), (H * VP, BF16, None), (H * VP, F32, None),
                                             (LANES, F32, None), qm_out), tm=512, name="mlstm_in_proj")
            gates_t = gates[:, :SUBLANES].T
            bias = jnp.concatenate([mlstm_i_bias[idx], mlstm_f_bias[idx]])
            bias_c = jnp.zeros((1, LANES), F32).at[0, :2 * H].set(bias)
            bias_r = jnp.broadcast_to(bias[:, None], (SUBLANES, MLSTM_CHUNK))
            mix = mlstm(qk, v, og, gates, gates_t,
                        _pad_heads(mlstm_conv_w[idx], 2 * H, MLSTM_QK_DIM, P, 1),
                        _pad_heads(mlstm_conv_b[idx][None], 2 * H, MLSTM_QK_DIM, P, 1),
                        bias_c, bias_r,
                        _pad_heads(mlstm_out_gain[idx][None], H, MLSTM_V_DIM, VP, 1), B)
            w_o_mix = _pad_heads(w_o_mix, H, MLSTM_V_DIM, VP, 0)

        mem_out = mem_attention(qm, mk, mv, B, tm=512)
        h = out_proj(h, mix, mem_out, w_o_mix, w_o_mem, tm=512)
        j = layer // 2
        if layer % 2 == 0:
            h = dense_ffn(h, ln_ffn[layer], ffn_w_gate[j].astype(BF16), ffn_w_up[j].astype(BF16),
                          ffn_w_down[j].astype(BF16), tm=1024, tf=896)
        else:
            h = moe_ffn(h, ln_ffn[layer], moe_router[j], moe_w_gate[j], moe_w_up[j], moe_w_down[j])
    return h.reshape(B, S, D)
```

```python
import functools
import math

import jax
import jax.numpy as jnp
from jax import lax
from jax.experimental import pallas as pl
from jax.experimental.pallas import tpu as pltpu

F32 = jnp.float32
BF16 = jnp.bfloat16
I32 = jnp.int32

D_MODEL = 1024
N_MIXERS = 3
HEAD_DIM = 64
MEM_HEADS = 4
MEM_HEAD_DIM = 64
MEM_WIDTH = MEM_HEADS * MEM_HEAD_DIM
MIX_WIDTH = D_MODEL - MEM_WIDTH
RMS_EPS = 1e-6
NEG_INF = -1e30

SWA_Q_HEADS = MIX_WIDTH // HEAD_DIM
SWA_KV_HEADS = 4
SWA_GROUP = SWA_Q_HEADS // SWA_KV_HEADS
SWA_WINDOW = 128
SWA_BLOCK = 128

RET_HEADS = 6
RET_QK_DIM = 64
RET_V_DIM = MIX_WIDTH // RET_HEADS
RET_CHUNK = 128

MLSTM_HEADS = 4
MLSTM_V_DIM = MIX_WIDTH // MLSTM_HEADS
MLSTM_QK_DIM = MLSTM_V_DIM // 2
MLSTM_CHUNK = 128
MLSTM_CONV = 4
MLSTM_QK_PAD = 128
MLSTM_V_PAD = 256

N_EXPERTS = 8
TOP_K = 2

LANES = 128
SUBLANES = 8
VMEM_LIMIT = 56 * 1024 * 1024
ROW_DMA_UNROLL = 8


def _cparams(n_axes):
    return pltpu.CompilerParams(dimension_semantics=("arbitrary",) * n_axes,
                                vmem_limit_bytes=VMEM_LIMIT)


def _rms(x, eps=RMS_EPS):
    return x * lax.rsqrt(jnp.mean(x * x, axis=-1, keepdims=True) + eps)


def _dot(a, b):
    return jnp.dot(a, b, preferred_element_type=F32)


def _dot_nt(a, b):
    return lax.dot_general(a, b, (((1,), (1,)), ((), ())), preferred_element_type=F32)


def _dot_tn(a, b):
    return lax.dot_general(a, b, (((0,), (0,)), ((), ())), preferred_element_type=F32)


def _silu(x):
    return x * jax.nn.sigmoid(x)


NORM_CHUNK = 256


def _head_rms(y, seg_ref, hg):
    sq = y * y
    hi = sq.astype(BF16)
    lo = (sq - hi.astype(F32)).astype(BF16)
    ms = _dot(hi, seg_ref[...]) + _dot(lo, seg_ref[...])
    return y * lax.rsqrt(ms + RMS_EPS) * hg


def _rms_proj_kernel(x_ref, g_ref, w_ref, seg_ref, *refs, widths, normed):
    n_gain = sum(normed)
    gain_refs, o_refs = refs[:n_gain], refs[n_gain:]
    xn = (_rms(x_ref[...]) * g_ref[...]).astype(BF16)
    off = 0
    gi = 0
    for o_ref, wd, nrm in zip(o_refs, widths, normed):
        if nrm:
            hg = gain_refs[gi][...]
            gi += 1
            for c in range(0, wd, NORM_CHUNK):
                y = _dot(xn, w_ref[:, off + c:off + c + NORM_CHUNK])
                o_ref[:, c:c + NORM_CHUNK] = _head_rms(y, seg_ref, hg).astype(o_ref.dtype)
        else:
            o_ref[...] = _dot(xn, w_ref[:, off:off + wd]).astype(o_ref.dtype)
        off += wd


def rms_proj(x, gain, w, outs, tm, name):
    T, D = x.shape
    N = w.shape[1]
    widths = tuple(o[0] for o in outs)
    normed = tuple(o[2] is not None for o in outs)
    head_gains = [jnp.tile(o[2], NORM_CHUNK // HEAD_DIM).reshape(1, NORM_CHUNK) for o in outs if o[2] is not None]
    assert sum(widths) == N and T % tm == 0
    assert all(wd % NORM_CHUNK == 0 for wd, nrm in zip(widths, normed) if nrm)
    head_of = jnp.arange(NORM_CHUNK) // HEAD_DIM
    seg = jnp.where(head_of[:, None] == head_of[None, :], 1.0 / HEAD_DIM, 0.0).astype(BF16)
    return pl.pallas_call(
        functools.partial(_rms_proj_kernel, widths=widths, normed=normed),
        grid=(T // tm,),
        in_specs=[pl.BlockSpec((tm, D), lambda i: (i, 0)),
                  pl.BlockSpec((1, D), lambda i: (0, 0)),
                  pl.BlockSpec((D, N), lambda i: (0, 0)),
                  pl.BlockSpec((NORM_CHUNK, NORM_CHUNK), lambda i: (0, 0))]
                 + [pl.BlockSpec((1, NORM_CHUNK), lambda i: (0, 0))] * len(head_gains),
        out_specs=[pl.BlockSpec((tm, wd), lambda i: (i, 0)) for wd in widths],
        out_shape=[jax.ShapeDtypeStruct((T, o[0]), o[1]) for o in outs],
        compiler_params=_cparams(1),
        name=name,
    )(x, gain.reshape(1, D), w, seg, *head_gains)


def _swa_kernel(sink_ref, q_ref, kc_ref, kp_ref, vc_ref, vp_ref, bias_ref, o_ref):
    n = pl.program_id(1)
    L = SWA_BLOCK
    G = SWA_GROUP
    row = lax.broadcasted_iota(I32, (G * L, 2 * L), 0) & (L - 1)
    col = lax.broadcasted_iota(I32, (G * L, 2 * L), 1)
    dist = row + L - col
    mask = (dist >= 0) & (dist < SWA_WINDOW) & ((col >= L) | (n > 0))
    band = lax.broadcasted_iota(I32, (G * L, 1), 0)
    outs = []
    for g in range(SWA_KV_HEADS):
        ksl = slice(g * HEAD_DIM, (g + 1) * HEAD_DIM)
        k = jnp.concatenate([kp_ref[:, ksl], kc_ref[:, ksl]], axis=0)
        v = jnp.concatenate([vp_ref[:, ksl], vc_ref[:, ksl]], axis=0)
        heads = range(g * G, (g + 1) * G)
        q = jnp.concatenate([q_ref[:, h * HEAD_DIM:(h + 1) * HEAD_DIM] for h in heads], axis=0)
        sink = jnp.full((G * L, 1), sink_ref[g * G + G - 1], F32)
        for j in range(G - 2, -1, -1):
            sink = jnp.where(band < (j + 1) * L, sink_ref[g * G + j], sink)
        s = _dot_nt(q, k) * (HEAD_DIM ** -0.5) - bias_ref[g]
        s = jnp.where(mask, s, NEG_INF)
        m = jnp.maximum(jnp.max(s, axis=-1, keepdims=True), sink)
        e = jnp.exp(s - m)
        denom = jnp.sum(e, axis=-1, keepdims=True) + jnp.exp(sink - m)
        o = _dot(e.astype(BF16), v) / denom
        outs.extend(o[j * L:(j + 1) * L] for j in range(G))
    o_ref[...] = jnp.concatenate(outs, axis=-1).astype(o_ref.dtype)


def swa_attention(q, k, v, sink, batch):
    T = q.shape[0]
    L = SWA_BLOCK
    G = SWA_GROUP
    nb = T // batch // L
    kvw = SWA_KV_HEADS * HEAD_DIM
    slopes = jnp.exp2(-8.0 * jnp.arange(1, SWA_Q_HEADS + 1, dtype=F32) / SWA_Q_HEADS)
    adist = jnp.abs(jnp.arange(L)[:, None] + L - jnp.arange(2 * L)[None, :]).astype(F32)
    bias = (slopes[:, None, None] * adist).reshape(SWA_KV_HEADS, G * L, 2 * L)
    cur = lambda b, n: (b * nb + n, 0)
    prev = lambda b, n: (b * nb + jnp.maximum(n - 1, 0), 0)
    return pl.pallas_call(
        _swa_kernel,
        grid=(batch, nb),
        in_specs=[pl.BlockSpec(memory_space=pltpu.SMEM),
                  pl.BlockSpec((L, MIX_WIDTH), cur),
                  pl.BlockSpec((L, kvw), cur), pl.BlockSpec((L, kvw), prev),
                  pl.BlockSpec((L, kvw), cur), pl.BlockSpec((L, kvw), prev),
                  pl.BlockSpec((SWA_KV_HEADS, G * L, 2 * L), lambda b, n: (0, 0, 0))],
        out_specs=pl.BlockSpec((L, MIX_WIDTH), cur),
        out_shape=jax.ShapeDtypeStruct((T, MIX_WIDTH), BF16),
        compiler_params=_cparams(2),
        name="swa_attention",
    )(sink, q, k, k, v, v, bias)


def _retention_kernel(q_ref, k_ref, v_ref, g_ref, dec_ref, qd_ref, kd_ref, cd_ref, o_ref, state_ref):
    @pl.when(pl.program_id(1) == 0)
    def _():
        state_ref[...] = jnp.zeros_like(state_ref)

    outs = []
    for h in range(RET_HEADS):
        qsl = slice(h * RET_QK_DIM, (h + 1) * RET_QK_DIM)
        vsl = slice(h * RET_V_DIM, (h + 1) * RET_V_DIM)
        q = q_ref[:, qsl].astype(BF16)
        k = k_ref[:, qsl] * (RET_QK_DIM ** -0.5)
        v = v_ref[:, vsl]
        state = state_ref[h]
        s = _dot_nt(q, k.astype(BF16)) * dec_ref[h]
        o = _dot(s.astype(BF16), v) + _dot(q, state.astype(BF16)) * qd_ref[h]
        state_ref[h] = state * cd_ref[h] + _dot_tn((k * kd_ref[h]).astype(BF16), v)
        outs.append(_silu(g_ref[:, vsl]) * _rms(o))
    o_ref[...] = jnp.concatenate(outs, axis=-1).astype(o_ref.dtype)


def _retention_consts():
    H, L = RET_HEADS, RET_CHUNK
    log_gamma = jnp.log1p(-jnp.exp2(-5.0 - jnp.arange(H, dtype=F32)))
    pos = jnp.arange(L, dtype=F32)
    rel = pos[:, None] - pos[None, :]
    intra = jnp.exp(jnp.where(rel >= 0, log_gamma[:, None, None] * rel, -jnp.inf))
    q_decay = jnp.exp(log_gamma[:, None] * (pos + 1.0))[..., None]
    k_decay = jnp.exp(log_gamma[:, None] * (L - 1.0 - pos))[..., None]
    chunk_decay = jnp.exp(log_gamma * L)[:, None, None]
    return (intra,
            jnp.broadcast_to(q_decay, (H, L, RET_V_DIM)),
            jnp.broadcast_to(k_decay, (H, L, RET_QK_DIM)),
            jnp.broadcast_to(chunk_decay, (H, RET_QK_DIM, RET_V_DIM)))


def retention(q, k, v, g, batch):
    T = q.shape[0]
    L = RET_CHUNK
    nc = T // batch // L
    H = RET_HEADS
    intra, qd, kd, cd = _retention_consts()
    cur = lambda b, c: (b * nc + c, 0)
    const = lambda b, c: (0, 0, 0)
    return pl.pallas_call(
        _retention_kernel,
        grid=(batch, nc),
        in_specs=[pl.BlockSpec((L, H * RET_QK_DIM), cur), pl.BlockSpec((L, H * RET_QK_DIM), cur),
                  pl.BlockSpec((L, MIX_WIDTH), cur), pl.BlockSpec((L, MIX_WIDTH), cur),
                  pl.BlockSpec((H, L, L), const), pl.BlockSpec((H, L, RET_V_DIM), const),
                  pl.BlockSpec((H, L, RET_QK_DIM), const), pl.BlockSpec((H, RET_QK_DIM, RET_V_DIM), const)],
        out_specs=pl.BlockSpec((L, MIX_WIDTH), cur),
        out_shape=jax.ShapeDtypeStruct((T, MIX_WIDTH), BF16),
        scratch_shapes=[pltpu.VMEM((H, RET_QK_DIM, RET_V_DIM), F32)],
        compiler_params=_cparams(2),
        name="retention",
    )(q, k, v, g, intra, qd, kd, cd)


def _split3(x):
    x1 = x.astype(BF16)
    r1 = x - x1.astype(F32)
    x2 = r1.astype(BF16)
    x3 = (r1 - x2.astype(F32)).astype(BF16)
    return x1, x2, x3


def _mlstm_kernel(qk_ref, v_ref, og_ref, gc_ref, gr_ref, cw_ref, cb_ref, bc_ref, br_ref, gain_ref,
                  tri_ref, trit_ref, o_ref, xbuf_ref, c_ref, n_ref, m_ref):
    L = MLSTM_CHUNK
    H = MLSTM_HEADS
    P = MLSTM_QK_PAD
    VP = MLSTM_V_PAD
    KT = SUBLANES

    @pl.when(pl.program_id(1) == 0)
    def _():
        xbuf_ref[0:KT, :] = jnp.zeros((KT, xbuf_ref.shape[1]), F32)
        c_ref[...] = jnp.zeros_like(c_ref)
        n_ref[...] = jnp.zeros_like(n_ref)
        m_ref[...] = jnp.zeros_like(m_ref)

    xbuf_ref[KT:KT + L, :] = qk_ref[...]
    acc = cb_ref[...] + cw_ref[MLSTM_CONV - 1:MLSTM_CONV, :] * xbuf_ref[KT:KT + L, :]
    for j in range(MLSTM_CONV - 1):
        sh = MLSTM_CONV - 1 - j
        acc = acc + cw_ref[j:j + 1, :] * xbuf_ref[KT - sh:KT - sh + L, :]
    xbuf_ref[0:KT, :] = qk_ref[L - KT:L, :]
    qk = _silu(acc)

    xc = gc_ref[...] + bc_ref[...]
    xr = gr_ref[...] + br_ref[...]
    lfc = jax.nn.log_sigmoid(xc)
    lfr = jax.nn.log_sigmoid(xr)
    tri = tri_ref[...]
    trit = trit_ref[...]
    bc = sum(_dot(tri, t) for t in _split3(lfc))
    br = sum(_dot(t, trit) for t in _split3(lfr))

    rowi = lax.broadcasted_iota(I32, (L, L), 0)
    coli = lax.broadcasted_iota(I32, (L, L), 1)
    causal = rowi >= coli
    outs = []
    for h in range(H):
        q = qk[:, h * P:(h + 1) * P].astype(BF16)
        k = qk[:, (H + h) * P:(H + h + 1) * P] * (MLSTM_QK_DIM ** -0.5)
        v = v_ref[:, h * VP:(h + 1) * VP]
        li_c = xc[:, h:h + 1]
        b_c = bc[:, H + h:H + h + 1]
        li_r = xr[h:h + 1, :]
        b_r = br[H + h:H + h + 1, :]
        g = b_c[L - 1:L, :]
        m = m_ref[h:h + 1, 0:1]
        C = c_ref[h]
        nvec = n_ref[h:h + 1, :]

        dmat = jnp.where(causal, b_c - b_r + li_r, -jnp.inf)
        inter = b_c + m
        m_t = jnp.maximum(inter, jnp.max(dmat, axis=-1, keepdims=True))
        w = jnp.exp(dmat - m_t)
        a = jnp.exp(inter - m_t)
        s = _dot_nt(q, k.astype(BF16)) * w
        num = _dot(s.astype(BF16), v) + a * _dot(q, C.astype(BF16))
        qf = qk[:, h * P:(h + 1) * P]
        den = jnp.sum(s, axis=-1, keepdims=True) + a * jnp.sum(qf * nvec, axis=-1, keepdims=True)
        hh = num / jnp.maximum(jnp.abs(den), jnp.exp(-m_t))

        u_c = g - b_c + li_c
        u_r = g - b_r + li_r
        m_new = jnp.maximum(g + m, jnp.max(u_r, axis=-1, keepdims=True))
        wk = jnp.exp(u_c - m_new)
        decay = jnp.exp(g + m - m_new)
        kw = k * wk
        c_ref[h] = decay * C + _dot_tn(kw.astype(BF16), v)
        n_ref[h:h + 1, :] = decay * nvec + jnp.sum(kw, axis=0, keepdims=True)
        m_ref[h:h + 1, :] = jnp.broadcast_to(m_new, (1, m_ref.shape[1]))

        ms = jnp.sum(hh * hh, axis=-1, keepdims=True) * (1.0 / MLSTM_V_DIM)
        hc = hh * lax.rsqrt(ms + RMS_EPS) * gain_ref[:, h * VP:(h + 1) * VP]
        outs.append(jax.nn.sigmoid(og_ref[:, h * VP:(h + 1) * VP]) * hc)
    o_ref[...] = jnp.concatenate(outs, axis=-1).astype(o_ref.dtype)


def mlstm(qk, v, og, gates, gates_t, conv_w, conv_b, bias_c, bias_r, gain, batch):
    T = qk.shape[0]
    L = MLSTM_CHUNK
    nc = T // batch // L
    H, P, VP = MLSTM_HEADS, MLSTM_QK_PAD, MLSTM_V_PAD
    tri = jnp.tril(jnp.ones((L, L), BF16))
    cur = lambda b, c: (b * nc + c, 0)
    c2 = lambda b, c: (0, 0)
    return pl.pallas_call(
        _mlstm_kernel,
        grid=(batch, nc),
        in_specs=[pl.BlockSpec((L, 2 * H * P), cur), pl.BlockSpec((L, H * VP), cur),
                  pl.BlockSpec((L, H * VP), cur), pl.BlockSpec((L, LANES), cur),
                  pl.BlockSpec((SUBLANES, L), lambda b, c: (0, b * nc + c)),
                  pl.BlockSpec((MLSTM_CONV, 2 * H * P), c2), pl.BlockSpec((1, 2 * H * P), c2),
                  pl.BlockSpec((1, LANES), c2), pl.BlockSpec((SUBLANES, L), c2),
                  pl.BlockSpec((1, H * VP), c2),
                  pl.BlockSpec((L, L), c2), pl.BlockSpec((L, L), c2)],
        out_specs=pl.BlockSpec((L, H * VP), cur),
        out_shape=jax.ShapeDtypeStruct((T, H * VP), BF16),
        scratch_shapes=[pltpu.VMEM((SUBLANES + L, 2 * H * P), F32),
                        pltpu.VMEM((H, P, VP), F32),
                        pltpu.VMEM((SUBLANES, P), F32),
                        pltpu.VMEM((SUBLANES, LANES), F32)],
        compiler_params=_cparams(2),
        name="mlstm",
    )(qk, v, og, gates, gates_t, conv_w, conv_b, bias_c, bias_r, gain, tri, tri.T)


def _mem_attn_kernel(q_ref, k_ref, v_ref, o_ref):
    outs = []
    for h in range(MEM_HEADS):
        sl = slice(h * MEM_HEAD_DIM, (h + 1) * MEM_HEAD_DIM)
        s = _dot_nt(q_ref[:, sl], k_ref[:, sl]) * (MEM_HEAD_DIM ** -0.5)
        e = jnp.exp(s - jnp.max(s, axis=-1, keepdims=True))
        outs.append(_dot(e.astype(BF16), v_ref[:, sl]) / jnp.sum(e, axis=-1, keepdims=True))
    o_ref[...] = jnp.concatenate(outs, axis=-1).astype(o_ref.dtype)


def mem_attention(qm, mk, mv, batch, tm):
    T = qm.shape[0]
    M = mk.shape[0] // batch
    nt = T // batch // tm
    return pl.pallas_call(
        _mem_attn_kernel,
        grid=(batch, nt),
        in_specs=[pl.BlockSpec((tm, MEM_WIDTH), lambda b, i: (b * nt + i, 0)),
                  pl.BlockSpec((M, MEM_WIDTH), lambda b, i: (b, 0)),
                  pl.BlockSpec((M, MEM_WIDTH), lambda b, i: (b, 0))],
        out_specs=pl.BlockSpec((tm, MEM_WIDTH), lambda b, i: (b * nt + i, 0)),
        out_shape=jax.ShapeDtypeStruct((T, MEM_WIDTH), BF16),
        compiler_params=_cparams(2),
        name="mem_attention",
    )(qm, mk, mv)


def _out_proj_kernel(h_ref, mix_ref, mem_ref, w1_ref, w2_ref, o_ref):
    o_ref[...] = h_ref[...] + _dot(mix_ref[...], w1_ref[...]) + _dot(mem_ref[...], w2_ref[...])


def out_proj(h, mix, mem_out, w_mix, w_mem, tm):
    T, D = h.shape
    Wm = mix.shape[1]
    return pl.pallas_call(
        _out_proj_kernel,
        grid=(T // tm,),
        in_specs=[pl.BlockSpec((tm, D), lambda i: (i, 0)),
                  pl.BlockSpec((tm, Wm), lambda i: (i, 0)),
                  pl.BlockSpec((tm, MEM_WIDTH), lambda i: (i, 0)),
                  pl.BlockSpec((Wm, D), lambda i: (0, 0)),
                  pl.BlockSpec((MEM_WIDTH, D), lambda i: (0, 0))],
        out_specs=pl.BlockSpec((tm, D), lambda i: (i, 0)),
        out_shape=jax.ShapeDtypeStruct((T, D), F32),
        compiler_params=_cparams(1),
        name="out_proj",
    )(h, mix, mem_out, w_mix, w_mem)


def _dense_ffn_kernel(h_ref, g_ref, wg_ref, wu_ref, wd_ref, o_ref, xn_ref, acc_ref):
    j = pl.program_id(1)

    @pl.when(j == 0)
    def _():
        h = h_ref[...]
        xn_ref[...] = (_rms(h) * g_ref[...]).astype(BF16)
        acc_ref[...] = h

    xn = xn_ref[...]
    a = _silu(_dot(xn, wg_ref[...])) * _dot(xn, wu_ref[...])
    acc_ref[...] += _dot(a.astype(BF16), wd_ref[...])

    @pl.when(j == pl.num_programs(1) - 1)
    def _():
        o_ref[...] = acc_ref[...]


def dense_ffn(h, gain, w_gate, w_up, w_down, tm, tf):
    T, D = h.shape
    Fd = w_gate.shape[1]
    return pl.pallas_call(
        _dense_ffn_kernel,
        grid=(T // tm, Fd // tf),
        in_specs=[pl.BlockSpec((tm, D), lambda i, j: (i, 0)),
                  pl.BlockSpec((1, D), lambda i, j: (0, 0)),
                  pl.BlockSpec((D, tf), lambda i, j: (0, j)),
                  pl.BlockSpec((D, tf), lambda i, j: (0, j)),
                  pl.BlockSpec((tf, D), lambda i, j: (j, 0))],
        out_specs=pl.BlockSpec((tm, D), lambda i, j: (i, 0)),
        out_shape=jax.ShapeDtypeStruct((T, D), F32),
        scratch_shapes=[pltpu.VMEM((tm, D), BF16), pltpu.VMEM((tm, D), F32)],
        compiler_params=_cparams(2),
        name="dense_ffn",
    )(h, gain.reshape(1, D), w_gate, w_up, w_down)


def _router_kernel(h_ref, g_ref, wr_ref, tri_ref, o_ref, cnt_ref, carry_ref):
    i = pl.program_id(0)

    @pl.when(i == 0)
    def _():
        carry_ref[...] = jnp.zeros_like(carry_ref)

    tm = h_ref.shape[0]
    xn = _rms(h_ref[...]) * g_ref[...]
    logits = jnp.dot(xn, wr_ref[...], preferred_element_type=F32, precision=lax.Precision.HIGHEST)
    lane = lax.broadcasted_iota(I32, (tm, LANES), 1)
    logits = jnp.where(lane < N_EXPERTS, logits, -jnp.inf)
    t1 = jnp.max(logits, axis=-1, keepdims=True)
    e1 = jnp.min(jnp.where(logits == t1, lane, LANES), axis=-1, keepdims=True)
    rest = jnp.where(lane == e1, -jnp.inf, logits)
    t2 = jnp.max(rest, axis=-1, keepdims=True)
    e2 = jnp.min(jnp.where(rest == t2, lane, LANES), axis=-1, keepdims=True)
    x2 = jnp.exp(t2 - t1)
    w1 = 1.0 / (1.0 + x2)
    w2 = x2 / (1.0 + x2)

    oh1 = lane == e1
    oh2 = lane == e2
    cnt = jnp.where(oh1 | oh2, 1.0, 0.0)
    before = _dot(tri_ref[...], cnt.astype(BF16)) + carry_ref[...]
    r1 = jnp.sum(jnp.where(oh1, before, 0.0), axis=-1, keepdims=True)
    r2 = jnp.sum(jnp.where(oh2, before, 0.0), axis=-1, keepdims=True)
    carry_ref[...] += jnp.sum(cnt, axis=0, keepdims=True)

    cols = (e1.astype(F32), e2.astype(F32), w1, w2, r1, r2)
    out = jnp.zeros((tm, LANES), F32)
    for c, val in enumerate(cols):
        out = jnp.where(lane == c, val, out)
    o_ref[...] = out
    cnt_ref[...] = jnp.broadcast_to(carry_ref[...], cnt_ref.shape)


def moe_router(h, gain, w_router, tm):
    T, D = h.shape
    wr = jnp.zeros((D, LANES), F32).at[:, :N_EXPERTS].set(w_router)
    tri = jnp.tril(jnp.ones((tm, tm), BF16), k=-1)
    return pl.pallas_call(
        _router_kernel,
        grid=(T // tm,),
        in_specs=[pl.BlockSpec((tm, D), lambda i: (i, 0)),
                  pl.BlockSpec((1, D), lambda i: (0, 0)),
                  pl.BlockSpec((D, LANES), lambda i: (0, 0)),
                  pl.BlockSpec((tm, tm), lambda i: (0, 0))],
        out_specs=[pl.BlockSpec((tm, LANES), lambda i: (i, 0)),
                   pl.BlockSpec((SUBLANES, LANES), lambda i: (0, 0))],
        out_shape=[jax.ShapeDtypeStruct((T, LANES), F32),
                   jax.ShapeDtypeStruct((SUBLANES, LANES), F32)],
        scratch_shapes=[pltpu.VMEM((1, LANES), F32)],
        compiler_params=_cparams(1),
        name="moe_router",
    )(h, gain.reshape(1, D), wr, tri)


def _dispatch_kernel(p1_ref, p2_ref, h_ref, xs_in_ref, xs_ref, sem):
    del xs_in_ref
    tb = p1_ref.shape[0]

    def copy(t, pos):
        return pltpu.make_async_copy(h_ref.at[pl.ds(t, 1), :], xs_ref.at[pl.ds(pos, 1), :], sem)

    def issue(t, c):
        copy(t, p1_ref[t]).start()
        copy(t, p2_ref[t]).start()
        return c

    lax.fori_loop(0, tb, issue, 0, unroll=ROW_DMA_UNROLL)
    for _ in range(TOP_K):
        pltpu.make_async_copy(h_ref, xs_ref.at[pl.ds(0, tb), :], sem).wait()


def moe_dispatch(h, pos1, pos2, n_rows, tb):
    T, D = h.shape
    xs0 = jnp.zeros((n_rows, D), h.dtype)
    return pl.pallas_call(
        _dispatch_kernel,
        grid=(T // tb,),
        in_specs=[pl.BlockSpec((tb,), lambda i: (i,), memory_space=pltpu.SMEM),
                  pl.BlockSpec((tb,), lambda i: (i,), memory_space=pltpu.SMEM),
                  pl.BlockSpec((tb, D), lambda i: (i, 0)),
                  pl.BlockSpec(memory_space=pl.ANY)],
        out_specs=pl.BlockSpec(memory_space=pl.ANY),
        out_shape=jax.ShapeDtypeStruct(xs0.shape, xs0.dtype),
        scratch_shapes=[pltpu.SemaphoreType.DMA(())],
        input_output_aliases={3: 0},
        compiler_params=_cparams(1),
        name="moe_dispatch",
    )(pos1, pos2, h, xs0)


def _expert_ffn_kernel(te_ref, tv_ref, x_ref, g_ref, wg_ref, wu_ref, wd_ref, o_ref,
                       xn_ref, wgb_ref, wub_ref, wdb_ref, *, ts):
    i = pl.program_id(0)
    j = pl.program_id(1)
    valid = tv_ref[i]
    n_sub = x_ref.shape[0] // ts

    @pl.when(valid > 0)
    def _():
        wgb_ref[...] = wg_ref[...].astype(BF16)
        wub_ref[...] = wu_ref[...].astype(BF16)
        wdb_ref[...] = wd_ref[...].astype(BF16)

    n_occ = (valid + ts - 1) // ts
    for k in range(1, n_sub + 1):
        rows = pl.ds(0, k * ts)

        @pl.when(n_occ == k)
        def _():
            @pl.when(j == 0)
            def _():
                xn_ref[rows, :] = (_rms(x_ref[rows, :]) * g_ref[...]).astype(BF16)

            xn = xn_ref[rows, :]
            a = _silu(_dot(xn, wgb_ref[...])) * _dot(xn, wub_ref[...])
            y = _dot(a.astype(BF16), wdb_ref[...])

            @pl.when(j == 0)
            def _():
                o_ref[rows, :] = y
                if k < n_sub:
                    o_ref[pl.ds(k * ts, (n_sub - k) * ts), :] = jnp.zeros(((n_sub - k) * ts, o_ref.shape[1]), F32)

            @pl.when(j > 0)
            def _():
                o_ref[rows, :] += y

    @pl.when((n_occ == 0) & (j == 0))
    def _():
        o_ref[...] = jnp.zeros_like(o_ref)


def expert_ffn(xs, gain, w_gate, w_up, w_down, layer, tile_expert, tile_valid, tm, ts, tf):
    R, D = xs.shape
    Fd = w_gate.shape[3]
    nf = Fd // tf

    def jj(i, j, tv):
        return jnp.where(tv[i] > 0, j, nf - 1)

    grid_spec = pltpu.PrefetchScalarGridSpec(
        num_scalar_prefetch=2,
        grid=(R // tm, nf),
        in_specs=[pl.BlockSpec((tm, D), lambda i, j, te, tv: (i, 0)),
                  pl.BlockSpec((1, D), lambda i, j, te, tv: (0, 0)),
                  pl.BlockSpec((None, None, D, tf), lambda i, j, te, tv: (layer, te[i], 0, jj(i, j, tv))),
                  pl.BlockSpec((None, None, D, tf), lambda i, j, te, tv: (layer, te[i], 0, jj(i, j, tv))),
                  pl.BlockSpec((None, None, tf, D), lambda i, j, te, tv: (layer, te[i], jj(i, j, tv), 0))],
        out_specs=pl.BlockSpec((tm, D), lambda i, j, te, tv: (i, 0)),
        scratch_shapes=[pltpu.VMEM((tm, D), BF16),
                        pltpu.VMEM((D, tf), BF16), pltpu.VMEM((D, tf), BF16), pltpu.VMEM((tf, D), BF16)],
    )
    return pl.pallas_call(
        functools.partial(_expert_ffn_kernel, ts=ts),
        grid_spec=grid_spec,
        out_shape=jax.ShapeDtypeStruct((R, D), F32),
        compiler_params=_cparams(2),
        name="expert_ffn",
    )(tile_expert, tile_valid, xs, gain.reshape(1, D), w_gate, w_up, w_down)


def _combine_kernel(p1_ref, p2_ref, route_ref, h_ref, y_ref, o_ref, b1_ref, b2_ref, sem):
    tb = p1_ref.shape[0]

    def copies(t):
        dst = pl.ds(t, 1)
        return (pltpu.make_async_copy(y_ref.at[pl.ds(p1_ref[t], 1), :], b1_ref.at[dst, :], sem),
                pltpu.make_async_copy(y_ref.at[pl.ds(p2_ref[t], 1), :], b2_ref.at[dst, :], sem))

    def issue(t, c):
        for cp in copies(t):
            cp.start()
        return c

    lax.fori_loop(0, tb, issue, 0, unroll=ROW_DMA_UNROLL)
    for b_ref in (b1_ref, b2_ref):
        pltpu.make_async_copy(y_ref.at[pl.ds(0, tb), :], b_ref, sem).wait()
    w1 = route_ref[:, 2:3]
    w2 = route_ref[:, 3:4]
    o_ref[...] = h_ref[...] + w1 * b1_ref[...] + w2 * b2_ref[...]


def moe_combine(h, y, route, pos1, pos2, tb):
    T, D = h.shape
    smem = lambda: pl.BlockSpec((tb,), lambda i: (i,), memory_space=pltpu.SMEM)
    return pl.pallas_call(
        _combine_kernel,
        grid=(T // tb,),
        in_specs=[smem(), smem(),
                  pl.BlockSpec((tb, LANES), lambda i: (i, 0)),
                  pl.BlockSpec((tb, D), lambda i: (i, 0)),
                  pl.BlockSpec(memory_space=pl.ANY)],
        out_specs=pl.BlockSpec((tb, D), lambda i: (i, 0)),
        out_shape=jax.ShapeDtypeStruct((T, D), F32),
        scratch_shapes=[pltpu.VMEM((tb, D), F32), pltpu.VMEM((tb, D), F32),
                        pltpu.SemaphoreType.DMA(())],
        compiler_params=_cparams(1),
        name="moe_combine",
    )(pos1, pos2, route, h, y)


def moe_ffn(h, gain, w_router, w_gate, w_up, w_down, layer, tm=1024, ts=256, tf=512, tb=512):
    T = h.shape[0]
    route, counts = moe_router(h, gain, w_router, tm=512)
    e1 = route[:, 0].astype(I32)
    e2 = route[:, 1].astype(I32)
    counts = counts[0, :N_EXPERTS].astype(I32)
    tiles_per = (counts + tm - 1) // tm
    tile_end = jnp.cumsum(tiles_per)
    tile_start = tile_end - tiles_per
    row_start = tile_start * tm
    pos1 = row_start[e1] + route[:, 4].astype(I32)
    pos2 = row_start[e2] + route[:, 5].astype(I32)
    n_tiles = (TOP_K * T) // tm + N_EXPERTS
    tile_ids = jnp.arange(n_tiles, dtype=I32)
    tile_expert = jnp.sum(jnp.minimum(tile_ids, tile_end[-1] - 1)[:, None] >= tile_end[None, :], axis=1).astype(I32)
    tile_valid = jnp.clip(counts[tile_expert] - (tile_ids - tile_start[tile_expert]) * tm, 0, tm)
    tile_valid = jnp.where(tile_ids < tile_end[-1], tile_valid, 0).astype(I32)

    xs = moe_dispatch(h, pos1, pos2, n_tiles * tm, tb)
    y = expert_ffn(xs, gain, w_gate, w_up, w_down, layer, tile_expert, tile_valid, tm, ts, tf)
    return moe_combine(h, y, route, pos1, pos2, tb)


def _pad_heads(a, n_heads, width, pad_to, axis):
    shape = list(a.shape)
    a = a.reshape(shape[:axis] + [n_heads, width] + shape[axis + 1:])
    pads = [(0, 0)] * a.ndim
    pads[axis + 1] = (0, pad_to - width)
    a = jnp.pad(a, pads)
    return a.reshape(shape[:axis] + [n_heads * pad_to] + shape[axis + 1:])


def kernel(x, mem, ln_mix, ln_mem, w_mem_kv, mem_q_gain, mem_k_gain, w_out, ln_ffn, swa_w_in, swa_q_gain, swa_k_gain, swa_sink, ret_w_in, mlstm_w_in, mlstm_conv_w, mlstm_conv_b, mlstm_i_bias, mlstm_f_bias, mlstm_out_gain, ffn_w_gate, ffn_w_up, ffn_w_down, moe_router, moe_w_gate, moe_w_up, moe_w_down):
    B, S, D = x.shape
    M = mem.shape[1]
    T = B * S
    depth = ln_mix.shape[0]
    h = x.reshape(T, D)
    mem2 = mem.reshape(B * M, D)

    for layer in range(depth):
        kind = layer % N_MIXERS
        idx = layer // N_MIXERS
        mk, mv = rms_proj(mem2, ln_mem[layer], w_mem_kv[layer].astype(BF16),
                          ((MEM_WIDTH, BF16, mem_k_gain[layer]), (MEM_WIDTH, BF16, None)),
                          tm=B * M, name="mem_kv_proj")
        w_o = w_out[layer].astype(BF16)
        w_o_mix, w_o_mem = w_o[:MIX_WIDTH], w_o[MIX_WIDTH:]
        qm_out = (MEM_WIDTH, BF16, mem_q_gain[layer])

        if kind == 0:
            kvw = SWA_KV_HEADS * HEAD_DIM
            q, k, v, qm = rms_proj(h, ln_mix[layer], swa_w_in[idx].astype(BF16),
                                   ((MIX_WIDTH, BF16, swa_q_gain[idx]), (kvw, BF16, swa_k_gain[idx]),
                                    (kvw, BF16, None), qm_out), tm=512, name="swa_in_proj")
            mix = swa_attention(q, k, v, swa_sink[idx], B)
        elif kind == 1:
            qkw = RET_HEADS * RET_QK_DIM
            q, k, v, g, qm = rms_proj(h, ln_mix[layer], ret_w_in[idx].astype(BF16),
                                      ((qkw, BF16, None), (qkw, F32, None), (MIX_WIDTH, BF16, None),
                                       (MIX_WIDTH, F32, None), qm_out), tm=512, name="ret_in_proj")
            mix = retention(q, k, v, g, B)
        else:
            H, P, VP = MLSTM_HEADS, MLSTM_QK_PAD, MLSTM_V_PAD
            w = mlstm_w_in[idx]
            qkw = 2 * H * MLSTM_QK_DIM
            o_v, o_og, o_ig = qkw, qkw + MIX_WIDTH, qkw + 2 * MIX_WIDTH
            o_fg, o_qm = o_ig + H, o_ig + 2 * H
            w_gates = jnp.zeros((D, LANES), F32).at[:, :2 * H].set(w[:, o_ig:o_qm])
            w_pad = jnp.concatenate([
                _pad_heads(w[:, :qkw], 2 * H, MLSTM_QK_DIM, P, 1),
                _pad_heads(w[:, o_v:o_og], H, MLSTM_V_DIM, VP, 1),
                _pad_heads(w[:, o_og:o_ig], H, MLSTM_V_DIM, VP, 1),
                w_gates, w[:, o_qm:]], axis=1).astype(BF16)
            qk, v, og, gates, qm = rms_proj(h, ln_mix[layer], w_pad,
                                            ((2 * H * P, F32, None), (H * VP, BF16, None), (H * VP, F32, None),
                                             (LANES, F32, None), qm_out), tm=512, name="mlstm_in_proj")
            gates_t = gates[:, :SUBLANES].T
            bias = jnp.concatenate([mlstm_i_bias[idx], mlstm_f_bias[idx]])
            bias_c = jnp.zeros((1, LANES), F32).at[0, :2 * H].set(bias)
            bias_r = jnp.broadcast_to(bias[:, None], (SUBLANES, MLSTM_CHUNK))
            mix = mlstm(qk, v, og, gates, gates_t,
                        _pad_heads(mlstm_conv_w[idx], 2 * H, MLSTM_QK_DIM, P, 1),
                        _pad_heads(mlstm_conv_b[idx][None], 2 * H, MLSTM_QK_DIM, P, 1),
                        bias_c, bias_r,
                        _pad_heads(mlstm_out_gain[idx][None], H, MLSTM_V_DIM, VP, 1), B)
            w_o_mix = _pad_heads(w_o_mix, H, MLSTM_V_DIM, VP, 0)

        mem_out = mem_attention(qm, mk, mv, B, tm=512)
        h = out_proj(h, mix, mem_out, w_o_mix, w_o_mem, tm=512)
        j = layer // 2
        if layer % 2 == 0:
            h = dense_ffn(h, ln_ffn[layer], ffn_w_gate[j].astype(BF16), ffn_w_up[j].astype(BF16),
                          ffn_w_down[j].astype(BF16), tm=1024, tf=896)
        else:
            h = moe_ffn(h, ln_ffn[layer], moe_router[j], moe_w_gate, moe_w_up, moe_w_down, j)
    return h.reshape(B, S, D)
```

```python
import functools
import math

import jax
import jax.numpy as jnp
from jax import lax
from jax.experimental import pallas as pl
from jax.experimental.pallas import tpu as pltpu

F32 = jnp.float32
BF16 = jnp.bfloat16
I32 = jnp.int32

D_MODEL = 1024
N_MIXERS = 3
HEAD_DIM = 64
MEM_HEADS = 4
MEM_HEAD_DIM = 64
MEM_WIDTH = MEM_HEADS * MEM_HEAD_DIM
MIX_WIDTH = D_MODEL - MEM_WIDTH
RMS_EPS = 1e-6
NEG_INF = -1e30

SWA_Q_HEADS = MIX_WIDTH // HEAD_DIM
SWA_KV_HEADS = 4
SWA_GROUP = SWA_Q_HEADS // SWA_KV_HEADS
SWA_WINDOW = 128
SWA_BLOCK = 128

RET_HEADS = 6
RET_QK_DIM = 64
RET_V_DIM = MIX_WIDTH // RET_HEADS
RET_CHUNK = 128

MLSTM_HEADS = 4
MLSTM_V_DIM = MIX_WIDTH // MLSTM_HEADS
MLSTM_QK_DIM = MLSTM_V_DIM // 2
MLSTM_CHUNK = 128
MLSTM_CONV = 4
MLSTM_QK_PAD = 128
MLSTM_V_PAD = 256

N_EXPERTS = 8
TOP_K = 2

LANES = 128
SUBLANES = 8
VMEM_LIMIT = 56 * 1024 * 1024
ROW_DMA_UNROLL = 8


def _cparams(n_axes):
    return pltpu.CompilerParams(dimension_semantics=("arbitrary",) * n_axes,
                                vmem_limit_bytes=VMEM_LIMIT)


def _rms(x, eps=RMS_EPS):
    return x * lax.rsqrt(jnp.mean(x * x, axis=-1, keepdims=True) + eps)


def _dot(a, b):
    return jnp.dot(a, b, preferred_element_type=F32)


def _dot_nt(a, b):
    return lax.dot_general(a, b, (((1,), (1,)), ((), ())), preferred_element_type=F32)


def _dot_tn(a, b):
    return lax.dot_general(a, b, (((0,), (0,)), ((), ())), preferred_element_type=F32)


def _silu(x):
    return x * jax.nn.sigmoid(x)


NORM_CHUNK = 256


def _head_rms(y, seg_ref, hg):
    sq = y * y
    hi = sq.astype(BF16)
    lo = (sq - hi.astype(F32)).astype(BF16)
    ms = _dot(hi, seg_ref[...]) + _dot(lo, seg_ref[...])
    return y * lax.rsqrt(ms + RMS_EPS) * hg


def _rms_proj_kernel(x_ref, g_ref, w_ref, seg_ref, *refs, widths, normed):
    n_gain = sum(normed)
    gain_refs, o_refs = refs[:n_gain], refs[n_gain:]
    xn = (_rms(x_ref[...]) * g_ref[...]).astype(BF16)
    off = 0
    gi = 0
    for o_ref, wd, nrm in zip(o_refs, widths, normed):
        if nrm:
            hg = gain_refs[gi][...]
            gi += 1
            for c in range(0, wd, NORM_CHUNK):
                y = _dot(xn, w_ref[:, off + c:off + c + NORM_CHUNK])
                o_ref[:, c:c + NORM_CHUNK] = _head_rms(y, seg_ref, hg).astype(o_ref.dtype)
        else:
            o_ref[...] = _dot(xn, w_ref[:, off:off + wd]).astype(o_ref.dtype)
        off += wd


def rms_proj(x, gain, w, outs, tm, name):
    T, D = x.shape
    N = w.shape[1]
    widths = tuple(o[0] for o in outs)
    normed = tuple(o[2] is not None for o in outs)
    head_gains = [jnp.tile(o[2], NORM_CHUNK // HEAD_DIM).reshape(1, NORM_CHUNK) for o in outs if o[2] is not None]
    assert sum(widths) == N and T % tm == 0
    assert all(wd % NORM_CHUNK == 0 for wd, nrm in zip(widths, normed) if nrm)
    head_of = jnp.arange(NORM_CHUNK) // HEAD_DIM
    seg = jnp.where(head_of[:, None] == head_of[None, :], 1.0 / HEAD_DIM, 0.0).astype(BF16)
    return pl.pallas_call(
        functools.partial(_rms_proj_kernel, widths=widths, normed=normed),
        grid=(T // tm,),
        in_specs=[pl.BlockSpec((tm, D), lambda i: (i, 0)),
                  pl.BlockSpec((1, D), lambda i: (0, 0)),
                  pl.BlockSpec((D, N), lambda i: (0, 0)),
                  pl.BlockSpec((NORM_CHUNK, NORM_CHUNK), lambda i: (0, 0))]
                 + [pl.BlockSpec((1, NORM_CHUNK), lambda i: (0, 0))] * len(head_gains),
        out_specs=[pl.BlockSpec((tm, wd), lambda i: (i, 0)) for wd in widths],
        out_shape=[jax.ShapeDtypeStruct((T, o[0]), o[1]) for o in outs],
        compiler_params=_cparams(1),
        name=name,
    )(x, gain.reshape(1, D), w, seg, *head_gains)


def _swa_kernel(sink_ref, q_ref, kc_ref, kp_ref, vc_ref, vp_ref, bias_ref, o_ref):
    n = pl.program_id(1)
    L = SWA_BLOCK
    row = lax.broadcasted_iota(I32, (L, 2 * L), 0)
    col = lax.broadcasted_iota(I32, (L, 2 * L), 1)
    dist = row + L - col
    mask = (dist >= 0) & (dist < SWA_WINDOW) & ((col >= L) | (n > 0))
    outs = []
    for g in range(SWA_KV_HEADS):
        ksl = slice(g * HEAD_DIM, (g + 1) * HEAD_DIM)
        k = jnp.concatenate([kp_ref[:, ksl], kc_ref[:, ksl]], axis=0)
        v = jnp.concatenate([vp_ref[:, ksl], vc_ref[:, ksl]], axis=0)
        for h in range(g * SWA_GROUP, (g + 1) * SWA_GROUP):
            sink = sink_ref[h]
            s = _dot_nt(q_ref[:, h * HEAD_DIM:(h + 1) * HEAD_DIM], k) * (HEAD_DIM ** -0.5) - bias_ref[h]
            s = jnp.where(mask, s, NEG_INF)
            m = jnp.maximum(jnp.max(s, axis=-1, keepdims=True), sink)
            e = jnp.exp(s - m)
            denom = jnp.sum(e, axis=-1, keepdims=True) + jnp.exp(sink - m)
            outs.append(_dot(e.astype(BF16), v) / denom)
    o_ref[...] = jnp.concatenate(outs, axis=-1).astype(o_ref.dtype)


def swa_attention(q, k, v, sink, batch):
    T = q.shape[0]
    L = SWA_BLOCK
    nb = T // batch // L
    kvw = SWA_KV_HEADS * HEAD_DIM
    slopes = jnp.exp2(-8.0 * jnp.arange(1, SWA_Q_HEADS + 1, dtype=F32) / SWA_Q_HEADS)
    adist = jnp.abs(jnp.arange(L)[:, None] + L - jnp.arange(2 * L)[None, :]).astype(F32)
    bias = slopes[:, None, None] * adist
    cur = lambda b, n: (b * nb + n, 0)
    prev = lambda b, n: (b * nb + jnp.maximum(n - 1, 0), 0)
    return pl.pallas_call(
        _swa_kernel,
        grid=(batch, nb),
        in_specs=[pl.BlockSpec(memory_space=pltpu.SMEM),
                  pl.BlockSpec((L, MIX_WIDTH), cur),
                  pl.BlockSpec((L, kvw), cur), pl.BlockSpec((L, kvw), prev),
                  pl.BlockSpec((L, kvw), cur), pl.BlockSpec((L, kvw), prev),
                  pl.BlockSpec((SWA_Q_HEADS, L, 2 * L), lambda b, n: (0, 0, 0))],
        out_specs=pl.BlockSpec((L, MIX_WIDTH), cur),
        out_shape=jax.ShapeDtypeStruct((T, MIX_WIDTH), BF16),
        compiler_params=_cparams(2),
        name="swa_attention",
    )(sink, q, k, k, v, v, bias)


def _retention_kernel(q_ref, k_ref, v_ref, g_ref, dec_ref, qd_ref, kd_ref, cd_ref, o_ref, state_ref):
    @pl.when(pl.program_id(1) == 0)
    def _():
        state_ref[...] = jnp.zeros_like(state_ref)

    outs = []
    for h in range(RET_HEADS):
        qsl = slice(h * RET_QK_DIM, (h + 1) * RET_QK_DIM)
        vsl = slice(h * RET_V_DIM, (h + 1) * RET_V_DIM)
        q = q_ref[:, qsl].astype(BF16)
        k = k_ref[:, qsl] * (RET_QK_DIM ** -0.5)
        v = v_ref[:, vsl]
        state = state_ref[h]
        s = _dot_nt(q, k.astype(BF16)) * dec_ref[h]
        o = _dot(s.astype(BF16), v) + _dot(q, state.astype(BF16)) * qd_ref[h]
        state_ref[h] = state * cd_ref[h] + _dot_tn((k * kd_ref[h]).astype(BF16), v)
        outs.append(_silu(g_ref[:, vsl]) * _rms(o))
    o_ref[...] = jnp.concatenate(outs, axis=-1).astype(o_ref.dtype)


def _retention_consts():
    H, L = RET_HEADS, RET_CHUNK
    log_gamma = jnp.log1p(-jnp.exp2(-5.0 - jnp.arange(H, dtype=F32)))
    pos = jnp.arange(L, dtype=F32)
    rel = pos[:, None] - pos[None, :]
    intra = jnp.exp(jnp.where(rel >= 0, log_gamma[:, None, None] * rel, -jnp.inf))
    q_decay = jnp.exp(log_gamma[:, None] * (pos + 1.0))[..., None]
    k_decay = jnp.exp(log_gamma[:, None] * (L - 1.0 - pos))[..., None]
    chunk_decay = jnp.exp(log_gamma * L)[:, None, None]
    return (intra,
            jnp.broadcast_to(q_decay, (H, L, RET_V_DIM)),
            jnp.broadcast_to(k_decay, (H, L, RET_QK_DIM)),
            jnp.broadcast_to(chunk_decay, (H, RET_QK_DIM, RET_V_DIM)))


def retention(q, k, v, g, batch):
    T = q.shape[0]
    L = RET_CHUNK
    nc = T // batch // L
    H = RET_HEADS
    intra, qd, kd, cd = _retention_consts()
    cur = lambda b, c: (b * nc + c, 0)
    const = lambda b, c: (0, 0, 0)
    return pl.pallas_call(
        _retention_kernel,
        grid=(batch, nc),
        in_specs=[pl.BlockSpec((L, H * RET_QK_DIM), cur), pl.BlockSpec((L, H * RET_QK_DIM), cur),
                  pl.BlockSpec((L, MIX_WIDTH), cur), pl.BlockSpec((L, MIX_WIDTH), cur),
                  pl.BlockSpec((H, L, L), const), pl.BlockSpec((H, L, RET_V_DIM), const),
                  pl.BlockSpec((H, L, RET_QK_DIM), const), pl.BlockSpec((H, RET_QK_DIM, RET_V_DIM), const)],
        out_specs=pl.BlockSpec((L, MIX_WIDTH), cur),
        out_shape=jax.ShapeDtypeStruct((T, MIX_WIDTH), BF16),
        scratch_shapes=[pltpu.VMEM((H, RET_QK_DIM, RET_V_DIM), F32)],
        compiler_params=_cparams(2),
        name="retention",
    )(q, k, v, g, intra, qd, kd, cd)


def _split3(x):
    x1 = x.astype(BF16)
    r1 = x - x1.astype(F32)
    x2 = r1.astype(BF16)
    x3 = (r1 - x2.astype(F32)).astype(BF16)
    return x1, x2, x3


def _mlstm_kernel(qk_ref, v_ref, og_ref, gc_ref, gr_ref, cw_ref, cb_ref, bc_ref, br_ref, gain_ref,
                  tri_ref, trit_ref, o_ref, xbuf_ref, c_ref, n_ref, m_ref):
    L = MLSTM_CHUNK
    H = MLSTM_HEADS
    P = MLSTM_QK_PAD
    VP = MLSTM_V_PAD
    KT = SUBLANES

    @pl.when(pl.program_id(1) == 0)
    def _():
        xbuf_ref[0:KT, :] = jnp.zeros((KT, xbuf_ref.shape[1]), F32)
        c_ref[...] = jnp.zeros_like(c_ref)
        n_ref[...] = jnp.zeros_like(n_ref)
        m_ref[...] = jnp.zeros_like(m_ref)

    xbuf_ref[KT:KT + L, :] = qk_ref[...]
    acc = cb_ref[...] + cw_ref[MLSTM_CONV - 1:MLSTM_CONV, :] * xbuf_ref[KT:KT + L, :]
    for j in range(MLSTM_CONV - 1):
        sh = MLSTM_CONV - 1 - j
        acc = acc + cw_ref[j:j + 1, :] * xbuf_ref[KT - sh:KT - sh + L, :]
    xbuf_ref[0:KT, :] = qk_ref[L - KT:L, :]
    qk = _silu(acc)

    xc = gc_ref[...] + bc_ref[...]
    xr = gr_ref[...] + br_ref[...]
    lfc = jax.nn.log_sigmoid(xc)
    lfr = jax.nn.log_sigmoid(xr)
    tri = tri_ref[...]
    trit = trit_ref[...]
    bc = sum(_dot(tri, t) for t in _split3(lfc))
    br = sum(_dot(t, trit) for t in _split3(lfr))

    rowi = lax.broadcasted_iota(I32, (L, L), 0)
    coli = lax.broadcasted_iota(I32, (L, L), 1)
    causal = rowi >= coli
    outs = []
    for h in range(H):
        q = qk[:, h * P:(h + 1) * P].astype(BF16)
        k = qk[:, (H + h) * P:(H + h + 1) * P] * (MLSTM_QK_DIM ** -0.5)
        v = v_ref[:, h * VP:(h + 1) * VP]
        li_c = xc[:, h:h + 1]
        b_c = bc[:, H + h:H + h + 1]
        li_r = xr[h:h + 1, :]
        b_r = br[H + h:H + h + 1, :]
        g = b_c[L - 1:L, :]
        m = m_ref[h:h + 1, 0:1]
        C = c_ref[h]
        nvec = n_ref[h:h + 1, :]

        dmat = jnp.where(causal, b_c - b_r + li_r, -jnp.inf)
        inter = b_c + m
        m_t = jnp.maximum(inter, jnp.max(dmat, axis=-1, keepdims=True))
        w = jnp.exp(dmat - m_t)
        a = jnp.exp(inter - m_t)
        s = _dot_nt(q, k.astype(BF16)) * w
        num = _dot(s.astype(BF16), v) + a * _dot(q, C.astype(BF16))
        qf = qk[:, h * P:(h + 1) * P]
        den = jnp.sum(s, axis=-1, keepdims=True) + a * jnp.sum(qf * nvec, axis=-1, keepdims=True)
        hh = num / jnp.maximum(jnp.abs(den), jnp.exp(-m_t))

        u_c = g - b_c + li_c
        u_r = g - b_r + li_r
        m_new = jnp.maximum(g + m, jnp.max(u_r, axis=-1, keepdims=True))
        wk = jnp.exp(u_c - m_new)
        decay = jnp.exp(g + m - m_new)
        kw = k * wk
        c_ref[h] = decay * C + _dot_tn(kw.astype(BF16), v)
        n_ref[h:h + 1, :] = decay * nvec + jnp.sum(kw, axis=0, keepdims=True)
        m_ref[h:h + 1, :] = jnp.broadcast_to(m_new, (1, m_ref.shape[1]))

        ms = jnp.sum(hh * hh, axis=-1, keepdims=True) * (1.0 / MLSTM_V_DIM)
        hc = hh * lax.rsqrt(ms + RMS_EPS) * gain_ref[:, h * VP:(h + 1) * VP]
        outs.append(jax.nn.sigmoid(og_ref[:, h * VP:(h + 1) * VP]) * hc)
    o_ref[...] = jnp.concatenate(outs, axis=-1).astype(o_ref.dtype)


def mlstm(qk, v, og, gates, gates_t, conv_w, conv_b, bias_c, bias_r, gain, batch):
    T = qk.shape[0]
    L = MLSTM_CHUNK
    nc = T // batch // L
    H, P, VP = MLSTM_HEADS, MLSTM_QK_PAD, MLSTM_V_PAD
    tri = jnp.tril(jnp.ones((L, L), BF16))
    cur = lambda b, c: (b * nc + c, 0)
    c2 = lambda b, c: (0, 0)
    return pl.pallas_call(
        _mlstm_kernel,
        grid=(batch, nc),
        in_specs=[pl.BlockSpec((L, 2 * H * P), cur), pl.BlockSpec((L, H * VP), cur),
                  pl.BlockSpec((L, H * VP), cur), pl.BlockSpec((L, LANES), cur),
                  pl.BlockSpec((SUBLANES, L), lambda b, c: (0, b * nc + c)),
                  pl.BlockSpec((MLSTM_CONV, 2 * H * P), c2), pl.BlockSpec((1, 2 * H * P), c2),
                  pl.BlockSpec((1, LANES), c2), pl.BlockSpec((SUBLANES, L), c2),
                  pl.BlockSpec((1, H * VP), c2),
                  pl.BlockSpec((L, L), c2), pl.BlockSpec((L, L), c2)],
        out_specs=pl.BlockSpec((L, H * VP), cur),
        out_shape=jax.ShapeDtypeStruct((T, H * VP), BF16),
        scratch_shapes=[pltpu.VMEM((SUBLANES + L, 2 * H * P), F32),
                        pltpu.VMEM((H, P, VP), F32),
                        pltpu.VMEM((SUBLANES, P), F32),
                        pltpu.VMEM((SUBLANES, LANES), F32)],
        compiler_params=_cparams(2),
        name="mlstm",
    )(qk, v, og, gates, gates_t, conv_w, conv_b, bias_c, bias_r, gain, tri, tri.T)


def _out_proj_kernel(h_ref, mix_ref, q_ref, k_ref, v_ref, w1_ref, w2_ref, o_ref):
    outs = []
    for hd in range(MEM_HEADS):
        sl = slice(hd * MEM_HEAD_DIM, (hd + 1) * MEM_HEAD_DIM)
        s = _dot_nt(q_ref[:, sl], k_ref[:, sl]) * (MEM_HEAD_DIM ** -0.5)
        e = jnp.exp(s - jnp.max(s, axis=-1, keepdims=True))
        outs.append(_dot(e.astype(BF16), v_ref[:, sl]) / jnp.sum(e, axis=-1, keepdims=True))
    mem_out = jnp.concatenate(outs, axis=-1).astype(BF16)
    o_ref[...] = h_ref[...] + _dot(mix_ref[...], w1_ref[...]) + _dot(mem_out, w2_ref[...])


def out_proj(h, mix, qm, mk, mv, w_mix, w_mem, batch, tm):
    T, D = h.shape
    Wm = mix.shape[1]
    M = mk.shape[0] // batch
    nt = T // batch // tm
    rows = lambda b, i: (b * nt + i, 0)
    mem = lambda b, i: (b, 0)
    const = lambda b, i: (0, 0)
    return pl.pallas_call(
        _out_proj_kernel,
        grid=(batch, nt),
        in_specs=[pl.BlockSpec((tm, D), rows),
                  pl.BlockSpec((tm, Wm), rows),
                  pl.BlockSpec((tm, MEM_WIDTH), rows),
                  pl.BlockSpec((M, MEM_WIDTH), mem),
                  pl.BlockSpec((M, MEM_WIDTH), mem),
                  pl.BlockSpec((Wm, D), const),
                  pl.BlockSpec((MEM_WIDTH, D), const)],
        out_specs=pl.BlockSpec((tm, D), rows),
        out_shape=jax.ShapeDtypeStruct((T, D), F32),
        compiler_params=_cparams(2),
        name="out_proj",
    )(h, mix, qm, mk, mv, w_mix, w_mem)


def _dense_ffn_kernel(h_ref, g_ref, wg_ref, wu_ref, wd_ref, o_ref, xn_ref, acc_ref):
    j = pl.program_id(1)

    @pl.when(j == 0)
    def _():
        h = h_ref[...]
        xn_ref[...] = (_rms(h) * g_ref[...]).astype(BF16)
        acc_ref[...] = h

    xn = xn_ref[...]
    a = _silu(_dot(xn, wg_ref[...])) * _dot(xn, wu_ref[...])
    acc_ref[...] += _dot(a.astype(BF16), wd_ref[...])

    @pl.when(j == pl.num_programs(1) - 1)
    def _():
        o_ref[...] = acc_ref[...]


def dense_ffn(h, gain, w_gate, w_up, w_down, tm, tf):
    T, D = h.shape
    Fd = w_gate.shape[1]
    return pl.pallas_call(
        _dense_ffn_kernel,
        grid=(T // tm, Fd // tf),
        in_specs=[pl.BlockSpec((tm, D), lambda i, j: (i, 0)),
                  pl.BlockSpec((1, D), lambda i, j: (0, 0)),
                  pl.BlockSpec((D, tf), lambda i, j: (0, j)),
                  pl.BlockSpec((D, tf), lambda i, j: (0, j)),
                  pl.BlockSpec((tf, D), lambda i, j: (j, 0))],
        out_specs=pl.BlockSpec((tm, D), lambda i, j: (i, 0)),
        out_shape=jax.ShapeDtypeStruct((T, D), F32),
        scratch_shapes=[pltpu.VMEM((tm, D), BF16), pltpu.VMEM((tm, D), F32)],
        compiler_params=_cparams(2),
        name="dense_ffn",
    )(h, gain.reshape(1, D), w_gate, w_up, w_down)


def _router_kernel(h_ref, g_ref, wr_ref, tri_ref, o_ref, cnt_ref, carry_ref):
    i = pl.program_id(0)

    @pl.when(i == 0)
    def _():
        carry_ref[...] = jnp.zeros_like(carry_ref)

    tm = h_ref.shape[0]
    xn = _rms(h_ref[...]) * g_ref[...]
    x_hi = xn.astype(BF16)
    x_lo = (xn - x_hi.astype(F32)).astype(BF16)
    wr = wr_ref[...]
    w_hi = wr.astype(BF16)
    w_lo = (wr - w_hi.astype(F32)).astype(BF16)
    logits = _dot(x_hi, w_hi) + (_dot(x_hi, w_lo) + _dot(x_lo, w_hi))
    lane = lax.broadcasted_iota(I32, (tm, LANES), 1)
    logits = jnp.where(lane < N_EXPERTS, logits, -jnp.inf)
    t1 = jnp.max(logits, axis=-1, keepdims=True)
    e1 = jnp.min(jnp.where(logits == t1, lane, LANES), axis=-1, keepdims=True)
    rest = jnp.where(lane == e1, -jnp.inf, logits)
    t2 = jnp.max(rest, axis=-1, keepdims=True)
    e2 = jnp.min(jnp.where(rest == t2, lane, LANES), axis=-1, keepdims=True)
    x2 = jnp.exp(t2 - t1)
    w1 = 1.0 / (1.0 + x2)
    w2 = x2 / (1.0 + x2)

    oh1 = lane == e1
    oh2 = lane == e2
    cnt = jnp.where(oh1 | oh2, 1.0, 0.0)
    before = _dot(tri_ref[...], cnt.astype(BF16)) + carry_ref[...]
    r1 = jnp.sum(jnp.where(oh1, before, 0.0), axis=-1, keepdims=True)
    r2 = jnp.sum(jnp.where(oh2, before, 0.0), axis=-1, keepdims=True)
    carry_ref[...] += jnp.sum(cnt, axis=0, keepdims=True)

    cols = (e1.astype(F32), e2.astype(F32), w1, w2, r1, r2)
    out = jnp.zeros((tm, LANES), F32)
    for c, val in enumerate(cols):
        out = jnp.where(lane == c, val, out)
    o_ref[...] = out
    cnt_ref[...] = jnp.broadcast_to(carry_ref[...], cnt_ref.shape)


def moe_router(h, gain, w_router, tm):
    T, D = h.shape
    wr = jnp.zeros((D, LANES), F32).at[:, :N_EXPERTS].set(w_router)
    tri = jnp.tril(jnp.ones((tm, tm), BF16), k=-1)
    return pl.pallas_call(
        _router_kernel,
        grid=(T // tm,),
        in_specs=[pl.BlockSpec((tm, D), lambda i: (i, 0)),
                  pl.BlockSpec((1, D), lambda i: (0, 0)),
                  pl.BlockSpec((D, LANES), lambda i: (0, 0)),
                  pl.BlockSpec((tm, tm), lambda i: (0, 0))],
        out_specs=[pl.BlockSpec((tm, LANES), lambda i: (i, 0)),
                   pl.BlockSpec((SUBLANES, LANES), lambda i: (0, 0))],
        out_shape=[jax.ShapeDtypeStruct((T, LANES), F32),
                   jax.ShapeDtypeStruct((SUBLANES, LANES), F32)],
        scratch_shapes=[pltpu.VMEM((1, LANES), F32)],
        compiler_params=_cparams(1),
        name="moe_router",
    )(h, gain.reshape(1, D), wr, tri)


def _dispatch_kernel(p1_ref, p2_ref, h_ref, xs_in_ref, xs_ref, sem):
    del xs_in_ref
    tb = p1_ref.shape[0]

    def copy(t, pos):
        return pltpu.make_async_copy(h_ref.at[pl.ds(t, 1), :], xs_ref.at[pl.ds(pos, 1), :], sem)

    def issue(t, c):
        copy(t, p1_ref[t]).start()
        copy(t, p2_ref[t]).start()
        return c

    lax.fori_loop(0, tb, issue, 0, unroll=ROW_DMA_UNROLL)
    for _ in range(TOP_K):
        pltpu.make_async_copy(h_ref, xs_ref.at[pl.ds(0, tb), :], sem).wait()


def moe_dispatch(h, pos1, pos2, n_rows, tb):
    T, D = h.shape
    xs0 = jnp.zeros((n_rows, D), h.dtype)
    return pl.pallas_call(
        _dispatch_kernel,
        grid=(T // tb,),
        in_specs=[pl.BlockSpec((tb,), lambda i: (i,), memory_space=pltpu.SMEM),
                  pl.BlockSpec((tb,), lambda i: (i,), memory_space=pltpu.SMEM),
                  pl.BlockSpec((tb, D), lambda i: (i, 0)),
                  pl.BlockSpec(memory_space=pl.ANY)],
        out_specs=pl.BlockSpec(memory_space=pl.ANY),
        out_shape=jax.ShapeDtypeStruct(xs0.shape, xs0.dtype),
        scratch_shapes=[pltpu.SemaphoreType.DMA(())],
        input_output_aliases={3: 0},
        compiler_params=_cparams(1),
        name="moe_dispatch",
    )(pos1, pos2, h, xs0)


def _expert_ffn_kernel(te_ref, tv_ref, x_ref, g_ref, wg_ref, wu_ref, wd_ref, o_ref, xn_ref, *, ts):
    i = pl.program_id(0)
    j = pl.program_id(1)
    valid = tv_ref[i]
    n_sub = x_ref.shape[0] // ts

    n_occ = (valid + ts - 1) // ts
    for k in range(1, n_sub + 1):
        rows = pl.ds(0, k * ts)

        @pl.when(n_occ == k)
        def _():
            @pl.when(j == 0)
            def _():
                xn_ref[rows, :] = (_rms(x_ref[rows, :]) * g_ref[...]).astype(BF16)

            xn = xn_ref[rows, :]
            a = _silu(_dot(xn, wg_ref[...].astype(BF16))) * _dot(xn, wu_ref[...].astype(BF16))
            y = _dot(a.astype(BF16), wd_ref[...].astype(BF16))

            @pl.when(j == 0)
            def _():
                o_ref[rows, :] = y
                if k < n_sub:
                    o_ref[pl.ds(k * ts, (n_sub - k) * ts), :] = jnp.zeros(((n_sub - k) * ts, o_ref.shape[1]), F32)

            @pl.when(j > 0)
            def _():
                o_ref[rows, :] += y

    @pl.when((n_occ == 0) & (j == 0))
    def _():
        o_ref[...] = jnp.zeros_like(o_ref)


def expert_ffn(xs, gain, w_gate, w_up, w_down, layer, tile_expert, tile_valid, tm, ts, tf):
    R, D = xs.shape
    Fd = w_gate.shape[3]
    nf = Fd // tf

    def jj(i, j, tv):
        return jnp.where(tv[i] > 0, j, nf - 1)

    grid_spec = pltpu.PrefetchScalarGridSpec(
        num_scalar_prefetch=2,
        grid=(R // tm, nf),
        in_specs=[pl.BlockSpec((tm, D), lambda i, j, te, tv: (i, 0)),
                  pl.BlockSpec((1, D), lambda i, j, te, tv: (0, 0)),
                  pl.BlockSpec((None, None, D, tf), lambda i, j, te, tv: (layer, te[i], 0, jj(i, j, tv))),
                  pl.BlockSpec((None, None, D, tf), lambda i, j, te, tv: (layer, te[i], 0, jj(i, j, tv))),
                  pl.BlockSpec((None, None, tf, D), lambda i, j, te, tv: (layer, te[i], jj(i, j, tv), 0))],
        out_specs=pl.BlockSpec((tm, D), lambda i, j, te, tv: (i, 0)),
        scratch_shapes=[pltpu.VMEM((tm, D), BF16)],
    )
    return pl.pallas_call(
        functools.partial(_expert_ffn_kernel, ts=ts),
        grid_spec=grid_spec,
        out_shape=jax.ShapeDtypeStruct((R, D), F32),
        compiler_params=_cparams(2),
        name="expert_ffn",
    )(tile_expert, tile_valid, xs, gain.reshape(1, D), w_gate, w_up, w_down)


def _combine_kernel(p1_ref, p2_ref, route_ref, h_ref, y_ref, o_ref, b1_ref, b2_ref, sem):
    tb = p1_ref.shape[0]

    def copies(t):
        dst = pl.ds(t, 1)
        return (pltpu.make_async_copy(y_ref.at[pl.ds(p1_ref[t], 1), :], b1_ref.at[dst, :], sem),
                pltpu.make_async_copy(y_ref.at[pl.ds(p2_ref[t], 1), :], b2_ref.at[dst, :], sem))

    def issue(t, c):
        for cp in copies(t):
            cp.start()
        return c

    lax.fori_loop(0, tb, issue, 0, unroll=ROW_DMA_UNROLL)
    for b_ref in (b1_ref, b2_ref):
        pltpu.make_async_copy(y_ref.at[pl.ds(0, tb), :], b_ref, sem).wait()
    w1 = route_ref[:, 2:3]
    w2 = route_ref[:, 3:4]
    o_ref[...] = h_ref[...] + w1 * b1_ref[...] + w2 * b2_ref[...]


def moe_combine(h, y, route, pos1, pos2, tb):
    T, D = h.shape
    smem = lambda: pl.BlockSpec((tb,), lambda i: (i,), memory_space=pltpu.SMEM)
    return pl.pallas_call(
        _combine_kernel,
        grid=(T // tb,),
        in_specs=[smem(), smem(),
                  pl.BlockSpec((tb, LANES), lambda i: (i, 0)),
                  pl.BlockSpec((tb, D), lambda i: (i, 0)),
                  pl.BlockSpec(memory_space=pl.ANY)],
        out_specs=pl.BlockSpec((tb, D), lambda i: (i, 0)),
        out_shape=jax.ShapeDtypeStruct((T, D), F32),
        scratch_shapes=[pltpu.VMEM((tb, D), F32), pltpu.VMEM((tb, D), F32),
                        pltpu.SemaphoreType.DMA(())],
        compiler_params=_cparams(1),
        name="moe_combine",
    )(pos1, pos2, route, h, y)


def moe_ffn(h, gain, w_router, w_gate, w_up, w_down, layer, tm=1024, ts=256, tf=512, tb=512):
    T = h.shape[0]
    route, counts = moe_router(h, gain, w_router, tm=512)
    e1 = route[:, 0].astype(I32)
    e2 = route[:, 1].astype(I32)
    counts = counts[0, :N_EXPERTS].astype(I32)
    tiles_per = (counts + tm - 1) // tm
    tile_end = jnp.cumsum(tiles_per)
    tile_start = tile_end - tiles_per
    row_start = tile_start * tm
    pos1 = row_start[e1] + route[:, 4].astype(I32)
    pos2 = row_start[e2] + route[:, 5].astype(I32)
    n_tiles = (TOP_K * T) // tm + N_EXPERTS
    tile_ids = jnp.arange(n_tiles, dtype=I32)
    tile_expert = jnp.sum(jnp.minimum(tile_ids, tile_end[-1] - 1)[:, None] >= tile_end[None, :], axis=1).astype(I32)
    tile_valid = jnp.clip(counts[tile_expert] - (tile_ids - tile_start[tile_expert]) * tm, 0, tm)
    tile_valid = jnp.where(tile_ids < tile_end[-1], tile_valid, 0).astype(I32)

    xs = moe_dispatch(h, pos1, pos2, n_tiles * tm, tb)
    y = expert_ffn(xs, gain, w_gate, w_up, w_down, layer, tile_expert, tile_valid, tm, ts, tf)
    return moe_combine(h, y, route, pos1, pos2, tb)


def _pad_heads(a, n_heads, width, pad_to, axis):
    shape = list(a.shape)
    a = a.reshape(shape[:axis] + [n_heads, width] + shape[axis + 1:])
    pads = [(0, 0)] * a.ndim
    pads[axis + 1] = (0, pad_to - width)
    a = jnp.pad(a, pads)
    return a.reshape(shape[:axis] + [n_heads * pad_to] + shape[axis + 1:])


def kernel(x, mem, ln_mix, ln_mem, w_mem_kv, mem_q_gain, mem_k_gain, w_out, ln_ffn, swa_w_in, swa_q_gain, swa_k_gain, swa_sink, ret_w_in, mlstm_w_in, mlstm_conv_w, mlstm_conv_b, mlstm_i_bias, mlstm_f_bias, mlstm_out_gain, ffn_w_gate, ffn_w_up, ffn_w_down, moe_router, moe_w_gate, moe_w_up, moe_w_down):
    B, S, D = x.shape
    M = mem.shape[1]
    T = B * S
    depth = ln_mix.shape[0]
    h = x.reshape(T, D)
    mem2 = mem.reshape(B * M, D)

    for layer in range(depth):
        kind = layer % N_MIXERS
        idx = layer // N_MIXERS
        mk, mv = rms_proj(mem2, ln_mem[layer], w_mem_kv[layer].astype(BF16),
                          ((MEM_WIDTH, BF16, mem_k_gain[layer]), (MEM_WIDTH, BF16, None)),
                          tm=B * M, name="mem_kv_proj")
        w_o = w_out[layer].astype(BF16)
        w_o_mix, w_o_mem = w_o[:MIX_WIDTH], w_o[MIX_WIDTH:]
        qm_out = (MEM_WIDTH, BF16, mem_q_gain[layer])

        if kind == 0:
            kvw = SWA_KV_HEADS * HEAD_DIM
            q, k, v, qm = rms_proj(h, ln_mix[layer], swa_w_in[idx].astype(BF16),
                                   ((MIX_WIDTH, BF16, swa_q_gain[idx]), (kvw, BF16, swa_k_gain[idx]),
                                    (kvw, BF16, None), qm_out), tm=512, name="swa_in_proj")
            mix = swa_attention(q, k, v, swa_sink[idx], B)
        elif kind == 1:
            qkw = RET_HEADS * RET_QK_DIM
            q, k, v, g, qm = rms_proj(h, ln_mix[layer], ret_w_in[idx].astype(BF16),
                                      ((qkw, BF16, None), (qkw, F32, None), (MIX_WIDTH, BF16, None),
                                       (MIX_WIDTH, F32, None), qm_out), tm=512, name="ret_in_proj")
            mix = retention(q, k, v, g, B)
        else:
            H, P, VP = MLSTM_HEADS, MLSTM_QK_PAD, MLSTM_V_PAD
            w = mlstm_w_in[idx]
            qkw = 2 * H * MLSTM_QK_DIM
            o_v, o_og, o_ig = qkw, qkw + MIX_WIDTH, qkw + 2 * MIX_WIDTH
            o_fg, o_qm = o_ig + H, o_ig + 2 * H
            w_gates = jnp.zeros((D, LANES), F32).at[:, :2 * H].set(w[:, o_ig:o_qm])
            w_pad = jnp.concatenate([
                _pad_heads(w[:, :qkw], 2 * H, MLSTM_QK_DIM, P, 1),
                _pad_heads(w[:, o_v:o_og], H, MLSTM_V_DIM, VP, 1),
                _pad_heads(w[:, o_og:o_ig], H, MLSTM_V_DIM, VP, 1),
                w_gates, w[:, o_qm:]], axis=1).astype(BF16)
            qk, v, og, gates, qm = rms_proj(h, ln_mix[layer], w_pad,
                                            ((2 * H * P, F32, None), (H * VP, BF16, None), (H * VP, F32, None),
                                             (LANES, F32, None), qm_out), tm=512, name="mlstm_in_proj")
            gates_t = gates[:, :SUBLANES].T
            bias = jnp.concatenate([mlstm_i_bias[idx], mlstm_f_bias[idx]])
            bias_c = jnp.zeros((1, LANES), F32).at[0, :2 * H].set(bias)
            bias_r = jnp.broadcast_to(bias[:, None], (SUBLANES, MLSTM_CHUNK))
            mix = mlstm(qk, v, og, gates, gates_t,
                        _pad_heads(mlstm_conv_w[idx], 2 * H, MLSTM_QK_DIM, P, 1),
                        _pad_heads(mlstm_conv_b[idx][None], 2 * H, MLSTM_QK_DIM, P, 1),
                        bias_c, bias_r,
                        _pad_heads(mlstm_out_gain[idx][None], H, MLSTM_V_DIM, VP, 1), B)
            w_o_mix = _pad_heads(w_o_mix, H, MLSTM_V_DIM, VP, 0)

        h = out_proj(h, mix, qm, mk, mv, w_o_mix, w_o_mem, B, tm=512)
        j = layer // 2
        if layer % 2 == 0:
            h = dense_ffn(h, ln_ffn[layer], ffn_w_gate[j].astype(BF16), ffn_w_up[j].astype(BF16),
                          ffn_w_down[j].astype(BF16), tm=1024, tf=896)
        else:
            h = moe_ffn(h, ln_ffn[layer], moe_router[j], moe_w_gate, moe_w_up, moe_w_down, j)
    return h.reshape(B, S, D)
```

```python
import functools
import math

import jax
import jax.numpy as jnp
from jax import lax
from jax.experimental import pallas as pl
from jax.experimental.pallas import tpu as pltpu

F32 = jnp.float32
BF16 = jnp.bfloat16
I32 = jnp.int32

D_MODEL = 1024
N_MIXERS = 3
HEAD_DIM = 64
MEM_HEADS = 4
MEM_HEAD_DIM = 64
MEM_WIDTH = MEM_HEADS * MEM_HEAD_DIM
MIX_WIDTH = D_MODEL - MEM_WIDTH
RMS_EPS = 1e-6
NEG_INF = -1e30

SWA_Q_HEADS = MIX_WIDTH // HEAD_DIM
SWA_KV_HEADS = 4
SWA_GROUP = SWA_Q_HEADS // SWA_KV_HEADS
SWA_WINDOW = 128
SWA_BLOCK = 128

RET_HEADS = 6
RET_QK_DIM = 64
RET_V_DIM = MIX_WIDTH // RET_HEADS
RET_CHUNK = 128

MLSTM_HEADS = 4
MLSTM_V_DIM = MIX_WIDTH // MLSTM_HEADS
MLSTM_QK_DIM = MLSTM_V_DIM // 2
MLSTM_CHUNK = 128
MLSTM_CONV = 4
MLSTM_QK_PAD = 128
MLSTM_V_PAD = 256

N_EXPERTS = 8
TOP_K = 2

LANES = 128
SUBLANES = 8
VMEM_LIMIT = 56 * 1024 * 1024
ROW_DMA_UNROLL = 8


def _cparams(n_axes):
    return pltpu.CompilerParams(dimension_semantics=("arbitrary",) * n_axes,
                                vmem_limit_bytes=VMEM_LIMIT)


def _rms(x, eps=RMS_EPS):
    return x * lax.rsqrt(jnp.mean(x * x, axis=-1, keepdims=True) + eps)


def _dot(a, b):
    return jnp.dot(a, b, preferred_element_type=F32)


def _dot_nt(a, b):
    return lax.dot_general(a, b, (((1,), (1,)), ((), ())), preferred_element_type=F32)


def _dot_tn(a, b):
    return lax.dot_general(a, b, (((0,), (0,)), ((), ())), preferred_element_type=F32)


def _silu(x):
    return x * jax.nn.sigmoid(x)


NORM_CHUNK = 256


def _head_rms(y, seg_ref, hg):
    sq = y * y
    hi = sq.astype(BF16)
    lo = (sq - hi.astype(F32)).astype(BF16)
    ms = _dot(hi, seg_ref[...]) + _dot(lo, seg_ref[...])
    return y * lax.rsqrt(ms + RMS_EPS) * hg


def _rms_proj_kernel(x_ref, g_ref, w_ref, seg_ref, *refs, widths, normed):
    n_gain = sum(normed)
    gain_refs, o_refs = refs[:n_gain], refs[n_gain:]
    xn = (_rms(x_ref[...]) * g_ref[...]).astype(BF16)
    off = 0
    gi = 0
    for o_ref, wd, nrm in zip(o_refs, widths, normed):
        if nrm:
            hg = gain_refs[gi][...]
            gi += 1
            for c in range(0, wd, NORM_CHUNK):
                y = _dot(xn, w_ref[:, off + c:off + c + NORM_CHUNK])
                o_ref[:, c:c + NORM_CHUNK] = _head_rms(y, seg_ref, hg).astype(o_ref.dtype)
        else:
            o_ref[...] = _dot(xn, w_ref[:, off:off + wd]).astype(o_ref.dtype)
        off += wd


def rms_proj(x, gain, w, outs, tm, name):
    T, D = x.shape
    N = w.shape[1]
    widths = tuple(o[0] for o in outs)
    normed = tuple(o[2] is not None for o in outs)
    head_gains = [jnp.tile(o[2], NORM_CHUNK // HEAD_DIM).reshape(1, NORM_CHUNK) for o in outs if o[2] is not None]
    assert sum(widths) == N and T % tm == 0
    assert all(wd % NORM_CHUNK == 0 for wd, nrm in zip(widths, normed) if nrm)
    head_of = jnp.arange(NORM_CHUNK) // HEAD_DIM
    seg = jnp.where(head_of[:, None] == head_of[None, :], 1.0 / HEAD_DIM, 0.0).astype(BF16)
    return pl.pallas_call(
        functools.partial(_rms_proj_kernel, widths=widths, normed=normed),
        grid=(T // tm,),
        in_specs=[pl.BlockSpec((tm, D), lambda i: (i, 0)),
                  pl.BlockSpec((1, D), lambda i: (0, 0)),
                  pl.BlockSpec((D, N), lambda i: (0, 0)),
                  pl.BlockSpec((NORM_CHUNK, NORM_CHUNK), lambda i: (0, 0))]
                 + [pl.BlockSpec((1, NORM_CHUNK), lambda i: (0, 0))] * len(head_gains),
        out_specs=[pl.BlockSpec((tm, wd), lambda i: (i, 0)) for wd in widths],
        out_shape=[jax.ShapeDtypeStruct((T, o[0]), o[1]) for o in outs],
        compiler_params=_cparams(1),
        name=name,
    )(x, gain.reshape(1, D), w, seg, *head_gains)


def _swa_kernel(sink_ref, q_ref, kc_ref, kp_ref, vc_ref, vp_ref, bias_ref, o_ref):
    n = pl.program_id(1)
    L = SWA_BLOCK
    row = lax.broadcasted_iota(I32, (L, 2 * L), 0)
    col = lax.broadcasted_iota(I32, (L, 2 * L), 1)
    dist = row + L - col
    mask = (dist >= 0) & (dist < SWA_WINDOW) & ((col >= L) | (n > 0))
    outs = []
    for g in range(SWA_KV_HEADS):
        ksl = slice(g * HEAD_DIM, (g + 1) * HEAD_DIM)
        k = jnp.concatenate([kp_ref[:, ksl], kc_ref[:, ksl]], axis=0)
        v = jnp.concatenate([vp_ref[:, ksl], vc_ref[:, ksl]], axis=0)
        for h in range(g * SWA_GROUP, (g + 1) * SWA_GROUP):
            sink = sink_ref[h]
            s = _dot_nt(q_ref[:, h * HEAD_DIM:(h + 1) * HEAD_DIM], k) * (HEAD_DIM ** -0.5) - bias_ref[h]
            s = jnp.where(mask, s, NEG_INF)
            m = jnp.maximum(jnp.max(s, axis=-1, keepdims=True), sink)
            e = jnp.exp(s - m)
            denom = jnp.sum(e, axis=-1, keepdims=True) + jnp.exp(sink - m)
            outs.append(_dot(e.astype(BF16), v) / denom)
    o_ref[...] = jnp.concatenate(outs, axis=-1).astype(o_ref.dtype)


def swa_attention(q, k, v, sink, batch):
    T = q.shape[0]
    L = SWA_BLOCK
    nb = T // batch // L
    kvw = SWA_KV_HEADS * HEAD_DIM
    slopes = jnp.exp2(-8.0 * jnp.arange(1, SWA_Q_HEADS + 1, dtype=F32) / SWA_Q_HEADS)
    adist = jnp.abs(jnp.arange(L)[:, None] + L - jnp.arange(2 * L)[None, :]).astype(F32)
    bias = slopes[:, None, None] * adist
    cur = lambda b, n: (b * nb + n, 0)
    prev = lambda b, n: (b * nb + jnp.maximum(n - 1, 0), 0)
    return pl.pallas_call(
        _swa_kernel,
        grid=(batch, nb),
        in_specs=[pl.BlockSpec(memory_space=pltpu.SMEM),
                  pl.BlockSpec((L, MIX_WIDTH), cur),
                  pl.BlockSpec((L, kvw), cur), pl.BlockSpec((L, kvw), prev),
                  pl.BlockSpec((L, kvw), cur), pl.BlockSpec((L, kvw), prev),
                  pl.BlockSpec((SWA_Q_HEADS, L, 2 * L), lambda b, n: (0, 0, 0))],
        out_specs=pl.BlockSpec((L, MIX_WIDTH), cur),
        out_shape=jax.ShapeDtypeStruct((T, MIX_WIDTH), BF16),
        compiler_params=_cparams(2),
        name="swa_attention",
    )(sink, q, k, k, v, v, bias)


def _retention_kernel(q_ref, k_ref, v_ref, g_ref, dec_ref, qd_ref, kd_ref, cd_ref, o_ref, state_ref):
    @pl.when(pl.program_id(0) == 0)
    def _():
        state_ref[...] = jnp.zeros_like(state_ref)

    for b in range(q_ref.shape[0]):
        outs = []
        for h in range(RET_HEADS):
            qsl = slice(h * RET_QK_DIM, (h + 1) * RET_QK_DIM)
            vsl = slice(h * RET_V_DIM, (h + 1) * RET_V_DIM)
            q = q_ref[b, :, qsl]
            k = k_ref[b, :, qsl] * (RET_QK_DIM ** -0.5)
            v = v_ref[b, :, vsl]
            state = state_ref[b, h]
            s = _dot_nt(q, k.astype(BF16)) * dec_ref[h]
            o = _dot(s.astype(BF16), v) + _dot(q, state.astype(BF16)) * qd_ref[h]
            state_ref[b, h] = state * cd_ref[h] + _dot_tn((k * kd_ref[h]).astype(BF16), v)
            outs.append(_silu(g_ref[b, :, vsl]) * _rms(o))
        o_ref[b] = jnp.concatenate(outs, axis=-1).astype(o_ref.dtype)


def _retention_consts():
    H, L = RET_HEADS, RET_CHUNK
    log_gamma = jnp.log1p(-jnp.exp2(-5.0 - jnp.arange(H, dtype=F32)))
    pos = jnp.arange(L, dtype=F32)
    rel = pos[:, None] - pos[None, :]
    intra = jnp.exp(jnp.where(rel >= 0, log_gamma[:, None, None] * rel, -jnp.inf))
    q_decay = jnp.exp(log_gamma[:, None] * (pos + 1.0))[..., None]
    k_decay = jnp.exp(log_gamma[:, None] * (L - 1.0 - pos))[..., None]
    chunk_decay = jnp.exp(log_gamma * L)[:, None, None]
    return (intra,
            jnp.broadcast_to(q_decay, (H, L, RET_V_DIM)),
            jnp.broadcast_to(k_decay, (H, L, RET_QK_DIM)),
            jnp.broadcast_to(chunk_decay, (H, RET_QK_DIM, RET_V_DIM)))


def retention(q, k, v, g, batch):
    T = q.shape[0]
    L = RET_CHUNK
    nc = T // batch // L
    H = RET_HEADS
    intra, qd, kd, cd = _retention_consts()
    cur = lambda c: (0, c, 0)
    const = lambda c: (0, 0, 0)
    qkw = H * RET_QK_DIM
    per_batch = lambda a: a.reshape(batch, T // batch, a.shape[1])
    out = pl.pallas_call(
        _retention_kernel,
        grid=(nc,),
        in_specs=[pl.BlockSpec((batch, L, qkw), cur), pl.BlockSpec((batch, L, qkw), cur),
                  pl.BlockSpec((batch, L, MIX_WIDTH), cur), pl.BlockSpec((batch, L, MIX_WIDTH), cur),
                  pl.BlockSpec((H, L, L), const), pl.BlockSpec((H, L, RET_V_DIM), const),
                  pl.BlockSpec((H, L, RET_QK_DIM), const), pl.BlockSpec((H, RET_QK_DIM, RET_V_DIM), const)],
        out_specs=pl.BlockSpec((batch, L, MIX_WIDTH), cur),
        out_shape=jax.ShapeDtypeStruct((batch, T // batch, MIX_WIDTH), BF16),
        scratch_shapes=[pltpu.VMEM((batch, H, RET_QK_DIM, RET_V_DIM), F32)],
        compiler_params=_cparams(1),
        name="retention",
    )(per_batch(q), per_batch(k), per_batch(v), per_batch(g), intra, qd, kd, cd)
    return out.reshape(T, MIX_WIDTH)


def _split3(x):
    x1 = x.astype(BF16)
    r1 = x - x1.astype(F32)
    x2 = r1.astype(BF16)
    x3 = (r1 - x2.astype(F32)).astype(BF16)
    return x1, x2, x3


def _mlstm_kernel(qk_ref, v_ref, og_ref, gc_ref, gr_ref, cw_ref, cb_ref, bc_ref, br_ref, gain_ref,
                  tri_ref, trit_ref, o_ref, xbuf_ref, c_ref, n_ref, m_ref):
    for b in range(qk_ref.shape[0]):
        _mlstm_chunk(qk_ref.at[b], v_ref.at[b], og_ref.at[b], gc_ref.at[b], gr_ref.at[b], cw_ref, cb_ref,
                     bc_ref, br_ref, gain_ref, tri_ref, trit_ref, o_ref.at[b],
                     xbuf_ref.at[b], c_ref.at[b], n_ref.at[b], m_ref.at[b])


def _mlstm_chunk(qk_ref, v_ref, og_ref, gc_ref, gr_ref, cw_ref, cb_ref, bc_ref, br_ref, gain_ref,
                 tri_ref, trit_ref, o_ref, xbuf_ref, c_ref, n_ref, m_ref):
    L = MLSTM_CHUNK
    H = MLSTM_HEADS
    P = MLSTM_QK_PAD
    VP = MLSTM_V_PAD
    KT = SUBLANES

    @pl.when(pl.program_id(0) == 0)
    def _():
        xbuf_ref[0:KT, :] = jnp.zeros((KT, xbuf_ref.shape[1]), F32)
        c_ref[...] = jnp.zeros_like(c_ref)
        n_ref[...] = jnp.zeros_like(n_ref)
        m_ref[...] = jnp.zeros_like(m_ref)

    xbuf_ref[KT:KT + L, :] = qk_ref[...]
    acc = cb_ref[...] + cw_ref[MLSTM_CONV - 1:MLSTM_CONV, :] * xbuf_ref[KT:KT + L, :]
    for j in range(MLSTM_CONV - 1):
        sh = MLSTM_CONV - 1 - j
        acc = acc + cw_ref[j:j + 1, :] * xbuf_ref[KT - sh:KT - sh + L, :]
    xbuf_ref[0:KT, :] = qk_ref[L - KT:L, :]
    qk = _silu(acc)

    xc = gc_ref[...] + bc_ref[...]
    xr = gr_ref[...] + br_ref[...]
    lfc = jax.nn.log_sigmoid(xc)
    lfr = jax.nn.log_sigmoid(xr)
    tri = tri_ref[...]
    trit = trit_ref[...]
    bc = sum(_dot(tri, t) for t in _split3(lfc))
    br = sum(_dot(t, trit) for t in _split3(lfr))

    rowi = lax.broadcasted_iota(I32, (L, L), 0)
    coli = lax.broadcasted_iota(I32, (L, L), 1)
    causal = rowi >= coli
    outs = []
    for h in range(H):
        q = qk[:, h * P:(h + 1) * P].astype(BF16)
        k = qk[:, (H + h) * P:(H + h + 1) * P] * (MLSTM_QK_DIM ** -0.5)
        v = v_ref[:, h * VP:(h + 1) * VP]
        li_c = xc[:, h:h + 1]
        b_c = bc[:, H + h:H + h + 1]
        li_r = xr[h:h + 1, :]
        b_r = br[H + h:H + h + 1, :]
        g = b_c[L - 1:L, :]
        m = m_ref[h:h + 1, 0:1]
        C = c_ref[h]
        nvec = n_ref[h:h + 1, :]

        dmat = jnp.where(causal, b_c - b_r + li_r, -jnp.inf)
        inter = b_c + m
        m_t = jnp.maximum(inter, jnp.max(dmat, axis=-1, keepdims=True))
        w = jnp.exp(dmat - m_t)
        a = jnp.exp(inter - m_t)
        s = _dot_nt(q, k.astype(BF16)) * w
        num = _dot(s.astype(BF16), v) + a * _dot(q, C.astype(BF16))
        qf = qk[:, h * P:(h + 1) * P]
        den = jnp.sum(s, axis=-1, keepdims=True) + a * jnp.sum(qf * nvec, axis=-1, keepdims=True)
        hh = num / jnp.maximum(jnp.abs(den), jnp.exp(-m_t))

        u_c = g - b_c + li_c
        u_r = g - b_r + li_r
        m_new = jnp.maximum(g + m, jnp.max(u_r, axis=-1, keepdims=True))
        wk = jnp.exp(u_c - m_new)
        decay = jnp.exp(g + m - m_new)
        kw = k * wk
        c_ref[h] = decay * C + _dot_tn(kw.astype(BF16), v)
        n_ref[h:h + 1, :] = decay * nvec + jnp.sum(kw, axis=0, keepdims=True)
        m_ref[h:h + 1, :] = jnp.broadcast_to(m_new, (1, m_ref.shape[1]))

        ms = jnp.sum(hh * hh, axis=-1, keepdims=True) * (1.0 / MLSTM_V_DIM)
        hc = hh * lax.rsqrt(ms + RMS_EPS) * gain_ref[:, h * VP:(h + 1) * VP]
        outs.append(jax.nn.sigmoid(og_ref[:, h * VP:(h + 1) * VP]) * hc)
    o_ref[...] = jnp.concatenate(outs, axis=-1).astype(o_ref.dtype)


def mlstm(qk, v, og, gates, gates_t, conv_w, conv_b, bias_c, bias_r, gain, batch):
    T = qk.shape[0]
    L = MLSTM_CHUNK
    nc = T // batch // L
    H, P, VP = MLSTM_HEADS, MLSTM_QK_PAD, MLSTM_V_PAD
    tri = jnp.tril(jnp.ones((L, L), BF16))
    cur = lambda c: (0, c, 0)
    c2 = lambda c: (0, 0)
    per_batch = lambda a: a.reshape(batch, T // batch, a.shape[1])
    out = pl.pallas_call(
        _mlstm_kernel,
        grid=(nc,),
        in_specs=[pl.BlockSpec((batch, L, 2 * H * P), cur), pl.BlockSpec((batch, L, H * VP), cur),
                  pl.BlockSpec((batch, L, H * VP), cur), pl.BlockSpec((batch, L, LANES), cur),
                  pl.BlockSpec((batch, SUBLANES, L), lambda c: (0, 0, c)),
                  pl.BlockSpec((MLSTM_CONV, 2 * H * P), c2), pl.BlockSpec((1, 2 * H * P), c2),
                  pl.BlockSpec((1, LANES), c2), pl.BlockSpec((SUBLANES, L), c2),
                  pl.BlockSpec((1, H * VP), c2),
                  pl.BlockSpec((L, L), c2), pl.BlockSpec((L, L), c2)],
        out_specs=pl.BlockSpec((batch, L, H * VP), cur),
        out_shape=jax.ShapeDtypeStruct((batch, T // batch, H * VP), BF16),
        scratch_shapes=[pltpu.VMEM((batch, SUBLANES + L, 2 * H * P), F32),
                        pltpu.VMEM((batch, H, P, VP), F32),
                        pltpu.VMEM((batch, SUBLANES, P), F32),
                        pltpu.VMEM((batch, SUBLANES, LANES), F32)],
        compiler_params=_cparams(1),
        name="mlstm",
    )(per_batch(qk), per_batch(v), per_batch(og), per_batch(gates), gates_t,
      conv_w, conv_b, bias_c, bias_r, gain, tri, tri.T)
    return out.reshape(T, H * VP)


def _out_proj_kernel(h_ref, mix_ref, q_ref, k_ref, v_ref, w1_ref, w2_ref, o_ref):
    outs = []
    for hd in range(MEM_HEADS):
        sl = slice(hd * MEM_HEAD_DIM, (hd + 1) * MEM_HEAD_DIM)
        s = _dot_nt(q_ref[:, sl], k_ref[:, sl]) * (MEM_HEAD_DIM ** -0.5)
        e = jnp.exp(s - jnp.max(s, axis=-1, keepdims=True))
        outs.append(_dot(e.astype(BF16), v_ref[:, sl]) / jnp.sum(e, axis=-1, keepdims=True))
    mem_out = jnp.concatenate(outs, axis=-1).astype(BF16)
    o_ref[...] = h_ref[...] + _dot(mix_ref[...], w1_ref[...]) + _dot(mem_out, w2_ref[...])


def out_proj(h, mix, qm, mk, mv, w_mix, w_mem, batch, tm):
    T, D = h.shape
    Wm = mix.shape[1]
    M = mk.shape[0] // batch
    nt = T // batch // tm
    rows = lambda b, i: (b * nt + i, 0)
    mem = lambda b, i: (b, 0)
    const = lambda b, i: (0, 0)
    return pl.pallas_call(
        _out_proj_kernel,
        grid=(batch, nt),
        in_specs=[pl.BlockSpec((tm, D), rows),
                  pl.BlockSpec((tm, Wm), rows),
                  pl.BlockSpec((tm, MEM_WIDTH), rows),
                  pl.BlockSpec((M, MEM_WIDTH), mem),
                  pl.BlockSpec((M, MEM_WIDTH), mem),
                  pl.BlockSpec((Wm, D), const),
                  pl.BlockSpec((MEM_WIDTH, D), const)],
        out_specs=pl.BlockSpec((tm, D), rows),
        out_shape=jax.ShapeDtypeStruct((T, D), F32),
        compiler_params=_cparams(2),
        name="out_proj",
    )(h, mix, qm, mk, mv, w_mix, w_mem)


def _dense_ffn_kernel(h_ref, g_ref, wg_ref, wu_ref, wd_ref, o_ref, xn_ref):
    @pl.when(pl.program_id(1) == 0)
    def _():
        h = h_ref[...]
        xn_ref[...] = (_rms(h) * g_ref[...]).astype(BF16)
        o_ref[...] = h

    xn = xn_ref[...]
    a = _silu(_dot(xn, wg_ref[...])) * _dot(xn, wu_ref[...])
    o_ref[...] += _dot(a.astype(BF16), wd_ref[...])


def dense_ffn(h, gain, w_gate, w_up, w_down, tm, tf):
    T, D = h.shape
    Fd = w_gate.shape[1]
    return pl.pallas_call(
        _dense_ffn_kernel,
        grid=(T // tm, Fd // tf),
        in_specs=[pl.BlockSpec((tm, D), lambda i, j: (i, 0)),
                  pl.BlockSpec((1, D), lambda i, j: (0, 0)),
                  pl.BlockSpec((D, tf), lambda i, j: (0, j)),
                  pl.BlockSpec((D, tf), lambda i, j: (0, j)),
                  pl.BlockSpec((tf, D), lambda i, j: (j, 0))],
        out_specs=pl.BlockSpec((tm, D), lambda i, j: (i, 0)),
        out_shape=jax.ShapeDtypeStruct((T, D), F32),
        scratch_shapes=[pltpu.VMEM((tm, D), BF16)],
        compiler_params=_cparams(2),
        name="dense_ffn",
    )(h, gain.reshape(1, D), w_gate, w_up, w_down)


def _router_kernel(h_ref, g_ref, wr_ref, tri_ref, o_ref, ot_ref, cnt_ref, carry_ref):
    i = pl.program_id(0)

    @pl.when(i == 0)
    def _():
        carry_ref[...] = jnp.zeros_like(carry_ref)

    tm = h_ref.shape[0]
    xn = _rms(h_ref[...]) * g_ref[...]
    x_hi = xn.astype(BF16)
    x_lo = (xn - x_hi.astype(F32)).astype(BF16)
    wr = wr_ref[...]
    w_hi = wr.astype(BF16)
    w_lo = (wr - w_hi.astype(F32)).astype(BF16)
    logits = _dot(x_hi, w_hi) + (_dot(x_hi, w_lo) + _dot(x_lo, w_hi))
    lane = lax.broadcasted_iota(I32, (tm, LANES), 1)
    logits = jnp.where(lane < N_EXPERTS, logits, -jnp.inf)
    t1 = jnp.max(logits, axis=-1, keepdims=True)
    e1 = jnp.min(jnp.where(logits == t1, lane, LANES), axis=-1, keepdims=True)
    rest = jnp.where(lane == e1, -jnp.inf, logits)
    t2 = jnp.max(rest, axis=-1, keepdims=True)
    e2 = jnp.min(jnp.where(rest == t2, lane, LANES), axis=-1, keepdims=True)
    x2 = jnp.exp(t2 - t1)
    w1 = 1.0 / (1.0 + x2)
    w2 = x2 / (1.0 + x2)

    oh1 = lane == e1
    oh2 = lane == e2
    cnt = jnp.where(oh1 | oh2, 1.0, 0.0)
    before = _dot(tri_ref[...], cnt.astype(BF16)) + carry_ref[...]
    r1 = jnp.sum(jnp.where(oh1, before, 0.0), axis=-1, keepdims=True)
    r2 = jnp.sum(jnp.where(oh2, before, 0.0), axis=-1, keepdims=True)
    carry_ref[...] += jnp.sum(cnt, axis=0, keepdims=True)

    cols = (e1.astype(F32), e2.astype(F32), w1, w2, r1, r2)
    out = jnp.zeros((tm, LANES), F32)
    for c, val in enumerate(cols):
        out = jnp.where(lane == c, val, out)
    o_ref[...] = out
    ot_ref[...] = out.T[:SUBLANES, :]
    cnt_ref[...] = jnp.broadcast_to(carry_ref[...], cnt_ref.shape)


def moe_router(h, gain, w_router, tm):
    T, D = h.shape
    wr = jnp.zeros((D, LANES), F32).at[:, :N_EXPERTS].set(w_router)
    tri = jnp.tril(jnp.ones((tm, tm), BF16), k=-1)
    return pl.pallas_call(
        _router_kernel,
        grid=(T // tm,),
        in_specs=[pl.BlockSpec((tm, D), lambda i: (i, 0)),
                  pl.BlockSpec((1, D), lambda i: (0, 0)),
                  pl.BlockSpec((D, LANES), lambda i: (0, 0)),
                  pl.BlockSpec((tm, tm), lambda i: (0, 0))],
        out_specs=[pl.BlockSpec((tm, LANES), lambda i: (i, 0)),
                   pl.BlockSpec((SUBLANES, tm), lambda i: (0, i)),
                   pl.BlockSpec((SUBLANES, LANES), lambda i: (0, 0))],
        out_shape=[jax.ShapeDtypeStruct((T, LANES), F32),
                   jax.ShapeDtypeStruct((SUBLANES, T), F32),
                   jax.ShapeDtypeStruct((SUBLANES, LANES), F32)],
        scratch_shapes=[pltpu.VMEM((1, LANES), F32)],
        compiler_params=_cparams(1),
        name="moe_router",
    )(h, gain.reshape(1, D), wr, tri)


def _dispatch_kernel(ze_ref, zn_ref, p1_ref, p2_ref, h_ref, xs_ref, zero_ref, sem, zsem, *, pad_max, tail_max):
    tb = p1_ref.shape[0]
    zr = zero_ref.shape[0]

    @pl.when(pl.program_id(0) == 0)
    def _():
        zero_ref[...] = jnp.zeros_like(zero_ref)
        chunks = [(e, c) for e in range(N_EXPERTS) for c in range(pad_max // zr)]
        chunks += [(N_EXPERTS, c) for c in range(tail_max // zr)]

        def zcopy(e, c):
            start = pl.multiple_of(ze_ref[e] - (c + 1) * zr, zr)
            return pltpu.make_async_copy(zero_ref, xs_ref.at[pl.ds(start, zr), :], zsem)

        def needed(e, c):
            return c * zr < zn_ref[e]

        for e, c in chunks:
            @pl.when(needed(e, c))
            def _():
                zcopy(e, c).start()

        for e, c in chunks:
            @pl.when(needed(e, c))
            def _():
                zcopy(e, c).wait()

    def copy(t, pos):
        return pltpu.make_async_copy(h_ref.at[pl.ds(t, 1), :], xs_ref.at[pl.ds(pos, 1), :], sem)

    def issue(t, c):
        copy(t, p1_ref[t]).start()
        copy(t, p2_ref[t]).start()
        return c

    lax.fori_loop(0, tb, issue, 0, unroll=ROW_DMA_UNROLL)
    for _ in range(TOP_K):
        pltpu.make_async_copy(h_ref, xs_ref.at[pl.ds(0, tb), :], sem).wait()


def moe_dispatch(h, pos1, pos2, pad_end, pad_len, used_rows, n_rows, pad_max, tb, zr=256):
    T, D = h.shape
    tail_max = n_rows - TOP_K * T
    assert tail_max % zr == 0 and pad_max % zr == 0
    zero_end = jnp.concatenate([pad_end, jnp.full((1,), n_rows, I32)])
    zero_len = jnp.concatenate([pad_len, n_rows - used_rows[None]])
    grid_spec = pltpu.PrefetchScalarGridSpec(
        num_scalar_prefetch=2,
        grid=(T // tb,),
        in_specs=[pl.BlockSpec((tb,), lambda i, ps, pn: (i,), memory_space=pltpu.SMEM),
                  pl.BlockSpec((tb,), lambda i, ps, pn: (i,), memory_space=pltpu.SMEM),
                  pl.BlockSpec((tb, D), lambda i, ps, pn: (i, 0))],
        out_specs=pl.BlockSpec(memory_space=pl.ANY),
        scratch_shapes=[pltpu.VMEM((zr, D), h.dtype), pltpu.SemaphoreType.DMA(()), pltpu.SemaphoreType.DMA(())],
    )
    return pl.pallas_call(
        functools.partial(_dispatch_kernel, pad_max=pad_max, tail_max=tail_max),
        grid_spec=grid_spec,
        out_shape=jax.ShapeDtypeStruct((n_rows, D), h.dtype),
        compiler_params=_cparams(1),
        name="moe_dispatch",
    )(zero_end, zero_len, pos1, pos2, h)


def _expert_ffn_kernel(te_ref, tv_ref, x_ref, g_ref, wg_ref, wu_ref, wd_ref, o_ref, xn_ref, *, ts):
    i = pl.program_id(0)
    j = pl.program_id(1)
    valid = tv_ref[i]
    n_sub = x_ref.shape[0] // ts

    n_occ = (valid + ts - 1) // ts
    for k in range(1, n_sub + 1):
        rows = pl.ds(0, k * ts)

        @pl.when(n_occ == k)
        def _():
            @pl.when(j == 0)
            def _():
                xn_ref[rows, :] = (_rms(x_ref[rows, :]) * g_ref[...]).astype(BF16)

            xn = xn_ref[rows, :]
            a = _silu(_dot(xn, wg_ref[...].astype(BF16))) * _dot(xn, wu_ref[...].astype(BF16))
            y = _dot(a.astype(BF16), wd_ref[...].astype(BF16))

            @pl.when(j == 0)
            def _():
                o_ref[rows, :] = y
                if k < n_sub:
                    o_ref[pl.ds(k * ts, (n_sub - k) * ts), :] = jnp.zeros(((n_sub - k) * ts, o_ref.shape[1]), F32)

            @pl.when(j > 0)
            def _():
                o_ref[rows, :] += y

    @pl.when((n_occ == 0) & (j == 0))
    def _():
        o_ref[...] = jnp.zeros_like(o_ref)


def expert_ffn(xs, gain, w_gate, w_up, w_down, layer, tile_expert, tile_valid, n_tiles, tm, ts, tf):
    D = xs.shape[1]
    R = n_tiles * tm
    Fd = w_gate.shape[3]
    nf = Fd // tf

    def jj(i, j, tv):
        return jnp.where(tv[i] > 0, j, nf - 1)

    grid_spec = pltpu.PrefetchScalarGridSpec(
        num_scalar_prefetch=2,
        grid=(n_tiles, nf),
        in_specs=[pl.BlockSpec((tm, D), lambda i, j, te, tv: (jnp.where(tv[i] > 0, i, 0), 0)),
                  pl.BlockSpec((1, D), lambda i, j, te, tv: (0, 0)),
                  pl.BlockSpec((None, None, D, tf), lambda i, j, te, tv: (layer, te[i], 0, jj(i, j, tv))),
                  pl.BlockSpec((None, None, D, tf), lambda i, j, te, tv: (layer, te[i], 0, jj(i, j, tv))),
                  pl.BlockSpec((None, None, tf, D), lambda i, j, te, tv: (layer, te[i], jj(i, j, tv), 0))],
        out_specs=pl.BlockSpec((tm, D), lambda i, j, te, tv: (i, 0)),
        scratch_shapes=[pltpu.VMEM((tm, D), BF16)],
    )
    return pl.pallas_call(
        functools.partial(_expert_ffn_kernel, ts=ts),
        grid_spec=grid_spec,
        out_shape=jax.ShapeDtypeStruct((R, D), F32),
        compiler_params=_cparams(2),
        name="expert_ffn",
    )(tile_expert, tile_valid, xs, gain.reshape(1, D), w_gate, w_up, w_down)


def _combine_kernel(p1_ref, p2_ref, route_ref, h_ref, y_ref, o_ref, b1_ref, b2_ref, sem):
    tb = p1_ref.shape[0]

    def copies(t):
        dst = pl.ds(t, 1)
        return (pltpu.make_async_copy(y_ref.at[pl.ds(p1_ref[t], 1), :], b1_ref.at[dst, :], sem),
                pltpu.make_async_copy(y_ref.at[pl.ds(p2_ref[t], 1), :], b2_ref.at[dst, :], sem))

    def issue(t, c):
        for cp in copies(t):
            cp.start()
        return c

    lax.fori_loop(0, tb, issue, 0, unroll=ROW_DMA_UNROLL)
    for b_ref in (b1_ref, b2_ref):
        pltpu.make_async_copy(y_ref.at[pl.ds(0, tb), :], b_ref, sem).wait()
    w1 = route_ref[:, 2:3]
    w2 = route_ref[:, 3:4]
    o_ref[...] = h_ref[...] + w1 * b1_ref[...] + w2 * b2_ref[...]


def moe_combine(h, y, route, pos1, pos2, tb):
    T, D = h.shape
    smem = lambda: pl.BlockSpec((tb,), lambda i: (i,), memory_space=pltpu.SMEM)
    return pl.pallas_call(
        _combine_kernel,
        grid=(T // tb,),
        in_specs=[smem(), smem(),
                  pl.BlockSpec((tb, LANES), lambda i: (i, 0)),
                  pl.BlockSpec((tb, D), lambda i: (i, 0)),
                  pl.BlockSpec(memory_space=pl.ANY)],
        out_specs=pl.BlockSpec((tb, D), lambda i: (i, 0)),
        out_shape=jax.ShapeDtypeStruct((T, D), F32),
        scratch_shapes=[pltpu.VMEM((tb, D), F32), pltpu.VMEM((tb, D), F32),
                        pltpu.SemaphoreType.DMA(())],
        compiler_params=_cparams(1),
        name="moe_combine",
    )(pos1, pos2, route, h, y)


def moe_ffn(h, gain, w_router, w_gate, w_up, w_down, layer, tm=1024, ts=256, tf=512, tb=512):
    T = h.shape[0]
    route, route_t, counts = moe_router(h, gain, w_router, tm=512)
    e1 = route_t[0].astype(I32)
    e2 = route_t[1].astype(I32)
    counts = counts[0, :N_EXPERTS].astype(I32)
    tiles_per = (counts + tm - 1) // tm
    tile_end = jnp.cumsum(tiles_per)
    tile_start = tile_end - tiles_per
    row_start = tile_start * tm
    pos1 = row_start[e1] + route_t[4].astype(I32)
    pos2 = row_start[e2] + route_t[5].astype(I32)
    n_tiles = (TOP_K * T) // tm + N_EXPERTS
    tile_ids = jnp.arange(n_tiles, dtype=I32)
    tile_expert = jnp.sum(jnp.minimum(tile_ids, tile_end[-1] - 1)[:, None] >= tile_end[None, :], axis=1).astype(I32)
    tile_valid = jnp.clip(counts[tile_expert] - (tile_ids - tile_start[tile_expert]) * tm, 0, tm)
    tile_valid = jnp.where(tile_ids < tile_end[-1], tile_valid, 0).astype(I32)
    pad_len = tiles_per * tm - counts

    xs = moe_dispatch(h, pos1, pos2, tile_end * tm, pad_len, tile_end[-1] * tm, n_tiles * tm, tm, tb)
    y = expert_ffn(xs, gain, w_gate, w_up, w_down, layer, tile_expert, tile_valid, n_tiles, tm, ts, tf)
    return moe_combine(h, y, route, pos1, pos2, tb)


def _pad_heads(a, n_heads, width, pad_to, axis):
    shape = list(a.shape)
    a = a.reshape(shape[:axis] + [n_heads, width] + shape[axis + 1:])
    pads = [(0, 0)] * a.ndim
    pads[axis + 1] = (0, pad_to - width)
    a = jnp.pad(a, pads)
    return a.reshape(shape[:axis] + [n_heads * pad_to] + shape[axis + 1:])


def kernel(x, mem, ln_mix, ln_mem, w_mem_kv, mem_q_gain, mem_k_gain, w_out, ln_ffn, swa_w_in, swa_q_gain, swa_k_gain, swa_sink, ret_w_in, mlstm_w_in, mlstm_conv_w, mlstm_conv_b, mlstm_i_bias, mlstm_f_bias, mlstm_out_gain, ffn_w_gate, ffn_w_up, ffn_w_down, moe_router, moe_w_gate, moe_w_up, moe_w_down):
    B, S, D = x.shape
    M = mem.shape[1]
    T = B * S
    depth = ln_mix.shape[0]
    h = x.reshape(T, D)
    mem2 = mem.reshape(B * M, D)

    for layer in range(depth):
        kind = layer % N_MIXERS
        idx = layer // N_MIXERS
        mk, mv = rms_proj(mem2, ln_mem[layer], w_mem_kv[layer].astype(BF16),
                          ((MEM_WIDTH, BF16, mem_k_gain[layer]), (MEM_WIDTH, BF16, None)),
                          tm=B * M, name="mem_kv_proj")
        w_o = w_out[layer].astype(BF16)
        w_o_mix, w_o_mem = w_o[:MIX_WIDTH], w_o[MIX_WIDTH:]
        qm_out = (MEM_WIDTH, BF16, mem_q_gain[layer])

        if kind == 0:
            kvw = SWA_KV_HEADS * HEAD_DIM
            q, k, v, qm = rms_proj(h, ln_mix[layer], swa_w_in[idx].astype(BF16),
                                   ((MIX_WIDTH, BF16, swa_q_gain[idx]), (kvw, BF16, swa_k_gain[idx]),
                                    (kvw, BF16, None), qm_out), tm=512, name="swa_in_proj")
            mix = swa_attention(q, k, v, swa_sink[idx], B)
        elif kind == 1:
            qkw = RET_HEADS * RET_QK_DIM
            q, k, v, g, qm = rms_proj(h, ln_mix[layer], ret_w_in[idx].astype(BF16),
                                      ((qkw, BF16, None), (qkw, F32, None), (MIX_WIDTH, BF16, None),
                                       (MIX_WIDTH, F32, None), qm_out), tm=512, name="ret_in_proj")
            mix = retention(q, k, v, g, B)
        else:
            H, P, VP = MLSTM_HEADS, MLSTM_QK_PAD, MLSTM_V_PAD
            w = mlstm_w_in[idx]
            qkw = 2 * H * MLSTM_QK_DIM
            o_v, o_og, o_ig = qkw, qkw + MIX_WIDTH, qkw + 2 * MIX_WIDTH
            o_fg, o_qm = o_ig + H, o_ig + 2 * H
            w_gates = jnp.zeros((D, LANES), F32).at[:, :2 * H].set(w[:, o_ig:o_qm])
            w_pad = jnp.concatenate([
                _pad_heads(w[:, :qkw], 2 * H, MLSTM_QK_DIM, P, 1),
                _pad_heads(w[:, o_v:o_og], H, MLSTM_V_DIM, VP, 1),
                _pad_heads(w[:, o_og:o_ig], H, MLSTM_V_DIM, VP, 1),
                w_gates, w[:, o_qm:]], axis=1).astype(BF16)
            qk, v, og, gates, qm = rms_proj(h, ln_mix[layer], w_pad,
                                            ((2 * H * P, F32, None), (H * VP, BF16, None), (H * VP, F32, None),
                                             (LANES, F32, None), qm_out), tm=512, name="mlstm_in_proj")
            gates_t = gates[:, :SUBLANES].reshape(B, S, SUBLANES).transpose(0, 2, 1)
            bias = jnp.concatenate([mlstm_i_bias[idx], mlstm_f_bias[idx]])
            bias_c = jnp.zeros((1, LANES), F32).at[0, :2 * H].set(bias)
            bias_r = jnp.broadcast_to(bias[:, None], (SUBLANES, MLSTM_CHUNK))
            mix = mlstm(qk, v, og, gates, gates_t,
                        _pad_heads(mlstm_conv_w[idx], 2 * H, MLSTM_QK_DIM, P, 1),
                        _pad_heads(mlstm_conv_b[idx][None], 2 * H, MLSTM_QK_DIM, P, 1),
                        bias_c, bias_r,
                        _pad_heads(mlstm_out_gain[idx][None], H, MLSTM_V_DIM, VP, 1), B)
            w_o_mix = _pad_heads(w_o_mix, H, MLSTM_V_DIM, VP, 0)

        h = out_proj(h, mix, qm, mk, mv, w_o_mix, w_o_mem, B, tm=512)
        j = layer // 2
        if layer % 2 == 0:
            h = dense_ffn(h, ln_ffn[layer], ffn_w_gate[j].astype(BF16), ffn_w_up[j].astype(BF16),
                          ffn_w_down[j].astype(BF16), tm=1024, tf=896)
        else:
            h = moe_ffn(h, ln_ffn[layer], moe_router[j], moe_w_gate, moe_w_up, moe_w_down, j)
    return h.reshape(B, S, D)
```

```python
import functools
import math

import jax
import jax.numpy as jnp
from jax import lax
from jax.experimental import pallas as pl
from jax.experimental.pallas import tpu as pltpu

F32 = jnp.float32
BF16 = jnp.bfloat16
I32 = jnp.int32

D_MODEL = 1024
N_MIXERS = 3
HEAD_DIM = 64
MEM_HEADS = 4
MEM_HEAD_DIM = 64
MEM_WIDTH = MEM_HEADS * MEM_HEAD_DIM
MIX_WIDTH = D_MODEL - MEM_WIDTH
RMS_EPS = 1e-6
NEG_INF = -1e30

SWA_Q_HEADS = MIX_WIDTH // HEAD_DIM
SWA_KV_HEADS = 4
SWA_GROUP = SWA_Q_HEADS // SWA_KV_HEADS
SWA_WINDOW = 128
SWA_BLOCK = 128

RET_HEADS = 6
RET_QK_DIM = 64
RET_V_DIM = MIX_WIDTH // RET_HEADS
RET_CHUNK = 128

MLSTM_HEADS = 4
MLSTM_V_DIM = MIX_WIDTH // MLSTM_HEADS
MLSTM_QK_DIM = MLSTM_V_DIM // 2
MLSTM_CHUNK = 128
MLSTM_CONV = 4
MLSTM_QK_PAD = 128
MLSTM_V_PAD = 256

N_EXPERTS = 8
TOP_K = 2

LANES = 128
SUBLANES = 8
VMEM_LIMIT = 56 * 1024 * 1024
ROW_DMA_UNROLL = 8


def _cparams(n_axes):
    return pltpu.CompilerParams(dimension_semantics=("arbitrary",) * n_axes,
                                vmem_limit_bytes=VMEM_LIMIT)


def _rms(x, eps=RMS_EPS):
    return x * lax.rsqrt(jnp.mean(x * x, axis=-1, keepdims=True) + eps)


def _dot(a, b):
    return jnp.dot(a, b, preferred_element_type=F32)


def _dot_nt(a, b):
    return lax.dot_general(a, b, (((1,), (1,)), ((), ())), preferred_element_type=F32)


def _dot_tn(a, b):
    return lax.dot_general(a, b, (((0,), (0,)), ((), ())), preferred_element_type=F32)


def _silu(x):
    return x * jax.nn.sigmoid(x)


NORM_CHUNK = 256


def _head_rms(y, seg_ref, hg):
    sq = y * y
    hi = sq.astype(BF16)
    lo = (sq - hi.astype(F32)).astype(BF16)
    ms = _dot(hi, seg_ref[...]) + _dot(lo, seg_ref[...])
    return y * lax.rsqrt(ms + RMS_EPS) * hg


def _rms_proj_kernel(x_ref, g_ref, w_ref, seg_ref, *refs, widths, normed):
    n_gain = sum(normed)
    gain_refs, o_refs = refs[:n_gain], refs[n_gain:]
    xn = (_rms(x_ref[...]) * g_ref[...]).astype(BF16)
    off = 0
    gi = 0
    for o_ref, wd, nrm in zip(o_refs, widths, normed):
        if nrm:
            hg = gain_refs[gi][...]
            gi += 1
            for c in range(0, wd, NORM_CHUNK):
                y = _dot(xn, w_ref[:, off + c:off + c + NORM_CHUNK])
                o_ref[:, c:c + NORM_CHUNK] = _head_rms(y, seg_ref, hg).astype(o_ref.dtype)
        else:
            o_ref[...] = _dot(xn, w_ref[:, off:off + wd]).astype(o_ref.dtype)
        off += wd


def rms_proj(x, gain, w, outs, tm, name):
    T, D = x.shape
    N = w.shape[1]
    widths = tuple(o[0] for o in outs)
    normed = tuple(o[2] is not None for o in outs)
    head_gains = [jnp.tile(o[2], NORM_CHUNK // HEAD_DIM).reshape(1, NORM_CHUNK) for o in outs if o[2] is not None]
    assert sum(widths) == N and T % tm == 0
    assert all(wd % NORM_CHUNK == 0 for wd, nrm in zip(widths, normed) if nrm)
    head_of = jnp.arange(NORM_CHUNK) // HEAD_DIM
    seg = jnp.where(head_of[:, None] == head_of[None, :], 1.0 / HEAD_DIM, 0.0).astype(BF16)
    return pl.pallas_call(
        functools.partial(_rms_proj_kernel, widths=widths, normed=normed),
        grid=(T // tm,),
        in_specs=[pl.BlockSpec((tm, D), lambda i: (i, 0)),
                  pl.BlockSpec((1, D), lambda i: (0, 0)),
                  pl.BlockSpec((D, N), lambda i: (0, 0)),
                  pl.BlockSpec((NORM_CHUNK, NORM_CHUNK), lambda i: (0, 0))]
                 + [pl.BlockSpec((1, NORM_CHUNK), lambda i: (0, 0))] * len(head_gains),
        out_specs=[pl.BlockSpec((tm, wd), lambda i: (i, 0)) for wd in widths],
        out_shape=[jax.ShapeDtypeStruct((T, o[0]), o[1]) for o in outs],
        compiler_params=_cparams(1),
        name=name,
    )(x, gain.reshape(1, D), w, seg, *head_gains)


def _swa_kernel(sink_ref, q_ref, kc_ref, kp_ref, vc_ref, vp_ref, bias_ref, o_ref):
    n = pl.program_id(1)
    L = SWA_BLOCK
    key = lax.broadcasted_iota(I32, (L, L), 0)
    qry = lax.broadcasted_iota(I32, (L, L), 1)
    from_prev = key > qry
    valid = jnp.logical_or(n > 0, key <= qry)
    zero = jnp.zeros((L, L), BF16)
    outs = []
    for g in range(SWA_KV_HEADS):
        ksl = slice(g * HEAD_DIM, (g + 1) * HEAD_DIM)
        k = jnp.concatenate([kp_ref[:, ksl], kc_ref[:, ksl]], axis=0)
        v = jnp.concatenate([vp_ref[:, ksl], vc_ref[:, ksl]], axis=0)
        for h in range(g * SWA_GROUP, (g + 1) * SWA_GROUP):
            sink = sink_ref[h]
            s2 = _dot_nt(k, q_ref[:, h * HEAD_DIM:(h + 1) * HEAD_DIM])
            s = jnp.where(from_prev, s2[:L], s2[L:]) * (HEAD_DIM ** -0.5) - bias_ref[h]
            s = jnp.where(valid, s, NEG_INF)
            m = jnp.maximum(jnp.max(s, axis=0, keepdims=True), sink)
            e = jnp.exp(s - m)
            denom = jnp.sum(e, axis=0, keepdims=True) + jnp.exp(sink - m)
            p = (e * (1.0 / denom)).astype(BF16)
            p2 = jnp.concatenate([jnp.where(from_prev, p, zero), jnp.where(from_prev, zero, p)], axis=0)
            outs.append(_dot_tn(p2, v))
    o_ref[...] = jnp.concatenate(outs, axis=-1).astype(o_ref.dtype)


def swa_attention(q, k, v, sink, batch):
    T = q.shape[0]
    L = SWA_BLOCK
    nb = T // batch // L
    kvw = SWA_KV_HEADS * HEAD_DIM
    slopes = jnp.exp2(-8.0 * jnp.arange(1, SWA_Q_HEADS + 1, dtype=F32) / SWA_Q_HEADS)
    assert SWA_WINDOW == L
    qi, kj = jnp.arange(L)[:, None], jnp.arange(L)[None, :]
    dist = jnp.where(kj > qi, qi + L - kj, qi - kj).astype(F32)
    bias = slopes[:, None, None] * dist.T
    cur = lambda b, n: (b * nb + n, 0)
    prev = lambda b, n: (b * nb + jnp.maximum(n - 1, 0), 0)
    return pl.pallas_call(
        _swa_kernel,
        grid=(batch, nb),
        in_specs=[pl.BlockSpec(memory_space=pltpu.SMEM),
                  pl.BlockSpec((L, MIX_WIDTH), cur),
                  pl.BlockSpec((L, kvw), cur), pl.BlockSpec((L, kvw), prev),
                  pl.BlockSpec((L, kvw), cur), pl.BlockSpec((L, kvw), prev),
                  pl.BlockSpec((SWA_Q_HEADS, L, L), lambda b, n: (0, 0, 0))],
        out_specs=pl.BlockSpec((L, MIX_WIDTH), cur),
        out_shape=jax.ShapeDtypeStruct((T, MIX_WIDTH), BF16),
        compiler_params=_cparams(2),
        name="swa_attention",
    )(sink, q, k, k, v, v, bias)


def _retention_kernel(q_ref, k_ref, v_ref, g_ref, dec_ref, qd_ref, kd_ref, cd_ref, o_ref, state_ref):
    @pl.when(pl.program_id(0) == 0)
    def _():
        state_ref[...] = jnp.zeros_like(state_ref)

    for b in range(q_ref.shape[0]):
        outs = []
        for h in range(RET_HEADS):
            qsl = slice(h * RET_QK_DIM, (h + 1) * RET_QK_DIM)
            vsl = slice(h * RET_V_DIM, (h + 1) * RET_V_DIM)
            q = q_ref[b, :, qsl]
            k = k_ref[b, :, qsl] * (RET_QK_DIM ** -0.5)
            v = v_ref[b, :, vsl]
            state = state_ref[b, h]
            s = _dot_nt(q, k.astype(BF16)) * dec_ref[h]
            o = _dot(s.astype(BF16), v) + _dot(q, state.astype(BF16)) * qd_ref[h]
            state_ref[b, h] = state * cd_ref[h] + _dot_tn((k * kd_ref[h]).astype(BF16), v)
            outs.append(_silu(g_ref[b, :, vsl]) * _rms(o))
        o_ref[b] = jnp.concatenate(outs, axis=-1).astype(o_ref.dtype)


def _retention_consts():
    H, L = RET_HEADS, RET_CHUNK
    log_gamma = jnp.log1p(-jnp.exp2(-5.0 - jnp.arange(H, dtype=F32)))
    pos = jnp.arange(L, dtype=F32)
    rel = pos[:, None] - pos[None, :]
    intra = jnp.exp(jnp.where(rel >= 0, log_gamma[:, None, None] * rel, -jnp.inf))
    q_decay = jnp.exp(log_gamma[:, None] * (pos + 1.0))[..., None]
    k_decay = jnp.exp(log_gamma[:, None] * (L - 1.0 - pos))[..., None]
    chunk_decay = jnp.exp(log_gamma * L)[:, None, None]
    return (intra,
            jnp.broadcast_to(q_decay, (H, L, RET_V_DIM)),
            jnp.broadcast_to(k_decay, (H, L, RET_QK_DIM)),
            jnp.broadcast_to(chunk_decay, (H, RET_QK_DIM, RET_V_DIM)))


def retention(q, k, v, g, batch):
    T = q.shape[0]
    L = RET_CHUNK
    nc = T // batch // L
    H = RET_HEADS
    intra, qd, kd, cd = _retention_consts()
    cur = lambda c: (0, c, 0)
    const = lambda c: (0, 0, 0)
    qkw = H * RET_QK_DIM
    per_batch = lambda a: a.reshape(batch, T // batch, a.shape[1])
    out = pl.pallas_call(
        _retention_kernel,
        grid=(nc,),
        in_specs=[pl.BlockSpec((batch, L, qkw), cur), pl.BlockSpec((batch, L, qkw), cur),
                  pl.BlockSpec((batch, L, MIX_WIDTH), cur), pl.BlockSpec((batch, L, MIX_WIDTH), cur),
                  pl.BlockSpec((H, L, L), const), pl.BlockSpec((H, L, RET_V_DIM), const),
                  pl.BlockSpec((H, L, RET_QK_DIM), const), pl.BlockSpec((H, RET_QK_DIM, RET_V_DIM), const)],
        out_specs=pl.BlockSpec((batch, L, MIX_WIDTH), cur),
        out_shape=jax.ShapeDtypeStruct((batch, T // batch, MIX_WIDTH), BF16),
        scratch_shapes=[pltpu.VMEM((batch, H, RET_QK_DIM, RET_V_DIM), F32)],
        compiler_params=_cparams(1),
        name="retention",
    )(per_batch(q), per_batch(k), per_batch(v), per_batch(g), intra, qd, kd, cd)
    return out.reshape(T, MIX_WIDTH)


def _split3(x):
    x1 = x.astype(BF16)
    r1 = x - x1.astype(F32)
    x2 = r1.astype(BF16)
    x3 = (r1 - x2.astype(F32)).astype(BF16)
    return x1, x2, x3


def _mlstm_kernel(qk_ref, v_ref, og_ref, gc_ref, gr_ref, cw_ref, cb_ref, bc_ref, br_ref, gain_ref,
                  tri_ref, trit_ref, o_ref, xbuf_ref, c_ref, n_ref, m_ref):
    for b in range(qk_ref.shape[0]):
        _mlstm_chunk(qk_ref.at[b], v_ref.at[b], og_ref.at[b], gc_ref.at[b], gr_ref.at[b], cw_ref, cb_ref,
                     bc_ref, br_ref, gain_ref, tri_ref, trit_ref, o_ref.at[b],
                     xbuf_ref.at[b], c_ref.at[b], n_ref.at[b], m_ref.at[b])


def _mlstm_chunk(qk_ref, v_ref, og_ref, gc_ref, gr_ref, cw_ref, cb_ref, bc_ref, br_ref, gain_ref,
                 tri_ref, trit_ref, o_ref, xbuf_ref, c_ref, n_ref, m_ref):
    L = MLSTM_CHUNK
    H = MLSTM_HEADS
    P = MLSTM_QK_PAD
    VP = MLSTM_V_PAD
    KT = SUBLANES

    @pl.when(pl.program_id(0) == 0)
    def _():
        xbuf_ref[0:KT, :] = jnp.zeros((KT, xbuf_ref.shape[1]), F32)
        c_ref[...] = jnp.zeros_like(c_ref)
        n_ref[...] = jnp.zeros_like(n_ref)
        m_ref[...] = jnp.zeros_like(m_ref)

    xbuf_ref[KT:KT + L, :] = qk_ref[...]
    acc = cb_ref[...] + cw_ref[MLSTM_CONV - 1:MLSTM_CONV, :] * xbuf_ref[KT:KT + L, :]
    for j in range(MLSTM_CONV - 1):
        sh = MLSTM_CONV - 1 - j
        acc = acc + cw_ref[j:j + 1, :] * xbuf_ref[KT - sh:KT - sh + L, :]
    xbuf_ref[0:KT, :] = qk_ref[L - KT:L, :]
    qk = _silu(acc)

    xc = gc_ref[...] + bc_ref[...]
    xr = gr_ref[...] + br_ref[...]
    lfc = jax.nn.log_sigmoid(xc)
    lfr = jax.nn.log_sigmoid(xr)
    tri = tri_ref[...]
    trit = trit_ref[...]
    bc = sum(_dot(tri, t) for t in _split3(lfc))
    br = sum(_dot(t, trit) for t in _split3(lfr))

    rowi = lax.broadcasted_iota(I32, (L, L), 0)
    coli = lax.broadcasted_iota(I32, (L, L), 1)
    causal = rowi >= coli
    outs = []
    for h in range(H):
        q = qk[:, h * P:(h + 1) * P].astype(BF16)
        k = qk[:, (H + h) * P:(H + h + 1) * P] * (MLSTM_QK_DIM ** -0.5)
        v = v_ref[:, h * VP:(h + 1) * VP]
        li_c = xc[:, h:h + 1]
        b_c = bc[:, H + h:H + h + 1]
        li_r = xr[h:h + 1, :]
        b_r = br[H + h:H + h + 1, :]
        g = b_c[L - 1:L, :]
        m = m_ref[h:h + 1, 0:1]
        C = c_ref[h]
        nvec = n_ref[h:h + 1, :]

        dmat = jnp.where(causal, b_c - b_r + li_r, -jnp.inf)
        inter = b_c + m
        m_t = jnp.maximum(inter, jnp.max(dmat, axis=-1, keepdims=True))
        w = jnp.exp(dmat - m_t)
        a = jnp.exp(inter - m_t)
        s = _dot_nt(q, k.astype(BF16)) * w
        num = _dot(s.astype(BF16), v) + a * _dot(q, C.astype(BF16))
        qf = qk[:, h * P:(h + 1) * P]
        den = jnp.sum(s, axis=-1, keepdims=True) + a * jnp.sum(qf * nvec, axis=-1, keepdims=True)
        hh = num / jnp.maximum(jnp.abs(den), jnp.exp(-m_t))

        u_c = g - b_c + li_c
        u_r = g - b_r + li_r
        m_new = jnp.maximum(g + m, jnp.max(u_r, axis=-1, keepdims=True))
        wk = jnp.exp(u_c - m_new)
        decay = jnp.exp(g + m - m_new)
        kw = k * wk
        c_ref[h] = decay * C + _dot_tn(kw.astype(BF16), v)
        n_ref[h:h + 1, :] = decay * nvec + jnp.sum(kw, axis=0, keepdims=True)
        m_ref[h:h + 1, :] = jnp.broadcast_to(m_new, (1, m_ref.shape[1]))

        ms = jnp.sum(hh * hh, axis=-1, keepdims=True) * (1.0 / MLSTM_V_DIM)
        hc = hh * lax.rsqrt(ms + RMS_EPS) * gain_ref[:, h * VP:(h + 1) * VP]
        outs.append(jax.nn.sigmoid(og_ref[:, h * VP:(h + 1) * VP]) * hc)
    o_ref[...] = jnp.concatenate(outs, axis=-1).astype(o_ref.dtype)


def mlstm(qk, v, og, gates, gates_t, conv_w, conv_b, bias_c, bias_r, gain, batch):
    T = qk.shape[0]
    L = MLSTM_CHUNK
    nc = T // batch // L
    H, P, VP = MLSTM_HEADS, MLSTM_QK_PAD, MLSTM_V_PAD
    tri = jnp.tril(jnp.ones((L, L), BF16))
    cur = lambda c: (0, c, 0)
    c2 = lambda c: (0, 0)
    per_batch = lambda a: a.reshape(batch, T // batch, a.shape[1])
    out = pl.pallas_call(
        _mlstm_kernel,
        grid=(nc,),
        in_specs=[pl.BlockSpec((batch, L, 2 * H * P), cur), pl.BlockSpec((batch, L, H * VP), cur),
                  pl.BlockSpec((batch, L, H * VP), cur), pl.BlockSpec((batch, L, LANES), cur),
                  pl.BlockSpec((batch, SUBLANES, L), lambda c: (0, 0, c)),
                  pl.BlockSpec((MLSTM_CONV, 2 * H * P), c2), pl.BlockSpec((1, 2 * H * P), c2),
                  pl.BlockSpec((1, LANES), c2), pl.BlockSpec((SUBLANES, L), c2),
                  pl.BlockSpec((1, H * VP), c2),
                  pl.BlockSpec((L, L), c2), pl.BlockSpec((L, L), c2)],
        out_specs=pl.BlockSpec((batch, L, H * VP), cur),
        out_shape=jax.ShapeDtypeStruct((batch, T // batch, H * VP), BF16),
        scratch_shapes=[pltpu.VMEM((batch, SUBLANES + L, 2 * H * P), F32),
                        pltpu.VMEM((batch, H, P, VP), F32),
                        pltpu.VMEM((batch, SUBLANES, P), F32),
                        pltpu.VMEM((batch, SUBLANES, LANES), F32)],
        compiler_params=_cparams(1),
        name="mlstm",
    )(per_batch(qk), per_batch(v), per_batch(og), per_batch(gates), gates_t,
      conv_w, conv_b, bias_c, bias_r, gain, tri, tri.T)
    return out.reshape(T, H * VP)


def _out_proj_kernel(h_ref, mix_ref, q_ref, k_ref, v_ref, w1_ref, w2_ref, o_ref):
    outs = []
    for hd in range(MEM_HEADS):
        sl = slice(hd * MEM_HEAD_DIM, (hd + 1) * MEM_HEAD_DIM)
        s = _dot_nt(q_ref[:, sl], k_ref[:, sl]) * (MEM_HEAD_DIM ** -0.5)
        e = jnp.exp(s - jnp.max(s, axis=-1, keepdims=True))
        outs.append(_dot(e.astype(BF16), v_ref[:, sl]) / jnp.sum(e, axis=-1, keepdims=True))
    mem_out = jnp.concatenate(outs, axis=-1).astype(BF16)
    o_ref[...] = h_ref[...] + _dot(mix_ref[...], w1_ref[...]) + _dot(mem_out, w2_ref[...])


def out_proj(h, mix, qm, mk, mv, w_mix, w_mem, batch, tm):
    T, D = h.shape
    Wm = mix.shape[1]
    M = mk.shape[0] // batch
    nt = T // batch // tm
    rows = lambda b, i: (b * nt + i, 0)
    mem = lambda b, i: (b, 0)
    const = lambda b, i: (0, 0)
    return pl.pallas_call(
        _out_proj_kernel,
        grid=(batch, nt),
        in_specs=[pl.BlockSpec((tm, D), rows),
                  pl.BlockSpec((tm, Wm), rows),
                  pl.BlockSpec((tm, MEM_WIDTH), rows),
                  pl.BlockSpec((M, MEM_WIDTH), mem),
                  pl.BlockSpec((M, MEM_WIDTH), mem),
                  pl.BlockSpec((Wm, D), const),
                  pl.BlockSpec((MEM_WIDTH, D), const)],
        out_specs=pl.BlockSpec((tm, D), rows),
        out_shape=jax.ShapeDtypeStruct((T, D), F32),
        compiler_params=_cparams(2),
        name="out_proj",
    )(h, mix, qm, mk, mv, w_mix, w_mem)


def _dense_ffn_kernel(h_ref, g_ref, wg_ref, wu_ref, wd_ref, o_ref, xn_ref):
    @pl.when(pl.program_id(1) == 0)
    def _():
        h = h_ref[...]
        xn_ref[...] = (_rms(h) * g_ref[...]).astype(BF16)
        o_ref[...] = h

    xn = xn_ref[...]
    a = _silu(_dot(xn, wg_ref[...])) * _dot(xn, wu_ref[...])
    o_ref[...] += _dot(a.astype(BF16), wd_ref[...])


def dense_ffn(h, gain, w_gate, w_up, w_down, tm, tf):
    T, D = h.shape
    Fd = w_gate.shape[1]
    return pl.pallas_call(
        _dense_ffn_kernel,
        grid=(T // tm, Fd // tf),
        in_specs=[pl.BlockSpec((tm, D), lambda i, j: (i, 0)),
                  pl.BlockSpec((1, D), lambda i, j: (0, 0)),
                  pl.BlockSpec((D, tf), lambda i, j: (0, j)),
                  pl.BlockSpec((D, tf), lambda i, j: (0, j)),
                  pl.BlockSpec((tf, D), lambda i, j: (j, 0))],
        out_specs=pl.BlockSpec((tm, D), lambda i, j: (i, 0)),
        out_shape=jax.ShapeDtypeStruct((T, D), F32),
        scratch_shapes=[pltpu.VMEM((tm, D), BF16)],
        compiler_params=_cparams(2),
        name="dense_ffn",
    )(h, gain.reshape(1, D), w_gate, w_up, w_down)


def _router_kernel(h_ref, g_ref, wr_ref, tri_ref, o_ref, ot_ref, cnt_ref, carry_ref):
    i = pl.program_id(0)

    @pl.when(i == 0)
    def _():
        carry_ref[...] = jnp.zeros_like(carry_ref)

    tm = h_ref.shape[0]
    xn = _rms(h_ref[...]) * g_ref[...]
    x_hi = xn.astype(BF16)
    x_lo = (xn - x_hi.astype(F32)).astype(BF16)
    wr = wr_ref[...]
    w_hi = wr.astype(BF16)
    w_lo = (wr - w_hi.astype(F32)).astype(BF16)
    logits = _dot(x_hi, w_hi) + (_dot(x_hi, w_lo) + _dot(x_lo, w_hi))
    lane = lax.broadcasted_iota(I32, (tm, LANES), 1)
    logits = jnp.where(lane < N_EXPERTS, logits, -jnp.inf)
    t1 = jnp.max(logits, axis=-1, keepdims=True)
    e1 = jnp.min(jnp.where(logits == t1, lane, LANES), axis=-1, keepdims=True)
    rest = jnp.where(lane == e1, -jnp.inf, logits)
    t2 = jnp.max(rest, axis=-1, keepdims=True)
    e2 = jnp.min(jnp.where(rest == t2, lane, LANES), axis=-1, keepdims=True)
    x2 = jnp.exp(t2 - t1)
    w1 = 1.0 / (1.0 + x2)
    w2 = x2 / (1.0 + x2)

    oh1 = lane == e1
    oh2 = lane == e2
    cnt = jnp.where(oh1 | oh2, 1.0, 0.0)
    before = _dot(tri_ref[...], cnt.astype(BF16)) + carry_ref[...]
    r1 = jnp.sum(jnp.where(oh1, before, 0.0), axis=-1, keepdims=True)
    r2 = jnp.sum(jnp.where(oh2, before, 0.0), axis=-1, keepdims=True)
    carry_ref[...] += jnp.sum(cnt, axis=0, keepdims=True)

    cols = (e1.astype(F32), e2.astype(F32), w1, w2, r1, r2)
    out = jnp.zeros((tm, LANES), F32)
    for c, val in enumerate(cols):
        out = jnp.where(lane == c, val, out)
    o_ref[...] = out
    ot_ref[...] = out.T[:SUBLANES, :]
    cnt_ref[...] = jnp.broadcast_to(carry_ref[...], cnt_ref.shape)


def moe_router(h, gain, w_router, tm):
    T, D = h.shape
    wr = jnp.zeros((D, LANES), F32).at[:, :N_EXPERTS].set(w_router)
    tri = jnp.tril(jnp.ones((tm, tm), BF16), k=-1)
    return pl.pallas_call(
        _router_kernel,
        grid=(T // tm,),
        in_specs=[pl.BlockSpec((tm, D), lambda i: (i, 0)),
                  pl.BlockSpec((1, D), lambda i: (0, 0)),
                  pl.BlockSpec((D, LANES), lambda i: (0, 0)),
                  pl.BlockSpec((tm, tm), lambda i: (0, 0))],
        out_specs=[pl.BlockSpec((tm, LANES), lambda i: (i, 0)),
                   pl.BlockSpec((SUBLANES, tm), lambda i: (0, i)),
                   pl.BlockSpec((SUBLANES, LANES), lambda i: (0, 0))],
        out_shape=[jax.ShapeDtypeStruct((T, LANES), F32),
                   jax.ShapeDtypeStruct((SUBLANES, T), F32),
                   jax.ShapeDtypeStruct((SUBLANES, LANES), F32)],
        scratch_shapes=[pltpu.VMEM((1, LANES), F32)],
        compiler_params=_cparams(1),
        name="moe_router",
    )(h, gain.reshape(1, D), wr, tri)


def _dispatch_kernel(ze_ref, zn_ref, p1_ref, p2_ref, h_ref, xs_ref, zero_ref, sem, zsem, *, pad_max, tail_max):
    tb = p1_ref.shape[0]
    zr = zero_ref.shape[0]

    @pl.when(pl.program_id(0) == 0)
    def _():
        zero_ref[...] = jnp.zeros_like(zero_ref)
        chunks = [(e, c) for e in range(N_EXPERTS) for c in range(pad_max // zr)]
        chunks += [(N_EXPERTS, c) for c in range(tail_max // zr)]

        def zcopy(e, c):
            start = pl.multiple_of(ze_ref[e] - (c + 1) * zr, zr)
            return pltpu.make_async_copy(zero_ref, xs_ref.at[pl.ds(start, zr), :], zsem)

        def needed(e, c):
            return c * zr < zn_ref[e]

        for e, c in chunks:
            @pl.when(needed(e, c))
            def _():
                zcopy(e, c).start()

        for e, c in chunks:
            @pl.when(needed(e, c))
            def _():
                zcopy(e, c).wait()

    def copy(t, pos):
        return pltpu.make_async_copy(h_ref.at[pl.ds(t, 1), :], xs_ref.at[pl.ds(pos, 1), :], sem)

    def issue(t, c):
        copy(t, p1_ref[t]).start()
        copy(t, p2_ref[t]).start()
        return c

    lax.fori_loop(0, tb, issue, 0, unroll=ROW_DMA_UNROLL)
    for _ in range(TOP_K):
        pltpu.make_async_copy(h_ref, xs_ref.at[pl.ds(0, tb), :], sem).wait()


def moe_dispatch(h, pos1, pos2, pad_end, pad_len, used_rows, n_rows, pad_max, tb, zr=256):
    T, D = h.shape
    tail_max = n_rows - TOP_K * T
    assert tail_max % zr == 0 and pad_max % zr == 0
    zero_end = jnp.concatenate([pad_end, jnp.full((1,), n_rows, I32)])
    zero_len = jnp.concatenate([pad_len, n_rows - used_rows[None]])
    grid_spec = pltpu.PrefetchScalarGridSpec(
        num_scalar_prefetch=2,
        grid=(T // tb,),
        in_specs=[pl.BlockSpec((tb,), lambda i, ps, pn: (i,), memory_space=pltpu.SMEM),
                  pl.BlockSpec((tb,), lambda i, ps, pn: (i,), memory_space=pltpu.SMEM),
                  pl.BlockSpec((tb, D), lambda i, ps, pn: (i, 0))],
        out_specs=pl.BlockSpec(memory_space=pl.ANY),
        scratch_shapes=[pltpu.VMEM((zr, D), h.dtype), pltpu.SemaphoreType.DMA(()), pltpu.SemaphoreType.DMA(())],
    )
    return pl.pallas_call(
        functools.partial(_dispatch_kernel, pad_max=pad_max, tail_max=tail_max),
        grid_spec=grid_spec,
        out_shape=jax.ShapeDtypeStruct((n_rows, D), h.dtype),
        compiler_params=_cparams(1),
        name="moe_dispatch",
    )(zero_end, zero_len, pos1, pos2, h)


def _expert_ffn_kernel(te_ref, tv_ref, x_ref, g_ref, wg_ref, wu_ref, wd_ref, o_ref, xn_ref, *, ts):
    i = pl.program_id(0)
    j = pl.program_id(1)
    valid = tv_ref[i]
    n_sub = x_ref.shape[0] // ts

    n_occ = (valid + ts - 1) // ts
    for k in range(1, n_sub + 1):
        rows = pl.ds(0, k * ts)

        @pl.when(n_occ == k)
        def _():
            @pl.when(j == 0)
            def _():
                xn_ref[rows, :] = (_rms(x_ref[rows, :]) * g_ref[...]).astype(BF16)

            xn = xn_ref[rows, :]
            a = _silu(_dot(xn, wg_ref[...].astype(BF16))) * _dot(xn, wu_ref[...].astype(BF16))
            y = _dot(a.astype(BF16), wd_ref[...].astype(BF16))

            @pl.when(j == 0)
            def _():
                o_ref[rows, :] = y
                if k < n_sub:
                    o_ref[pl.ds(k * ts, (n_sub - k) * ts), :] = jnp.zeros(((n_sub - k) * ts, o_ref.shape[1]), F32)

            @pl.when(j > 0)
            def _():
                o_ref[rows, :] += y

    @pl.when((n_occ == 0) & (j == 0))
    def _():
        o_ref[...] = jnp.zeros_like(o_ref)


def expert_ffn(xs, gain, w_gate, w_up, w_down, layer, tile_expert, tile_valid, n_tiles, tm, ts, tf):
    D = xs.shape[1]
    R = n_tiles * tm
    Fd = w_gate.shape[3]
    nf = Fd // tf

    def jj(i, j, tv):
        return jnp.where(tv[i] > 0, j, nf - 1)

    grid_spec = pltpu.PrefetchScalarGridSpec(
        num_scalar_prefetch=2,
        grid=(n_tiles, nf),
        in_specs=[pl.BlockSpec((tm, D), lambda i, j, te, tv: (jnp.where(tv[i] > 0, i, 0), 0)),
                  pl.BlockSpec((1, D), lambda i, j, te, tv: (0, 0)),
                  pl.BlockSpec((None, None, D, tf), lambda i, j, te, tv: (layer, te[i], 0, jj(i, j, tv))),
                  pl.BlockSpec((None, None, D, tf), lambda i, j, te, tv: (layer, te[i], 0, jj(i, j, tv))),
                  pl.BlockSpec((None, None, tf, D), lambda i, j, te, tv: (layer, te[i], jj(i, j, tv), 0))],
        out_specs=pl.BlockSpec((tm, D), lambda i, j, te, tv: (i, 0)),
        scratch_shapes=[pltpu.VMEM((tm, D), BF16)],
    )
    return pl.pallas_call(
        functools.partial(_expert_ffn_kernel, ts=ts),
        grid_spec=grid_spec,
        out_shape=jax.ShapeDtypeStruct((R, D), F32),
        compiler_params=_cparams(2),
        name="expert_ffn",
    )(tile_expert, tile_valid, xs, gain.reshape(1, D), w_gate, w_up, w_down)


def _combine_kernel(p1_ref, p2_ref, route_ref, h_ref, y_ref, o_ref, b1_ref, b2_ref, sem):
    tb = p1_ref.shape[0]

    def copies(t):
        dst = pl.ds(t, 1)
        return (pltpu.make_async_copy(y_ref.at[pl.ds(p1_ref[t], 1), :], b1_ref.at[dst, :], sem),
                pltpu.make_async_copy(y_ref.at[pl.ds(p2_ref[t], 1), :], b2_ref.at[dst, :], sem))

    def issue(t, c):
        for cp in copies(t):
            cp.start()
        return c

    lax.fori_loop(0, tb, issue, 0, unroll=ROW_DMA_UNROLL)
    for b_ref in (b1_ref, b2_ref):
        pltpu.make_async_copy(y_ref.at[pl.ds(0, tb), :], b_ref, sem).wait()
    w1 = route_ref[:, 2:3]
    w2 = route_ref[:, 3:4]
    o_ref[...] = h_ref[...] + w1 * b1_ref[...] + w2 * b2_ref[...]


def moe_combine(h, y, route, pos1, pos2, tb):
    T, D = h.shape
    smem = lambda: pl.BlockSpec((tb,), lambda i: (i,), memory_space=pltpu.SMEM)
    return pl.pallas_call(
        _combine_kernel,
        grid=(T // tb,),
        in_specs=[smem(), smem(),
                  pl.BlockSpec((tb, LANES), lambda i: (i, 0)),
                  pl.BlockSpec((tb, D), lambda i: (i, 0)),
                  pl.BlockSpec(memory_space=pl.ANY)],
        out_specs=pl.BlockSpec((tb, D), lambda i: (i, 0)),
        out_shape=jax.ShapeDtypeStruct((T, D), F32),
        scratch_shapes=[pltpu.VMEM((tb, D), F32), pltpu.VMEM((tb, D), F32),
                        pltpu.SemaphoreType.DMA(())],
        compiler_params=_cparams(1),
        name="moe_combine",
    )(pos1, pos2, route, h, y)


def moe_ffn(h, gain, w_router, w_gate, w_up, w_down, layer, tm=1024, ts=256, tf=512, tb=512):
    T = h.shape[0]
    route, route_t, counts = moe_router(h, gain, w_router, tm=512)
    e1 = route_t[0].astype(I32)
    e2 = route_t[1].astype(I32)
    counts = counts[0, :N_EXPERTS].astype(I32)
    tiles_per = (counts + tm - 1) // tm
    tile_end = jnp.cumsum(tiles_per)
    tile_start = tile_end - tiles_per
    row_start = tile_start * tm
    pos1 = row_start[e1] + route_t[4].astype(I32)
    pos2 = row_start[e2] + route_t[5].astype(I32)
    n_tiles = (TOP_K * T) // tm + N_EXPERTS
    tile_ids = jnp.arange(n_tiles, dtype=I32)
    tile_expert = jnp.sum(jnp.minimum(tile_ids, tile_end[-1] - 1)[:, None] >= tile_end[None, :], axis=1).astype(I32)
    tile_valid = jnp.clip(counts[tile_expert] - (tile_ids - tile_start[tile_expert]) * tm, 0, tm)
    tile_valid = jnp.where(tile_ids < tile_end[-1], tile_valid, 0).astype(I32)
    pad_len = tiles_per * tm - counts

    xs = moe_dispatch(h, pos1, pos2, tile_end * tm, pad_len, tile_end[-1] * tm, n_tiles * tm, tm, tb)
    y = expert_ffn(xs, gain, w_gate, w_up, w_down, layer, tile_expert, tile_valid, n_tiles, tm, ts, tf)
    return moe_combine(h, y, route, pos1, pos2, tb)


def _pad_heads(a, n_heads, width, pad_to, axis):
    shape = list(a.shape)
    a = a.reshape(shape[:axis] + [n_heads, width] + shape[axis + 1:])
    pads = [(0, 0)] * a.ndim
    pads[axis + 1] = (0, pad_to - width)
    a = jnp.pad(a, pads)
    return a.reshape(shape[:axis] + [n_heads * pad_to] + shape[axis + 1:])


def kernel(x, mem, ln_mix, ln_mem, w_mem_kv, mem_q_gain, mem_k_gain, w_out, ln_ffn, swa_w_in, swa_q_gain, swa_k_gain, swa_sink, ret_w_in, mlstm_w_in, mlstm_conv_w, mlstm_conv_b, mlstm_i_bias, mlstm_f_bias, mlstm_out_gain, ffn_w_gate, ffn_w_up, ffn_w_down, moe_router, moe_w_gate, moe_w_up, moe_w_down):
    B, S, D = x.shape
    M = mem.shape[1]
    T = B * S
    depth = ln_mix.shape[0]
    h = x.reshape(T, D)
    mem2 = mem.reshape(B * M, D)

    for layer in range(depth):
        kind = layer % N_MIXERS
        idx = layer // N_MIXERS
        mk, mv = rms_proj(mem2, ln_mem[layer], w_mem_kv[layer].astype(BF16),
                          ((MEM_WIDTH, BF16, mem_k_gain[layer]), (MEM_WIDTH, BF16, None)),
                          tm=B * M, name="mem_kv_proj")
        w_o = w_out[layer].astype(BF16)
        w_o_mix, w_o_mem = w_o[:MIX_WIDTH], w_o[MIX_WIDTH:]
        qm_out = (MEM_WIDTH, BF16, mem_q_gain[layer])

        if kind == 0:
            kvw = SWA_KV_HEADS * HEAD_DIM
            q, k, v, qm = rms_proj(h, ln_mix[layer], swa_w_in[idx].astype(BF16),
                                   ((MIX_WIDTH, BF16, swa_q_gain[idx]), (kvw, BF16, swa_k_gain[idx]),
                                    (kvw, BF16, None), qm_out), tm=512, name="swa_in_proj")
            mix = swa_attention(q, k, v, swa_sink[idx], B)
        elif kind == 1:
            qkw = RET_HEADS * RET_QK_DIM
            q, k, v, g, qm = rms_proj(h, ln_mix[layer], ret_w_in[idx].astype(BF16),
                                      ((qkw, BF16, None), (qkw, F32, None), (MIX_WIDTH, BF16, None),
                                       (MIX_WIDTH, F32, None), qm_out), tm=512, name="ret_in_proj")
            mix = retention(q, k, v, g, B)
        else:
            H, P, VP = MLSTM_HEADS, MLSTM_QK_PAD, MLSTM_V_PAD
            w = mlstm_w_in[idx]
            qkw = 2 * H * MLSTM_QK_DIM
            o_v, o_og, o_ig = qkw, qkw + MIX_WIDTH, qkw + 2 * MIX_WIDTH
            o_fg, o_qm = o_ig + H, o_ig + 2 * H
            w_gates = jnp.zeros((D, LANES), F32).at[:, :2 * H].set(w[:, o_ig:o_qm])
            w_pad = jnp.concatenate([
                _pad_heads(w[:, :qkw], 2 * H, MLSTM_QK_DIM, P, 1),
                _pad_heads(w[:, o_v:o_og], H, MLSTM_V_DIM, VP, 1),
                _pad_heads(w[:, o_og:o_ig], H, MLSTM_V_DIM, VP, 1),
                w_gates, w[:, o_qm:]], axis=1).astype(BF16)
            qk, v, og, gates, qm = rms_proj(h, ln_mix[layer], w_pad,
                                            ((2 * H * P, F32, None), (H * VP, BF16, None), (H * VP, F32, None),
                                             (LANES, F32, None), qm_out), tm=512, name="mlstm_in_proj")
            gates_t = gates[:, :SUBLANES].reshape(B, S, SUBLANES).transpose(0, 2, 1)
            bias = jnp.concatenate([mlstm_i_bias[idx], mlstm_f_bias[idx]])
            bias_c = jnp.zeros((1, LANES), F32).at[0, :2 * H].set(bias)
            bias_r = jnp.broadcast_to(bias[:, None], (SUBLANES, MLSTM_CHUNK))
            mix = mlstm(qk, v, og, gates, gates_t,
                        _pad_heads(mlstm_conv_w[idx], 2 * H, MLSTM_QK_DIM, P, 1),
                        _pad_heads(mlstm_conv_b[idx][None], 2 * H, MLSTM_QK_DIM, P, 1),
                        bias_c, bias_r,
                        _pad_heads(mlstm_out_gain[idx][None], H, MLSTM_V_DIM, VP, 1), B)
            w_o_mix = _pad_heads(w_o_mix, H, MLSTM_V_DIM, VP, 0)

        h = out_proj(h, mix, qm, mk, mv, w_o_mix, w_o_mem, B, tm=512)
        j = layer // 2
        if layer % 2 == 0:
            h = dense_ffn(h, ln_ffn[layer], ffn_w_gate[j].astype(BF16), ffn_w_up[j].astype(BF16),
                          ffn_w_down[j].astype(BF16), tm=1024, tf=1792)
        else:
            h = moe_ffn(h, ln_ffn[layer], moe_router[j], moe_w_gate, moe_w_up, moe_w_down, j)
    return h.reshape(B, S, D)
```

```python
import functools
import math

import jax
import jax.numpy as jnp
from jax import lax
from jax.experimental import pallas as pl
from jax.experimental.pallas import tpu as pltpu

F32 = jnp.float32
BF16 = jnp.bfloat16
I32 = jnp.int32

D_MODEL = 1024
N_MIXERS = 3
HEAD_DIM = 64
MEM_HEADS = 4
MEM_HEAD_DIM = 64
MEM_WIDTH = MEM_HEADS * MEM_HEAD_DIM
MIX_WIDTH = D_MODEL - MEM_WIDTH
RMS_EPS = 1e-6
NEG_INF = -1e30

SWA_Q_HEADS = MIX_WIDTH // HEAD_DIM
SWA_KV_HEADS = 4
SWA_GROUP = SWA_Q_HEADS // SWA_KV_HEADS
SWA_WINDOW = 128
SWA_BLOCK = 128

RET_HEADS = 6
RET_QK_DIM = 64
RET_V_DIM = MIX_WIDTH // RET_HEADS
RET_CHUNK = 128

MLSTM_HEADS = 4
MLSTM_V_DIM = MIX_WIDTH // MLSTM_HEADS
MLSTM_QK_DIM = MLSTM_V_DIM // 2
MLSTM_CHUNK = 128
MLSTM_CONV = 4
MLSTM_QK_PAD = 128
MLSTM_V_PAD = 256

N_EXPERTS = 8
TOP_K = 2

LANES = 128
SUBLANES = 8
VMEM_LIMIT = 56 * 1024 * 1024
ROW_DMA_UNROLL = 8


def _cparams(n_axes):
    return pltpu.CompilerParams(dimension_semantics=("arbitrary",) * n_axes,
                                vmem_limit_bytes=VMEM_LIMIT)


def _rms(x, eps=RMS_EPS):
    return x * lax.rsqrt(jnp.mean(x * x, axis=-1, keepdims=True) + eps)


def _dot(a, b):
    return jnp.dot(a, b, preferred_element_type=F32)


def _dot_nt(a, b):
    return lax.dot_general(a, b, (((1,), (1,)), ((), ())), preferred_element_type=F32)


def _dot_tn(a, b):
    return lax.dot_general(a, b, (((0,), (0,)), ((), ())), preferred_element_type=F32)


def _silu(x):
    return x * jax.nn.sigmoid(x)


NORM_CHUNK = 256


def _head_rms(y, seg_ref, hg):
    sq = y * y
    hi = sq.astype(BF16)
    lo = (sq - hi.astype(F32)).astype(BF16)
    ms = _dot(hi, seg_ref[...]) + _dot(lo, seg_ref[...])
    return y * lax.rsqrt(ms + RMS_EPS) * hg


def _rms_proj_kernel(x_ref, g_ref, w_ref, seg_ref, *refs, widths, normed):
    n_gain = sum(normed)
    gain_refs, o_refs = refs[:n_gain], refs[n_gain:]
    xn = (_rms(x_ref[...]) * g_ref[...]).astype(BF16)
    off = 0
    gi = 0
    for o_ref, wd, nrm in zip(o_refs, widths, normed):
        if nrm:
            hg = gain_refs[gi][...]
            gi += 1
            for c in range(0, wd, NORM_CHUNK):
                y = _dot(xn, w_ref[:, off + c:off + c + NORM_CHUNK])
                o_ref[:, c:c + NORM_CHUNK] = _head_rms(y, seg_ref, hg).astype(o_ref.dtype)
        else:
            o_ref[...] = _dot(xn, w_ref[:, off:off + wd]).astype(o_ref.dtype)
        off += wd


def rms_proj(x, gain, w, outs, tm, name):
    T, D = x.shape
    N = w.shape[1]
    widths = tuple(o[0] for o in outs)
    normed = tuple(o[2] is not None for o in outs)
    head_gains = [jnp.tile(o[2], NORM_CHUNK // HEAD_DIM).reshape(1, NORM_CHUNK) for o in outs if o[2] is not None]
    assert sum(widths) == N and T % tm == 0
    assert all(wd % NORM_CHUNK == 0 for wd, nrm in zip(widths, normed) if nrm)
    head_of = jnp.arange(NORM_CHUNK) // HEAD_DIM
    seg = jnp.where(head_of[:, None] == head_of[None, :], 1.0 / HEAD_DIM, 0.0).astype(BF16)
    return pl.pallas_call(
        functools.partial(_rms_proj_kernel, widths=widths, normed=normed),
        grid=(T // tm,),
        in_specs=[pl.BlockSpec((tm, D), lambda i: (i, 0)),
                  pl.BlockSpec((1, D), lambda i: (0, 0)),
                  pl.BlockSpec((D, N), lambda i: (0, 0)),
                  pl.BlockSpec((NORM_CHUNK, NORM_CHUNK), lambda i: (0, 0))]
                 + [pl.BlockSpec((1, NORM_CHUNK), lambda i: (0, 0))] * len(head_gains),
        out_specs=[pl.BlockSpec((tm, wd), lambda i: (i, 0)) for wd in widths],
        out_shape=[jax.ShapeDtypeStruct((T, o[0]), o[1]) for o in outs],
        compiler_params=_cparams(1),
        name=name,
    )(x, gain.reshape(1, D), w, seg, *head_gains)


def _swa_kernel(sink_ref, q_ref, kc_ref, kp_ref, vc_ref, vp_ref, bias_ref, o_ref):
    n = pl.program_id(1)
    L = SWA_BLOCK
    n_blk = q_ref.shape[0] // L
    key = lax.broadcasted_iota(I32, (L, L), 0)
    qry = lax.broadcasted_iota(I32, (L, L), 1)
    from_prev = key > qry
    own = key <= qry
    zero = jnp.zeros((L, L), BF16)
    for blk in range(n_blk):
        rows = slice(blk * L, (blk + 1) * L)
        prows = slice((blk - 1) * L, blk * L)
        outs = []
        for g in range(SWA_KV_HEADS):
            ksl = slice(g * HEAD_DIM, (g + 1) * HEAD_DIM)
            k_prev = kp_ref[:, ksl] if blk == 0 else kc_ref[prows, ksl]
            v_prev = vp_ref[:, ksl] if blk == 0 else vc_ref[prows, ksl]
            k = jnp.concatenate([k_prev, kc_ref[rows, ksl]], axis=0)
            v = jnp.concatenate([v_prev, vc_ref[rows, ksl]], axis=0)
            for h in range(g * SWA_GROUP, (g + 1) * SWA_GROUP):
                sink = sink_ref[h]
                s2 = _dot_nt(k, q_ref[rows, h * HEAD_DIM:(h + 1) * HEAD_DIM])
                s = jnp.where(from_prev, s2[:L], s2[L:]) * (HEAD_DIM ** -0.5) - bias_ref[h]
                if blk == 0:
                    s = jnp.where(jnp.logical_or(n > 0, own), s, NEG_INF)
                m = jnp.maximum(jnp.max(s, axis=0, keepdims=True), sink)
                e = jnp.exp(s - m)
                denom = jnp.sum(e, axis=0, keepdims=True) + jnp.exp(sink - m)
                p = (e * (1.0 / denom)).astype(BF16)
                p2 = jnp.concatenate([jnp.where(from_prev, p, zero), jnp.where(from_prev, zero, p)], axis=0)
                outs.append(_dot_tn(p2, v))
        o_ref[rows, :] = jnp.concatenate(outs, axis=-1).astype(o_ref.dtype)


def swa_attention(q, k, v, sink, batch, n_blk=4):
    T = q.shape[0]
    L = SWA_BLOCK
    nb = T // batch // L
    kvw = SWA_KV_HEADS * HEAD_DIM
    slopes = jnp.exp2(-8.0 * jnp.arange(1, SWA_Q_HEADS + 1, dtype=F32) / SWA_Q_HEADS)
    assert SWA_WINDOW == L
    qi, kj = jnp.arange(L)[:, None], jnp.arange(L)[None, :]
    dist = jnp.where(kj > qi, qi + L - kj, qi - kj).astype(F32)
    bias = slopes[:, None, None] * dist.T
    ns = nb // n_blk
    slab = n_blk * L
    cur = lambda b, n: (b * ns + n, 0)
    prev = lambda b, n: (b * nb + jnp.maximum(n * n_blk - 1, 0), 0)
    return pl.pallas_call(
        _swa_kernel,
        grid=(batch, ns),
        in_specs=[pl.BlockSpec(memory_space=pltpu.SMEM),
                  pl.BlockSpec((slab, MIX_WIDTH), cur),
                  pl.BlockSpec((slab, kvw), cur), pl.BlockSpec((L, kvw), prev),
                  pl.BlockSpec((slab, kvw), cur), pl.BlockSpec((L, kvw), prev),
                  pl.BlockSpec((SWA_Q_HEADS, L, L), lambda b, n: (0, 0, 0))],
        out_specs=pl.BlockSpec((slab, MIX_WIDTH), cur),
        out_shape=jax.ShapeDtypeStruct((T, MIX_WIDTH), BF16),
        compiler_params=_cparams(2),
        name="swa_attention",
    )(sink, q, k, k, v, v, bias)


def _retention_kernel(q_ref, k_ref, v_ref, g_ref, dec_ref, qd_ref, kd_ref, cd_ref, o_ref, state_ref):
    @pl.when(pl.program_id(0) == 0)
    def _():
        state_ref[...] = jnp.zeros_like(state_ref)

    L = RET_CHUNK
    n_ch = q_ref.shape[1] // L
    for b in range(q_ref.shape[0]):
        outs = [[] for _ in range(n_ch)]
        for h in range(RET_HEADS):
            qsl = slice(h * RET_QK_DIM, (h + 1) * RET_QK_DIM)
            vsl = slice(h * RET_V_DIM, (h + 1) * RET_V_DIM)
            state = state_ref[b, h]
            for c in range(n_ch):
                rows = slice(c * L, (c + 1) * L)
                q = q_ref[b, rows, qsl]
                k = k_ref[b, rows, qsl] * (RET_QK_DIM ** -0.5)
                v = v_ref[b, rows, vsl]
                s = _dot_nt(q, k.astype(BF16)) * dec_ref[h]
                o = _dot(s.astype(BF16), v) + _dot(q, state.astype(BF16)) * qd_ref[h]
                state = state * cd_ref[h] + _dot_tn((k * kd_ref[h]).astype(BF16), v)
                outs[c].append(_silu(g_ref[b, rows, vsl]) * _rms(o))
            state_ref[b, h] = state
        for c in range(n_ch):
            o_ref[b, c * L:(c + 1) * L, :] = jnp.concatenate(outs[c], axis=-1).astype(o_ref.dtype)


def _retention_consts():
    H, L = RET_HEADS, RET_CHUNK
    log_gamma = jnp.log1p(-jnp.exp2(-5.0 - jnp.arange(H, dtype=F32)))
    pos = jnp.arange(L, dtype=F32)
    rel = pos[:, None] - pos[None, :]
    intra = jnp.exp(jnp.where(rel >= 0, log_gamma[:, None, None] * rel, -jnp.inf))
    q_decay = jnp.exp(log_gamma[:, None] * (pos + 1.0))[..., None]
    k_decay = jnp.exp(log_gamma[:, None] * (L - 1.0 - pos))[..., None]
    chunk_decay = jnp.exp(log_gamma * L)[:, None, None]
    return (intra,
            jnp.broadcast_to(q_decay, (H, L, RET_V_DIM)),
            jnp.broadcast_to(k_decay, (H, L, RET_QK_DIM)),
            jnp.broadcast_to(chunk_decay, (H, RET_QK_DIM, RET_V_DIM)))


def retention(q, k, v, g, batch, n_ch=4):
    T = q.shape[0]
    L = RET_CHUNK
    nc = T // batch // L
    H = RET_HEADS
    intra, qd, kd, cd = _retention_consts()
    cur = lambda c: (0, c, 0)
    const = lambda c: (0, 0, 0)
    qkw = H * RET_QK_DIM
    per_batch = lambda a: a.reshape(batch, T // batch, a.shape[1])
    slab = n_ch * L
    out = pl.pallas_call(
        _retention_kernel,
        grid=(nc // n_ch,),
        in_specs=[pl.BlockSpec((batch, slab, qkw), cur), pl.BlockSpec((batch, slab, qkw), cur),
                  pl.BlockSpec((batch, slab, MIX_WIDTH), cur), pl.BlockSpec((batch, slab, MIX_WIDTH), cur),
                  pl.BlockSpec((H, L, L), const), pl.BlockSpec((H, L, RET_V_DIM), const),
                  pl.BlockSpec((H, L, RET_QK_DIM), const), pl.BlockSpec((H, RET_QK_DIM, RET_V_DIM), const)],
        out_specs=pl.BlockSpec((batch, slab, MIX_WIDTH), cur),
        out_shape=jax.ShapeDtypeStruct((batch, T // batch, MIX_WIDTH), BF16),
        scratch_shapes=[pltpu.VMEM((batch, H, RET_QK_DIM, RET_V_DIM), F32)],
        compiler_params=_cparams(1),
        name="retention",
    )(per_batch(q), per_batch(k), per_batch(v), per_batch(g), intra, qd, kd, cd)
    return out.reshape(T, MIX_WIDTH)


def _split3(x):
    x1 = x.astype(BF16)
    r1 = x - x1.astype(F32)
    x2 = r1.astype(BF16)
    x3 = (r1 - x2.astype(F32)).astype(BF16)
    return x1, x2, x3


def _mlstm_kernel(qk_ref, v_ref, og_ref, gc_ref, gr_ref, cw_ref, cb_ref, bc_ref, br_ref, gain_ref,
                  tri_ref, trit_ref, o_ref, xbuf_ref, c_ref, n_ref, m_ref):
    for b in range(qk_ref.shape[0]):
        _mlstm_chunk(qk_ref.at[b], v_ref.at[b], og_ref.at[b], gc_ref.at[b], gr_ref.at[b], cw_ref, cb_ref,
                     bc_ref, br_ref, gain_ref, tri_ref, trit_ref, o_ref.at[b],
                     xbuf_ref.at[b], c_ref.at[b], n_ref.at[b], m_ref.at[b])


def _mlstm_chunk(qk_ref, v_ref, og_ref, gc_ref, gr_ref, cw_ref, cb_ref, bc_ref, br_ref, gain_ref,
                 tri_ref, trit_ref, o_ref, xbuf_ref, c_ref, n_ref, m_ref):
    L = MLSTM_CHUNK
    H = MLSTM_HEADS
    P = MLSTM_QK_PAD
    VP = MLSTM_V_PAD
    KT = SUBLANES

    @pl.when(pl.program_id(0) == 0)
    def _():
        xbuf_ref[0:KT, :] = jnp.zeros((KT, xbuf_ref.shape[1]), F32)
        c_ref[...] = jnp.zeros_like(c_ref)
        n_ref[...] = jnp.zeros_like(n_ref)
        m_ref[...] = jnp.zeros_like(m_ref)

    xbuf_ref[KT:KT + L, :] = qk_ref[...]
    acc = cb_ref[...] + cw_ref[MLSTM_CONV - 1:MLSTM_CONV, :] * xbuf_ref[KT:KT + L, :]
    for j in range(MLSTM_CONV - 1):
        sh = MLSTM_CONV - 1 - j
        acc = acc + cw_ref[j:j + 1, :] * xbuf_ref[KT - sh:KT - sh + L, :]
    xbuf_ref[0:KT, :] = qk_ref[L - KT:L, :]
    qk = _silu(acc)

    xc = gc_ref[...] + bc_ref[...]
    xr = gr_ref[...] + br_ref[...]
    lfc = jax.nn.log_sigmoid(xc)
    lfr = jax.nn.log_sigmoid(xr)
    tri = tri_ref[...]
    trit = trit_ref[...]
    bc = sum(_dot(tri, t) for t in _split3(lfc))
    br = sum(_dot(t, trit) for t in _split3(lfr))

    rowi = lax.broadcasted_iota(I32, (L, L), 0)
    coli = lax.broadcasted_iota(I32, (L, L), 1)
    causal = rowi >= coli
    outs = []
    for h in range(H):
        q = qk[:, h * P:(h + 1) * P].astype(BF16)
        k = qk[:, (H + h) * P:(H + h + 1) * P] * (MLSTM_QK_DIM ** -0.5)
        v = v_ref[:, h * VP:(h + 1) * VP]
        li_c = xc[:, h:h + 1]
        b_c = bc[:, H + h:H + h + 1]
        li_r = xr[h:h + 1, :]
        b_r = br[H + h:H + h + 1, :]
        g = b_c[L - 1:L, :]
        m = m_ref[h:h + 1, 0:1]
        C = c_ref[h]
        nvec = n_ref[h:h + 1, :]

        dmat = jnp.where(causal, b_c - b_r + li_r, -jnp.inf)
        inter = b_c + m
        m_t = jnp.maximum(inter, jnp.max(dmat, axis=-1, keepdims=True))
        w = jnp.exp(dmat - m_t)
        a = jnp.exp(inter - m_t)
        s = _dot_nt(q, k.astype(BF16)) * w
        num = _dot(s.astype(BF16), v) + a * _dot(q, C.astype(BF16))
        qf = qk[:, h * P:(h + 1) * P]
        den = jnp.sum(s, axis=-1, keepdims=True) + a * jnp.sum(qf * nvec, axis=-1, keepdims=True)
        hh = num / jnp.maximum(jnp.abs(den), jnp.exp(-m_t))

        u_c = g - b_c + li_c
        u_r = g - b_r + li_r
        m_new = jnp.maximum(g + m, jnp.max(u_r, axis=-1, keepdims=True))
        wk = jnp.exp(u_c - m_new)
        decay = jnp.exp(g + m - m_new)
        kw = k * wk
        c_ref[h] = decay * C + _dot_tn(kw.astype(BF16), v)
        n_ref[h:h + 1, :] = decay * nvec + jnp.sum(kw, axis=0, keepdims=True)
        m_ref[h:h + 1, :] = jnp.broadcast_to(m_new, (1, m_ref.shape[1]))

        ms = jnp.sum(hh * hh, axis=-1, keepdims=True) * (1.0 / MLSTM_V_DIM)
        hc = hh * lax.rsqrt(ms + RMS_EPS) * gain_ref[:, h * VP:(h + 1) * VP]
        outs.append(jax.nn.sigmoid(og_ref[:, h * VP:(h + 1) * VP]) * hc)
    o_ref[...] = jnp.concatenate(outs, axis=-1).astype(o_ref.dtype)


def mlstm(qk, v, og, gates, gates_t, conv_w, conv_b, bias_c, bias_r, gain, batch):
    T = qk.shape[0]
    L = MLSTM_CHUNK
    nc = T // batch // L
    H, P, VP = MLSTM_HEADS, MLSTM_QK_PAD, MLSTM_V_PAD
    tri = jnp.tril(jnp.ones((L, L), BF16))
    cur = lambda c: (0, c, 0)
    c2 = lambda c: (0, 0)
    per_batch = lambda a: a.reshape(batch, T // batch, a.shape[1])
    out = pl.pallas_call(
        _mlstm_kernel,
        grid=(nc,),
        in_specs=[pl.BlockSpec((batch, L, 2 * H * P), cur), pl.BlockSpec((batch, L, H * VP), cur),
                  pl.BlockSpec((batch, L, H * VP), cur), pl.BlockSpec((batch, L, LANES), cur),
                  pl.BlockSpec((batch, SUBLANES, L), lambda c: (0, 0, c)),
                  pl.BlockSpec((MLSTM_CONV, 2 * H * P), c2), pl.BlockSpec((1, 2 * H * P), c2),
                  pl.BlockSpec((1, LANES), c2), pl.BlockSpec((SUBLANES, L), c2),
                  pl.BlockSpec((1, H * VP), c2),
                  pl.BlockSpec((L, L), c2), pl.BlockSpec((L, L), c2)],
        out_specs=pl.BlockSpec((batch, L, H * VP), cur),
        out_shape=jax.ShapeDtypeStruct((batch, T // batch, H * VP), BF16),
        scratch_shapes=[pltpu.VMEM((batch, SUBLANES + L, 2 * H * P), F32),
                        pltpu.VMEM((batch, H, P, VP), F32),
                        pltpu.VMEM((batch, SUBLANES, P), F32),
                        pltpu.VMEM((batch, SUBLANES, LANES), F32)],
        compiler_params=_cparams(1),
        name="mlstm",
    )(per_batch(qk), per_batch(v), per_batch(og), per_batch(gates), gates_t,
      conv_w, conv_b, bias_c, bias_r, gain, tri, tri.T)
    return out.reshape(T, H * VP)


def _out_proj_kernel(h_ref, mix_ref, q_ref, k_ref, v_ref, w1_ref, w2_ref, o_ref):
    outs = []
    for hd in range(MEM_HEADS):
        sl = slice(hd * MEM_HEAD_DIM, (hd + 1) * MEM_HEAD_DIM)
        s = _dot_nt(q_ref[:, sl], k_ref[:, sl]) * (MEM_HEAD_DIM ** -0.5)
        e = jnp.exp(s - jnp.max(s, axis=-1, keepdims=True))
        outs.append(_dot(e.astype(BF16), v_ref[:, sl]) / jnp.sum(e, axis=-1, keepdims=True))
    mem_out = jnp.concatenate(outs, axis=-1).astype(BF16)
    o_ref[...] = h_ref[...] + _dot(mix_ref[...], w1_ref[...]) + _dot(mem_out, w2_ref[...])


def out_proj(h, mix, qm, mk, mv, w_mix, w_mem, batch, tm):
    T, D = h.shape
    Wm = mix.shape[1]
    M = mk.shape[0] // batch
    nt = T // batch // tm
    rows = lambda b, i: (b * nt + i, 0)
    mem = lambda b, i: (b, 0)
    const = lambda b, i: (0, 0)
    return pl.pallas_call(
        _out_proj_kernel,
        grid=(batch, nt),
        in_specs=[pl.BlockSpec((tm, D), rows),
                  pl.BlockSpec((tm, Wm), rows),
                  pl.BlockSpec((tm, MEM_WIDTH), rows),
                  pl.BlockSpec((M, MEM_WIDTH), mem),
                  pl.BlockSpec((M, MEM_WIDTH), mem),
                  pl.BlockSpec((Wm, D), const),
                  pl.BlockSpec((MEM_WIDTH, D), const)],
        out_specs=pl.BlockSpec((tm, D), rows),
        out_shape=jax.ShapeDtypeStruct((T, D), F32),
        compiler_params=_cparams(2),
        name="out_proj",
    )(h, mix, qm, mk, mv, w_mix, w_mem)


def _dense_ffn_kernel(h_ref, g_ref, wg_ref, wu_ref, wd_ref, o_ref, xn_ref):
    @pl.when(pl.program_id(1) == 0)
    def _():
        h = h_ref[...]
        xn_ref[...] = (_rms(h) * g_ref[...]).astype(BF16)
        o_ref[...] = h

    xn = xn_ref[...]
    a = _silu(_dot(xn, wg_ref[...])) * _dot(xn, wu_ref[...])
    o_ref[...] += _dot(a.astype(BF16), wd_ref[...])


def dense_ffn(h, gain, w_gate, w_up, w_down, tm, tf):
    T, D = h.shape
    Fd = w_gate.shape[1]
    return pl.pallas_call(
        _dense_ffn_kernel,
        grid=(T // tm, Fd // tf),
        in_specs=[pl.BlockSpec((tm, D), lambda i, j: (i, 0)),
                  pl.BlockSpec((1, D), lambda i, j: (0, 0)),
                  pl.BlockSpec((D, tf), lambda i, j: (0, j)),
                  pl.BlockSpec((D, tf), lambda i, j: (0, j)),
                  pl.BlockSpec((tf, D), lambda i, j: (j, 0))],
        out_specs=pl.BlockSpec((tm, D), lambda i, j: (i, 0)),
        out_shape=jax.ShapeDtypeStruct((T, D), F32),
        scratch_shapes=[pltpu.VMEM((tm, D), BF16)],
        compiler_params=_cparams(2),
        name="dense_ffn",
    )(h, gain.reshape(1, D), w_gate, w_up, w_down)


def _router_kernel(h_ref, g_ref, wr_ref, tri_ref, o_ref, ot_ref, cnt_ref, carry_ref):
    i = pl.program_id(0)

    @pl.when(i == 0)
    def _():
        carry_ref[...] = jnp.zeros_like(carry_ref)

    tm = h_ref.shape[0]
    xn = _rms(h_ref[...]) * g_ref[...]
    x_hi = xn.astype(BF16)
    x_lo = (xn - x_hi.astype(F32)).astype(BF16)
    wr = wr_ref[...]
    w_hi = wr.astype(BF16)
    w_lo = (wr - w_hi.astype(F32)).astype(BF16)
    logits = _dot(x_hi, w_hi) + (_dot(x_hi, w_lo) + _dot(x_lo, w_hi))
    lane = lax.broadcasted_iota(I32, (tm, LANES), 1)
    logits = jnp.where(lane < N_EXPERTS, logits, -jnp.inf)
    t1 = jnp.max(logits, axis=-1, keepdims=True)
    e1 = jnp.min(jnp.where(logits == t1, lane, LANES), axis=-1, keepdims=True)
    rest = jnp.where(lane == e1, -jnp.inf, logits)
    t2 = jnp.max(rest, axis=-1, keepdims=True)
    e2 = jnp.min(jnp.where(rest == t2, lane, LANES), axis=-1, keepdims=True)
    x2 = jnp.exp(t2 - t1)
    w1 = 1.0 / (1.0 + x2)
    w2 = x2 / (1.0 + x2)

    oh1 = lane == e1
    oh2 = lane == e2
    cnt = jnp.where(oh1 | oh2, 1.0, 0.0)
    before = _dot(tri_ref[...], cnt.astype(BF16)) + carry_ref[...]
    r1 = jnp.sum(jnp.where(oh1, before, 0.0), axis=-1, keepdims=True)
    r2 = jnp.sum(jnp.where(oh2, before, 0.0), axis=-1, keepdims=True)
    carry_ref[...] += jnp.sum(cnt, axis=0, keepdims=True)

    cols = (e1.astype(F32), e2.astype(F32), w1, w2, r1, r2)
    out = jnp.zeros((tm, LANES), F32)
    for c, val in enumerate(cols):
        out = jnp.where(lane == c, val, out)
    o_ref[...] = out
    ot_ref[...] = out.T[:SUBLANES, :]
    cnt_ref[...] = jnp.broadcast_to(carry_ref[...], cnt_ref.shape)


def moe_router(h, gain, w_router, tm):
    T, D = h.shape
    wr = jnp.zeros((D, LANES), F32).at[:, :N_EXPERTS].set(w_router)
    tri = jnp.tril(jnp.ones((tm, tm), BF16), k=-1)
    return pl.pallas_call(
        _router_kernel,
        grid=(T // tm,),
        in_specs=[pl.BlockSpec((tm, D), lambda i: (i, 0)),
                  pl.BlockSpec((1, D), lambda i: (0, 0)),
                  pl.BlockSpec((D, LANES), lambda i: (0, 0)),
                  pl.BlockSpec((tm, tm), lambda i: (0, 0))],
        out_specs=[pl.BlockSpec((tm, LANES), lambda i: (i, 0)),
                   pl.BlockSpec((SUBLANES, tm), lambda i: (0, i)),
                   pl.BlockSpec((SUBLANES, LANES), lambda i: (0, 0))],
        out_shape=[jax.ShapeDtypeStruct((T, LANES), F32),
                   jax.ShapeDtypeStruct((SUBLANES, T), F32),
                   jax.ShapeDtypeStruct((SUBLANES, LANES), F32)],
        scratch_shapes=[pltpu.VMEM((1, LANES), F32)],
        compiler_params=_cparams(1),
        name="moe_router",
    )(h, gain.reshape(1, D), wr, tri)


def _dispatch_kernel(ze_ref, zn_ref, p1_ref, p2_ref, h_ref, xs_ref, zero_ref, sem, zsem, *, pad_max, tail_max):
    tb = p1_ref.shape[0]
    zr = zero_ref.shape[0]

    @pl.when(pl.program_id(0) == 0)
    def _():
        zero_ref[...] = jnp.zeros_like(zero_ref)
        chunks = [(e, c) for e in range(N_EXPERTS) for c in range(pad_max // zr)]
        chunks += [(N_EXPERTS, c) for c in range(tail_max // zr)]

        def zcopy(e, c):
            start = pl.multiple_of(ze_ref[e] - (c + 1) * zr, zr)
            return pltpu.make_async_copy(zero_ref, xs_ref.at[pl.ds(start, zr), :], zsem)

        def needed(e, c):
            return c * zr < zn_ref[e]

        for e, c in chunks:
            @pl.when(needed(e, c))
            def _():
                zcopy(e, c).start()

        for e, c in chunks:
            @pl.when(needed(e, c))
            def _():
                zcopy(e, c).wait()

    def copy(t, pos):
        return pltpu.make_async_copy(h_ref.at[pl.ds(t, 1), :], xs_ref.at[pl.ds(pos, 1), :], sem)

    def issue(t, c):
        copy(t, p1_ref[t]).start()
        copy(t, p2_ref[t]).start()
        return c

    lax.fori_loop(0, tb, issue, 0, unroll=ROW_DMA_UNROLL)
    for _ in range(TOP_K):
        pltpu.make_async_copy(h_ref, xs_ref.at[pl.ds(0, tb), :], sem).wait()


def moe_dispatch(h, pos1, pos2, pad_end, pad_len, used_rows, n_rows, pad_max, tb, zr=256):
    T, D = h.shape
    tail_max = n_rows - TOP_K * T
    assert tail_max % zr == 0 and pad_max % zr == 0
    zero_end = jnp.concatenate([pad_end, jnp.full((1,), n_rows, I32)])
    zero_len = jnp.concatenate([pad_len, n_rows - used_rows[None]])
    grid_spec = pltpu.PrefetchScalarGridSpec(
        num_scalar_prefetch=2,
        grid=(T // tb,),
        in_specs=[pl.BlockSpec((tb,), lambda i, ps, pn: (i,), memory_space=pltpu.SMEM),
                  pl.BlockSpec((tb,), lambda i, ps, pn: (i,), memory_space=pltpu.SMEM),
                  pl.BlockSpec((tb, D), lambda i, ps, pn: (i, 0))],
        out_specs=pl.BlockSpec(memory_space=pl.ANY),
        scratch_shapes=[pltpu.VMEM((zr, D), h.dtype), pltpu.SemaphoreType.DMA(()), pltpu.SemaphoreType.DMA(())],
    )
    return pl.pallas_call(
        functools.partial(_dispatch_kernel, pad_max=pad_max, tail_max=tail_max),
        grid_spec=grid_spec,
        out_shape=jax.ShapeDtypeStruct((n_rows, D), h.dtype),
        compiler_params=_cparams(1),
        name="moe_dispatch",
    )(zero_end, zero_len, pos1, pos2, h)


def _expert_ffn_kernel(te_ref, tv_ref, x_ref, g_ref, wg_ref, wu_ref, wd_ref, o_ref, xn_ref, *, ts):
    i = pl.program_id(0)
    j = pl.program_id(1)
    valid = tv_ref[i]
    n_sub = x_ref.shape[0] // ts

    n_occ = (valid + ts - 1) // ts
    for k in range(1, n_sub + 1):
        rows = pl.ds(0, k * ts)

        @pl.when(n_occ == k)
        def _():
            @pl.when(j == 0)
            def _():
                xn_ref[rows, :] = (_rms(x_ref[rows, :]) * g_ref[...]).astype(BF16)

            xn = xn_ref[rows, :]
            a = _silu(_dot(xn, wg_ref[...].astype(BF16))) * _dot(xn, wu_ref[...].astype(BF16))
            y = _dot(a.astype(BF16), wd_ref[...].astype(BF16))

            @pl.when(j == 0)
            def _():
                o_ref[rows, :] = y
                if k < n_sub:
                    o_ref[pl.ds(k * ts, (n_sub - k) * ts), :] = jnp.zeros(((n_sub - k) * ts, o_ref.shape[1]), F32)

            @pl.when(j > 0)
            def _():
                o_ref[rows, :] += y

    @pl.when((n_occ == 0) & (j == 0))
    def _():
        o_ref[...] = jnp.zeros_like(o_ref)


def expert_ffn(xs, gain, w_gate, w_up, w_down, layer, tile_expert, tile_valid, n_tiles, tm, ts, tf):
    D = xs.shape[1]
    R = n_tiles * tm
    Fd = w_gate.shape[3]
    nf = Fd // tf

    def jj(i, j, tv):
        return jnp.where(tv[i] > 0, j, nf - 1)

    grid_spec = pltpu.PrefetchScalarGridSpec(
        num_scalar_prefetch=2,
        grid=(n_tiles, nf),
        in_specs=[pl.BlockSpec((tm, D), lambda i, j, te, tv: (jnp.where(tv[i] > 0, i, 0), 0)),
                  pl.BlockSpec((1, D), lambda i, j, te, tv: (0, 0)),
                  pl.BlockSpec((None, None, D, tf), lambda i, j, te, tv: (layer, te[i], 0, jj(i, j, tv))),
                  pl.BlockSpec((None, None, D, tf), lambda i, j, te, tv: (layer, te[i], 0, jj(i, j, tv))),
                  pl.BlockSpec((None, None, tf, D), lambda i, j, te, tv: (layer, te[i], jj(i, j, tv), 0))],
        out_specs=pl.BlockSpec((tm, D), lambda i, j, te, tv: (i, 0)),
        scratch_shapes=[pltpu.VMEM((tm, D), BF16)],
    )
    return pl.pallas_call(
        functools.partial(_expert_ffn_kernel, ts=ts),
        grid_spec=grid_spec,
        out_shape=jax.ShapeDtypeStruct((R, D), F32),
        compiler_params=_cparams(2),
        name="expert_ffn",
    )(tile_expert, tile_valid, xs, gain.reshape(1, D), w_gate, w_up, w_down)


def _combine_kernel(p1_ref, p2_ref, route_ref, h_ref, y_ref, o_ref, b1_ref, b2_ref, sem):
    tb = p1_ref.shape[0]

    def copies(t):
        dst = pl.ds(t, 1)
        return (pltpu.make_async_copy(y_ref.at[pl.ds(p1_ref[t], 1), :], b1_ref.at[dst, :], sem),
                pltpu.make_async_copy(y_ref.at[pl.ds(p2_ref[t], 1), :], b2_ref.at[dst, :], sem))

    def issue(t, c):
        for cp in copies(t):
            cp.start()
        return c

    lax.fori_loop(0, tb, issue, 0, unroll=ROW_DMA_UNROLL)
    for b_ref in (b1_ref, b2_ref):
        pltpu.make_async_copy(y_ref.at[pl.ds(0, tb), :], b_ref, sem).wait()
    w1 = route_ref[:, 2:3]
    w2 = route_ref[:, 3:4]
    o_ref[...] = h_ref[...] + w1 * b1_ref[...] + w2 * b2_ref[...]


def moe_combine(h, y, route, pos1, pos2, tb):
    T, D = h.shape
    smem = lambda: pl.BlockSpec((tb,), lambda i: (i,), memory_space=pltpu.SMEM)
    return pl.pallas_call(
        _combine_kernel,
        grid=(T // tb,),
        in_specs=[smem(), smem(),
                  pl.BlockSpec((tb, LANES), lambda i: (i, 0)),
                  pl.BlockSpec((tb, D), lambda i: (i, 0)),
                  pl.BlockSpec(memory_space=pl.ANY)],
        out_specs=pl.BlockSpec((tb, D), lambda i: (i, 0)),
        out_shape=jax.ShapeDtypeStruct((T, D), F32),
        scratch_shapes=[pltpu.VMEM((tb, D), F32), pltpu.VMEM((tb, D), F32),
                        pltpu.SemaphoreType.DMA(())],
        compiler_params=_cparams(1),
        name="moe_combine",
    )(pos1, pos2, route, h, y)


def moe_ffn(h, gain, w_router, w_gate, w_up, w_down, layer, tm=1024, ts=256, tf=512, tb=512):
    T = h.shape[0]
    route, route_t, counts = moe_router(h, gain, w_router, tm=512)
    e1 = route_t[0].astype(I32)
    e2 = route_t[1].astype(I32)
    counts = counts[0, :N_EXPERTS].astype(I32)
    tiles_per = (counts + tm - 1) // tm
    tile_end = jnp.cumsum(tiles_per)
    tile_start = tile_end - tiles_per
    row_start = tile_start * tm
    pos1 = row_start[e1] + route_t[4].astype(I32)
    pos2 = row_start[e2] + route_t[5].astype(I32)
    n_tiles = (TOP_K * T) // tm + N_EXPERTS
    tile_ids = jnp.arange(n_tiles, dtype=I32)
    tile_expert = jnp.sum(jnp.minimum(tile_ids, tile_end[-1] - 1)[:, None] >= tile_end[None, :], axis=1).astype(I32)
    tile_valid = jnp.clip(counts[tile_expert] - (tile_ids - tile_start[tile_expert]) * tm, 0, tm)
    tile_valid = jnp.where(tile_ids < tile_end[-1], tile_valid, 0).astype(I32)
    pad_len = tiles_per * tm - counts

    xs = moe_dispatch(h, pos1, pos2, tile_end * tm, pad_len, tile_end[-1] * tm, n_tiles * tm, tm, tb)
    y = expert_ffn(xs, gain, w_gate, w_up, w_down, layer, tile_expert, tile_valid, n_tiles, tm, ts, tf)
    return moe_combine(h, y, route, pos1, pos2, tb)


def _pad_heads(a, n_heads, width, pad_to, axis):
    shape = list(a.shape)
    a = a.reshape(shape[:axis] + [n_heads, width] + shape[axis + 1:])
    pads = [(0, 0)] * a.ndim
    pads[axis + 1] = (0, pad_to - width)
    a = jnp.pad(a, pads)
    return a.reshape(shape[:axis] + [n_heads * pad_to] + shape[axis + 1:])


def kernel(x, mem, ln_mix, ln_mem, w_mem_kv, mem_q_gain, mem_k_gain, w_out, ln_ffn, swa_w_in, swa_q_gain, swa_k_gain, swa_sink, ret_w_in, mlstm_w_in, mlstm_conv_w, mlstm_conv_b, mlstm_i_bias, mlstm_f_bias, mlstm_out_gain, ffn_w_gate, ffn_w_up, ffn_w_down, moe_router, moe_w_gate, moe_w_up, moe_w_down):
    B, S, D = x.shape
    M = mem.shape[1]
    T = B * S
    depth = ln_mix.shape[0]
    h = x.reshape(T, D)
    mem2 = mem.reshape(B * M, D)

    for layer in range(depth):
        kind = layer % N_MIXERS
        idx = layer // N_MIXERS
        mk, mv = rms_proj(mem2, ln_mem[layer], w_mem_kv[layer].astype(BF16),
                          ((MEM_WIDTH, BF16, mem_k_gain[layer]), (MEM_WIDTH, BF16, None)),
                          tm=B * M, name="mem_kv_proj")
        w_o = w_out[layer].astype(BF16)
        w_o_mix, w_o_mem = w_o[:MIX_WIDTH], w_o[MIX_WIDTH:]
        qm_out = (MEM_WIDTH, BF16, mem_q_gain[layer])

        if kind == 0:
            kvw = SWA_KV_HEADS * HEAD_DIM
            q, k, v, qm = rms_proj(h, ln_mix[layer], swa_w_in[idx].astype(BF16),
                                   ((MIX_WIDTH, BF16, swa_q_gain[idx]), (kvw, BF16, swa_k_gain[idx]),
                                    (kvw, BF16, None), qm_out), tm=1024, name="swa_in_proj")
            mix = swa_attention(q, k, v, swa_sink[idx], B)
        elif kind == 1:
            qkw = RET_HEADS * RET_QK_DIM
            q, k, v, g, qm = rms_proj(h, ln_mix[layer], ret_w_in[idx].astype(BF16),
                                      ((qkw, BF16, None), (qkw, F32, None), (MIX_WIDTH, BF16, None),
                                       (MIX_WIDTH, F32, None), qm_out), tm=1024, name="ret_in_proj")
            mix = retention(q, k, v, g, B)
        else:
            H, P, VP = MLSTM_HEADS, MLSTM_QK_PAD, MLSTM_V_PAD
            w = mlstm_w_in[idx]
            qkw = 2 * H * MLSTM_QK_DIM
            o_v, o_og, o_ig = qkw, qkw + MIX_WIDTH, qkw + 2 * MIX_WIDTH
            o_fg, o_qm = o_ig + H, o_ig + 2 * H
            w_gates = jnp.zeros((D, LANES), F32).at[:, :2 * H].set(w[:, o_ig:o_qm])
            w_pad = jnp.concatenate([
                _pad_heads(w[:, :qkw], 2 * H, MLSTM_QK_DIM, P, 1),
                _pad_heads(w[:, o_v:o_og], H, MLSTM_V_DIM, VP, 1),
                _pad_heads(w[:, o_og:o_ig], H, MLSTM_V_DIM, VP, 1),
                w_gates, w[:, o_qm:]], axis=1).astype(BF16)
            qk, v, og, gates, qm = rms_proj(h, ln_mix[layer], w_pad,
                                            ((2 * H * P, F32, None), (H * VP, BF16, None), (H * VP, F32, None),
                                             (LANES, F32, None), qm_out), tm=1024, name="mlstm_in_proj")
            gates_t = gates[:, :SUBLANES].reshape(B, S, SUBLANES).transpose(0, 2, 1)
            bias = jnp.concatenate([mlstm_i_bias[idx], mlstm_f_bias[idx]])
            bias_c = jnp.zeros((1, LANES), F32).at[0, :2 * H].set(bias)
            bias_r = jnp.broadcast_to(bias[:, None], (SUBLANES, MLSTM_CHUNK))
            mix = mlstm(qk, v, og, gates, gates_t,
                        _pad_heads(mlstm_conv_w[idx], 2 * H, MLSTM_QK_DIM, P, 1),
                        _pad_heads(mlstm_conv_b[idx][None], 2 * H, MLSTM_QK_DIM, P, 1),
                        bias_c, bias_r,
                        _pad_heads(mlstm_out_gain[idx][None], H, MLSTM_V_DIM, VP, 1), B)
            w_o_mix = _pad_heads(w_o_mix, H, MLSTM_V_DIM, VP, 0)

        h = out_proj(h, mix, qm, mk, mv, w_o_mix, w_o_mem, B, tm=1024)
        j = layer // 2
        if layer % 2 == 0:
            h = dense_ffn(h, ln_ffn[layer], ffn_w_gate[j].astype(BF16), ffn_w_up[j].astype(BF16),
                          ffn_w_down[j].astype(BF16), tm=1024, tf=1792)
        else:
            h = moe_ffn(h, ln_ffn[layer], moe_router[j], moe_w_gate, moe_w_up, moe_w_down, j)
    return h.reshape(B, S, D)
```

```python
import functools
import math

import jax
import jax.numpy as jnp
from jax import lax
from jax.experimental import pallas as pl
from jax.experimental.pallas import tpu as pltpu

F32 = jnp.float32
BF16 = jnp.bfloat16
I32 = jnp.int32

D_MODEL = 1024
N_MIXERS = 3
HEAD_DIM = 64
MEM_HEADS = 4
MEM_HEAD_DIM = 64
MEM_WIDTH = MEM_HEADS * MEM_HEAD_DIM
MIX_WIDTH = D_MODEL - MEM_WIDTH
RMS_EPS = 1e-6
NEG_INF = -1e30

SWA_Q_HEADS = MIX_WIDTH // HEAD_DIM
SWA_KV_HEADS = 4
SWA_GROUP = SWA_Q_HEADS // SWA_KV_HEADS
SWA_WINDOW = 128
SWA_BLOCK = 128

RET_HEADS = 6
RET_QK_DIM = 64
RET_V_DIM = MIX_WIDTH // RET_HEADS
RET_CHUNK = 128

MLSTM_HEADS = 4
MLSTM_V_DIM = MIX_WIDTH // MLSTM_HEADS
MLSTM_QK_DIM = MLSTM_V_DIM // 2
MLSTM_CHUNK = 128
MLSTM_CONV = 4
MLSTM_QK_PAD = 128
MLSTM_V_PAD = 256

N_EXPERTS = 8
TOP_K = 2

LANES = 128
SUBLANES = 8
VMEM_LIMIT = 56 * 1024 * 1024
ROW_DMA_UNROLL = 8


def _cparams(n_axes):
    return pltpu.CompilerParams(dimension_semantics=("arbitrary",) * n_axes,
                                vmem_limit_bytes=VMEM_LIMIT)


def _rms(x, eps=RMS_EPS):
    return x * lax.rsqrt(jnp.mean(x * x, axis=-1, keepdims=True) + eps)


def _dot(a, b):
    return jnp.dot(a, b, preferred_element_type=F32)


def _dot_nt(a, b):
    return lax.dot_general(a, b, (((1,), (1,)), ((), ())), preferred_element_type=F32)


def _dot_tn(a, b):
    return lax.dot_general(a, b, (((0,), (0,)), ((), ())), preferred_element_type=F32)


def _silu(x):
    return x * jax.nn.sigmoid(x)


NORM_CHUNK = 256


def _head_rms(y, seg_ref, hg):
    sq = y * y
    hi = sq.astype(BF16)
    lo = (sq - hi.astype(F32)).astype(BF16)
    ms = _dot(hi, seg_ref[...]) + _dot(lo, seg_ref[...])
    return y * lax.rsqrt(ms + RMS_EPS) * hg


def _rms_proj_kernel(x_ref, g_ref, w_ref, seg_ref, *refs, widths, normed):
    n_gain = sum(normed)
    gain_refs, o_refs = refs[:n_gain], refs[n_gain:]
    xn = (_rms(x_ref[...]) * g_ref[...]).astype(BF16)
    off = 0
    gi = 0
    for o_ref, wd, nrm in zip(o_refs, widths, normed):
        if nrm:
            hg = gain_refs[gi][...]
            gi += 1
            for c in range(0, wd, NORM_CHUNK):
                y = _dot(xn, w_ref[:, off + c:off + c + NORM_CHUNK])
                o_ref[:, c:c + NORM_CHUNK] = _head_rms(y, seg_ref, hg).astype(o_ref.dtype)
        else:
            o_ref[...] = _dot(xn, w_ref[:, off:off + wd]).astype(o_ref.dtype)
        off += wd


def rms_proj(x, gain, w, outs, tm, name):
    T, D = x.shape
    N = w.shape[1]
    widths = tuple(o[0] for o in outs)
    normed = tuple(o[2] is not None for o in outs)
    head_gains = [jnp.tile(o[2], NORM_CHUNK // HEAD_DIM).reshape(1, NORM_CHUNK) for o in outs if o[2] is not None]
    assert sum(widths) == N and T % tm == 0
    assert all(wd % NORM_CHUNK == 0 for wd, nrm in zip(widths, normed) if nrm)
    head_of = jnp.arange(NORM_CHUNK) // HEAD_DIM
    seg = jnp.where(head_of[:, None] == head_of[None, :], 1.0 / HEAD_DIM, 0.0).astype(BF16)
    return pl.pallas_call(
        functools.partial(_rms_proj_kernel, widths=widths, normed=normed),
        grid=(T // tm,),
        in_specs=[pl.BlockSpec((tm, D), lambda i: (i, 0)),
                  pl.BlockSpec((1, D), lambda i: (0, 0)),
                  pl.BlockSpec((D, N), lambda i: (0, 0)),
                  pl.BlockSpec((NORM_CHUNK, NORM_CHUNK), lambda i: (0, 0))]
                 + [pl.BlockSpec((1, NORM_CHUNK), lambda i: (0, 0))] * len(head_gains),
        out_specs=[pl.BlockSpec((tm, wd), lambda i: (i, 0)) for wd in widths],
        out_shape=[jax.ShapeDtypeStruct((T, o[0]), o[1]) for o in outs],
        compiler_params=_cparams(1),
        name=name,
    )(x, gain.reshape(1, D), w, seg, *head_gains)


def _swa_kernel(sink_ref, q_ref, kc_ref, kp_ref, vc_ref, vp_ref, bias_ref, o_ref):
    n = pl.program_id(1)
    L = SWA_BLOCK
    n_blk = q_ref.shape[0] // L
    key = lax.broadcasted_iota(I32, (L, L), 0)
    qry = lax.broadcasted_iota(I32, (L, L), 1)
    from_prev = key > qry
    own = key <= qry
    zero = jnp.zeros((L, L), BF16)
    for blk in range(n_blk):
        rows = slice(blk * L, (blk + 1) * L)
        prows = slice((blk - 1) * L, blk * L)
        outs = []
        for g in range(SWA_KV_HEADS):
            ksl = slice(g * HEAD_DIM, (g + 1) * HEAD_DIM)
            k_prev = kp_ref[:, ksl] if blk == 0 else kc_ref[prows, ksl]
            v_prev = vp_ref[:, ksl] if blk == 0 else vc_ref[prows, ksl]
            k = jnp.concatenate([k_prev, kc_ref[rows, ksl]], axis=0)
            v = jnp.concatenate([v_prev, vc_ref[rows, ksl]], axis=0)
            for h in range(g * SWA_GROUP, (g + 1) * SWA_GROUP):
                sink = sink_ref[h]
                s2 = _dot_nt(k, q_ref[rows, h * HEAD_DIM:(h + 1) * HEAD_DIM])
                s = jnp.where(from_prev, s2[:L], s2[L:]) * (HEAD_DIM ** -0.5) - bias_ref[h]
                if blk == 0:
                    s = jnp.where(jnp.logical_or(n > 0, own), s, NEG_INF)
                m = jnp.maximum(jnp.max(s, axis=0, keepdims=True), sink)
                e = jnp.exp(s - m)
                denom = jnp.sum(e, axis=0, keepdims=True) + jnp.exp(sink - m)
                p = (e * (1.0 / denom)).astype(BF16)
                p2 = jnp.concatenate([jnp.where(from_prev, p, zero), jnp.where(from_prev, zero, p)], axis=0)
                outs.append(_dot_tn(p2, v))
        o_ref[rows, :] = jnp.concatenate(outs, axis=-1).astype(o_ref.dtype)


def swa_attention(q, k, v, sink, batch, n_blk=4):
    T = q.shape[0]
    L = SWA_BLOCK
    nb = T // batch // L
    kvw = SWA_KV_HEADS * HEAD_DIM
    slopes = jnp.exp2(-8.0 * jnp.arange(1, SWA_Q_HEADS + 1, dtype=F32) / SWA_Q_HEADS)
    assert SWA_WINDOW == L
    qi, kj = jnp.arange(L)[:, None], jnp.arange(L)[None, :]
    dist = jnp.where(kj > qi, qi + L - kj, qi - kj).astype(F32)
    bias = slopes[:, None, None] * dist.T
    ns = nb // n_blk
    slab = n_blk * L
    cur = lambda b, n: (b * ns + n, 0)
    prev = lambda b, n: (b * nb + jnp.maximum(n * n_blk - 1, 0), 0)
    return pl.pallas_call(
        _swa_kernel,
        grid=(batch, ns),
        in_specs=[pl.BlockSpec(memory_space=pltpu.SMEM),
                  pl.BlockSpec((slab, MIX_WIDTH), cur),
                  pl.BlockSpec((slab, kvw), cur), pl.BlockSpec((L, kvw), prev),
                  pl.BlockSpec((slab, kvw), cur), pl.BlockSpec((L, kvw), prev),
                  pl.BlockSpec((SWA_Q_HEADS, L, L), lambda b, n: (0, 0, 0))],
        out_specs=pl.BlockSpec((slab, MIX_WIDTH), cur),
        out_shape=jax.ShapeDtypeStruct((T, MIX_WIDTH), BF16),
        compiler_params=_cparams(2),
        name="swa_attention",
    )(sink, q, k, k, v, v, bias)


def _retention_kernel(q_ref, k_ref, v_ref, g_ref, dec_ref, qd_ref, kd_ref, cd_ref, o_ref, state_ref):
    @pl.when(pl.program_id(0) == 0)
    def _():
        state_ref[...] = jnp.zeros_like(state_ref)

    L = RET_CHUNK
    n_ch = q_ref.shape[1] // L
    for b in range(q_ref.shape[0]):
        outs = [[] for _ in range(n_ch)]
        for h in range(RET_HEADS):
            qsl = slice(h * RET_QK_DIM, (h + 1) * RET_QK_DIM)
            vsl = slice(h * RET_V_DIM, (h + 1) * RET_V_DIM)
            state = state_ref[b, h]
            for c in range(n_ch):
                rows = slice(c * L, (c + 1) * L)
                q = q_ref[b, rows, qsl]
                k = k_ref[b, rows, qsl] * (RET_QK_DIM ** -0.5)
                v = v_ref[b, rows, vsl]
                s = _dot_nt(q, k.astype(BF16)) * dec_ref[h]
                o = _dot(s.astype(BF16), v) + _dot(q, state.astype(BF16)) * qd_ref[h]
                state = state * cd_ref[h] + _dot_tn((k * kd_ref[h]).astype(BF16), v)
                outs[c].append(_silu(g_ref[b, rows, vsl]) * _rms(o))
            state_ref[b, h] = state
        for c in range(n_ch):
            o_ref[b, c * L:(c + 1) * L, :] = jnp.concatenate(outs[c], axis=-1).astype(o_ref.dtype)


def _retention_consts():
    H, L = RET_HEADS, RET_CHUNK
    log_gamma = jnp.log1p(-jnp.exp2(-5.0 - jnp.arange(H, dtype=F32)))
    pos = jnp.arange(L, dtype=F32)
    rel = pos[:, None] - pos[None, :]
    intra = jnp.exp(jnp.where(rel >= 0, log_gamma[:, None, None] * rel, -jnp.inf))
    q_decay = jnp.exp(log_gamma[:, None] * (pos + 1.0))[..., None]
    k_decay = jnp.exp(log_gamma[:, None] * (L - 1.0 - pos))[..., None]
    chunk_decay = jnp.exp(log_gamma * L)[:, None, None]
    return (intra,
            jnp.broadcast_to(q_decay, (H, L, RET_V_DIM)),
            jnp.broadcast_to(k_decay, (H, L, RET_QK_DIM)),
            jnp.broadcast_to(chunk_decay, (H, RET_QK_DIM, RET_V_DIM)))


def retention(q, k, v, g, batch, n_ch=4):
    T = q.shape[0]
    L = RET_CHUNK
    nc = T // batch // L
    H = RET_HEADS
    intra, qd, kd, cd = _retention_consts()
    cur = lambda c: (0, c, 0)
    const = lambda c: (0, 0, 0)
    qkw = H * RET_QK_DIM
    per_batch = lambda a: a.reshape(batch, T // batch, a.shape[1])
    slab = n_ch * L
    out = pl.pallas_call(
        _retention_kernel,
        grid=(nc // n_ch,),
        in_specs=[pl.BlockSpec((batch, slab, qkw), cur), pl.BlockSpec((batch, slab, qkw), cur),
                  pl.BlockSpec((batch, slab, MIX_WIDTH), cur), pl.BlockSpec((batch, slab, MIX_WIDTH), cur),
                  pl.BlockSpec((H, L, L), const), pl.BlockSpec((H, L, RET_V_DIM), const),
                  pl.BlockSpec((H, L, RET_QK_DIM), const), pl.BlockSpec((H, RET_QK_DIM, RET_V_DIM), const)],
        out_specs=pl.BlockSpec((batch, slab, MIX_WIDTH), cur),
        out_shape=jax.ShapeDtypeStruct((batch, T // batch, MIX_WIDTH), BF16),
        scratch_shapes=[pltpu.VMEM((batch, H, RET_QK_DIM, RET_V_DIM), F32)],
        compiler_params=_cparams(1),
        name="retention",
    )(per_batch(q), per_batch(k), per_batch(v), per_batch(g), intra, qd, kd, cd)
    return out.reshape(T, MIX_WIDTH)


def _split3(x):
    x1 = x.astype(BF16)
    r1 = x - x1.astype(F32)
    x2 = r1.astype(BF16)
    x3 = (r1 - x2.astype(F32)).astype(BF16)
    return x1, x2, x3


def _mlstm_kernel(qk_ref, v_ref, og_ref, gc_ref, gr_ref, cw_ref, cb_ref, bc_ref, br_ref, gain_ref,
                  tri_ref, trit_ref, o_ref, xbuf_ref, c_ref, n_ref, m_ref):
    for b in range(qk_ref.shape[0]):
        _mlstm_chunk(qk_ref.at[b], v_ref.at[b], og_ref.at[b], gc_ref.at[b], gr_ref.at[b], cw_ref, cb_ref,
                     bc_ref, br_ref, gain_ref, tri_ref, trit_ref, o_ref.at[b],
                     xbuf_ref.at[b], c_ref.at[b], n_ref.at[b], m_ref.at[b])


def _mlstm_chunk(qk_ref, v_ref, og_ref, gc_ref, gr_ref, cw_ref, cb_ref, bc_ref, br_ref, gain_ref,
                 tri_ref, trit_ref, o_ref, xbuf_ref, c_ref, n_ref, m_ref):
    L = MLSTM_CHUNK
    H = MLSTM_HEADS
    P = MLSTM_QK_PAD
    VP = MLSTM_V_PAD
    KT = SUBLANES

    @pl.when(pl.program_id(0) == 0)
    def _():
        xbuf_ref[0:KT, :] = jnp.zeros((KT, xbuf_ref.shape[1]), F32)
        c_ref[...] = jnp.zeros_like(c_ref)
        n_ref[...] = jnp.zeros_like(n_ref)
        m_ref[...] = jnp.zeros_like(m_ref)

    xbuf_ref[KT:KT + L, :] = qk_ref[...]
    acc = cb_ref[...] + cw_ref[MLSTM_CONV - 1:MLSTM_CONV, :] * xbuf_ref[KT:KT + L, :]
    for j in range(MLSTM_CONV - 1):
        sh = MLSTM_CONV - 1 - j
        acc = acc + cw_ref[j:j + 1, :] * xbuf_ref[KT - sh:KT - sh + L, :]
    xbuf_ref[0:KT, :] = qk_ref[L - KT:L, :]
    qk = _silu(acc)

    xc = gc_ref[...] + bc_ref[...]
    xr = gr_ref[...] + br_ref[...]
    lfc = jax.nn.log_sigmoid(xc)
    lfr = jax.nn.log_sigmoid(xr)
    tri = tri_ref[...]
    trit = trit_ref[...]
    bc = sum(_dot(tri, t) for t in _split3(lfc))
    br = sum(_dot(t, trit) for t in _split3(lfr))

    rowi = lax.broadcasted_iota(I32, (L, L), 0)
    coli = lax.broadcasted_iota(I32, (L, L), 1)
    causal = rowi >= coli
    outs = []
    for h in range(H):
        q = qk[:, h * P:(h + 1) * P].astype(BF16)
        k = qk[:, (H + h) * P:(H + h + 1) * P] * (MLSTM_QK_DIM ** -0.5)
        v = v_ref[:, h * VP:(h + 1) * VP]
        li_c = xc[:, h:h + 1]
        b_c = bc[:, H + h:H + h + 1]
        li_r = xr[h:h + 1, :]
        b_r = br[H + h:H + h + 1, :]
        g = b_c[L - 1:L, :]
        m = m_ref[h:h + 1, 0:1]
        C = c_ref[h]
        nvec = n_ref[h:h + 1, :]

        dmat = jnp.where(causal, b_c - b_r + li_r, -jnp.inf)
        inter = b_c + m
        m_t = jnp.maximum(inter, jnp.max(dmat, axis=-1, keepdims=True))
        w = jnp.exp(dmat - m_t)
        a = jnp.exp(inter - m_t)
        s = _dot_nt(q, k.astype(BF16)) * w
        num = _dot(s.astype(BF16), v) + a * _dot(q, C.astype(BF16))
        qf = qk[:, h * P:(h + 1) * P]
        den = jnp.sum(s, axis=-1, keepdims=True) + a * jnp.sum(qf * nvec, axis=-1, keepdims=True)
        hh = num / jnp.maximum(jnp.abs(den), jnp.exp(-m_t))

        u_c = g - b_c + li_c
        u_r = g - b_r + li_r
        m_new = jnp.maximum(g + m, jnp.max(u_r, axis=-1, keepdims=True))
        wk = jnp.exp(u_c - m_new)
        decay = jnp.exp(g + m - m_new)
        kw = k * wk
        c_ref[h] = decay * C + _dot_tn(kw.astype(BF16), v)
        n_ref[h:h + 1, :] = decay * nvec + jnp.sum(kw, axis=0, keepdims=True)
        m_ref[h:h + 1, :] = jnp.broadcast_to(m_new, (1, m_ref.shape[1]))

        ms = jnp.sum(hh * hh, axis=-1, keepdims=True) * (1.0 / MLSTM_V_DIM)
        hc = hh * lax.rsqrt(ms + RMS_EPS) * gain_ref[:, h * VP:(h + 1) * VP]
        outs.append(jax.nn.sigmoid(og_ref[:, h * VP:(h + 1) * VP]) * hc)
    o_ref[...] = jnp.concatenate(outs, axis=-1).astype(o_ref.dtype)


def mlstm(qk, v, og, gates, gates_t, conv_w, conv_b, bias_c, bias_r, gain, batch):
    T = qk.shape[0]
    L = MLSTM_CHUNK
    nc = T // batch // L
    H, P, VP = MLSTM_HEADS, MLSTM_QK_PAD, MLSTM_V_PAD
    tri = jnp.tril(jnp.ones((L, L), BF16))
    cur = lambda c: (0, c, 0)
    c2 = lambda c: (0, 0)
    per_batch = lambda a: a.reshape(batch, T // batch, a.shape[1])
    out = pl.pallas_call(
        _mlstm_kernel,
        grid=(nc,),
        in_specs=[pl.BlockSpec((batch, L, 2 * H * P), cur), pl.BlockSpec((batch, L, H * VP), cur),
                  pl.BlockSpec((batch, L, H * VP), cur), pl.BlockSpec((batch, L, LANES), cur),
                  pl.BlockSpec((batch, SUBLANES, L), lambda c: (0, 0, c)),
                  pl.BlockSpec((MLSTM_CONV, 2 * H * P), c2), pl.BlockSpec((1, 2 * H * P), c2),
                  pl.BlockSpec((1, LANES), c2), pl.BlockSpec((SUBLANES, L), c2),
                  pl.BlockSpec((1, H * VP), c2),
                  pl.BlockSpec((L, L), c2), pl.BlockSpec((L, L), c2)],
        out_specs=pl.BlockSpec((batch, L, H * VP), cur),
        out_shape=jax.ShapeDtypeStruct((batch, T // batch, H * VP), BF16),
        scratch_shapes=[pltpu.VMEM((batch, SUBLANES + L, 2 * H * P), F32),
                        pltpu.VMEM((batch, H, P, VP), F32),
                        pltpu.VMEM((batch, SUBLANES, P), F32),
                        pltpu.VMEM((batch, SUBLANES, LANES), F32)],
        compiler_params=_cparams(1),
        name="mlstm",
    )(per_batch(qk), per_batch(v), per_batch(og), per_batch(gates), gates_t,
      conv_w, conv_b, bias_c, bias_r, gain, tri, tri.T)
    return out.reshape(T, H * VP)


def _out_proj_kernel(h_ref, mix_ref, q_ref, k_ref, v_ref, w1_ref, w2_ref, o_ref):
    outs = []
    for hd in range(MEM_HEADS):
        sl = slice(hd * MEM_HEAD_DIM, (hd + 1) * MEM_HEAD_DIM)
        s = _dot_nt(k_ref[:, sl], q_ref[:, sl]) * (MEM_HEAD_DIM ** -0.5)
        e = jnp.exp(s - jnp.max(s, axis=0, keepdims=True))
        p = (e * (1.0 / jnp.sum(e, axis=0, keepdims=True))).astype(BF16)
        outs.append(_dot_tn(p, v_ref[:, sl]))
    mem_out = jnp.concatenate(outs, axis=-1).astype(BF16)
    o_ref[...] = h_ref[...] + _dot(mix_ref[...], w1_ref[...]) + _dot(mem_out, w2_ref[...])


def out_proj(h, mix, qm, mk, mv, w_mix, w_mem, batch, tm):
    T, D = h.shape
    Wm = mix.shape[1]
    M = mk.shape[0] // batch
    nt = T // batch // tm
    rows = lambda b, i: (b * nt + i, 0)
    mem = lambda b, i: (b, 0)
    const = lambda b, i: (0, 0)
    return pl.pallas_call(
        _out_proj_kernel,
        grid=(batch, nt),
        in_specs=[pl.BlockSpec((tm, D), rows),
                  pl.BlockSpec((tm, Wm), rows),
                  pl.BlockSpec((tm, MEM_WIDTH), rows),
                  pl.BlockSpec((M, MEM_WIDTH), mem),
                  pl.BlockSpec((M, MEM_WIDTH), mem),
                  pl.BlockSpec((Wm, D), const),
                  pl.BlockSpec((MEM_WIDTH, D), const)],
        out_specs=pl.BlockSpec((tm, D), rows),
        out_shape=jax.ShapeDtypeStruct((T, D), F32),
        compiler_params=_cparams(2),
        name="out_proj",
    )(h, mix, qm, mk, mv, w_mix, w_mem)


def _dense_ffn_kernel(h_ref, g_ref, wg_ref, wu_ref, wd_ref, o_ref, xn_ref):
    @pl.when(pl.program_id(1) == 0)
    def _():
        h = h_ref[...]
        xn_ref[...] = (_rms(h) * g_ref[...]).astype(BF16)
        o_ref[...] = h

    xn = xn_ref[...]
    a = _silu(_dot(xn, wg_ref[...])) * _dot(xn, wu_ref[...])
    o_ref[...] += _dot(a.astype(BF16), wd_ref[...])


def dense_ffn(h, gain, w_gate, w_up, w_down, tm, tf):
    T, D = h.shape
    Fd = w_gate.shape[1]
    return pl.pallas_call(
        _dense_ffn_kernel,
        grid=(T // tm, Fd // tf),
        in_specs=[pl.BlockSpec((tm, D), lambda i, j: (i, 0)),
                  pl.BlockSpec((1, D), lambda i, j: (0, 0)),
                  pl.BlockSpec((D, tf), lambda i, j: (0, j)),
                  pl.BlockSpec((D, tf), lambda i, j: (0, j)),
                  pl.BlockSpec((tf, D), lambda i, j: (j, 0))],
        out_specs=pl.BlockSpec((tm, D), lambda i, j: (i, 0)),
        out_shape=jax.ShapeDtypeStruct((T, D), F32),
        scratch_shapes=[pltpu.VMEM((tm, D), BF16)],
        compiler_params=_cparams(2),
        name="dense_ffn",
    )(h, gain.reshape(1, D), w_gate, w_up, w_down)


def _router_kernel(h_ref, g_ref, wr_ref, tri_ref, o_ref, ot_ref, cnt_ref, carry_ref):
    i = pl.program_id(0)

    @pl.when(i == 0)
    def _():
        carry_ref[...] = jnp.zeros_like(carry_ref)

    tm = h_ref.shape[0]
    xn = _rms(h_ref[...]) * g_ref[...]
    x_hi = xn.astype(BF16)
    x_lo = (xn - x_hi.astype(F32)).astype(BF16)
    wr = wr_ref[...]
    w_hi = wr.astype(BF16)
    w_lo = (wr - w_hi.astype(F32)).astype(BF16)
    logits = _dot(x_hi, w_hi) + (_dot(x_hi, w_lo) + _dot(x_lo, w_hi))
    lane = lax.broadcasted_iota(I32, (tm, LANES), 1)
    logits = jnp.where(lane < N_EXPERTS, logits, -jnp.inf)
    t1 = jnp.max(logits, axis=-1, keepdims=True)
    e1 = jnp.min(jnp.where(logits == t1, lane, LANES), axis=-1, keepdims=True)
    rest = jnp.where(lane == e1, -jnp.inf, logits)
    t2 = jnp.max(rest, axis=-1, keepdims=True)
    e2 = jnp.min(jnp.where(rest == t2, lane, LANES), axis=-1, keepdims=True)
    x2 = jnp.exp(t2 - t1)
    w1 = 1.0 / (1.0 + x2)
    w2 = x2 / (1.0 + x2)

    oh1 = lane == e1
    oh2 = lane == e2
    cnt = jnp.where(oh1 | oh2, 1.0, 0.0)
    before = _dot(tri_ref[...], cnt.astype(BF16)) + carry_ref[...]
    r1 = jnp.sum(jnp.where(oh1, before, 0.0), axis=-1, keepdims=True)
    r2 = jnp.sum(jnp.where(oh2, before, 0.0), axis=-1, keepdims=True)
    carry_ref[...] += jnp.sum(cnt, axis=0, keepdims=True)

    cols = (e1.astype(F32), e2.astype(F32), w1, w2, r1, r2)
    out = jnp.zeros((tm, LANES), F32)
    for c, val in enumerate(cols):
        out = jnp.where(lane == c, val, out)
    o_ref[...] = out
    ot_ref[...] = out.T[:SUBLANES, :]
    cnt_ref[...] = jnp.broadcast_to(carry_ref[...], cnt_ref.shape)


def moe_router(h, gain, w_router, tm):
    T, D = h.shape
    wr = jnp.zeros((D, LANES), F32).at[:, :N_EXPERTS].set(w_router)
    tri = jnp.tril(jnp.ones((tm, tm), BF16), k=-1)
    return pl.pallas_call(
        _router_kernel,
        grid=(T // tm,),
        in_specs=[pl.BlockSpec((tm, D), lambda i: (i, 0)),
                  pl.BlockSpec((1, D), lambda i: (0, 0)),
                  pl.BlockSpec((D, LANES), lambda i: (0, 0)),
                  pl.BlockSpec((tm, tm), lambda i: (0, 0))],
        out_specs=[pl.BlockSpec((tm, LANES), lambda i: (i, 0)),
                   pl.BlockSpec((SUBLANES, tm), lambda i: (0, i)),
                   pl.BlockSpec((SUBLANES, LANES), lambda i: (0, 0))],
        out_shape=[jax.ShapeDtypeStruct((T, LANES), F32),
                   jax.ShapeDtypeStruct((SUBLANES, T), F32),
                   jax.ShapeDtypeStruct((SUBLANES, LANES), F32)],
        scratch_shapes=[pltpu.VMEM((1, LANES), F32)],
        compiler_params=_cparams(1),
        name="moe_router",
    )(h, gain.reshape(1, D), wr, tri)


def _dispatch_kernel(ze_ref, zn_ref, p1_ref, p2_ref, h_ref, xs_ref, zero_ref, sem, zsem, *, pad_max, tail_max):
    tb = p1_ref.shape[0]
    zr = zero_ref.shape[0]

    @pl.when(pl.program_id(0) == 0)
    def _():
        zero_ref[...] = jnp.zeros_like(zero_ref)
        chunks = [(e, c) for e in range(N_EXPERTS) for c in range(pad_max // zr)]
        chunks += [(N_EXPERTS, c) for c in range(tail_max // zr)]

        def zcopy(e, c):
            start = pl.multiple_of(ze_ref[e] - (c + 1) * zr, zr)
            return pltpu.make_async_copy(zero_ref, xs_ref.at[pl.ds(start, zr), :], zsem)

        def needed(e, c):
            return c * zr < zn_ref[e]

        for e, c in chunks:
            @pl.when(needed(e, c))
            def _():
                zcopy(e, c).start()

        for e, c in chunks:
            @pl.when(needed(e, c))
            def _():
                zcopy(e, c).wait()

    def copy(t, pos):
        return pltpu.make_async_copy(h_ref.at[pl.ds(t, 1), :], xs_ref.at[pl.ds(pos, 1), :], sem)

    def issue(t, c):
        copy(t, p1_ref[t]).start()
        copy(t, p2_ref[t]).start()
        return c

    lax.fori_loop(0, tb, issue, 0, unroll=ROW_DMA_UNROLL)
    for _ in range(TOP_K):
        pltpu.make_async_copy(h_ref, xs_ref.at[pl.ds(0, tb), :], sem).wait()


def moe_dispatch(h, pos1, pos2, pad_end, pad_len, used_rows, n_rows, pad_max, tb, zr=256):
    T, D = h.shape
    tail_max = n_rows - TOP_K * T
    assert tail_max % zr == 0 and pad_max % zr == 0
    zero_end = jnp.concatenate([pad_end, jnp.full((1,), n_rows, I32)])
    zero_len = jnp.concatenate([pad_len, n_rows - used_rows[None]])
    grid_spec = pltpu.PrefetchScalarGridSpec(
        num_scalar_prefetch=2,
        grid=(T // tb,),
        in_specs=[pl.BlockSpec((tb,), lambda i, ps, pn: (i,), memory_space=pltpu.SMEM),
                  pl.BlockSpec((tb,), lambda i, ps, pn: (i,), memory_space=pltpu.SMEM),
                  pl.BlockSpec((tb, D), lambda i, ps, pn: (i, 0))],
        out_specs=pl.BlockSpec(memory_space=pl.ANY),
        scratch_shapes=[pltpu.VMEM((zr, D), h.dtype), pltpu.SemaphoreType.DMA(()), pltpu.SemaphoreType.DMA(())],
    )
    return pl.pallas_call(
        functools.partial(_dispatch_kernel, pad_max=pad_max, tail_max=tail_max),
        grid_spec=grid_spec,
        out_shape=jax.ShapeDtypeStruct((n_rows, D), h.dtype),
        compiler_params=_cparams(1),
        name="moe_dispatch",
    )(zero_end, zero_len, pos1, pos2, h)


def _expert_ffn_kernel(te_ref, tv_ref, x_ref, g_ref, wg_ref, wu_ref, wd_ref, o_ref, xn_ref, *, ts):
    i = pl.program_id(0)
    j = pl.program_id(1)
    valid = tv_ref[i]
    n_sub = x_ref.shape[0] // ts

    n_occ = (valid + ts - 1) // ts
    for k in range(1, n_sub + 1):
        rows = pl.ds(0, k * ts)

        @pl.when(n_occ == k)
        def _():
            @pl.when(j == 0)
            def _():
                xn_ref[rows, :] = (_rms(x_ref[rows, :]) * g_ref[...]).astype(BF16)

            xn = xn_ref[rows, :]
            a = _silu(_dot(xn, wg_ref[...].astype(BF16))) * _dot(xn, wu_ref[...].astype(BF16))
            y = _dot(a.astype(BF16), wd_ref[...].astype(BF16))

            @pl.when(j == 0)
            def _():
                o_ref[rows, :] = y
                if k < n_sub:
                    o_ref[pl.ds(k * ts, (n_sub - k) * ts), :] = jnp.zeros(((n_sub - k) * ts, o_ref.shape[1]), F32)

            @pl.when(j > 0)
            def _():
                o_ref[rows, :] += y

    @pl.when((n_occ == 0) & (j == 0))
    def _():
        o_ref[...] = jnp.zeros_like(o_ref)


def expert_ffn(xs, gain, w_gate, w_up, w_down, layer, tile_expert, tile_valid, n_tiles, tm, ts, tf):
    D = xs.shape[1]
    R = n_tiles * tm
    Fd = w_gate.shape[3]
    nf = Fd // tf

    def jj(i, j, tv):
        return jnp.where(tv[i] > 0, j, nf - 1)

    grid_spec = pltpu.PrefetchScalarGridSpec(
        num_scalar_prefetch=2,
        grid=(n_tiles, nf),
        in_specs=[pl.BlockSpec((tm, D), lambda i, j, te, tv: (jnp.where(tv[i] > 0, i, 0), 0)),
                  pl.BlockSpec((1, D), lambda i, j, te, tv: (0, 0)),
                  pl.BlockSpec((None, None, D, tf), lambda i, j, te, tv: (layer, te[i], 0, jj(i, j, tv))),
                  pl.BlockSpec((None, None, D, tf), lambda i, j, te, tv: (layer, te[i], 0, jj(i, j, tv))),
                  pl.BlockSpec((None, None, tf, D), lambda i, j, te, tv: (layer, te[i], jj(i, j, tv), 0))],
        out_specs=pl.BlockSpec((tm, D), lambda i, j, te, tv: (i, 0)),
        scratch_shapes=[pltpu.VMEM((tm, D), BF16)],
    )
    return pl.pallas_call(
        functools.partial(_expert_ffn_kernel, ts=ts),
        grid_spec=grid_spec,
        out_shape=jax.ShapeDtypeStruct((R, D), F32),
        compiler_params=_cparams(2),
        name="expert_ffn",
    )(tile_expert, tile_valid, xs, gain.reshape(1, D), w_gate, w_up, w_down)


def _combine_kernel(p1_ref, p2_ref, route_ref, h_ref, y_ref, o_ref, b1_ref, b2_ref, sem):
    tb = p1_ref.shape[0]

    def copies(t):
        dst = pl.ds(t, 1)
        return (pltpu.make_async_copy(y_ref.at[pl.ds(p1_ref[t], 1), :], b1_ref.at[dst, :], sem),
                pltpu.make_async_copy(y_ref.at[pl.ds(p2_ref[t], 1), :], b2_ref.at[dst, :], sem))

    def issue(t, c):
        for cp in copies(t):
            cp.start()
        return c

    lax.fori_loop(0, tb, issue, 0, unroll=ROW_DMA_UNROLL)
    for b_ref in (b1_ref, b2_ref):
        pltpu.make_async_copy(y_ref.at[pl.ds(0, tb), :], b_ref, sem).wait()
    w1 = route_ref[:, 2:3]
    w2 = route_ref[:, 3:4]
    o_ref[...] = h_ref[...] + w1 * b1_ref[...] + w2 * b2_ref[...]


def moe_combine(h, y, route, pos1, pos2, tb):
    T, D = h.shape
    smem = lambda: pl.BlockSpec((tb,), lambda i: (i,), memory_space=pltpu.SMEM)
    return pl.pallas_call(
        _combine_kernel,
        grid=(T // tb,),
        in_specs=[smem(), smem(),
                  pl.BlockSpec((tb, LANES), lambda i: (i, 0)),
                  pl.BlockSpec((tb, D), lambda i: (i, 0)),
                  pl.BlockSpec(memory_space=pl.ANY)],
        out_specs=pl.BlockSpec((tb, D), lambda i: (i, 0)),
        out_shape=jax.ShapeDtypeStruct((T, D), F32),
        scratch_shapes=[pltpu.VMEM((tb, D), F32), pltpu.VMEM((tb, D), F32),
                        pltpu.SemaphoreType.DMA(())],
        compiler_params=_cparams(1),
        name="moe_combine",
    )(pos1, pos2, route, h, y)


def moe_ffn(h, gain, w_router, w_gate, w_up, w_down, layer, tm=1024, ts=256, tf=512, tb=512):
    T = h.shape[0]
    route, route_t, counts = moe_router(h, gain, w_router, tm=512)
    e1 = route_t[0].astype(I32)
    e2 = route_t[1].astype(I32)
    counts = counts[0, :N_EXPERTS].astype(I32)
    tiles_per = (counts + tm - 1) // tm
    tile_end = jnp.cumsum(tiles_per)
    tile_start = tile_end - tiles_per
    row_start = tile_start * tm
    pos1 = row_start[e1] + route_t[4].astype(I32)
    pos2 = row_start[e2] + route_t[5].astype(I32)
    n_tiles = (TOP_K * T) // tm + N_EXPERTS
    tile_ids = jnp.arange(n_tiles, dtype=I32)
    tile_expert = jnp.sum(jnp.minimum(tile_ids, tile_end[-1] - 1)[:, None] >= tile_end[None, :], axis=1).astype(I32)
    tile_valid = jnp.clip(counts[tile_expert] - (tile_ids - tile_start[tile_expert]) * tm, 0, tm)
    tile_valid = jnp.where(tile_ids < tile_end[-1], tile_valid, 0).astype(I32)
    pad_len = tiles_per * tm - counts

    xs = moe_dispatch(h, pos1, pos2, tile_end * tm, pad_len, tile_end[-1] * tm, n_tiles * tm, tm, tb)
    y = expert_ffn(xs, gain, w_gate, w_up, w_down, layer, tile_expert, tile_valid, n_tiles, tm, ts, tf)
    return moe_combine(h, y, route, pos1, pos2, tb)


def _pad_heads(a, n_heads, width, pad_to, axis):
    shape = list(a.shape)
    a = a.reshape(shape[:axis] + [n_heads, width] + shape[axis + 1:])
    pads = [(0, 0)] * a.ndim
    pads[axis + 1] = (0, pad_to - width)
    a = jnp.pad(a, pads)
    return a.reshape(shape[:axis] + [n_heads * pad_to] + shape[axis + 1:])


def kernel(x, mem, ln_mix, ln_mem, w_mem_kv, mem_q_gain, mem_k_gain, w_out, ln_ffn, swa_w_in, swa_q_gain, swa_k_gain, swa_sink, ret_w_in, mlstm_w_in, mlstm_conv_w, mlstm_conv_b, mlstm_i_bias, mlstm_f_bias, mlstm_out_gain, ffn_w_gate, ffn_w_up, ffn_w_down, moe_router, moe_w_gate, moe_w_up, moe_w_down):
    B, S, D = x.shape
    M = mem.shape[1]
    T = B * S
    depth = ln_mix.shape[0]
    h = x.reshape(T, D)
    mem2 = mem.reshape(B * M, D)

    for layer in range(depth):
        kind = layer % N_MIXERS
        idx = layer // N_MIXERS
        mk, mv = rms_proj(mem2, ln_mem[layer], w_mem_kv[layer].astype(BF16),
                          ((MEM_WIDTH, BF16, mem_k_gain[layer]), (MEM_WIDTH, BF16, None)),
                          tm=B * M, name="mem_kv_proj")
        w_o = w_out[layer].astype(BF16)
        w_o_mix, w_o_mem = w_o[:MIX_WIDTH], w_o[MIX_WIDTH:]
        qm_out = (MEM_WIDTH, BF16, mem_q_gain[layer])

        if kind == 0:
            kvw = SWA_KV_HEADS * HEAD_DIM
            q, k, v, qm = rms_proj(h, ln_mix[layer], swa_w_in[idx].astype(BF16),
                                   ((MIX_WIDTH, BF16, swa_q_gain[idx]), (kvw, BF16, swa_k_gain[idx]),
                                    (kvw, BF16, None), qm_out), tm=1024, name="swa_in_proj")
            mix = swa_attention(q, k, v, swa_sink[idx], B)
        elif kind == 1:
            qkw = RET_HEADS * RET_QK_DIM
            q, k, v, g, qm = rms_proj(h, ln_mix[layer], ret_w_in[idx].astype(BF16),
                                      ((qkw, BF16, None), (qkw, F32, None), (MIX_WIDTH, BF16, None),
                                       (MIX_WIDTH, F32, None), qm_out), tm=1024, name="ret_in_proj")
            mix = retention(q, k, v, g, B)
        else:
            H, P, VP = MLSTM_HEADS, MLSTM_QK_PAD, MLSTM_V_PAD
            w = mlstm_w_in[idx]
            qkw = 2 * H * MLSTM_QK_DIM
            o_v, o_og, o_ig = qkw, qkw + MIX_WIDTH, qkw + 2 * MIX_WIDTH
            o_fg, o_qm = o_ig + H, o_ig + 2 * H
            w_gates = jnp.zeros((D, LANES), F32).at[:, :2 * H].set(w[:, o_ig:o_qm])
            w_pad = jnp.concatenate([
                _pad_heads(w[:, :qkw], 2 * H, MLSTM_QK_DIM, P, 1),
                _pad_heads(w[:, o_v:o_og], H, MLSTM_V_DIM, VP, 1),
                _pad_heads(w[:, o_og:o_ig], H, MLSTM_V_DIM, VP, 1),
                w_gates, w[:, o_qm:]], axis=1).astype(BF16)
            qk, v, og, gates, qm = rms_proj(h, ln_mix[layer], w_pad,
                                            ((2 * H * P, F32, None), (H * VP, BF16, None), (H * VP, F32, None),
                                             (LANES, F32, None), qm_out), tm=1024, name="mlstm_in_proj")
            gates_t = gates[:, :SUBLANES].reshape(B, S, SUBLANES).transpose(0, 2, 1)
            bias = jnp.concatenate([mlstm_i_bias[idx], mlstm_f_bias[idx]])
            bias_c = jnp.zeros((1, LANES), F32).at[0, :2 * H].set(bias)
            bias_r = jnp.broadcast_to(bias[:, None], (SUBLANES, MLSTM_CHUNK))
            mix = mlstm(qk, v, og, gates, gates_t,
                        _pad_heads(mlstm_conv_w[idx], 2 * H, MLSTM_QK_DIM, P, 1),
                        _pad_heads(mlstm_conv_b[idx][None], 2 * H, MLSTM_QK_DIM, P, 1),
                        bias_c, bias_r,
                        _pad_heads(mlstm_out_gain[idx][None], H, MLSTM_V_DIM, VP, 1), B)
            w_o_mix = _pad_heads(w_o_mix, H, MLSTM_V_DIM, VP, 0)

        h = out_proj(h, mix, qm, mk, mv, w_o_mix, w_o_mem, B, tm=1024)
        j = layer // 2
        if layer % 2 == 0:
            h = dense_ffn(h, ln_ffn[layer], ffn_w_gate[j].astype(BF16), ffn_w_up[j].astype(BF16),
                          ffn_w_down[j].astype(BF16), tm=1024, tf=1792)
        else:
            h = moe_ffn(h, ln_ffn[layer], moe_router[j], moe_w_gate, moe_w_up, moe_w_down, j)
    return h.reshape(B, S, D)
```

```python
import functools
import math

import jax
import jax.numpy as jnp
from jax import lax
from jax.experimental import pallas as pl
from jax.experimental.pallas import tpu as pltpu

F32 = jnp.float32
BF16 = jnp.bfloat16
I32 = jnp.int32

D_MODEL = 1024
N_MIXERS = 3
HEAD_DIM = 64
MEM_HEADS = 4
MEM_HEAD_DIM = 64
MEM_WIDTH = MEM_HEADS * MEM_HEAD_DIM
MIX_WIDTH = D_MODEL - MEM_WIDTH
RMS_EPS = 1e-6
NEG_INF = -1e30

SWA_Q_HEADS = MIX_WIDTH // HEAD_DIM
SWA_KV_HEADS = 4
SWA_GROUP = SWA_Q_HEADS // SWA_KV_HEADS
SWA_WINDOW = 128
SWA_BLOCK = 128

RET_HEADS = 6
RET_QK_DIM = 64
RET_V_DIM = MIX_WIDTH // RET_HEADS
RET_CHUNK = 128

MLSTM_HEADS = 4
MLSTM_V_DIM = MIX_WIDTH // MLSTM_HEADS
MLSTM_QK_DIM = MLSTM_V_DIM // 2
MLSTM_CHUNK = 128
MLSTM_CONV = 4
MLSTM_QK_PAD = 128
MLSTM_V_PAD = 256

N_EXPERTS = 8
TOP_K = 2

LANES = 128
SUBLANES = 8
VMEM_LIMIT = 56 * 1024 * 1024
ROW_DMA_UNROLL = 8


def _cparams(n_axes):
    return pltpu.CompilerParams(dimension_semantics=("arbitrary",) * n_axes,
                                vmem_limit_bytes=VMEM_LIMIT)


def _rms(x, eps=RMS_EPS):
    return x * lax.rsqrt(jnp.mean(x * x, axis=-1, keepdims=True) + eps)


def _dot(a, b):
    return jnp.dot(a, b, preferred_element_type=F32)


def _dot_nt(a, b):
    return lax.dot_general(a, b, (((1,), (1,)), ((), ())), preferred_element_type=F32)


def _dot_tn(a, b):
    return lax.dot_general(a, b, (((0,), (0,)), ((), ())), preferred_element_type=F32)


def _silu(x):
    return x * jax.nn.sigmoid(x)


NORM_CHUNK = 256


def _head_rms(y, seg_ref, hg):
    sq = y * y
    hi = sq.astype(BF16)
    lo = (sq - hi.astype(F32)).astype(BF16)
    ms = _dot(hi, seg_ref[...]) + _dot(lo, seg_ref[...])
    return y * lax.rsqrt(ms + RMS_EPS) * hg


def _rms_proj_kernel(x_ref, g_ref, w_ref, seg_ref, *refs, widths, normed):
    n_gain = sum(normed)
    gain_refs, o_refs = refs[:n_gain], refs[n_gain:]
    xn = (_rms(x_ref[...]) * g_ref[...]).astype(BF16)
    off = 0
    gi = 0
    for o_ref, wd, nrm in zip(o_refs, widths, normed):
        if nrm:
            hg = gain_refs[gi][...]
            gi += 1
            for c in range(0, wd, NORM_CHUNK):
                y = _dot(xn, w_ref[:, off + c:off + c + NORM_CHUNK])
                o_ref[:, c:c + NORM_CHUNK] = _head_rms(y, seg_ref, hg).astype(o_ref.dtype)
        else:
            o_ref[...] = _dot(xn, w_ref[:, off:off + wd]).astype(o_ref.dtype)
        off += wd


def rms_proj(x, gain, w, outs, tm, name):
    T, D = x.shape
    N = w.shape[1]
    widths = tuple(o[0] for o in outs)
    normed = tuple(o[2] is not None for o in outs)
    head_gains = [jnp.tile(o[2], NORM_CHUNK // HEAD_DIM).reshape(1, NORM_CHUNK) for o in outs if o[2] is not None]
    assert sum(widths) == N and T % tm == 0
    assert all(wd % NORM_CHUNK == 0 for wd, nrm in zip(widths, normed) if nrm)
    head_of = jnp.arange(NORM_CHUNK) // HEAD_DIM
    seg = jnp.where(head_of[:, None] == head_of[None, :], 1.0 / HEAD_DIM, 0.0).astype(BF16)
    return pl.pallas_call(
        functools.partial(_rms_proj_kernel, widths=widths, normed=normed),
        grid=(T // tm,),
        in_specs=[pl.BlockSpec((tm, D), lambda i: (i, 0)),
                  pl.BlockSpec((1, D), lambda i: (0, 0)),
                  pl.BlockSpec((D, N), lambda i: (0, 0)),
                  pl.BlockSpec((NORM_CHUNK, NORM_CHUNK), lambda i: (0, 0))]
                 + [pl.BlockSpec((1, NORM_CHUNK), lambda i: (0, 0))] * len(head_gains),
        out_specs=[pl.BlockSpec((tm, wd), lambda i: (i, 0)) for wd in widths],
        out_shape=[jax.ShapeDtypeStruct((T, o[0]), o[1]) for o in outs],
        compiler_params=_cparams(1),
        name=name,
    )(x, gain.reshape(1, D), w, seg, *head_gains)


def _swa_kernel(sink_ref, q_ref, kc_ref, kp_ref, vc_ref, vp_ref, bias_ref, o_ref):
    n = pl.program_id(1)
    L = SWA_BLOCK
    n_blk = q_ref.shape[0] // L
    key = lax.broadcasted_iota(I32, (L, L), 0)
    qry = lax.broadcasted_iota(I32, (L, L), 1)
    from_prev = key > qry
    own = key <= qry
    zero = jnp.zeros((L, L), BF16)
    for blk in range(n_blk):
        rows = slice(blk * L, (blk + 1) * L)
        prows = slice((blk - 1) * L, blk * L)
        outs = []
        for g in range(SWA_KV_HEADS):
            ksl = slice(g * HEAD_DIM, (g + 1) * HEAD_DIM)
            k_prev = kp_ref[:, ksl] if blk == 0 else kc_ref[prows, ksl]
            v_prev = vp_ref[:, ksl] if blk == 0 else vc_ref[prows, ksl]
            k = jnp.concatenate([k_prev, kc_ref[rows, ksl]], axis=0)
            v = jnp.concatenate([v_prev, vc_ref[rows, ksl]], axis=0)
            for h in range(g * SWA_GROUP, (g + 1) * SWA_GROUP):
                sink = sink_ref[h]
                s2 = _dot_nt(k, q_ref[rows, h * HEAD_DIM:(h + 1) * HEAD_DIM])
                s = jnp.where(from_prev, s2[:L], s2[L:]) * (HEAD_DIM ** -0.5) - bias_ref[h]
                if blk == 0:
                    s = jnp.where(jnp.logical_or(n > 0, own), s, NEG_INF)
                m = jnp.maximum(jnp.max(s, axis=0, keepdims=True), sink)
                e = jnp.exp(s - m)
                denom = jnp.sum(e, axis=0, keepdims=True) + jnp.exp(sink - m)
                p = (e * (1.0 / denom)).astype(BF16)
                p2 = jnp.concatenate([jnp.where(from_prev, p, zero), jnp.where(from_prev, zero, p)], axis=0)
                outs.append(_dot_tn(p2, v))
        o_ref[rows, :] = jnp.concatenate(outs, axis=-1).astype(o_ref.dtype)


def swa_attention(q, k, v, sink, batch, n_blk=4):
    T = q.shape[0]
    L = SWA_BLOCK
    nb = T // batch // L
    kvw = SWA_KV_HEADS * HEAD_DIM
    slopes = jnp.exp2(-8.0 * jnp.arange(1, SWA_Q_HEADS + 1, dtype=F32) / SWA_Q_HEADS)
    assert SWA_WINDOW == L
    qi, kj = jnp.arange(L)[:, None], jnp.arange(L)[None, :]
    dist = jnp.where(kj > qi, qi + L - kj, qi - kj).astype(F32)
    bias = slopes[:, None, None] * dist.T
    ns = nb // n_blk
    slab = n_blk * L
    cur = lambda b, n: (b * ns + n, 0)
    prev = lambda b, n: (b * nb + jnp.maximum(n * n_blk - 1, 0), 0)
    return pl.pallas_call(
        _swa_kernel,
        grid=(batch, ns),
        in_specs=[pl.BlockSpec(memory_space=pltpu.SMEM),
                  pl.BlockSpec((slab, MIX_WIDTH), cur),
                  pl.BlockSpec((slab, kvw), cur), pl.BlockSpec((L, kvw), prev),
                  pl.BlockSpec((slab, kvw), cur), pl.BlockSpec((L, kvw), prev),
                  pl.BlockSpec((SWA_Q_HEADS, L, L), lambda b, n: (0, 0, 0))],
        out_specs=pl.BlockSpec((slab, MIX_WIDTH), cur),
        out_shape=jax.ShapeDtypeStruct((T, MIX_WIDTH), BF16),
        compiler_params=_cparams(2),
        name="swa_attention",
    )(sink, q, k, k, v, v, bias)


def _retention_kernel(q_ref, k_ref, v_ref, g_ref, dec_ref, qd_ref, kd_ref, cd_ref, o_ref, state_ref):
    @pl.when(pl.program_id(0) == 0)
    def _():
        state_ref[...] = jnp.zeros_like(state_ref)

    L = RET_CHUNK
    n_ch = q_ref.shape[1] // L
    for b in range(q_ref.shape[0]):
        outs = [[] for _ in range(n_ch)]
        for h in range(RET_HEADS):
            qsl = slice(h * RET_QK_DIM, (h + 1) * RET_QK_DIM)
            vsl = slice(h * RET_V_DIM, (h + 1) * RET_V_DIM)
            state = state_ref[b, h]
            for c in range(n_ch):
                rows = slice(c * L, (c + 1) * L)
                q = q_ref[b, rows, qsl]
                k = k_ref[b, rows, qsl] * (RET_QK_DIM ** -0.5)
                v = v_ref[b, rows, vsl]
                s = _dot_nt(q, k.astype(BF16)) * dec_ref[h]
                o = _dot(s.astype(BF16), v) + _dot(q, state.astype(BF16)) * qd_ref[h]
                state = state * cd_ref[h] + _dot_tn((k * kd_ref[h]).astype(BF16), v)
                outs[c].append(_silu(g_ref[b, rows, vsl]) * _rms(o))
            state_ref[b, h] = state
        for c in range(n_ch):
            o_ref[b, c * L:(c + 1) * L, :] = jnp.concatenate(outs[c], axis=-1).astype(o_ref.dtype)


def _retention_consts():
    H, L = RET_HEADS, RET_CHUNK
    log_gamma = jnp.log1p(-jnp.exp2(-5.0 - jnp.arange(H, dtype=F32)))
    pos = jnp.arange(L, dtype=F32)
    rel = pos[:, None] - pos[None, :]
    intra = jnp.exp(jnp.where(rel >= 0, log_gamma[:, None, None] * rel, -jnp.inf))
    q_decay = jnp.exp(log_gamma[:, None] * (pos + 1.0))[..., None]
    k_decay = jnp.exp(log_gamma[:, None] * (L - 1.0 - pos))[..., None]
    chunk_decay = jnp.exp(log_gamma * L)[:, None, None]
    return (intra,
            jnp.broadcast_to(q_decay, (H, L, RET_V_DIM)),
            jnp.broadcast_to(k_decay, (H, L, RET_QK_DIM)),
            jnp.broadcast_to(chunk_decay, (H, RET_QK_DIM, RET_V_DIM)))


def retention(q, k, v, g, batch, n_ch=4):
    T = q.shape[0]
    L = RET_CHUNK
    nc = T // batch // L
    H = RET_HEADS
    intra, qd, kd, cd = _retention_consts()
    cur = lambda c: (0, c, 0)
    const = lambda c: (0, 0, 0)
    qkw = H * RET_QK_DIM
    per_batch = lambda a: a.reshape(batch, T // batch, a.shape[1])
    slab = n_ch * L
    out = pl.pallas_call(
        _retention_kernel,
        grid=(nc // n_ch,),
        in_specs=[pl.BlockSpec((batch, slab, qkw), cur), pl.BlockSpec((batch, slab, qkw), cur),
                  pl.BlockSpec((batch, slab, MIX_WIDTH), cur), pl.BlockSpec((batch, slab, MIX_WIDTH), cur),
                  pl.BlockSpec((H, L, L), const), pl.BlockSpec((H, L, RET_V_DIM), const),
                  pl.BlockSpec((H, L, RET_QK_DIM), const), pl.BlockSpec((H, RET_QK_DIM, RET_V_DIM), const)],
        out_specs=pl.BlockSpec((batch, slab, MIX_WIDTH), cur),
        out_shape=jax.ShapeDtypeStruct((batch, T // batch, MIX_WIDTH), BF16),
        scratch_shapes=[pltpu.VMEM((batch, H, RET_QK_DIM, RET_V_DIM), F32)],
        compiler_params=_cparams(1),
        name="retention",
    )(per_batch(q), per_batch(k), per_batch(v), per_batch(g), intra, qd, kd, cd)
    return out.reshape(T, MIX_WIDTH)


def _split3(x):
    x1 = x.astype(BF16)
    r1 = x - x1.astype(F32)
    x2 = r1.astype(BF16)
    x3 = (r1 - x2.astype(F32)).astype(BF16)
    return x1, x2, x3


def _mlstm_kernel(qk_ref, v_ref, og_ref, gc_ref, gr_ref, cw_ref, cb_ref, bc_ref, br_ref, gain_ref,
                  tri_ref, trit_ref, o_ref, xbuf_ref, c_ref, n_ref, m_ref):
    for b in range(qk_ref.shape[0]):
        _mlstm_chunk(qk_ref.at[b], v_ref.at[b], og_ref.at[b], gc_ref.at[b], gr_ref.at[b], cw_ref, cb_ref,
                     bc_ref, br_ref, gain_ref, tri_ref, trit_ref, o_ref.at[b],
                     xbuf_ref.at[b], c_ref.at[b], n_ref.at[b], m_ref.at[b])


def _mlstm_chunk(qk_ref, v_ref, og_ref, gc_ref, gr_ref, cw_ref, cb_ref, bc_ref, br_ref, gain_ref,
                 tri_ref, trit_ref, o_ref, xbuf_ref, c_ref, n_ref, m_ref):
    L = MLSTM_CHUNK
    H = MLSTM_HEADS
    P = MLSTM_QK_PAD
    VP = MLSTM_V_PAD
    KT = SUBLANES

    @pl.when(pl.program_id(0) == 0)
    def _():
        xbuf_ref[0:KT, :] = jnp.zeros((KT, xbuf_ref.shape[1]), F32)
        c_ref[...] = jnp.zeros_like(c_ref)
        n_ref[...] = jnp.zeros_like(n_ref)
        m_ref[...] = jnp.zeros_like(m_ref)

    xbuf_ref[KT:KT + L, :] = qk_ref[...]
    acc = cb_ref[...] + cw_ref[MLSTM_CONV - 1:MLSTM_CONV, :] * xbuf_ref[KT:KT + L, :]
    for j in range(MLSTM_CONV - 1):
        sh = MLSTM_CONV - 1 - j
        acc = acc + cw_ref[j:j + 1, :] * xbuf_ref[KT - sh:KT - sh + L, :]
    xbuf_ref[0:KT, :] = qk_ref[L - KT:L, :]
    qk = _silu(acc)

    xc = gc_ref[...] + bc_ref[...]
    xr = gr_ref[...] + br_ref[...]
    lfc = jax.nn.log_sigmoid(xc)
    lfr = jax.nn.log_sigmoid(xr)
    tri = tri_ref[...]
    trit = trit_ref[...]
    bc = sum(_dot(tri, t) for t in _split3(lfc))
    br = sum(_dot(t, trit) for t in _split3(lfr))

    rowi = lax.broadcasted_iota(I32, (L, L), 0)
    coli = lax.broadcasted_iota(I32, (L, L), 1)
    causal = rowi >= coli
    outs = []
    for h in range(H):
        q = qk[:, h * P:(h + 1) * P].astype(BF16)
        k = qk[:, (H + h) * P:(H + h + 1) * P] * (MLSTM_QK_DIM ** -0.5)
        v = v_ref[:, h * VP:(h + 1) * VP]
        li_c = xc[:, h:h + 1]
        b_c = bc[:, H + h:H + h + 1]
        li_r = xr[h:h + 1, :]
        b_r = br[H + h:H + h + 1, :]
        g = b_c[L - 1:L, :]
        m = m_ref[h:h + 1, 0:1]
        C = c_ref[h]
        nvec = n_ref[h:h + 1, :]

        dmat = jnp.where(causal, b_c - b_r + li_r, -jnp.inf)
        inter = b_c + m
        m_t = jnp.maximum(inter, jnp.max(dmat, axis=-1, keepdims=True))
        w = jnp.exp(dmat - m_t)
        a = jnp.exp(inter - m_t)
        s = _dot_nt(q, k.astype(BF16)) * w
        num = _dot(s.astype(BF16), v) + a * _dot(q, C.astype(BF16))
        qf = qk[:, h * P:(h + 1) * P]
        den = jnp.sum(s, axis=-1, keepdims=True) + a * jnp.sum(qf * nvec, axis=-1, keepdims=True)
        hh = num / jnp.maximum(jnp.abs(den), jnp.exp(-m_t))

        u_c = g - b_c + li_c
        u_r = g - b_r + li_r
        m_new = jnp.maximum(g + m, jnp.max(u_r, axis=-1, keepdims=True))
        wk = jnp.exp(u_c - m_new)
        decay = jnp.exp(g + m - m_new)
        kw = k * wk
        c_ref[h] = decay * C + _dot_tn(kw.astype(BF16), v)
        n_ref[h:h + 1, :] = decay * nvec + jnp.sum(kw, axis=0, keepdims=True)
        m_ref[h:h + 1, :] = jnp.broadcast_to(m_new, (1, m_ref.shape[1]))

        ms = jnp.sum(hh * hh, axis=-1, keepdims=True) * (1.0 / MLSTM_V_DIM)
        hc = hh * lax.rsqrt(ms + RMS_EPS) * gain_ref[:, h * VP:(h + 1) * VP]
        outs.append(jax.nn.sigmoid(og_ref[:, h * VP:(h + 1) * VP]) * hc)
    o_ref[...] = jnp.concatenate(outs, axis=-1).astype(o_ref.dtype)


def mlstm(qk, v, og, gates, gates_t, conv_w, conv_b, bias_c, bias_r, gain, batch):
    T = qk.shape[0]
    L = MLSTM_CHUNK
    nc = T // batch // L
    H, P, VP = MLSTM_HEADS, MLSTM_QK_PAD, MLSTM_V_PAD
    tri = jnp.tril(jnp.ones((L, L), BF16))
    cur = lambda c: (0, c, 0)
    c2 = lambda c: (0, 0)
    per_batch = lambda a: a.reshape(batch, T // batch, a.shape[1])
    out = pl.pallas_call(
        _mlstm_kernel,
        grid=(nc,),
        in_specs=[pl.BlockSpec((batch, L, 2 * H * P), cur), pl.BlockSpec((batch, L, H * VP), cur),
                  pl.BlockSpec((batch, L, H * VP), cur), pl.BlockSpec((batch, L, LANES), cur),
                  pl.BlockSpec((batch, SUBLANES, L), lambda c: (0, 0, c)),
                  pl.BlockSpec((MLSTM_CONV, 2 * H * P), c2), pl.BlockSpec((1, 2 * H * P), c2),
                  pl.BlockSpec((1, LANES), c2), pl.BlockSpec((SUBLANES, L), c2),
                  pl.BlockSpec((1, H * VP), c2),
                  pl.BlockSpec((L, L), c2), pl.BlockSpec((L, L), c2)],
        out_specs=pl.BlockSpec((batch, L, H * VP), cur),
        out_shape=jax.ShapeDtypeStruct((batch, T // batch, H * VP), BF16),
        scratch_shapes=[pltpu.VMEM((batch, SUBLANES + L, 2 * H * P), F32),
                        pltpu.VMEM((batch, H, P, VP), F32),
                        pltpu.VMEM((batch, SUBLANES, P), F32),
                        pltpu.VMEM((batch, SUBLANES, LANES), F32)],
        compiler_params=_cparams(1),
        name="mlstm",
    )(per_batch(qk), per_batch(v), per_batch(og), per_batch(gates), gates_t,
      conv_w, conv_b, bias_c, bias_r, gain, tri, tri.T)
    return out.reshape(T, H * VP)


def _out_proj_kernel(h_ref, mix_ref, q_ref, k_ref, v_ref, w1_ref, w2_ref, o_ref):
    outs = []
    for hd in range(MEM_HEADS):
        sl = slice(hd * MEM_HEAD_DIM, (hd + 1) * MEM_HEAD_DIM)
        s = _dot_nt(k_ref[:, sl], q_ref[:, sl]) * (MEM_HEAD_DIM ** -0.5)
        e = jnp.exp(s - jnp.max(s, axis=0, keepdims=True))
        p = (e * (1.0 / jnp.sum(e, axis=0, keepdims=True))).astype(BF16)
        outs.append(_dot_tn(p, v_ref[:, sl]))
    mem_out = jnp.concatenate(outs, axis=-1).astype(BF16)
    o_ref[...] = h_ref[...] + _dot(mix_ref[...], w1_ref[...]) + _dot(mem_out, w2_ref[...])


def out_proj(h, mix, qm, mk, mv, w_mix, w_mem, batch, tm):
    T, D = h.shape
    Wm = mix.shape[1]
    M = mk.shape[0] // batch
    nt = T // batch // tm
    rows = lambda b, i: (b * nt + i, 0)
    mem = lambda b, i: (b, 0)
    const = lambda b, i: (0, 0)
    return pl.pallas_call(
        _out_proj_kernel,
        grid=(batch, nt),
        in_specs=[pl.BlockSpec((tm, D), rows),
                  pl.BlockSpec((tm, Wm), rows),
                  pl.BlockSpec((tm, MEM_WIDTH), rows),
                  pl.BlockSpec((M, MEM_WIDTH), mem),
                  pl.BlockSpec((M, MEM_WIDTH), mem),
                  pl.BlockSpec((Wm, D), const),
                  pl.BlockSpec((MEM_WIDTH, D), const)],
        out_specs=pl.BlockSpec((tm, D), rows),
        out_shape=jax.ShapeDtypeStruct((T, D), F32),
        compiler_params=_cparams(2),
        name="out_proj",
    )(h, mix, qm, mk, mv, w_mix, w_mem)


def _dense_ffn_kernel(h_ref, g_ref, wg_ref, wu_ref, wd_ref, o_ref, xn_ref):
    @pl.when(pl.program_id(1) == 0)
    def _():
        h = h_ref[...]
        xn_ref[...] = (_rms(h) * g_ref[...]).astype(BF16)
        o_ref[...] = h

    xn = xn_ref[...]
    a = _silu(_dot(xn, wg_ref[...])) * _dot(xn, wu_ref[...])
    o_ref[...] += _dot(a.astype(BF16), wd_ref[...])


def dense_ffn(h, gain, w_gate, w_up, w_down, tm, tf):
    T, D = h.shape
    Fd = w_gate.shape[1]
    return pl.pallas_call(
        _dense_ffn_kernel,
        grid=(T // tm, Fd // tf),
        in_specs=[pl.BlockSpec((tm, D), lambda i, j: (i, 0)),
                  pl.BlockSpec((1, D), lambda i, j: (0, 0)),
                  pl.BlockSpec((D, tf), lambda i, j: (0, j)),
                  pl.BlockSpec((D, tf), lambda i, j: (0, j)),
                  pl.BlockSpec((tf, D), lambda i, j: (j, 0))],
        out_specs=pl.BlockSpec((tm, D), lambda i, j: (i, 0)),
        out_shape=jax.ShapeDtypeStruct((T, D), F32),
        scratch_shapes=[pltpu.VMEM((tm, D), BF16)],
        compiler_params=_cparams(2),
        name="dense_ffn",
    )(h, gain.reshape(1, D), w_gate, w_up, w_down)


def _router_kernel(h_ref, g_ref, wr_ref, tri_ref, o_ref, ot_ref, cnt_ref, carry_ref):
    i = pl.program_id(0)

    @pl.when(i == 0)
    def _():
        carry_ref[...] = jnp.zeros_like(carry_ref)

    tm = h_ref.shape[0]
    xn = _rms(h_ref[...]) * g_ref[...]
    x_hi = xn.astype(BF16)
    x_lo = (xn - x_hi.astype(F32)).astype(BF16)
    wr = wr_ref[...]
    w_hi = wr.astype(BF16)
    w_lo = (wr - w_hi.astype(F32)).astype(BF16)
    logits = _dot(x_hi, w_hi) + (_dot(x_hi, w_lo) + _dot(x_lo, w_hi))
    lane = lax.broadcasted_iota(I32, (tm, LANES), 1)
    logits = jnp.where(lane < N_EXPERTS, logits, -jnp.inf)
    t1 = jnp.max(logits, axis=-1, keepdims=True)
    e1 = jnp.min(jnp.where(logits == t1, lane, LANES), axis=-1, keepdims=True)
    rest = jnp.where(lane == e1, -jnp.inf, logits)
    t2 = jnp.max(rest, axis=-1, keepdims=True)
    e2 = jnp.min(jnp.where(rest == t2, lane, LANES), axis=-1, keepdims=True)
    x2 = jnp.exp(t2 - t1)
    w1 = 1.0 / (1.0 + x2)
    w2 = x2 / (1.0 + x2)

    oh1 = lane == e1
    oh2 = lane == e2
    cnt = jnp.where(oh1 | oh2, 1.0, 0.0)
    before = _dot(tri_ref[...], cnt.astype(BF16)) + carry_ref[...]
    r1 = jnp.sum(jnp.where(oh1, before, 0.0), axis=-1, keepdims=True)
    r2 = jnp.sum(jnp.where(oh2, before, 0.0), axis=-1, keepdims=True)
    carry_ref[...] += jnp.sum(cnt, axis=0, keepdims=True)

    cols = (e1.astype(F32), e2.astype(F32), w1, w2, r1, r2)
    out = jnp.zeros((tm, LANES), F32)
    for c, val in enumerate(cols):
        out = jnp.where(lane == c, val, out)
    o_ref[...] = out
    ot_ref[...] = out.T[:SUBLANES, :]
    cnt_ref[...] = jnp.broadcast_to(carry_ref[...], cnt_ref.shape)


def moe_router(h, gain, w_router, tm):
    T, D = h.shape
    wr = jnp.zeros((D, LANES), F32).at[:, :N_EXPERTS].set(w_router)
    tri = jnp.tril(jnp.ones((tm, tm), BF16), k=-1)
    return pl.pallas_call(
        _router_kernel,
        grid=(T // tm,),
        in_specs=[pl.BlockSpec((tm, D), lambda i: (i, 0)),
                  pl.BlockSpec((1, D), lambda i: (0, 0)),
                  pl.BlockSpec((D, LANES), lambda i: (0, 0)),
                  pl.BlockSpec((tm, tm), lambda i: (0, 0))],
        out_specs=[pl.BlockSpec((tm, LANES), lambda i: (i, 0)),
                   pl.BlockSpec((SUBLANES, tm), lambda i: (0, i)),
                   pl.BlockSpec((SUBLANES, LANES), lambda i: (0, 0))],
        out_shape=[jax.ShapeDtypeStruct((T, LANES), F32),
                   jax.ShapeDtypeStruct((SUBLANES, T), F32),
                   jax.ShapeDtypeStruct((SUBLANES, LANES), F32)],
        scratch_shapes=[pltpu.VMEM((1, LANES), F32)],
        compiler_params=_cparams(1),
        name="moe_router",
    )(h, gain.reshape(1, D), wr, tri)


def _dispatch_kernel(ze_ref, zn_ref, p1_ref, p2_ref, h_ref, xs_ref, zero_ref, sem, zsem, *, pad_max, tail_max):
    tb = p1_ref.shape[0]
    zr = zero_ref.shape[0]

    @pl.when(pl.program_id(0) == 0)
    def _():
        zero_ref[...] = jnp.zeros_like(zero_ref)
        chunks = [(e, c) for e in range(N_EXPERTS) for c in range(pad_max // zr)]
        chunks += [(N_EXPERTS, c) for c in range(tail_max // zr)]

        def zcopy(e, c):
            start = pl.multiple_of(ze_ref[e] - (c + 1) * zr, zr)
            return pltpu.make_async_copy(zero_ref, xs_ref.at[pl.ds(start, zr), :], zsem)

        def needed(e, c):
            return c * zr < zn_ref[e]

        for e, c in chunks:
            @pl.when(needed(e, c))
            def _():
                zcopy(e, c).start()

        for e, c in chunks:
            @pl.when(needed(e, c))
            def _():
                zcopy(e, c).wait()

    def copy(t, pos):
        return pltpu.make_async_copy(h_ref.at[pl.ds(t, 1), :], xs_ref.at[pl.ds(pos, 1), :], sem)

    def issue(t, c):
        copy(t, p1_ref[t]).start()
        copy(t, p2_ref[t]).start()
        return c

    lax.fori_loop(0, tb, issue, 0, unroll=ROW_DMA_UNROLL)
    for _ in range(TOP_K):
        pltpu.make_async_copy(h_ref, xs_ref.at[pl.ds(0, tb), :], sem).wait()


def moe_dispatch(h, pos1, pos2, pad_end, pad_len, used_rows, n_rows, pad_max, tb, zr=256):
    T, D = h.shape
    tail_max = n_rows - TOP_K * T
    assert tail_max % zr == 0 and pad_max % zr == 0
    zero_end = jnp.concatenate([pad_end, jnp.full((1,), n_rows, I32)])
    zero_len = jnp.concatenate([pad_len, n_rows - used_rows[None]])
    grid_spec = pltpu.PrefetchScalarGridSpec(
        num_scalar_prefetch=2,
        grid=(T // tb,),
        in_specs=[pl.BlockSpec((tb,), lambda i, ps, pn: (i,), memory_space=pltpu.SMEM),
                  pl.BlockSpec((tb,), lambda i, ps, pn: (i,), memory_space=pltpu.SMEM),
                  pl.BlockSpec((tb, D), lambda i, ps, pn: (i, 0))],
        out_specs=pl.BlockSpec(memory_space=pl.ANY),
        scratch_shapes=[pltpu.VMEM((zr, D), h.dtype), pltpu.SemaphoreType.DMA(()), pltpu.SemaphoreType.DMA(())],
    )
    return pl.pallas_call(
        functools.partial(_dispatch_kernel, pad_max=pad_max, tail_max=tail_max),
        grid_spec=grid_spec,
        out_shape=jax.ShapeDtypeStruct((n_rows, D), h.dtype),
        compiler_params=_cparams(1),
        name="moe_dispatch",
    )(zero_end, zero_len, pos1, pos2, h)


def _expert_ffn_kernel(te_ref, tv_ref, x_ref, g_ref, wg_ref, wu_ref, wd_ref, o_ref, xn_ref, a_ref, *, ts, nf):
    i = pl.program_id(0)
    j = pl.program_id(1)
    valid = tv_ref[i]
    n_sub = x_ref.shape[0] // ts

    n_occ = (valid + ts - 1) // ts
    for k in range(1, n_sub + 1):
        rows = pl.ds(0, k * ts)

        def gate_up(slot):
            xn = xn_ref[rows, :]
            a = _silu(_dot(xn, wg_ref[...].astype(BF16))) * _dot(xn, wu_ref[...].astype(BF16))
            a_ref[slot, rows, :] = a.astype(BF16)

        def down(slot, first):
            y = _dot(a_ref[slot, rows, :], wd_ref[...].astype(BF16))
            if first:
                o_ref[rows, :] = y
                if k < n_sub:
                    o_ref[pl.ds(k * ts, (n_sub - k) * ts), :] = jnp.zeros(((n_sub - k) * ts, o_ref.shape[1]), F32)
            else:
                o_ref[rows, :] += y

        @pl.when(n_occ == k)
        def _():
            @pl.when(j == 0)
            def _():
                xn_ref[rows, :] = (_rms(x_ref[rows, :]) * g_ref[...]).astype(BF16)
                gate_up(0)

            @pl.when(j == 1)
            def _():
                gate_up(1)
                down(0, first=True)

            @pl.when((j > 1) & (j < nf))
            def _():
                gate_up(j % 2)
                down((j - 1) % 2, first=False)

            @pl.when(j == nf)
            def _():
                down((nf - 1) % 2, first=False)

    @pl.when((n_occ == 0) & (j == 0))
    def _():
        o_ref[...] = jnp.zeros_like(o_ref)


def expert_ffn(xs, gain, w_gate, w_up, w_down, layer, tile_expert, tile_valid, n_tiles, tm, ts, tf):
    D = xs.shape[1]
    R = n_tiles * tm
    Fd = w_gate.shape[3]
    nf = Fd // tf

    def up_tile(i, j, tv):
        return jnp.where(tv[i] > 0, jnp.minimum(j, nf - 1), nf - 1)

    def down_tile(i, j, tv):
        return jnp.where(tv[i] > 0, jnp.maximum(j - 1, 0), nf - 1)

    grid_spec = pltpu.PrefetchScalarGridSpec(
        num_scalar_prefetch=2,
        grid=(n_tiles, nf + 1),
        in_specs=[pl.BlockSpec((tm, D), lambda i, j, te, tv: (jnp.where(tv[i] > 0, i, 0), 0)),
                  pl.BlockSpec((1, D), lambda i, j, te, tv: (0, 0)),
                  pl.BlockSpec((None, None, D, tf), lambda i, j, te, tv: (layer, te[i], 0, up_tile(i, j, tv))),
                  pl.BlockSpec((None, None, D, tf), lambda i, j, te, tv: (layer, te[i], 0, up_tile(i, j, tv))),
                  pl.BlockSpec((None, None, tf, D), lambda i, j, te, tv: (layer, te[i], down_tile(i, j, tv), 0))],
        out_specs=pl.BlockSpec((tm, D), lambda i, j, te, tv: (i, 0)),
        scratch_shapes=[pltpu.VMEM((tm, D), BF16), pltpu.VMEM((2, tm, tf), BF16)],
    )
    return pl.pallas_call(
        functools.partial(_expert_ffn_kernel, ts=ts, nf=nf),
        grid_spec=grid_spec,
        out_shape=jax.ShapeDtypeStruct((R, D), F32),
        compiler_params=_cparams(2),
        name="expert_ffn",
    )(tile_expert, tile_valid, xs, gain.reshape(1, D), w_gate, w_up, w_down)


def _combine_kernel(p1_ref, p2_ref, route_ref, h_ref, y_ref, o_ref, b1_ref, b2_ref, sem):
    tb = p1_ref.shape[0]

    def copies(t):
        dst = pl.ds(t, 1)
        return (pltpu.make_async_copy(y_ref.at[pl.ds(p1_ref[t], 1), :], b1_ref.at[dst, :], sem),
                pltpu.make_async_copy(y_ref.at[pl.ds(p2_ref[t], 1), :], b2_ref.at[dst, :], sem))

    def issue(t, c):
        for cp in copies(t):
            cp.start()
        return c

    lax.fori_loop(0, tb, issue, 0, unroll=ROW_DMA_UNROLL)
    for b_ref in (b1_ref, b2_ref):
        pltpu.make_async_copy(y_ref.at[pl.ds(0, tb), :], b_ref, sem).wait()
    w1 = route_ref[:, 2:3]
    w2 = route_ref[:, 3:4]
    o_ref[...] = h_ref[...] + w1 * b1_ref[...] + w2 * b2_ref[...]


def moe_combine(h, y, route, pos1, pos2, tb):
    T, D = h.shape
    smem = lambda: pl.BlockSpec((tb,), lambda i: (i,), memory_space=pltpu.SMEM)
    return pl.pallas_call(
        _combine_kernel,
        grid=(T // tb,),
        in_specs=[smem(), smem(),
                  pl.BlockSpec((tb, LANES), lambda i: (i, 0)),
                  pl.BlockSpec((tb, D), lambda i: (i, 0)),
                  pl.BlockSpec(memory_space=pl.ANY)],
        out_specs=pl.BlockSpec((tb, D), lambda i: (i, 0)),
        out_shape=jax.ShapeDtypeStruct((T, D), F32),
        scratch_shapes=[pltpu.VMEM((tb, D), F32), pltpu.VMEM((tb, D), F32),
                        pltpu.SemaphoreType.DMA(())],
        compiler_params=_cparams(1),
        name="moe_combine",
    )(pos1, pos2, route, h, y)


def moe_ffn(h, gain, w_router, w_gate, w_up, w_down, layer, tm=1024, ts=256, tf=512, tb=512):
    T = h.shape[0]
    route, route_t, counts = moe_router(h, gain, w_router, tm=512)
    e1 = route_t[0].astype(I32)
    e2 = route_t[1].astype(I32)
    counts = counts[0, :N_EXPERTS].astype(I32)
    tiles_per = (counts + tm - 1) // tm
    tile_end = jnp.cumsum(tiles_per)
    tile_start = tile_end - tiles_per
    row_start = tile_start * tm
    pos1 = row_start[e1] + route_t[4].astype(I32)
    pos2 = row_start[e2] + route_t[5].astype(I32)
    n_tiles = (TOP_K * T) // tm + N_EXPERTS
    tile_ids = jnp.arange(n_tiles, dtype=I32)
    tile_expert = jnp.sum(jnp.minimum(tile_ids, tile_end[-1] - 1)[:, None] >= tile_end[None, :], axis=1).astype(I32)
    tile_valid = jnp.clip(counts[tile_expert] - (tile_ids - tile_start[tile_expert]) * tm, 0, tm)
    tile_valid = jnp.where(tile_ids < tile_end[-1], tile_valid, 0).astype(I32)
    pad_len = tiles_per * tm - counts

    xs = moe_dispatch(h, pos1, pos2, tile_end * tm, pad_len, tile_end[-1] * tm, n_tiles * tm, tm, tb)
    y = expert_ffn(xs, gain, w_gate, w_up, w_down, layer, tile_expert, tile_valid, n_tiles, tm, ts, tf)
    return moe_combine(h, y, route, pos1, pos2, tb)


def _pad_heads(a, n_heads, width, pad_to, axis):
    shape = list(a.shape)
    a = a.reshape(shape[:axis] + [n_heads, width] + shape[axis + 1:])
    pads = [(0, 0)] * a.ndim
    pads[axis + 1] = (0, pad_to - width)
    a = jnp.pad(a, pads)
    return a.reshape(shape[:axis] + [n_heads * pad_to] + shape[axis + 1:])


def kernel(x, mem, ln_mix, ln_mem, w_mem_kv, mem_q_gain, mem_k_gain, w_out, ln_ffn, swa_w_in, swa_q_gain, swa_k_gain, swa_sink, ret_w_in, mlstm_w_in, mlstm_conv_w, mlstm_conv_b, mlstm_i_bias, mlstm_f_bias, mlstm_out_gain, ffn_w_gate, ffn_w_up, ffn_w_down, moe_router, moe_w_gate, moe_w_up, moe_w_down):
    B, S, D = x.shape
    M = mem.shape[1]
    T = B * S
    depth = ln_mix.shape[0]
    h = x.reshape(T, D)
    mem2 = mem.reshape(B * M, D)

    for layer in range(depth):
        kind = layer % N_MIXERS
        idx = layer // N_MIXERS
        mk, mv = rms_proj(mem2, ln_mem[layer], w_mem_kv[layer].astype(BF16),
                          ((MEM_WIDTH, BF16, mem_k_gain[layer]), (MEM_WIDTH, BF16, None)),
                          tm=B * M, name="mem_kv_proj")
        w_o = w_out[layer].astype(BF16)
        w_o_mix, w_o_mem = w_o[:MIX_WIDTH], w_o[MIX_WIDTH:]
        qm_out = (MEM_WIDTH, BF16, mem_q_gain[layer])

        if kind == 0:
            kvw = SWA_KV_HEADS * HEAD_DIM
            q, k, v, qm = rms_proj(h, ln_mix[layer], swa_w_in[idx].astype(BF16),
                                   ((MIX_WIDTH, BF16, swa_q_gain[idx]), (kvw, BF16, swa_k_gain[idx]),
                                    (kvw, BF16, None), qm_out), tm=1024, name="swa_in_proj")
            mix = swa_attention(q, k, v, swa_sink[idx], B)
        elif kind == 1:
            qkw = RET_HEADS * RET_QK_DIM
            q, k, v, g, qm = rms_proj(h, ln_mix[layer], ret_w_in[idx].astype(BF16),
                                      ((qkw, BF16, None), (qkw, F32, None), (MIX_WIDTH, BF16, None),
                                       (MIX_WIDTH, F32, None), qm_out), tm=1024, name="ret_in_proj")
            mix = retention(q, k, v, g, B)
        else:
            H, P, VP = MLSTM_HEADS, MLSTM_QK_PAD, MLSTM_V_PAD
            w = mlstm_w_in[idx]
            qkw = 2 * H * MLSTM_QK_DIM
            o_v, o_og, o_ig = qkw, qkw + MIX_WIDTH, qkw + 2 * MIX_WIDTH
            o_fg, o_qm = o_ig + H, o_ig + 2 * H
            w_gates = jnp.zeros((D, LANES), F32).at[:, :2 * H].set(w[:, o_ig:o_qm])
            w_pad = jnp.concatenate([
                _pad_heads(w[:, :qkw], 2 * H, MLSTM_QK_DIM, P, 1),
                _pad_heads(w[:, o_v:o_og], H, MLSTM_V_DIM, VP, 1),
                _pad_heads(w[:, o_og:o_ig], H, MLSTM_V_DIM, VP, 1),
                w_gates, w[:, o_qm:]], axis=1).astype(BF16)
            qk, v, og, gates, qm = rms_proj(h, ln_mix[layer], w_pad,
                                            ((2 * H * P, F32, None), (H * VP, BF16, None), (H * VP, F32, None),
                                             (LANES, F32, None), qm_out), tm=1024, name="mlstm_in_proj")
            gates_t = gates[:, :SUBLANES].reshape(B, S, SUBLANES).transpose(0, 2, 1)
            bias = jnp.concatenate([mlstm_i_bias[idx], mlstm_f_bias[idx]])
            bias_c = jnp.zeros((1, LANES), F32).at[0, :2 * H].set(bias)
            bias_r = jnp.broadcast_to(bias[:, None], (SUBLANES, MLSTM_CHUNK))
            mix = mlstm(qk, v, og, gates, gates_t,
                        _pad_heads(mlstm_conv_w[idx], 2 * H, MLSTM_QK_DIM, P, 1),
                        _pad_heads(mlstm_conv_b[idx][None], 2 * H, MLSTM_QK_DIM, P, 1),
                        bias_c, bias_r,
                        _pad_heads(mlstm_out_gain[idx][None], H, MLSTM_V_DIM, VP, 1), B)
            w_o_mix = _pad_heads(w_o_mix, H, MLSTM_V_DIM, VP, 0)

        h = out_proj(h, mix, qm, mk, mv, w_o_mix, w_o_mem, B, tm=1024)
        j = layer // 2
        if layer % 2 == 0:
            h = dense_ffn(h, ln_ffn[layer], ffn_w_gate[j].astype(BF16), ffn_w_up[j].astype(BF16),
                          ffn_w_down[j].astype(BF16), tm=1024, tf=1792)
        else:
            h = moe_ffn(h, ln_ffn[layer], moe_router[j], moe_w_gate, moe_w_up, moe_w_down, j)
    return h.reshape(B, S, D)
```

```python
import functools
import math

import jax
import jax.numpy as jnp
from jax import lax
from jax.experimental import pallas as pl
from jax.experimental.pallas import tpu as pltpu

F32 = jnp.float32
BF16 = jnp.bfloat16
I32 = jnp.int32

D_MODEL = 1024
N_MIXERS = 3
HEAD_DIM = 64
MEM_HEADS = 4
MEM_HEAD_DIM = 64
MEM_WIDTH = MEM_HEADS * MEM_HEAD_DIM
MIX_WIDTH = D_MODEL - MEM_WIDTH
RMS_EPS = 1e-6
NEG_INF = -1e30

SWA_Q_HEADS = MIX_WIDTH // HEAD_DIM
SWA_KV_HEADS = 4
SWA_GROUP = SWA_Q_HEADS // SWA_KV_HEADS
SWA_WINDOW = 128
SWA_BLOCK = 128

RET_HEADS = 6
RET_QK_DIM = 64
RET_V_DIM = MIX_WIDTH // RET_HEADS
RET_CHUNK = 128

MLSTM_HEADS = 4
MLSTM_V_DIM = MIX_WIDTH // MLSTM_HEADS
MLSTM_QK_DIM = MLSTM_V_DIM // 2
MLSTM_CHUNK = 128
MLSTM_CONV = 4
MLSTM_QK_PAD = 128
MLSTM_V_PAD = 256

N_EXPERTS = 8
TOP_K = 2

LANES = 128
SUBLANES = 8
VMEM_LIMIT = 56 * 1024 * 1024
ROW_DMA_UNROLL = 8


def _cparams(n_axes):
    return pltpu.CompilerParams(dimension_semantics=("arbitrary",) * n_axes,
                                vmem_limit_bytes=VMEM_LIMIT)


def _rms(x, eps=RMS_EPS):
    return x * lax.rsqrt(jnp.mean(x * x, axis=-1, keepdims=True) + eps)


def _dot(a, b):
    return jnp.dot(a, b, preferred_element_type=F32)


def _dot_nt(a, b):
    return lax.dot_general(a, b, (((1,), (1,)), ((), ())), preferred_element_type=F32)


def _dot_tn(a, b):
    return lax.dot_general(a, b, (((0,), (0,)), ((), ())), preferred_element_type=F32)


def _silu(x):
    return x * jax.nn.sigmoid(x)


NORM_CHUNK = 256


def _head_rms(y, seg_ref, hg):
    sq = y * y
    hi = sq.astype(BF16)
    lo = (sq - hi.astype(F32)).astype(BF16)
    ms = _dot(hi, seg_ref[...]) + _dot(lo, seg_ref[...])
    return y * lax.rsqrt(ms + RMS_EPS) * hg


def _rms_proj_kernel(x_ref, g_ref, w_ref, seg_ref, *refs, widths, normed):
    n_gain = sum(normed)
    gain_refs, o_refs = refs[:n_gain], refs[n_gain:]
    xn = (_rms(x_ref[...]) * g_ref[...]).astype(BF16)
    off = 0
    gi = 0
    for o_ref, wd, nrm in zip(o_refs, widths, normed):
        if nrm:
            hg = gain_refs[gi][...]
            gi += 1
            for c in range(0, wd, NORM_CHUNK):
                y = _dot(xn, w_ref[:, off + c:off + c + NORM_CHUNK])
                o_ref[:, c:c + NORM_CHUNK] = _head_rms(y, seg_ref, hg).astype(o_ref.dtype)
        else:
            o_ref[...] = _dot(xn, w_ref[:, off:off + wd]).astype(o_ref.dtype)
        off += wd


def rms_proj(x, gain, w, outs, tm, name):
    T, D = x.shape
    N = w.shape[1]
    widths = tuple(o[0] for o in outs)
    normed = tuple(o[2] is not None for o in outs)
    head_gains = [jnp.tile(o[2], NORM_CHUNK // HEAD_DIM).reshape(1, NORM_CHUNK) for o in outs if o[2] is not None]
    assert sum(widths) == N and T % tm == 0
    assert all(wd % NORM_CHUNK == 0 for wd, nrm in zip(widths, normed) if nrm)
    head_of = jnp.arange(NORM_CHUNK) // HEAD_DIM
    seg = jnp.where(head_of[:, None] == head_of[None, :], 1.0 / HEAD_DIM, 0.0).astype(BF16)
    return pl.pallas_call(
        functools.partial(_rms_proj_kernel, widths=widths, normed=normed),
        grid=(T // tm,),
        in_specs=[pl.BlockSpec((tm, D), lambda i: (i, 0)),
                  pl.BlockSpec((1, D), lambda i: (0, 0)),
                  pl.BlockSpec((D, N), lambda i: (0, 0)),
                  pl.BlockSpec((NORM_CHUNK, NORM_CHUNK), lambda i: (0, 0))]
                 + [pl.BlockSpec((1, NORM_CHUNK), lambda i: (0, 0))] * len(head_gains),
        out_specs=[pl.BlockSpec((tm, wd), lambda i: (i, 0)) for wd in widths],
        out_shape=[jax.ShapeDtypeStruct((T, o[0]), o[1]) for o in outs],
        compiler_params=_cparams(1),
        name=name,
    )(x, gain.reshape(1, D), w, seg, *head_gains)


def _swa_kernel(sink_ref, q_ref, kc_ref, kp_ref, vc_ref, vp_ref, bias_ref, o_ref):
    n = pl.program_id(1)
    L = SWA_BLOCK
    n_blk = q_ref.shape[0] // L
    key = lax.broadcasted_iota(I32, (L, L), 0)
    qry = lax.broadcasted_iota(I32, (L, L), 1)
    from_prev = key > qry
    own = key <= qry
    zero = jnp.zeros((L, L), BF16)
    for blk in range(n_blk):
        rows = slice(blk * L, (blk + 1) * L)
        prows = slice((blk - 1) * L, blk * L)
        outs = []
        for g in range(SWA_KV_HEADS):
            ksl = slice(g * HEAD_DIM, (g + 1) * HEAD_DIM)
            k_prev = kp_ref[:, ksl] if blk == 0 else kc_ref[prows, ksl]
            v_prev = vp_ref[:, ksl] if blk == 0 else vc_ref[prows, ksl]
            k = jnp.concatenate([k_prev, kc_ref[rows, ksl]], axis=0)
            v = jnp.concatenate([v_prev, vc_ref[rows, ksl]], axis=0)
            for h in range(g * SWA_GROUP, (g + 1) * SWA_GROUP):
                sink = sink_ref[h]
                s2 = _dot_nt(k, q_ref[rows, h * HEAD_DIM:(h + 1) * HEAD_DIM])
                s = jnp.where(from_prev, s2[:L], s2[L:]) * (HEAD_DIM ** -0.5) - bias_ref[h]
                if blk == 0:
                    s = jnp.where(jnp.logical_or(n > 0, own), s, NEG_INF)
                m = jnp.maximum(jnp.max(s, axis=0, keepdims=True), sink)
                e = jnp.exp(s - m)
                denom = jnp.sum(e, axis=0, keepdims=True) + jnp.exp(sink - m)
                p = (e * (1.0 / denom)).astype(BF16)
                p2 = jnp.concatenate([jnp.where(from_prev, p, zero), jnp.where(from_prev, zero, p)], axis=0)
                outs.append(_dot_tn(p2, v))
        o_ref[rows, :] = jnp.concatenate(outs, axis=-1).astype(o_ref.dtype)


def swa_attention(q, k, v, sink, batch, n_blk=4):
    T = q.shape[0]
    L = SWA_BLOCK
    nb = T // batch // L
    kvw = SWA_KV_HEADS * HEAD_DIM
    slopes = jnp.exp2(-8.0 * jnp.arange(1, SWA_Q_HEADS + 1, dtype=F32) / SWA_Q_HEADS)
    assert SWA_WINDOW == L
    qi, kj = jnp.arange(L)[:, None], jnp.arange(L)[None, :]
    dist = jnp.where(kj > qi, qi + L - kj, qi - kj).astype(F32)
    bias = slopes[:, None, None] * dist.T
    ns = nb // n_blk
    slab = n_blk * L
    cur = lambda b, n: (b * ns + n, 0)
    prev = lambda b, n: (b * nb + jnp.maximum(n * n_blk - 1, 0), 0)
    return pl.pallas_call(
        _swa_kernel,
        grid=(batch, ns),
        in_specs=[pl.BlockSpec(memory_space=pltpu.SMEM),
                  pl.BlockSpec((slab, MIX_WIDTH), cur),
                  pl.BlockSpec((slab, kvw), cur), pl.BlockSpec((L, kvw), prev),
                  pl.BlockSpec((slab, kvw), cur), pl.BlockSpec((L, kvw), prev),
                  pl.BlockSpec((SWA_Q_HEADS, L, L), lambda b, n: (0, 0, 0))],
        out_specs=pl.BlockSpec((slab, MIX_WIDTH), cur),
        out_shape=jax.ShapeDtypeStruct((T, MIX_WIDTH), BF16),
        compiler_params=_cparams(2),
        name="swa_attention",
    )(sink, q, k, k, v, v, bias)


def _retention_kernel(q_ref, k_ref, v_ref, g_ref, dec_ref, qd_ref, kd_ref, cd_ref, o_ref, state_ref):
    @pl.when(pl.program_id(0) == 0)
    def _():
        state_ref[...] = jnp.zeros_like(state_ref)

    L = RET_CHUNK
    n_ch = q_ref.shape[1] // L
    for b in range(q_ref.shape[0]):
        outs = [[] for _ in range(n_ch)]
        for h in range(RET_HEADS):
            qsl = slice(h * RET_QK_DIM, (h + 1) * RET_QK_DIM)
            vsl = slice(h * RET_V_DIM, (h + 1) * RET_V_DIM)
            state = state_ref[b, h]
            for c in range(n_ch):
                rows = slice(c * L, (c + 1) * L)
                q = q_ref[b, rows, qsl]
                k = k_ref[b, rows, qsl] * (RET_QK_DIM ** -0.5)
                v = v_ref[b, rows, vsl]
                s = _dot_nt(q, k.astype(BF16)) * dec_ref[h]
                o = _dot(s.astype(BF16), v) + _dot(q, state.astype(BF16)) * qd_ref[h]
                state = state * cd_ref[h] + _dot_tn((k * kd_ref[h]).astype(BF16), v)
                outs[c].append(_silu(g_ref[b, rows, vsl]) * _rms(o))
            state_ref[b, h] = state
        for c in range(n_ch):
            o_ref[b, c * L:(c + 1) * L, :] = jnp.concatenate(outs[c], axis=-1).astype(o_ref.dtype)


def _retention_consts():
    H, L = RET_HEADS, RET_CHUNK
    log_gamma = jnp.log1p(-jnp.exp2(-5.0 - jnp.arange(H, dtype=F32)))
    pos = jnp.arange(L, dtype=F32)
    rel = pos[:, None] - pos[None, :]
    intra = jnp.exp(jnp.where(rel >= 0, log_gamma[:, None, None] * rel, -jnp.inf))
    q_decay = jnp.exp(log_gamma[:, None] * (pos + 1.0))[..., None]
    k_decay = jnp.exp(log_gamma[:, None] * (L - 1.0 - pos))[..., None]
    chunk_decay = jnp.exp(log_gamma * L)[:, None, None]
    return (intra,
            jnp.broadcast_to(q_decay, (H, L, RET_V_DIM)),
            jnp.broadcast_to(k_decay, (H, L, RET_QK_DIM)),
            jnp.broadcast_to(chunk_decay, (H, RET_QK_DIM, RET_V_DIM)))


def retention(q, k, v, g, batch, n_ch=4):
    T = q.shape[0]
    L = RET_CHUNK
    nc = T // batch // L
    H = RET_HEADS
    intra, qd, kd, cd = _retention_consts()
    cur = lambda c: (0, c, 0)
    const = lambda c: (0, 0, 0)
    qkw = H * RET_QK_DIM
    per_batch = lambda a: a.reshape(batch, T // batch, a.shape[1])
    slab = n_ch * L
    out = pl.pallas_call(
        _retention_kernel,
        grid=(nc // n_ch,),
        in_specs=[pl.BlockSpec((batch, slab, qkw), cur), pl.BlockSpec((batch, slab, qkw), cur),
                  pl.BlockSpec((batch, slab, MIX_WIDTH), cur), pl.BlockSpec((batch, slab, MIX_WIDTH), cur),
                  pl.BlockSpec((H, L, L), const), pl.BlockSpec((H, L, RET_V_DIM), const),
                  pl.BlockSpec((H, L, RET_QK_DIM), const), pl.BlockSpec((H, RET_QK_DIM, RET_V_DIM), const)],
        out_specs=pl.BlockSpec((batch, slab, MIX_WIDTH), cur),
        out_shape=jax.ShapeDtypeStruct((batch, T // batch, MIX_WIDTH), BF16),
        scratch_shapes=[pltpu.VMEM((batch, H, RET_QK_DIM, RET_V_DIM), F32)],
        compiler_params=_cparams(1),
        name="retention",
    )(per_batch(q), per_batch(k), per_batch(v), per_batch(g), intra, qd, kd, cd)
    return out.reshape(T, MIX_WIDTH)


def _split3(x):
    x1 = x.astype(BF16)
    r1 = x - x1.astype(F32)
    x2 = r1.astype(BF16)
    x3 = (r1 - x2.astype(F32)).astype(BF16)
    return x1, x2, x3


def _mlstm_kernel(qk_ref, v_ref, og_ref, gc_ref, gr_ref, cw_ref, cb_ref, bc_ref, br_ref, gain_ref,
                  tri_ref, trit_ref, o_ref, xbuf_ref, c_ref, n_ref, m_ref):
    for b in range(qk_ref.shape[0]):
        _mlstm_chunk(qk_ref.at[b], v_ref.at[b], og_ref.at[b], gc_ref.at[b], gr_ref.at[b], cw_ref, cb_ref,
                     bc_ref, br_ref, gain_ref, tri_ref, trit_ref, o_ref.at[b],
                     xbuf_ref.at[b], c_ref.at[b], n_ref.at[b], m_ref.at[b])


def _mlstm_chunk(qk_ref, v_ref, og_ref, gc_ref, gr_ref, cw_ref, cb_ref, bc_ref, br_ref, gain_ref,
                 tri_ref, trit_ref, o_ref, xbuf_ref, c_ref, n_ref, m_ref):
    L = MLSTM_CHUNK
    H = MLSTM_HEADS
    P = MLSTM_QK_PAD
    VP = MLSTM_V_PAD
    KT = SUBLANES

    @pl.when(pl.program_id(0) == 0)
    def _():
        xbuf_ref[0:KT, :] = jnp.zeros((KT, xbuf_ref.shape[1]), F32)
        c_ref[...] = jnp.zeros_like(c_ref)
        n_ref[...] = jnp.zeros_like(n_ref)
        m_ref[...] = jnp.zeros_like(m_ref)

    xbuf_ref[KT:KT + L, :] = qk_ref[...]
    acc = cb_ref[...] + cw_ref[MLSTM_CONV - 1:MLSTM_CONV, :] * xbuf_ref[KT:KT + L, :]
    for j in range(MLSTM_CONV - 1):
        sh = MLSTM_CONV - 1 - j
        acc = acc + cw_ref[j:j + 1, :] * xbuf_ref[KT - sh:KT - sh + L, :]
    xbuf_ref[0:KT, :] = qk_ref[L - KT:L, :]
    qk = _silu(acc)

    xc = gc_ref[...] + bc_ref[...]
    xr = gr_ref[...] + br_ref[...]
    lfc = jax.nn.log_sigmoid(xc)
    lfr = jax.nn.log_sigmoid(xr)
    tri = tri_ref[...]
    trit = trit_ref[...]
    bc = sum(_dot(tri, t) for t in _split3(lfc))
    br = sum(_dot(t, trit) for t in _split3(lfr))

    rowi = lax.broadcasted_iota(I32, (L, L), 0)
    coli = lax.broadcasted_iota(I32, (L, L), 1)
    causal = rowi >= coli
    outs = []
    for h in range(H):
        q = qk[:, h * P:(h + 1) * P].astype(BF16)
        k = qk[:, (H + h) * P:(H + h + 1) * P] * (MLSTM_QK_DIM ** -0.5)
        v = v_ref[:, h * VP:(h + 1) * VP]
        li_c = xc[:, h:h + 1]
        b_c = bc[:, H + h:H + h + 1]
        li_r = xr[h:h + 1, :]
        b_r = br[H + h:H + h + 1, :]
        g = b_c[L - 1:L, :]
        m = m_ref[h:h + 1, 0:1]
        C = c_ref[h]
        nvec = n_ref[h:h + 1, :]

        dmat = jnp.where(causal, b_c - b_r + li_r, -jnp.inf)
        inter = b_c + m
        m_t = jnp.maximum(inter, jnp.max(dmat, axis=-1, keepdims=True))
        w = jnp.exp(dmat - m_t)
        a = jnp.exp(inter - m_t)
        s = _dot_nt(q, k.astype(BF16)) * w
        num = _dot(s.astype(BF16), v) + a * _dot(q, C.astype(BF16))
        qf = qk[:, h * P:(h + 1) * P]
        den = jnp.sum(s, axis=-1, keepdims=True) + a * jnp.sum(qf * nvec, axis=-1, keepdims=True)
        hh = num / jnp.maximum(jnp.abs(den), jnp.exp(-m_t))

        u_c = g - b_c + li_c
        u_r = g - b_r + li_r
        m_new = jnp.maximum(g + m, jnp.max(u_r, axis=-1, keepdims=True))
        wk = jnp.exp(u_c - m_new)
        decay = jnp.exp(g + m - m_new)
        kw = k * wk
        c_ref[h] = decay * C + _dot_tn(kw.astype(BF16), v)
        n_ref[h:h + 1, :] = decay * nvec + jnp.sum(kw, axis=0, keepdims=True)
        m_ref[h:h + 1, :] = jnp.broadcast_to(m_new, (1, m_ref.shape[1]))

        ms = jnp.sum(hh * hh, axis=-1, keepdims=True) * (1.0 / MLSTM_V_DIM)
        hc = hh * lax.rsqrt(ms + RMS_EPS) * gain_ref[:, h * VP:(h + 1) * VP]
        outs.append(jax.nn.sigmoid(og_ref[:, h * VP:(h + 1) * VP]) * hc)
    o_ref[...] = jnp.concatenate(outs, axis=-1).astype(o_ref.dtype)


def mlstm(qk, v, og, gates, gates_t, conv_w, conv_b, bias_c, bias_r, gain, batch):
    T = qk.shape[0]
    L = MLSTM_CHUNK
    nc = T // batch // L
    H, P, VP = MLSTM_HEADS, MLSTM_QK_PAD, MLSTM_V_PAD
    tri = jnp.tril(jnp.ones((L, L), BF16))
    cur = lambda c: (0, c, 0)
    c2 = lambda c: (0, 0)
    per_batch = lambda a: a.reshape(batch, T // batch, a.shape[1])
    out = pl.pallas_call(
        _mlstm_kernel,
        grid=(nc,),
        in_specs=[pl.BlockSpec((batch, L, 2 * H * P), cur), pl.BlockSpec((batch, L, H * VP), cur),
                  pl.BlockSpec((batch, L, H * VP), cur), pl.BlockSpec((batch, L, LANES), cur),
                  pl.BlockSpec((batch, SUBLANES, L), lambda c: (0, 0, c)),
                  pl.BlockSpec((MLSTM_CONV, 2 * H * P), c2), pl.BlockSpec((1, 2 * H * P), c2),
                  pl.BlockSpec((1, LANES), c2), pl.BlockSpec((SUBLANES, L), c2),
                  pl.BlockSpec((1, H * VP), c2),
                  pl.BlockSpec((L, L), c2), pl.BlockSpec((L, L), c2)],
        out_specs=pl.BlockSpec((batch, L, H * VP), cur),
        out_shape=jax.ShapeDtypeStruct((batch, T // batch, H * VP), BF16),
        scratch_shapes=[pltpu.VMEM((batch, SUBLANES + L, 2 * H * P), F32),
                        pltpu.VMEM((batch, H, P, VP), F32),
                        pltpu.VMEM((batch, SUBLANES, P), F32),
                        pltpu.VMEM((batch, SUBLANES, LANES), F32)],
        compiler_params=_cparams(1),
        name="mlstm",
    )(per_batch(qk), per_batch(v), per_batch(og), per_batch(gates), gates_t,
      conv_w, conv_b, bias_c, bias_r, gain, tri, tri.T)
    return out.reshape(T, H * VP)


def _out_proj_kernel(h_ref, mix_ref, q_ref, k_ref, v_ref, w1_ref, w2_ref, o_ref):
    outs = []
    for hd in range(MEM_HEADS):
        sl = slice(hd * MEM_HEAD_DIM, (hd + 1) * MEM_HEAD_DIM)
        s = _dot_nt(k_ref[:, sl], q_ref[:, sl]) * (MEM_HEAD_DIM ** -0.5)
        e = jnp.exp(s - jnp.max(s, axis=0, keepdims=True))
        p = (e * (1.0 / jnp.sum(e, axis=0, keepdims=True))).astype(BF16)
        outs.append(_dot_tn(p, v_ref[:, sl]))
    mem_out = jnp.concatenate(outs, axis=-1).astype(BF16)
    o_ref[...] = h_ref[...] + _dot(mix_ref[...], w1_ref[...]) + _dot(mem_out, w2_ref[...])


def out_proj(h, mix, qm, mk, mv, w_mix, w_mem, batch, tm):
    T, D = h.shape
    Wm = mix.shape[1]
    M = mk.shape[0] // batch
    nt = T // batch // tm
    rows = lambda b, i: (b * nt + i, 0)
    mem = lambda b, i: (b, 0)
    const = lambda b, i: (0, 0)
    return pl.pallas_call(
        _out_proj_kernel,
        grid=(batch, nt),
        in_specs=[pl.BlockSpec((tm, D), rows),
                  pl.BlockSpec((tm, Wm), rows),
                  pl.BlockSpec((tm, MEM_WIDTH), rows),
                  pl.BlockSpec((M, MEM_WIDTH), mem),
                  pl.BlockSpec((M, MEM_WIDTH), mem),
                  pl.BlockSpec((Wm, D), const),
                  pl.BlockSpec((MEM_WIDTH, D), const)],
        out_specs=pl.BlockSpec((tm, D), rows),
        out_shape=jax.ShapeDtypeStruct((T, D), F32),
        compiler_params=_cparams(2),
        name="out_proj",
    )(h, mix, qm, mk, mv, w_mix, w_mem)


def _dense_ffn_kernel(h_ref, g_ref, wg_ref, wu_ref, wd_ref, o_ref, xn_ref):
    @pl.when(pl.program_id(1) == 0)
    def _():
        h = h_ref[...]
        xn_ref[...] = (_rms(h) * g_ref[...]).astype(BF16)
        o_ref[...] = h

    xn = xn_ref[...]
    a = _silu(_dot(xn, wg_ref[...])) * _dot(xn, wu_ref[...])
    o_ref[...] += _dot(a.astype(BF16), wd_ref[...])


def dense_ffn(h, gain, w_gate, w_up, w_down, tm, tf):
    T, D = h.shape
    Fd = w_gate.shape[1]
    return pl.pallas_call(
        _dense_ffn_kernel,
        grid=(T // tm, Fd // tf),
        in_specs=[pl.BlockSpec((tm, D), lambda i, j: (i, 0)),
                  pl.BlockSpec((1, D), lambda i, j: (0, 0)),
                  pl.BlockSpec((D, tf), lambda i, j: (0, j)),
                  pl.BlockSpec((D, tf), lambda i, j: (0, j)),
                  pl.BlockSpec((tf, D), lambda i, j: (j, 0))],
        out_specs=pl.BlockSpec((tm, D), lambda i, j: (i, 0)),
        out_shape=jax.ShapeDtypeStruct((T, D), F32),
        scratch_shapes=[pltpu.VMEM((tm, D), BF16)],
        compiler_params=_cparams(2),
        name="dense_ffn",
    )(h, gain.reshape(1, D), w_gate, w_up, w_down)


def _router_kernel(h_ref, g_ref, wr_ref, tri_ref, o_ref, ot_ref, cnt_ref, carry_ref):
    i = pl.program_id(0)

    @pl.when(i == 0)
    def _():
        carry_ref[...] = jnp.zeros_like(carry_ref)

    tm = h_ref.shape[0]
    xn = _rms(h_ref[...]) * g_ref[...]
    x_hi = xn.astype(BF16)
    x_lo = (xn - x_hi.astype(F32)).astype(BF16)
    wr = wr_ref[...]
    w_hi = wr.astype(BF16)
    w_lo = (wr - w_hi.astype(F32)).astype(BF16)
    logits = _dot(x_hi, w_hi) + (_dot(x_hi, w_lo) + _dot(x_lo, w_hi))
    lane = lax.broadcasted_iota(I32, (tm, LANES), 1)
    logits = jnp.where(lane < N_EXPERTS, logits, -jnp.inf)
    t1 = jnp.max(logits, axis=-1, keepdims=True)
    e1 = jnp.min(jnp.where(logits == t1, lane, LANES), axis=-1, keepdims=True)
    rest = jnp.where(lane == e1, -jnp.inf, logits)
    t2 = jnp.max(rest, axis=-1, keepdims=True)
    e2 = jnp.min(jnp.where(rest == t2, lane, LANES), axis=-1, keepdims=True)
    x2 = jnp.exp(t2 - t1)
    w1 = 1.0 / (1.0 + x2)
    w2 = x2 / (1.0 + x2)

    oh1 = lane == e1
    oh2 = lane == e2
    cnt = jnp.where(oh1 | oh2, 1.0, 0.0)
    before = _dot(tri_ref[...], cnt.astype(BF16)) + carry_ref[...]
    r1 = jnp.sum(jnp.where(oh1, before, 0.0), axis=-1, keepdims=True)
    r2 = jnp.sum(jnp.where(oh2, before, 0.0), axis=-1, keepdims=True)
    carry_ref[...] += jnp.sum(cnt, axis=0, keepdims=True)

    cols = (e1.astype(F32), e2.astype(F32), w1, w2, r1, r2)
    out = jnp.zeros((tm, LANES), F32)
    for c, val in enumerate(cols):
        out = jnp.where(lane == c, val, out)
    o_ref[...] = out
    ot_ref[...] = out.T[:SUBLANES, :]
    cnt_ref[...] = jnp.broadcast_to(carry_ref[...], cnt_ref.shape)


def moe_router(h, gain, w_router, tm):
    T, D = h.shape
    wr = jnp.zeros((D, LANES), F32).at[:, :N_EXPERTS].set(w_router)
    tri = jnp.tril(jnp.ones((tm, tm), BF16), k=-1)
    return pl.pallas_call(
        _router_kernel,
        grid=(T // tm,),
        in_specs=[pl.BlockSpec((tm, D), lambda i: (i, 0)),
                  pl.BlockSpec((1, D), lambda i: (0, 0)),
                  pl.BlockSpec((D, LANES), lambda i: (0, 0)),
                  pl.BlockSpec((tm, tm), lambda i: (0, 0))],
        out_specs=[pl.BlockSpec((tm, LANES), lambda i: (i, 0)),
                   pl.BlockSpec((SUBLANES, tm), lambda i: (0, i)),
                   pl.BlockSpec((SUBLANES, LANES), lambda i: (0, 0))],
        out_shape=[jax.ShapeDtypeStruct((T, LANES), F32),
                   jax.ShapeDtypeStruct((SUBLANES, T), F32),
                   jax.ShapeDtypeStruct((SUBLANES, LANES), F32)],
        scratch_shapes=[pltpu.VMEM((1, LANES), F32)],
        compiler_params=_cparams(1),
        name="moe_router",
    )(h, gain.reshape(1, D), wr, tri)


def _dispatch_kernel(ze_ref, zn_ref, p1_ref, p2_ref, h_ref, xs_ref, zero_ref, sem, zsem, *, pad_max, tail_max):
    tb = p1_ref.shape[0]
    zr = zero_ref.shape[0]

    @pl.when(pl.program_id(0) == 0)
    def _():
        zero_ref[...] = jnp.zeros_like(zero_ref)
        chunks = [(e, c) for e in range(N_EXPERTS) for c in range(pad_max // zr)]
        chunks += [(N_EXPERTS, c) for c in range(tail_max // zr)]

        def zcopy(e, c):
            start = pl.multiple_of(ze_ref[e] - (c + 1) * zr, zr)
            return pltpu.make_async_copy(zero_ref, xs_ref.at[pl.ds(start, zr), :], zsem)

        def needed(e, c):
            return c * zr < zn_ref[e]

        for e, c in chunks:
            @pl.when(needed(e, c))
            def _():
                zcopy(e, c).start()

        for e, c in chunks:
            @pl.when(needed(e, c))
            def _():
                zcopy(e, c).wait()

    def copy(t, pos):
        return pltpu.make_async_copy(h_ref.at[pl.ds(t, 1), :], xs_ref.at[pl.ds(pos, 1), :], sem)

    def issue(t, c):
        copy(t, p1_ref[t]).start()
        copy(t, p2_ref[t]).start()
        return c

    lax.fori_loop(0, tb, issue, 0, unroll=ROW_DMA_UNROLL)
    for _ in range(TOP_K):
        pltpu.make_async_copy(h_ref, xs_ref.at[pl.ds(0, tb), :], sem).wait()


def moe_dispatch(h, pos1, pos2, pad_end, pad_len, used_rows, n_rows, pad_max, tb, zr=256):
    T, D = h.shape
    tail_max = n_rows - TOP_K * T
    assert tail_max % zr == 0 and pad_max % zr == 0
    zero_end = jnp.concatenate([pad_end, jnp.full((1,), n_rows, I32)])
    zero_len = jnp.concatenate([pad_len, n_rows - used_rows[None]])
    grid_spec = pltpu.PrefetchScalarGridSpec(
        num_scalar_prefetch=2,
        grid=(T // tb,),
        in_specs=[pl.BlockSpec((tb,), lambda i, ps, pn: (i,), memory_space=pltpu.SMEM),
                  pl.BlockSpec((tb,), lambda i, ps, pn: (i,), memory_space=pltpu.SMEM),
                  pl.BlockSpec((tb, D), lambda i, ps, pn: (i, 0))],
        out_specs=pl.BlockSpec(memory_space=pl.ANY),
        scratch_shapes=[pltpu.VMEM((zr, D), h.dtype), pltpu.SemaphoreType.DMA(()), pltpu.SemaphoreType.DMA(())],
    )
    return pl.pallas_call(
        functools.partial(_dispatch_kernel, pad_max=pad_max, tail_max=tail_max),
        grid_spec=grid_spec,
        out_shape=jax.ShapeDtypeStruct((n_rows, D), h.dtype),
        compiler_params=_cparams(1),
        name="moe_dispatch",
    )(zero_end, zero_len, pos1, pos2, h)


def _expert_ffn_kernel(te_ref, tv_ref, x_ref, g_ref, wg_ref, wu_ref, wd_ref, wdl_ref, o_ref, xn_ref, a_ref,
                       *, ts, nf):
    i = pl.program_id(0)
    j = pl.program_id(1)
    valid = tv_ref[i]
    n_sub = x_ref.shape[0] // ts

    n_occ = (valid + ts - 1) // ts
    for k in range(1, n_sub + 1):
        rows = pl.ds(0, k * ts)

        def gate_up(slot):
            xn = xn_ref[rows, :]
            a = _silu(_dot(xn, wg_ref[...].astype(BF16))) * _dot(xn, wu_ref[...].astype(BF16))
            a_ref[slot, rows, :] = a.astype(BF16)

        def down(slot, first, w_ref=wd_ref):
            y = _dot(a_ref[slot, rows, :], w_ref[...].astype(BF16))
            if first:
                o_ref[rows, :] = y
                if k < n_sub:
                    o_ref[pl.ds(k * ts, (n_sub - k) * ts), :] = jnp.zeros(((n_sub - k) * ts, o_ref.shape[1]), F32)
            else:
                o_ref[rows, :] += y

        @pl.when(n_occ == k)
        def _():
            @pl.when(j == 0)
            def _():
                xn_ref[rows, :] = (_rms(x_ref[rows, :]) * g_ref[...]).astype(BF16)
                gate_up(0)

            @pl.when(j == 1)
            def _():
                gate_up(1)
                down(0, first=True)

            @pl.when((j > 1) & (j < nf - 1))
            def _():
                gate_up(j % 2)
                down((j - 1) % 2, first=False)

            @pl.when(j == nf - 1)
            def _():
                gate_up((nf - 1) % 2)
                down((nf - 2) % 2, first=False)
                down((nf - 1) % 2, first=False, w_ref=wdl_ref)

    @pl.when((n_occ == 0) & (j == 0))
    def _():
        o_ref[...] = jnp.zeros_like(o_ref)


def expert_ffn(xs, gain, w_gate, w_up, w_down, layer, tile_expert, tile_valid, n_tiles, tm, ts, tf):
    D = xs.shape[1]
    R = n_tiles * tm
    Fd = w_gate.shape[3]
    nf = Fd // tf

    assert nf >= 3
    def up_tile(i, j, tv):
        return jnp.where(tv[i] > 0, j, nf - 1)

    def down_tile(i, j, tv):
        return jnp.where(tv[i] > 0, jnp.maximum(j - 1, 0), nf - 2)

    grid_spec = pltpu.PrefetchScalarGridSpec(
        num_scalar_prefetch=2,
        grid=(n_tiles, nf),
        in_specs=[pl.BlockSpec((tm, D), lambda i, j, te, tv: (jnp.where(tv[i] > 0, i, 0), 0)),
                  pl.BlockSpec((1, D), lambda i, j, te, tv: (0, 0)),
                  pl.BlockSpec((None, None, D, tf), lambda i, j, te, tv: (layer, te[i], 0, up_tile(i, j, tv))),
                  pl.BlockSpec((None, None, D, tf), lambda i, j, te, tv: (layer, te[i], 0, up_tile(i, j, tv))),
                  pl.BlockSpec((None, None, tf, D), lambda i, j, te, tv: (layer, te[i], down_tile(i, j, tv), 0)),
                  pl.BlockSpec((None, None, tf, D), lambda i, j, te, tv: (layer, te[i], nf - 1, 0))],
        out_specs=pl.BlockSpec((tm, D), lambda i, j, te, tv: (i, 0)),
        scratch_shapes=[pltpu.VMEM((tm, D), BF16), pltpu.VMEM((2, tm, tf), BF16)],
    )
    return pl.pallas_call(
        functools.partial(_expert_ffn_kernel, ts=ts, nf=nf),
        grid_spec=grid_spec,
        out_shape=jax.ShapeDtypeStruct((R, D), F32),
        compiler_params=_cparams(2),
        name="expert_ffn",
    )(tile_expert, tile_valid, xs, gain.reshape(1, D), w_gate, w_up, w_down, w_down)


def _combine_kernel(p1_ref, p2_ref, route_ref, h_ref, y_ref, o_ref, b1_ref, b2_ref, sem):
    tb = p1_ref.shape[0]

    def copies(t):
        dst = pl.ds(t, 1)
        return (pltpu.make_async_copy(y_ref.at[pl.ds(p1_ref[t], 1), :], b1_ref.at[dst, :], sem),
                pltpu.make_async_copy(y_ref.at[pl.ds(p2_ref[t], 1), :], b2_ref.at[dst, :], sem))

    def issue(t, c):
        for cp in copies(t):
            cp.start()
        return c

    lax.fori_loop(0, tb, issue, 0, unroll=ROW_DMA_UNROLL)
    for b_ref in (b1_ref, b2_ref):
        pltpu.make_async_copy(y_ref.at[pl.ds(0, tb), :], b_ref, sem).wait()
    w1 = route_ref[:, 2:3]
    w2 = route_ref[:, 3:4]
    o_ref[...] = h_ref[...] + w1 * b1_ref[...] + w2 * b2_ref[...]


def moe_combine(h, y, route, pos1, pos2, tb):
    T, D = h.shape
    smem = lambda: pl.BlockSpec((tb,), lambda i: (i,), memory_space=pltpu.SMEM)
    return pl.pallas_call(
        _combine_kernel,
        grid=(T // tb,),
        in_specs=[smem(), smem(),
                  pl.BlockSpec((tb, LANES), lambda i: (i, 0)),
                  pl.BlockSpec((tb, D), lambda i: (i, 0)),
                  pl.BlockSpec(memory_space=pl.ANY)],
        out_specs=pl.BlockSpec((tb, D), lambda i: (i, 0)),
        out_shape=jax.ShapeDtypeStruct((T, D), F32),
        scratch_shapes=[pltpu.VMEM((tb, D), F32), pltpu.VMEM((tb, D), F32),
                        pltpu.SemaphoreType.DMA(())],
        compiler_params=_cparams(1),
        name="moe_combine",
    )(pos1, pos2, route, h, y)


def moe_ffn(h, gain, w_router, w_gate, w_up, w_down, layer, tm=1024, ts=256, tf=512, tb=512):
    T = h.shape[0]
    route, route_t, counts = moe_router(h, gain, w_router, tm=512)
    e1 = route_t[0].astype(I32)
    e2 = route_t[1].astype(I32)
    counts = counts[0, :N_EXPERTS].astype(I32)
    tiles_per = (counts + tm - 1) // tm
    tile_end = jnp.cumsum(tiles_per)
    tile_start = tile_end - tiles_per
    row_start = tile_start * tm
    pos1 = row_start[e1] + route_t[4].astype(I32)
    pos2 = row_start[e2] + route_t[5].astype(I32)
    n_tiles = (TOP_K * T) // tm + N_EXPERTS
    tile_ids = jnp.arange(n_tiles, dtype=I32)
    tile_expert = jnp.sum(jnp.minimum(tile_ids, tile_end[-1] - 1)[:, None] >= tile_end[None, :], axis=1).astype(I32)
    tile_valid = jnp.clip(counts[tile_expert] - (tile_ids - tile_start[tile_expert]) * tm, 0, tm)
    tile_valid = jnp.where(tile_ids < tile_end[-1], tile_valid, 0).astype(I32)
    pad_len = tiles_per * tm - counts

    xs = moe_dispatch(h, pos1, pos2, tile_end * tm, pad_len, tile_end[-1] * tm, n_tiles * tm, tm, tb)
    y = expert_ffn(xs, gain, w_gate, w_up, w_down, layer, tile_expert, tile_valid, n_tiles, tm, ts, tf)
    return moe_combine(h, y, route, pos1, pos2, tb)


def _pad_heads(a, n_heads, width, pad_to, axis):
    shape = list(a.shape)
    a = a.reshape(shape[:axis] + [n_heads, width] + shape[axis + 1:])
    pads = [(0, 0)] * a.ndim
    pads[axis + 1] = (0, pad_to - width)
    a = jnp.pad(a, pads)
    return a.reshape(shape[:axis] + [n_heads * pad_to] + shape[axis + 1:])


def kernel(x, mem, ln_mix, ln_mem, w_mem_kv, mem_q_gain, mem_k_gain, w_out, ln_ffn, swa_w_in, swa_q_gain, swa_k_gain, swa_sink, ret_w_in, mlstm_w_in, mlstm_conv_w, mlstm_conv_b, mlstm_i_bias, mlstm_f_bias, mlstm_out_gain, ffn_w_gate, ffn_w_up, ffn_w_down, moe_router, moe_w_gate, moe_w_up, moe_w_down):
    B, S, D = x.shape
    M = mem.shape[1]
    T = B * S
    depth = ln_mix.shape[0]
    h = x.reshape(T, D)
    mem2 = mem.reshape(B * M, D)

    for layer in range(depth):
        kind = layer % N_MIXERS
        idx = layer // N_MIXERS
        mk, mv = rms_proj(mem2, ln_mem[layer], w_mem_kv[layer].astype(BF16),
                          ((MEM_WIDTH, BF16, mem_k_gain[layer]), (MEM_WIDTH, BF16, None)),
                          tm=B * M, name="mem_kv_proj")
        w_o = w_out[layer].astype(BF16)
        w_o_mix, w_o_mem = w_o[:MIX_WIDTH], w_o[MIX_WIDTH:]
        qm_out = (MEM_WIDTH, BF16, mem_q_gain[layer])

        if kind == 0:
            kvw = SWA_KV_HEADS * HEAD_DIM
            q, k, v, qm = rms_proj(h, ln_mix[layer], swa_w_in[idx].astype(BF16),
                                   ((MIX_WIDTH, BF16, swa_q_gain[idx]), (kvw, BF16, swa_k_gain[idx]),
                                    (kvw, BF16, None), qm_out), tm=1024, name="swa_in_proj")
            mix = swa_attention(q, k, v, swa_sink[idx], B)
        elif kind == 1:
            qkw = RET_HEADS * RET_QK_DIM
            q, k, v, g, qm = rms_proj(h, ln_mix[layer], ret_w_in[idx].astype(BF16),
                                      ((qkw, BF16, None), (qkw, F32, None), (MIX_WIDTH, BF16, None),
                                       (MIX_WIDTH, F32, None), qm_out), tm=1024, name="ret_in_proj")
            mix = retention(q, k, v, g, B)
        else:
            H, P, VP = MLSTM_HEADS, MLSTM_QK_PAD, MLSTM_V_PAD
            w = mlstm_w_in[idx]
            qkw = 2 * H * MLSTM_QK_DIM
            o_v, o_og, o_ig = qkw, qkw + MIX_WIDTH, qkw + 2 * MIX_WIDTH
            o_fg, o_qm = o_ig + H, o_ig + 2 * H
            w_gates = jnp.zeros((D, LANES), F32).at[:, :2 * H].set(w[:, o_ig:o_qm])
            w_pad = jnp.concatenate([
                _pad_heads(w[:, :qkw], 2 * H, MLSTM_QK_DIM, P, 1),
                _pad_heads(w[:, o_v:o_og], H, MLSTM_V_DIM, VP, 1),
                _pad_heads(w[:, o_og:o_ig], H, MLSTM_V_DIM, VP, 1),
                w_gates, w[:, o_qm:]], axis=1).astype(BF16)
            qk, v, og, gates, qm = rms_proj(h, ln_mix[layer], w_pad,
                                            ((2 * H * P, F32, None), (H * VP, BF16, None), (H * VP, F32, None),
                                             (LANES, F32, None), qm_out), tm=1024, name="mlstm_in_proj")
            gates_t = gates[:, :SUBLANES].reshape(B, S, SUBLANES).transpose(0, 2, 1)
            bias = jnp.concatenate([mlstm_i_bias[idx], mlstm_f_bias[idx]])
            bias_c = jnp.zeros((1, LANES), F32).at[0, :2 * H].set(bias)
            bias_r = jnp.broadcast_to(bias[:, None], (SUBLANES, MLSTM_CHUNK))
            mix = mlstm(qk, v, og, gates, gates_t,
                        _pad_heads(mlstm_conv_w[idx], 2 * H, MLSTM_QK_DIM, P, 1),
                        _pad_heads(mlstm_conv_b[idx][None], 2 * H, MLSTM_QK_DIM, P, 1),
                        bias_c, bias_r,
                        _pad_heads(mlstm_out_gain[idx][None], H, MLSTM_V_DIM, VP, 1), B)
            w_o_mix = _pad_heads(w_o_mix, H, MLSTM_V_DIM, VP, 0)

        h = out_proj(h, mix, qm, mk, mv, w_o_mix, w_o_mem, B, tm=1024)
        j = layer // 2
        if layer % 2 == 0:
            h = dense_ffn(h, ln_ffn[layer], ffn_w_gate[j].astype(BF16), ffn_w_up[j].astype(BF16),
                          ffn_w_down[j].astype(BF16), tm=1024, tf=1792)
        else:
            h = moe_ffn(h, ln_ffn[layer], moe_router[j], moe_w_gate, moe_w_up, moe_w_down, j)
    return h.reshape(B, S, D)
```

```python
import functools
import math

import jax
import jax.numpy as jnp
from jax import lax
from jax.experimental import pallas as pl
from jax.experimental.pallas import tpu as pltpu

F32 = jnp.float32
BF16 = jnp.bfloat16
I32 = jnp.int32

D_MODEL = 1024
N_MIXERS = 3
HEAD_DIM = 64
MEM_HEADS = 4
MEM_HEAD_DIM = 64
MEM_WIDTH = MEM_HEADS * MEM_HEAD_DIM
MIX_WIDTH = D_MODEL - MEM_WIDTH
RMS_EPS = 1e-6
NEG_INF = -1e30

SWA_Q_HEADS = MIX_WIDTH // HEAD_DIM
SWA_KV_HEADS = 4
SWA_GROUP = SWA_Q_HEADS // SWA_KV_HEADS
SWA_WINDOW = 128
SWA_BLOCK = 128

RET_HEADS = 6
RET_QK_DIM = 64
RET_V_DIM = MIX_WIDTH // RET_HEADS
RET_CHUNK = 128

MLSTM_HEADS = 4
MLSTM_V_DIM = MIX_WIDTH // MLSTM_HEADS
MLSTM_QK_DIM = MLSTM_V_DIM // 2
MLSTM_CHUNK = 128
MLSTM_CONV = 4
MLSTM_QK_PAD = 128
MLSTM_V_PAD = 256

N_EXPERTS = 8
TOP_K = 2

LANES = 128
SUBLANES = 8
VMEM_LIMIT = 56 * 1024 * 1024


def _cparams(n_axes):
    return pltpu.CompilerParams(dimension_semantics=("arbitrary",) * n_axes,
                                vmem_limit_bytes=VMEM_LIMIT)


def _rms(x, eps=RMS_EPS):
    return x * lax.rsqrt(jnp.mean(x * x, axis=-1, keepdims=True) + eps)


def _dot(a, b):
    return jnp.dot(a, b, preferred_element_type=F32)


def _dot_nt(a, b):
    return lax.dot_general(a, b, (((1,), (1,)), ((), ())), preferred_element_type=F32)


def _dot_tn(a, b):
    return lax.dot_general(a, b, (((0,), (0,)), ((), ())), preferred_element_type=F32)


def _silu(x):
    return x * jax.nn.sigmoid(x)


NORM_CHUNK = 256


def _head_rms(y, seg_ref, hg):
    sq = y * y
    hi = sq.astype(BF16)
    lo = (sq - hi.astype(F32)).astype(BF16)
    ms = _dot(hi, seg_ref[...]) + _dot(lo, seg_ref[...])
    return y * lax.rsqrt(ms + RMS_EPS) * hg


def _rms_proj_kernel(x_ref, g_ref, w_ref, seg_ref, *refs, widths, normed):
    n_gain = sum(normed)
    gain_refs, o_refs = refs[:n_gain], refs[n_gain:]
    xn = (_rms(x_ref[...]) * g_ref[...]).astype(BF16)
    off = 0
    gi = 0
    for o_ref, wd, nrm in zip(o_refs, widths, normed):
        if nrm:
            hg = gain_refs[gi][...]
            gi += 1
            for c in range(0, wd, NORM_CHUNK):
                y = _dot(xn, w_ref[:, off + c:off + c + NORM_CHUNK])
                o_ref[:, c:c + NORM_CHUNK] = _head_rms(y, seg_ref, hg).astype(o_ref.dtype)
        else:
            o_ref[...] = _dot(xn, w_ref[:, off:off + wd]).astype(o_ref.dtype)
        off += wd


def rms_proj(x, gain, w, outs, tm, name):
    T, D = x.shape
    N = w.shape[1]
    widths = tuple(o[0] for o in outs)
    normed = tuple(o[2] is not None for o in outs)
    head_gains = [jnp.tile(o[2], NORM_CHUNK // HEAD_DIM).reshape(1, NORM_CHUNK) for o in outs if o[2] is not None]
    assert sum(widths) == N and T % tm == 0
    assert all(wd % NORM_CHUNK == 0 for wd, nrm in zip(widths, normed) if nrm)
    head_of = jnp.arange(NORM_CHUNK) // HEAD_DIM
    seg = jnp.where(head_of[:, None] == head_of[None, :], 1.0 / HEAD_DIM, 0.0).astype(BF16)
    return pl.pallas_call(
        functools.partial(_rms_proj_kernel, widths=widths, normed=normed),
        grid=(T // tm,),
        in_specs=[pl.BlockSpec((tm, D), lambda i: (i, 0)),
                  pl.BlockSpec((1, D), lambda i: (0, 0)),
                  pl.BlockSpec((D, N), lambda i: (0, 0)),
                  pl.BlockSpec((NORM_CHUNK, NORM_CHUNK), lambda i: (0, 0))]
                 + [pl.BlockSpec((1, NORM_CHUNK), lambda i: (0, 0))] * len(head_gains),
        out_specs=[pl.BlockSpec((tm, wd), lambda i: (i, 0)) for wd in widths],
        out_shape=[jax.ShapeDtypeStruct((T, o[0]), o[1]) for o in outs],
        compiler_params=_cparams(1),
        name=name,
    )(x, gain.reshape(1, D), w, seg, *head_gains)


def _swa_kernel(sink_ref, q_ref, kc_ref, kp_ref, vc_ref, vp_ref, bias_ref, o_ref):
    n = pl.program_id(1)
    L = SWA_BLOCK
    n_blk = q_ref.shape[0] // L
    key = lax.broadcasted_iota(I32, (L, L), 0)
    qry = lax.broadcasted_iota(I32, (L, L), 1)
    from_prev = key > qry
    own = key <= qry
    zero = jnp.zeros((L, L), BF16)
    for blk in range(n_blk):
        rows = slice(blk * L, (blk + 1) * L)
        prows = slice((blk - 1) * L, blk * L)
        outs = []
        for g in range(SWA_KV_HEADS):
            ksl = slice(g * HEAD_DIM, (g + 1) * HEAD_DIM)
            k_prev = kp_ref[:, ksl] if blk == 0 else kc_ref[prows, ksl]
            v_prev = vp_ref[:, ksl] if blk == 0 else vc_ref[prows, ksl]
            k = jnp.concatenate([k_prev, kc_ref[rows, ksl]], axis=0)
            v = jnp.concatenate([v_prev, vc_ref[rows, ksl]], axis=0)
            for h in range(g * SWA_GROUP, (g + 1) * SWA_GROUP):
                sink = sink_ref[h]
                s2 = _dot_nt(k, q_ref[rows, h * HEAD_DIM:(h + 1) * HEAD_DIM])
                s = jnp.where(from_prev, s2[:L], s2[L:]) * (HEAD_DIM ** -0.5) - bias_ref[h]
                if blk == 0:
                    s = jnp.where(jnp.logical_or(n > 0, own), s, NEG_INF)
                m = jnp.maximum(jnp.max(s, axis=0, keepdims=True), sink)
                e = jnp.exp(s - m)
                denom = jnp.sum(e, axis=0, keepdims=True) + jnp.exp(sink - m)
                p = (e * (1.0 / denom)).astype(BF16)
                p2 = jnp.concatenate([jnp.where(from_prev, p, zero), jnp.where(from_prev, zero, p)], axis=0)
                outs.append(_dot_tn(p2, v))
        o_ref[rows, :] = jnp.concatenate(outs, axis=-1).astype(o_ref.dtype)


def swa_attention(q, k, v, sink, batch, n_blk=4):
    T = q.shape[0]
    L = SWA_BLOCK
    nb = T // batch // L
    kvw = SWA_KV_HEADS * HEAD_DIM
    slopes = jnp.exp2(-8.0 * jnp.arange(1, SWA_Q_HEADS + 1, dtype=F32) / SWA_Q_HEADS)
    assert SWA_WINDOW == L
    qi, kj = jnp.arange(L)[:, None], jnp.arange(L)[None, :]
    dist = jnp.where(kj > qi, qi + L - kj, qi - kj).astype(F32)
    bias = slopes[:, None, None] * dist.T
    ns = nb // n_blk
    slab = n_blk * L
    cur = lambda b, n: (b * ns + n, 0)
    prev = lambda b, n: (b * nb + jnp.maximum(n * n_blk - 1, 0), 0)
    return pl.pallas_call(
        _swa_kernel,
        grid=(batch, ns),
        in_specs=[pl.BlockSpec(memory_space=pltpu.SMEM),
                  pl.BlockSpec((slab, MIX_WIDTH), cur),
                  pl.BlockSpec((slab, kvw), cur), pl.BlockSpec((L, kvw), prev),
                  pl.BlockSpec((slab, kvw), cur), pl.BlockSpec((L, kvw), prev),
                  pl.BlockSpec((SWA_Q_HEADS, L, L), lambda b, n: (0, 0, 0))],
        out_specs=pl.BlockSpec((slab, MIX_WIDTH), cur),
        out_shape=jax.ShapeDtypeStruct((T, MIX_WIDTH), BF16),
        compiler_params=_cparams(2),
        name="swa_attention",
    )(sink, q, k, k, v, v, bias)


def _retention_kernel(q_ref, k_ref, v_ref, g_ref, dec_ref, qd_ref, kd_ref, cd_ref, o_ref, state_ref):
    @pl.when(pl.program_id(0) == 0)
    def _():
        state_ref[...] = jnp.zeros_like(state_ref)

    L = RET_CHUNK
    n_ch = q_ref.shape[1] // L
    for b in range(q_ref.shape[0]):
        outs = [[] for _ in range(n_ch)]
        for h in range(RET_HEADS):
            qsl = slice(h * RET_QK_DIM, (h + 1) * RET_QK_DIM)
            vsl = slice(h * RET_V_DIM, (h + 1) * RET_V_DIM)
            state = state_ref[b, h]
            for c in range(n_ch):
                rows = slice(c * L, (c + 1) * L)
                q = q_ref[b, rows, qsl]
                k = k_ref[b, rows, qsl] * (RET_QK_DIM ** -0.5)
                v = v_ref[b, rows, vsl]
                s = _dot_nt(q, k.astype(BF16)) * dec_ref[h]
                o = _dot(s.astype(BF16), v) + _dot(q, state.astype(BF16)) * qd_ref[h]
                state = state * cd_ref[h] + _dot_tn((k * kd_ref[h]).astype(BF16), v)
                outs[c].append(_silu(g_ref[b, rows, vsl]) * _rms(o))
            state_ref[b, h] = state
        for c in range(n_ch):
            o_ref[b, c * L:(c + 1) * L, :] = jnp.concatenate(outs[c], axis=-1).astype(o_ref.dtype)


def _retention_consts():
    H, L = RET_HEADS, RET_CHUNK
    log_gamma = jnp.log1p(-jnp.exp2(-5.0 - jnp.arange(H, dtype=F32)))
    pos = jnp.arange(L, dtype=F32)
    rel = pos[:, None] - pos[None, :]
    intra = jnp.exp(jnp.where(rel >= 0, log_gamma[:, None, None] * rel, -jnp.inf))
    q_decay = jnp.exp(log_gamma[:, None] * (pos + 1.0))[..., None]
    k_decay = jnp.exp(log_gamma[:, None] * (L - 1.0 - pos))[..., None]
    chunk_decay = jnp.exp(log_gamma * L)[:, None, None]
    return (intra,
            jnp.broadcast_to(q_decay, (H, L, RET_V_DIM)),
            jnp.broadcast_to(k_decay, (H, L, RET_QK_DIM)),
            jnp.broadcast_to(chunk_decay, (H, RET_QK_DIM, RET_V_DIM)))


def retention(q, k, v, g, batch, n_ch=4):
    T = q.shape[0]
    L = RET_CHUNK
    nc = T // batch // L
    H = RET_HEADS
    intra, qd, kd, cd = _retention_consts()
    cur = lambda c: (0, c, 0)
    const = lambda c: (0, 0, 0)
    qkw = H * RET_QK_DIM
    per_batch = lambda a: a.reshape(batch, T // batch, a.shape[1])
    slab = n_ch * L
    out = pl.pallas_call(
        _retention_kernel,
        grid=(nc // n_ch,),
        in_specs=[pl.BlockSpec((batch, slab, qkw), cur), pl.BlockSpec((batch, slab, qkw), cur),
                  pl.BlockSpec((batch, slab, MIX_WIDTH), cur), pl.BlockSpec((batch, slab, MIX_WIDTH), cur),
                  pl.BlockSpec((H, L, L), const), pl.BlockSpec((H, L, RET_V_DIM), const),
                  pl.BlockSpec((H, L, RET_QK_DIM), const), pl.BlockSpec((H, RET_QK_DIM, RET_V_DIM), const)],
        out_specs=pl.BlockSpec((batch, slab, MIX_WIDTH), cur),
        out_shape=jax.ShapeDtypeStruct((batch, T // batch, MIX_WIDTH), BF16),
        scratch_shapes=[pltpu.VMEM((batch, H, RET_QK_DIM, RET_V_DIM), F32)],
        compiler_params=_cparams(1),
        name="retention",
    )(per_batch(q), per_batch(k), per_batch(v), per_batch(g), intra, qd, kd, cd)
    return out.reshape(T, MIX_WIDTH)


def _split3(x):
    x1 = x.astype(BF16)
    r1 = x - x1.astype(F32)
    x2 = r1.astype(BF16)
    x3 = (r1 - x2.astype(F32)).astype(BF16)
    return x1, x2, x3


def _mlstm_kernel(qk_ref, v_ref, og_ref, gc_ref, gr_ref, cw_ref, cb_ref, bc_ref, br_ref, gain_ref,
                  tri_ref, trit_ref, o_ref, xbuf_ref, c_ref, n_ref, m_ref):
    for b in range(qk_ref.shape[0]):
        _mlstm_chunk(qk_ref.at[b], v_ref.at[b], og_ref.at[b], gc_ref.at[b], gr_ref.at[b], cw_ref, cb_ref,
                     bc_ref, br_ref, gain_ref, tri_ref, trit_ref, o_ref.at[b],
                     xbuf_ref.at[b], c_ref.at[b], n_ref.at[b], m_ref.at[b])


def _mlstm_chunk(qk_ref, v_ref, og_ref, gc_ref, gr_ref, cw_ref, cb_ref, bc_ref, br_ref, gain_ref,
                 tri_ref, trit_ref, o_ref, xbuf_ref, c_ref, n_ref, m_ref):
    L = MLSTM_CHUNK
    H = MLSTM_HEADS
    P = MLSTM_QK_PAD
    VP = MLSTM_V_PAD
    KT = SUBLANES

    @pl.when(pl.program_id(0) == 0)
    def _():
        xbuf_ref[0:KT, :] = jnp.zeros((KT, xbuf_ref.shape[1]), F32)
        c_ref[...] = jnp.zeros_like(c_ref)
        n_ref[...] = jnp.zeros_like(n_ref)
        m_ref[...] = jnp.zeros_like(m_ref)

    xbuf_ref[KT:KT + L, :] = qk_ref[...]
    acc = cb_ref[...] + cw_ref[MLSTM_CONV - 1:MLSTM_CONV, :] * xbuf_ref[KT:KT + L, :]
    for j in range(MLSTM_CONV - 1):
        sh = MLSTM_CONV - 1 - j
        acc = acc + cw_ref[j:j + 1, :] * xbuf_ref[KT - sh:KT - sh + L, :]
    xbuf_ref[0:KT, :] = qk_ref[L - KT:L, :]
    qk = _silu(acc)

    xc = gc_ref[...] + bc_ref[...]
    xr = gr_ref[...] + br_ref[...]
    lfc = jax.nn.log_sigmoid(xc)
    lfr = jax.nn.log_sigmoid(xr)
    tri = tri_ref[...]
    trit = trit_ref[...]
    bc = sum(_dot(tri, t) for t in _split3(lfc))
    br = sum(_dot(t, trit) for t in _split3(lfr))

    rowi = lax.broadcasted_iota(I32, (L, L), 0)
    coli = lax.broadcasted_iota(I32, (L, L), 1)
    causal = rowi >= coli
    outs = []
    for h in range(H):
        q = qk[:, h * P:(h + 1) * P].astype(BF16)
        k = qk[:, (H + h) * P:(H + h + 1) * P] * (MLSTM_QK_DIM ** -0.5)
        v = v_ref[:, h * VP:(h + 1) * VP]
        li_c = xc[:, h:h + 1]
        b_c = bc[:, H + h:H + h + 1]
        li_r = xr[h:h + 1, :]
        b_r = br[H + h:H + h + 1, :]
        g = b_c[L - 1:L, :]
        m = m_ref[h:h + 1, 0:1]
        C = c_ref[h]
        nvec = n_ref[h:h + 1, :]

        dmat = jnp.where(causal, b_c - b_r + li_r, -jnp.inf)
        inter = b_c + m
        m_t = jnp.maximum(inter, jnp.max(dmat, axis=-1, keepdims=True))
        w = jnp.exp(dmat - m_t)
        a = jnp.exp(inter - m_t)
        s = _dot_nt(q, k.astype(BF16)) * w
        num = _dot(s.astype(BF16), v) + a * _dot(q, C.astype(BF16))
        qf = qk[:, h * P:(h + 1) * P]
        den = jnp.sum(s, axis=-1, keepdims=True) + a * jnp.sum(qf * nvec, axis=-1, keepdims=True)
        hh = num / jnp.maximum(jnp.abs(den), jnp.exp(-m_t))

        u_c = g - b_c + li_c
        u_r = g - b_r + li_r
        m_new = jnp.maximum(g + m, jnp.max(u_r, axis=-1, keepdims=True))
        wk = jnp.exp(u_c - m_new)
        decay = jnp.exp(g + m - m_new)
        kw = k * wk
        c_ref[h] = decay * C + _dot_tn(kw.astype(BF16), v)
        n_ref[h:h + 1, :] = decay * nvec + jnp.sum(kw, axis=0, keepdims=True)
        m_ref[h:h + 1, :] = jnp.broadcast_to(m_new, (1, m_ref.shape[1]))

        ms = jnp.sum(hh * hh, axis=-1, keepdims=True) * (1.0 / MLSTM_V_DIM)
        hc = hh * lax.rsqrt(ms + RMS_EPS) * gain_ref[:, h * VP:(h + 1) * VP]
        outs.append(jax.nn.sigmoid(og_ref[:, h * VP:(h + 1) * VP]) * hc)
    o_ref[...] = jnp.concatenate(outs, axis=-1).astype(o_ref.dtype)


def mlstm(qk, v, og, gates, gates_t, conv_w, conv_b, bias_c, bias_r, gain, batch):
    T = qk.shape[0]
    L = MLSTM_CHUNK
    nc = T // batch // L
    H, P, VP = MLSTM_HEADS, MLSTM_QK_PAD, MLSTM_V_PAD
    tri = jnp.tril(jnp.ones((L, L), BF16))
    cur = lambda c: (0, c, 0)
    c2 = lambda c: (0, 0)
    per_batch = lambda a: a.reshape(batch, T // batch, a.shape[1])
    out = pl.pallas_call(
        _mlstm_kernel,
        grid=(nc,),
        in_specs=[pl.BlockSpec((batch, L, 2 * H * P), cur), pl.BlockSpec((batch, L, H * VP), cur),
                  pl.BlockSpec((batch, L, H * VP), cur), pl.BlockSpec((batch, L, LANES), cur),
                  pl.BlockSpec((batch, SUBLANES, L), lambda c: (0, 0, c)),
                  pl.BlockSpec((MLSTM_CONV, 2 * H * P), c2), pl.BlockSpec((1, 2 * H * P), c2),
                  pl.BlockSpec((1, LANES), c2), pl.BlockSpec((SUBLANES, L), c2),
                  pl.BlockSpec((1, H * VP), c2),
                  pl.BlockSpec((L, L), c2), pl.BlockSpec((L, L), c2)],
        out_specs=pl.BlockSpec((batch, L, H * VP), cur),
        out_shape=jax.ShapeDtypeStruct((batch, T // batch, H * VP), BF16),
        scratch_shapes=[pltpu.VMEM((batch, SUBLANES + L, 2 * H * P), F32),
                        pltpu.VMEM((batch, H, P, VP), F32),
                        pltpu.VMEM((batch, SUBLANES, P), F32),
                        pltpu.VMEM((batch, SUBLANES, LANES), F32)],
        compiler_params=_cparams(1),
        name="mlstm",
    )(per_batch(qk), per_batch(v), per_batch(og), per_batch(gates), gates_t,
      conv_w, conv_b, bias_c, bias_r, gain, tri, tri.T)
    return out.reshape(T, H * VP)


def _out_proj_kernel(h_ref, mix_ref, q_ref, k_ref, v_ref, w1_ref, w2_ref, o_ref):
    outs = []
    for hd in range(MEM_HEADS):
        sl = slice(hd * MEM_HEAD_DIM, (hd + 1) * MEM_HEAD_DIM)
        s = _dot_nt(k_ref[:, sl], q_ref[:, sl]) * (MEM_HEAD_DIM ** -0.5)
        e = jnp.exp(s - jnp.max(s, axis=0, keepdims=True))
        p = (e * (1.0 / jnp.sum(e, axis=0, keepdims=True))).astype(BF16)
        outs.append(_dot_tn(p, v_ref[:, sl]))
    mem_out = jnp.concatenate(outs, axis=-1).astype(BF16)
    o_ref[...] = h_ref[...] + _dot(mix_ref[...], w1_ref[...]) + _dot(mem_out, w2_ref[...])


def out_proj(h, mix, qm, mk, mv, w_mix, w_mem, batch, tm):
    T, D = h.shape
    Wm = mix.shape[1]
    M = mk.shape[0] // batch
    nt = T // batch // tm
    rows = lambda b, i: (b * nt + i, 0)
    mem = lambda b, i: (b, 0)
    const = lambda b, i: (0, 0)
    return pl.pallas_call(
        _out_proj_kernel,
        grid=(batch, nt),
        in_specs=[pl.BlockSpec((tm, D), rows),
                  pl.BlockSpec((tm, Wm), rows),
                  pl.BlockSpec((tm, MEM_WIDTH), rows),
                  pl.BlockSpec((M, MEM_WIDTH), mem),
                  pl.BlockSpec((M, MEM_WIDTH), mem),
                  pl.BlockSpec((Wm, D), const),
                  pl.BlockSpec((MEM_WIDTH, D), const)],
        out_specs=pl.BlockSpec((tm, D), rows),
        out_shape=jax.ShapeDtypeStruct((T, D), F32),
        compiler_params=_cparams(2),
        name="out_proj",
    )(h, mix, qm, mk, mv, w_mix, w_mem)


def _dense_ffn_kernel(h_ref, g_ref, wg_ref, wu_ref, wd_ref, o_ref, xn_ref):
    @pl.when(pl.program_id(1) == 0)
    def _():
        h = h_ref[...]
        xn_ref[...] = (_rms(h) * g_ref[...]).astype(BF16)
        o_ref[...] = h

    xn = xn_ref[...]
    a = _silu(_dot(xn, wg_ref[...])) * _dot(xn, wu_ref[...])
    o_ref[...] += _dot(a.astype(BF16), wd_ref[...])


def dense_ffn(h, gain, w_gate, w_up, w_down, tm, tf):
    T, D = h.shape
    Fd = w_gate.shape[1]
    return pl.pallas_call(
        _dense_ffn_kernel,
        grid=(T // tm, Fd // tf),
        in_specs=[pl.BlockSpec((tm, D), lambda i, j: (i, 0)),
                  pl.BlockSpec((1, D), lambda i, j: (0, 0)),
                  pl.BlockSpec((D, tf), lambda i, j: (0, j)),
                  pl.BlockSpec((D, tf), lambda i, j: (0, j)),
                  pl.BlockSpec((tf, D), lambda i, j: (j, 0))],
        out_specs=pl.BlockSpec((tm, D), lambda i, j: (i, 0)),
        out_shape=jax.ShapeDtypeStruct((T, D), F32),
        scratch_shapes=[pltpu.VMEM((tm, D), BF16)],
        compiler_params=_cparams(2),
        name="dense_ffn",
    )(h, gain.reshape(1, D), w_gate, w_up, w_down)


def _router_kernel(h_ref, g_ref, wr_ref, tri_ref, o_ref, ot_ref, cnt_ref, carry_ref):
    i = pl.program_id(0)

    @pl.when(i == 0)
    def _():
        carry_ref[...] = jnp.zeros_like(carry_ref)

    tm = h_ref.shape[0]
    xn = _rms(h_ref[...]) * g_ref[...]
    x_hi = xn.astype(BF16)
    x_lo = (xn - x_hi.astype(F32)).astype(BF16)
    wr = wr_ref[...]
    w_hi = wr.astype(BF16)
    w_lo = (wr - w_hi.astype(F32)).astype(BF16)
    logits = _dot(x_hi, w_hi) + (_dot(x_hi, w_lo) + _dot(x_lo, w_hi))
    lane = lax.broadcasted_iota(I32, (tm, LANES), 1)
    logits = jnp.where(lane < N_EXPERTS, logits, -jnp.inf)
    t1 = jnp.max(logits, axis=-1, keepdims=True)
    e1 = jnp.min(jnp.where(logits == t1, lane, LANES), axis=-1, keepdims=True)
    rest = jnp.where(lane == e1, -jnp.inf, logits)
    t2 = jnp.max(rest, axis=-1, keepdims=True)
    e2 = jnp.min(jnp.where(rest == t2, lane, LANES), axis=-1, keepdims=True)
    x2 = jnp.exp(t2 - t1)
    w1 = 1.0 / (1.0 + x2)
    w2 = x2 / (1.0 + x2)

    oh1 = lane == e1
    oh2 = lane == e2
    cnt = jnp.where(oh1 | oh2, 1.0, 0.0)
    before = _dot(tri_ref[...], cnt.astype(BF16)) + carry_ref[...]
    r1 = jnp.sum(jnp.where(oh1, before, 0.0), axis=-1, keepdims=True)
    r2 = jnp.sum(jnp.where(oh2, before, 0.0), axis=-1, keepdims=True)
    carry_ref[...] += jnp.sum(cnt, axis=0, keepdims=True)

    cols = (e1.astype(F32), e2.astype(F32), w1, w2, r1, r2)
    out = jnp.zeros((tm, LANES), F32)
    for c, val in enumerate(cols):
        out = jnp.where(lane == c, val, out)
    o_ref[...] = out
    ot_ref[...] = out.T[:SUBLANES, :]
    cnt_ref[...] = jnp.broadcast_to(carry_ref[...], cnt_ref.shape)


def moe_router(h, gain, w_router, tm):
    T, D = h.shape
    wr = jnp.zeros((D, LANES), F32).at[:, :N_EXPERTS].set(w_router)
    tri = jnp.tril(jnp.ones((tm, tm), BF16), k=-1)
    return pl.pallas_call(
        _router_kernel,
        grid=(T // tm,),
        in_specs=[pl.BlockSpec((tm, D), lambda i: (i, 0)),
                  pl.BlockSpec((1, D), lambda i: (0, 0)),
                  pl.BlockSpec((D, LANES), lambda i: (0, 0)),
                  pl.BlockSpec((tm, tm), lambda i: (0, 0))],
        out_specs=[pl.BlockSpec((tm, LANES), lambda i: (i, 0)),
                   pl.BlockSpec((SUBLANES, tm), lambda i: (0, i)),
                   pl.BlockSpec((SUBLANES, LANES), lambda i: (0, 0))],
        out_shape=[jax.ShapeDtypeStruct((T, LANES), F32),
                   jax.ShapeDtypeStruct((SUBLANES, T), F32),
                   jax.ShapeDtypeStruct((SUBLANES, LANES), F32)],
        scratch_shapes=[pltpu.VMEM((1, LANES), F32)],
        compiler_params=_cparams(1),
        name="moe_router",
    )(h, gain.reshape(1, D), wr, tri)


def _dispatch_kernel(ze_ref, zn_ref, p1_ref, p2_ref, h_ref, xs_ref, zero_ref, sem, zsem, *, pad_max, tail_max):
    tb = p1_ref.shape[0]
    zr = zero_ref.shape[0]

    @pl.when(pl.program_id(0) == 0)
    def _():
        zero_ref[...] = jnp.zeros_like(zero_ref)
        chunks = [(e, c) for e in range(N_EXPERTS) for c in range(pad_max // zr)]
        chunks += [(N_EXPERTS, c) for c in range(tail_max // zr)]

        def zcopy(e, c):
            start = pl.multiple_of(ze_ref[e] - (c + 1) * zr, zr)
            return pltpu.make_async_copy(zero_ref, xs_ref.at[pl.ds(start, zr), :], zsem)

        def needed(e, c):
            return c * zr < zn_ref[e]

        for e, c in chunks:
            @pl.when(needed(e, c))
            def _():
                zcopy(e, c).start()

        for e, c in chunks:
            @pl.when(needed(e, c))
            def _():
                zcopy(e, c).wait()

    def copy(t, pos):
        return pltpu.make_async_copy(h_ref.at[pl.ds(t, 1), :], xs_ref.at[pl.ds(pos, 1), :], sem)

    for t in range(tb):
        copy(t, p1_ref[t]).start()
        copy(t, p2_ref[t]).start()
    for _ in range(TOP_K):
        pltpu.make_async_copy(h_ref, xs_ref.at[pl.ds(0, tb), :], sem).wait()


def moe_dispatch(h, pos1, pos2, pad_end, pad_len, used_rows, n_rows, pad_max, tb, zr=256):
    T, D = h.shape
    tail_max = n_rows - TOP_K * T
    assert tail_max % zr == 0 and pad_max % zr == 0
    zero_end = jnp.concatenate([pad_end, jnp.full((1,), n_rows, I32)])
    zero_len = jnp.concatenate([pad_len, n_rows - used_rows[None]])
    grid_spec = pltpu.PrefetchScalarGridSpec(
        num_scalar_prefetch=2,
        grid=(T // tb,),
        in_specs=[pl.BlockSpec((tb,), lambda i, ps, pn: (i,), memory_space=pltpu.SMEM),
                  pl.BlockSpec((tb,), lambda i, ps, pn: (i,), memory_space=pltpu.SMEM),
                  pl.BlockSpec((tb, D), lambda i, ps, pn: (i, 0))],
        out_specs=pl.BlockSpec(memory_space=pl.ANY),
        scratch_shapes=[pltpu.VMEM((zr, D), h.dtype), pltpu.SemaphoreType.DMA(()), pltpu.SemaphoreType.DMA(())],
    )
    return pl.pallas_call(
        functools.partial(_dispatch_kernel, pad_max=pad_max, tail_max=tail_max),
        grid_spec=grid_spec,
        out_shape=jax.ShapeDtypeStruct((n_rows, D), h.dtype),
        compiler_params=_cparams(1),
        name="moe_dispatch",
    )(zero_end, zero_len, pos1, pos2, h)


def _expert_ffn_kernel(te_ref, tv_ref, x_ref, g_ref, wg_ref, wu_ref, wd_ref, wdl_ref, o_ref, xn_ref, a_ref,
                       *, ts, nf):
    i = pl.program_id(0)
    j = pl.program_id(1)
    valid = tv_ref[i]
    n_sub = x_ref.shape[0] // ts

    n_occ = (valid + ts - 1) // ts
    for k in range(1, n_sub + 1):
        rows = pl.ds(0, k * ts)

        def gate_up(slot):
            xn = xn_ref[rows, :]
            a = _silu(_dot(xn, wg_ref[...].astype(BF16))) * _dot(xn, wu_ref[...].astype(BF16))
            a_ref[slot, rows, :] = a.astype(BF16)

        def down(slot, first, w_ref=wd_ref):
            y = _dot(a_ref[slot, rows, :], w_ref[...].astype(BF16))
            if first:
                o_ref[rows, :] = y
                if k < n_sub:
                    o_ref[pl.ds(k * ts, (n_sub - k) * ts), :] = jnp.zeros(((n_sub - k) * ts, o_ref.shape[1]), F32)
            else:
                o_ref[rows, :] += y

        @pl.when(n_occ == k)
        def _():
            @pl.when(j == 0)
            def _():
                xn_ref[rows, :] = (_rms(x_ref[rows, :]) * g_ref[...]).astype(BF16)
                gate_up(0)

            @pl.when(j == 1)
            def _():
                gate_up(1)
                down(0, first=True)

            @pl.when((j > 1) & (j < nf - 1))
            def _():
                gate_up(j % 2)
                down((j - 1) % 2, first=False)

            @pl.when(j == nf - 1)
            def _():
                gate_up((nf - 1) % 2)
                down((nf - 2) % 2, first=False)
                down((nf - 1) % 2, first=False, w_ref=wdl_ref)

    @pl.when((n_occ == 0) & (j == 0))
    def _():
        o_ref[...] = jnp.zeros_like(o_ref)


def expert_ffn(xs, gain, w_gate, w_up, w_down, layer, tile_expert, tile_valid, n_tiles, tm, ts, tf):
    D = xs.shape[1]
    R = n_tiles * tm
    Fd = w_gate.shape[3]
    nf = Fd // tf

    assert nf >= 3
    def up_tile(i, j, tv):
        return jnp.where(tv[i] > 0, j, nf - 1)

    def down_tile(i, j, tv):
        return jnp.where(tv[i] > 0, jnp.maximum(j - 1, 0), nf - 2)

    grid_spec = pltpu.PrefetchScalarGridSpec(
        num_scalar_prefetch=2,
        grid=(n_tiles, nf),
        in_specs=[pl.BlockSpec((tm, D), lambda i, j, te, tv: (jnp.where(tv[i] > 0, i, 0), 0)),
                  pl.BlockSpec((1, D), lambda i, j, te, tv: (0, 0)),
                  pl.BlockSpec((None, None, D, tf), lambda i, j, te, tv: (layer, te[i], 0, up_tile(i, j, tv))),
                  pl.BlockSpec((None, None, D, tf), lambda i, j, te, tv: (layer, te[i], 0, up_tile(i, j, tv))),
                  pl.BlockSpec((None, None, tf, D), lambda i, j, te, tv: (layer, te[i], down_tile(i, j, tv), 0)),
                  pl.BlockSpec((None, None, tf, D), lambda i, j, te, tv: (layer, te[i], nf - 1, 0))],
        out_specs=pl.BlockSpec((tm, D), lambda i, j, te, tv: (i, 0)),
        scratch_shapes=[pltpu.VMEM((tm, D), BF16), pltpu.VMEM((2, tm, tf), BF16)],
    )
    return pl.pallas_call(
        functools.partial(_expert_ffn_kernel, ts=ts, nf=nf),
        grid_spec=grid_spec,
        out_shape=jax.ShapeDtypeStruct((R, D), F32),
        compiler_params=_cparams(2),
        name="expert_ffn",
    )(tile_expert, tile_valid, xs, gain.reshape(1, D), w_gate, w_up, w_down, w_down)


def _combine_kernel(p1_ref, p2_ref, route_ref, h_ref, y_ref, o_ref, b1_ref, b2_ref, sem):
    tb = p1_ref.shape[0]

    def copies(t):
        dst = pl.ds(t, 1)
        return (pltpu.make_async_copy(y_ref.at[pl.ds(p1_ref[t], 1), :], b1_ref.at[dst, :], sem),
                pltpu.make_async_copy(y_ref.at[pl.ds(p2_ref[t], 1), :], b2_ref.at[dst, :], sem))

    for t in range(tb):
        for cp in copies(t):
            cp.start()
    for b_ref in (b1_ref, b2_ref):
        pltpu.make_async_copy(y_ref.at[pl.ds(0, tb), :], b_ref, sem).wait()
    w1 = route_ref[:, 2:3]
    w2 = route_ref[:, 3:4]
    o_ref[...] = h_ref[...] + w1 * b1_ref[...] + w2 * b2_ref[...]


def moe_combine(h, y, route, pos1, pos2, tb):
    T, D = h.shape
    smem = lambda: pl.BlockSpec((tb,), lambda i: (i,), memory_space=pltpu.SMEM)
    return pl.pallas_call(
        _combine_kernel,
        grid=(T // tb,),
        in_specs=[smem(), smem(),
                  pl.BlockSpec((tb, LANES), lambda i: (i, 0)),
                  pl.BlockSpec((tb, D), lambda i: (i, 0)),
                  pl.BlockSpec(memory_space=pl.ANY)],
        out_specs=pl.BlockSpec((tb, D), lambda i: (i, 0)),
        out_shape=jax.ShapeDtypeStruct((T, D), F32),
        scratch_shapes=[pltpu.VMEM((tb, D), F32), pltpu.VMEM((tb, D), F32),
                        pltpu.SemaphoreType.DMA(())],
        compiler_params=_cparams(1),
        name="moe_combine",
    )(pos1, pos2, route, h, y)


def moe_ffn(h, gain, w_router, w_gate, w_up, w_down, layer, tm=1024, ts=256, tf=512, tb=512):
    T = h.shape[0]
    route, route_t, counts = moe_router(h, gain, w_router, tm=512)
    e1 = route_t[0].astype(I32)
    e2 = route_t[1].astype(I32)
    counts = counts[0, :N_EXPERTS].astype(I32)
    tiles_per = (counts + tm - 1) // tm
    tile_end = jnp.cumsum(tiles_per)
    tile_start = tile_end - tiles_per
    row_start = tile_start * tm
    pos1 = row_start[e1] + route_t[4].astype(I32)
    pos2 = row_start[e2] + route_t[5].astype(I32)
    n_tiles = (TOP_K * T) // tm + N_EXPERTS
    tile_ids = jnp.arange(n_tiles, dtype=I32)
    tile_expert = jnp.sum(jnp.minimum(tile_ids, tile_end[-1] - 1)[:, None] >= tile_end[None, :], axis=1).astype(I32)
    tile_valid = jnp.clip(counts[tile_expert] - (tile_ids - tile_start[tile_expert]) * tm, 0, tm)
    tile_valid = jnp.where(tile_ids < tile_end[-1], tile_valid, 0).astype(I32)
    pad_len = tiles_per * tm - counts

    xs = moe_dispatch(h, pos1, pos2, tile_end * tm, pad_len, tile_end[-1] * tm, n_tiles * tm, tm, tb)
    y = expert_ffn(xs, gain, w_gate, w_up, w_down, layer, tile_expert, tile_valid, n_tiles, tm, ts, tf)
    return moe_combine(h, y, route, pos1, pos2, tb)


def _pad_heads(a, n_heads, width, pad_to, axis):
    shape = list(a.shape)
    a = a.reshape(shape[:axis] + [n_heads, width] + shape[axis + 1:])
    pads = [(0, 0)] * a.ndim
    pads[axis + 1] = (0, pad_to - width)
    a = jnp.pad(a, pads)
    return a.reshape(shape[:axis] + [n_heads * pad_to] + shape[axis + 1:])


def kernel(x, mem, ln_mix, ln_mem, w_mem_kv, mem_q_gain, mem_k_gain, w_out, ln_ffn, swa_w_in, swa_q_gain, swa_k_gain, swa_sink, ret_w_in, mlstm_w_in, mlstm_conv_w, mlstm_conv_b, mlstm_i_bias, mlstm_f_bias, mlstm_out_gain, ffn_w_gate, ffn_w_up, ffn_w_down, moe_router, moe_w_gate, moe_w_up, moe_w_down):
    B, S, D = x.shape
    M = mem.shape[1]
    T = B * S
    depth = ln_mix.shape[0]
    h = x.reshape(T, D)
    mem2 = mem.reshape(B * M, D)

    for layer in range(depth):
        kind = layer % N_MIXERS
        idx = layer // N_MIXERS
        mk, mv = rms_proj(mem2, ln_mem[layer], w_mem_kv[layer].astype(BF16),
                          ((MEM_WIDTH, BF16, mem_k_gain[layer]), (MEM_WIDTH, BF16, None)),
                          tm=B * M, name="mem_kv_proj")
        w_o = w_out[layer].astype(BF16)
        w_o_mix, w_o_mem = w_o[:MIX_WIDTH], w_o[MIX_WIDTH:]
        qm_out = (MEM_WIDTH, BF16, mem_q_gain[layer])

        if kind == 0:
            kvw = SWA_KV_HEADS * HEAD_DIM
            q, k, v, qm = rms_proj(h, ln_mix[layer], swa_w_in[idx].astype(BF16),
                                   ((MIX_WIDTH, BF16, swa_q_gain[idx]), (kvw, BF16, swa_k_gain[idx]),
                                    (kvw, BF16, None), qm_out), tm=1024, name="swa_in_proj")
            mix = swa_attention(q, k, v, swa_sink[idx], B)
        elif kind == 1:
            qkw = RET_HEADS * RET_QK_DIM
            q, k, v, g, qm = rms_proj(h, ln_mix[layer], ret_w_in[idx].astype(BF16),
                                      ((qkw, BF16, None), (qkw, F32, None), (MIX_WIDTH, BF16, None),
                                       (MIX_WIDTH, F32, None), qm_out), tm=1024, name="ret_in_proj")
            mix = retention(q, k, v, g, B)
        else:
            H, P, VP = MLSTM_HEADS, MLSTM_QK_PAD, MLSTM_V_PAD
            w = mlstm_w_in[idx]
            qkw = 2 * H * MLSTM_QK_DIM
            o_v, o_og, o_ig = qkw, qkw + MIX_WIDTH, qkw + 2 * MIX_WIDTH
            o_fg, o_qm = o_ig + H, o_ig + 2 * H
            w_gates = jnp.zeros((D, LANES), F32).at[:, :2 * H].set(w[:, o_ig:o_qm])
            w_pad = jnp.concatenate([
                _pad_heads(w[:, :qkw], 2 * H, MLSTM_QK_DIM, P, 1),
                _pad_heads(w[:, o_v:o_og], H, MLSTM_V_DIM, VP, 1),
                _pad_heads(w[:, o_og:o_ig], H, MLSTM_V_DIM, VP, 1),
                w_gates, w[:, o_qm:]], axis=1).astype(BF16)
            qk, v, og, gates, qm = rms_proj(h, ln_mix[layer], w_pad,
                                            ((2 * H * P, F32, None), (H * VP, BF16, None), (H * VP, F32, None),
                                             (LANES, F32, None), qm_out), tm=1024, name="mlstm_in_proj")
            gates_t = gates[:, :SUBLANES].reshape(B, S, SUBLANES).transpose(0, 2, 1)
            bias = jnp.concatenate([mlstm_i_bias[idx], mlstm_f_bias[idx]])
            bias_c = jnp.zeros((1, LANES), F32).at[0, :2 * H].set(bias)
            bias_r = jnp.broadcast_to(bias[:, None], (SUBLANES, MLSTM_CHUNK))
            mix = mlstm(qk, v, og, gates, gates_t,
                        _pad_heads(mlstm_conv_w[idx], 2 * H, MLSTM_QK_DIM, P, 1),
                        _pad_heads(mlstm_conv_b[idx][None], 2 * H, MLSTM_QK_DIM, P, 1),
                        bias_c, bias_r,
                        _pad_heads(mlstm_out_gain[idx][None], H, MLSTM_V_DIM, VP, 1), B)
            w_o_mix = _pad_heads(w_o_mix, H, MLSTM_V_DIM, VP, 0)

        h = out_proj(h, mix, qm, mk, mv, w_o_mix, w_o_mem, B, tm=1024)
        j = layer // 2
        if layer % 2 == 0:
            h = dense_ffn(h, ln_ffn[layer], ffn_w_gate[j].astype(BF16), ffn_w_up[j].astype(BF16),
                          ffn_w_down[j].astype(BF16), tm=1024, tf=1792)
        else:
            h = moe_ffn(h, ln_ffn[layer], moe_router[j], moe_w_gate, moe_w_up, moe_w_down, j)
    return h.reshape(B, S, D)
```

```python
import functools
import math

import jax
import jax.numpy as jnp
from jax import lax
from jax.experimental import pallas as pl
from jax.experimental.pallas import tpu as pltpu

F32 = jnp.float32
BF16 = jnp.bfloat16
I32 = jnp.int32

D_MODEL = 1024
N_MIXERS = 3
HEAD_DIM = 64
MEM_HEADS = 4
MEM_HEAD_DIM = 64
MEM_WIDTH = MEM_HEADS * MEM_HEAD_DIM
MIX_WIDTH = D_MODEL - MEM_WIDTH
RMS_EPS = 1e-6
NEG_INF = -1e30

SWA_Q_HEADS = MIX_WIDTH // HEAD_DIM
SWA_KV_HEADS = 4
SWA_GROUP = SWA_Q_HEADS // SWA_KV_HEADS
SWA_WINDOW = 128
SWA_BLOCK = 128

RET_HEADS = 6
RET_QK_DIM = 64
RET_V_DIM = MIX_WIDTH // RET_HEADS
RET_CHUNK = 128

MLSTM_HEADS = 4
MLSTM_V_DIM = MIX_WIDTH // MLSTM_HEADS
MLSTM_QK_DIM = MLSTM_V_DIM // 2
MLSTM_CHUNK = 128
MLSTM_CONV = 4
MLSTM_QK_PAD = 128
MLSTM_V_PAD = 256

N_EXPERTS = 8
TOP_K = 2

LANES = 128
SUBLANES = 8
VMEM_LIMIT = 56 * 1024 * 1024


def _cparams(n_axes):
    return pltpu.CompilerParams(dimension_semantics=("arbitrary",) * n_axes,
                                vmem_limit_bytes=VMEM_LIMIT)


def _rms(x, eps=RMS_EPS):
    return x * lax.rsqrt(jnp.mean(x * x, axis=-1, keepdims=True) + eps)


def _dot(a, b):
    return jnp.dot(a, b, preferred_element_type=F32)


def _dot_nt(a, b):
    return lax.dot_general(a, b, (((1,), (1,)), ((), ())), preferred_element_type=F32)


def _dot_tn(a, b):
    return lax.dot_general(a, b, (((0,), (0,)), ((), ())), preferred_element_type=F32)


def _silu(x):
    return x * jax.nn.sigmoid(x)


NORM_CHUNK = 256


def _head_rms(y, seg_ref, hg):
    sq = y * y
    hi = sq.astype(BF16)
    lo = (sq - hi.astype(F32)).astype(BF16)
    ms = _dot(hi, seg_ref[...]) + _dot(lo, seg_ref[...])
    return y * lax.rsqrt(ms + RMS_EPS) * hg


def _rms_proj_kernel(x_ref, g_ref, w_ref, seg_ref, *refs, widths, normed):
    n_gain = sum(normed)
    gain_refs, o_refs = refs[:n_gain], refs[n_gain:]
    xn = (_rms(x_ref[...]) * g_ref[...]).astype(BF16)
    off = 0
    gi = 0
    for o_ref, wd, nrm in zip(o_refs, widths, normed):
        if nrm:
            hg = gain_refs[gi][...]
            gi += 1
            for c in range(0, wd, NORM_CHUNK):
                y = _dot(xn, w_ref[:, off + c:off + c + NORM_CHUNK])
                o_ref[:, c:c + NORM_CHUNK] = _head_rms(y, seg_ref, hg).astype(o_ref.dtype)
        else:
            o_ref[...] = _dot(xn, w_ref[:, off:off + wd]).astype(o_ref.dtype)
        off += wd


def rms_proj(x, gain, w, outs, tm, name):
    T, D = x.shape
    N = w.shape[1]
    widths = tuple(o[0] for o in outs)
    normed = tuple(o[2] is not None for o in outs)
    head_gains = [jnp.tile(o[2], NORM_CHUNK // HEAD_DIM).reshape(1, NORM_CHUNK) for o in outs if o[2] is not None]
    assert sum(widths) == N and T % tm == 0
    assert all(wd % NORM_CHUNK == 0 for wd, nrm in zip(widths, normed) if nrm)
    head_of = jnp.arange(NORM_CHUNK) // HEAD_DIM
    seg = jnp.where(head_of[:, None] == head_of[None, :], 1.0 / HEAD_DIM, 0.0).astype(BF16)
    return pl.pallas_call(
        functools.partial(_rms_proj_kernel, widths=widths, normed=normed),
        grid=(T // tm,),
        in_specs=[pl.BlockSpec((tm, D), lambda i: (i, 0)),
                  pl.BlockSpec((1, D), lambda i: (0, 0)),
                  pl.BlockSpec((D, N), lambda i: (0, 0)),
                  pl.BlockSpec((NORM_CHUNK, NORM_CHUNK), lambda i: (0, 0))]
                 + [pl.BlockSpec((1, NORM_CHUNK), lambda i: (0, 0))] * len(head_gains),
        out_specs=[pl.BlockSpec((tm, wd), lambda i: (i, 0)) for wd in widths],
        out_shape=[jax.ShapeDtypeStruct((T, o[0]), o[1]) for o in outs],
        compiler_params=_cparams(1),
        name=name,
    )(x, gain.reshape(1, D), w, seg, *head_gains)


def _swa_kernel(sink_ref, q_ref, kc_ref, kp_ref, vc_ref, vp_ref, bias_ref, o_ref):
    n = pl.program_id(1)
    L = SWA_BLOCK
    n_blk = q_ref.shape[0] // L
    key = lax.broadcasted_iota(I32, (L, L), 0)
    qry = lax.broadcasted_iota(I32, (L, L), 1)
    from_prev = key > qry
    own = key <= qry
    zero = jnp.zeros((L, L), BF16)
    for blk in range(n_blk):
        rows = slice(blk * L, (blk + 1) * L)
        prows = slice((blk - 1) * L, blk * L)
        outs = []
        for g in range(SWA_KV_HEADS):
            ksl = slice(g * HEAD_DIM, (g + 1) * HEAD_DIM)
            k_prev = kp_ref[:, ksl] if blk == 0 else kc_ref[prows, ksl]
            v_prev = vp_ref[:, ksl] if blk == 0 else vc_ref[prows, ksl]
            k = jnp.concatenate([k_prev, kc_ref[rows, ksl]], axis=0)
            v = jnp.concatenate([v_prev, vc_ref[rows, ksl]], axis=0)
            for h in range(g * SWA_GROUP, (g + 1) * SWA_GROUP):
                sink = sink_ref[h]
                s2 = _dot_nt(k, q_ref[rows, h * HEAD_DIM:(h + 1) * HEAD_DIM])
                s = jnp.where(from_prev, s2[:L], s2[L:]) * (HEAD_DIM ** -0.5) - bias_ref[h]
                if blk == 0:
                    s = jnp.where(jnp.logical_or(n > 0, own), s, NEG_INF)
                m = jnp.maximum(jnp.max(s, axis=0, keepdims=True), sink)
                e = jnp.exp(s - m)
                denom = jnp.sum(e, axis=0, keepdims=True) + jnp.exp(sink - m)
                p = (e * (1.0 / denom)).astype(BF16)
                p2 = jnp.concatenate([jnp.where(from_prev, p, zero), jnp.where(from_prev, zero, p)], axis=0)
                outs.append(_dot_tn(p2, v))
        o_ref[rows, :] = jnp.concatenate(outs, axis=-1).astype(o_ref.dtype)


def swa_attention(q, k, v, sink, batch, n_blk=4):
    T = q.shape[0]
    L = SWA_BLOCK
    nb = T // batch // L
    kvw = SWA_KV_HEADS * HEAD_DIM
    slopes = jnp.exp2(-8.0 * jnp.arange(1, SWA_Q_HEADS + 1, dtype=F32) / SWA_Q_HEADS)
    assert SWA_WINDOW == L
    qi, kj = jnp.arange(L)[:, None], jnp.arange(L)[None, :]
    dist = jnp.where(kj > qi, qi + L - kj, qi - kj).astype(F32)
    bias = slopes[:, None, None] * dist.T
    ns = nb // n_blk
    slab = n_blk * L
    cur = lambda b, n: (b * ns + n, 0)
    prev = lambda b, n: (b * nb + jnp.maximum(n * n_blk - 1, 0), 0)
    return pl.pallas_call(
        _swa_kernel,
        grid=(batch, ns),
        in_specs=[pl.BlockSpec(memory_space=pltpu.SMEM),
                  pl.BlockSpec((slab, MIX_WIDTH), cur),
                  pl.BlockSpec((slab, kvw), cur), pl.BlockSpec((L, kvw), prev),
                  pl.BlockSpec((slab, kvw), cur), pl.BlockSpec((L, kvw), prev),
                  pl.BlockSpec((SWA_Q_HEADS, L, L), lambda b, n: (0, 0, 0))],
        out_specs=pl.BlockSpec((slab, MIX_WIDTH), cur),
        out_shape=jax.ShapeDtypeStruct((T, MIX_WIDTH), BF16),
        compiler_params=_cparams(2),
        name="swa_attention",
    )(sink, q, k, k, v, v, bias)


def _retention_kernel(q_ref, k_ref, v_ref, g_ref, dec_ref, qd_ref, kd_ref, cd_ref, o_ref, state_ref):
    @pl.when(pl.program_id(0) == 0)
    def _():
        state_ref[...] = jnp.zeros_like(state_ref)

    L = RET_CHUNK
    n_ch = q_ref.shape[1] // L
    for b in range(q_ref.shape[0]):
        outs = [[] for _ in range(n_ch)]
        for h in range(RET_HEADS):
            qsl = slice(h * RET_QK_DIM, (h + 1) * RET_QK_DIM)
            vsl = slice(h * RET_V_DIM, (h + 1) * RET_V_DIM)
            state = state_ref[b, h]
            for c in range(n_ch):
                rows = slice(c * L, (c + 1) * L)
                q = q_ref[b, rows, qsl]
                k = k_ref[b, rows, qsl] * (RET_QK_DIM ** -0.5)
                v = v_ref[b, rows, vsl]
                s = _dot_nt(q, k.astype(BF16)) * dec_ref[h]
                o = _dot(s.astype(BF16), v) + _dot(q, state.astype(BF16)) * qd_ref[h]
                state = state * cd_ref[h] + _dot_tn((k * kd_ref[h]).astype(BF16), v)
                outs[c].append(_silu(g_ref[b, rows, vsl]) * _rms(o))
            state_ref[b, h] = state
        for c in range(n_ch):
            o_ref[b, c * L:(c + 1) * L, :] = jnp.concatenate(outs[c], axis=-1).astype(o_ref.dtype)


def _retention_consts():
    H, L = RET_HEADS, RET_CHUNK
    log_gamma = jnp.log1p(-jnp.exp2(-5.0 - jnp.arange(H, dtype=F32)))
    pos = jnp.arange(L, dtype=F32)
    rel = pos[:, None] - pos[None, :]
    intra = jnp.exp(jnp.where(rel >= 0, log_gamma[:, None, None] * rel, -jnp.inf))
    q_decay = jnp.exp(log_gamma[:, None] * (pos + 1.0))[..., None]
    k_decay = jnp.exp(log_gamma[:, None] * (L - 1.0 - pos))[..., None]
    chunk_decay = jnp.exp(log_gamma * L)[:, None, None]
    return (intra,
            jnp.broadcast_to(q_decay, (H, L, RET_V_DIM)),
            jnp.broadcast_to(k_decay, (H, L, RET_QK_DIM)),
            jnp.broadcast_to(chunk_decay, (H, RET_QK_DIM, RET_V_DIM)))


def retention(q, k, v, g, batch, n_ch=4):
    T = q.shape[0]
    L = RET_CHUNK
    nc = T // batch // L
    H = RET_HEADS
    intra, qd, kd, cd = _retention_consts()
    cur = lambda c: (0, c, 0)
    const = lambda c: (0, 0, 0)
    qkw = H * RET_QK_DIM
    per_batch = lambda a: a.reshape(batch, T // batch, a.shape[1])
    slab = n_ch * L
    out = pl.pallas_call(
        _retention_kernel,
        grid=(nc // n_ch,),
        in_specs=[pl.BlockSpec((batch, slab, qkw), cur), pl.BlockSpec((batch, slab, qkw), cur),
                  pl.BlockSpec((batch, slab, MIX_WIDTH), cur), pl.BlockSpec((batch, slab, MIX_WIDTH), cur),
                  pl.BlockSpec((H, L, L), const), pl.BlockSpec((H, L, RET_V_DIM), const),
                  pl.BlockSpec((H, L, RET_QK_DIM), const), pl.BlockSpec((H, RET_QK_DIM, RET_V_DIM), const)],
        out_specs=pl.BlockSpec((batch, slab, MIX_WIDTH), cur),
        out_shape=jax.ShapeDtypeStruct((batch, T // batch, MIX_WIDTH), BF16),
        scratch_shapes=[pltpu.VMEM((batch, H, RET_QK_DIM, RET_V_DIM), F32)],
        compiler_params=_cparams(1),
        name="retention",
    )(per_batch(q), per_batch(k), per_batch(v), per_batch(g), intra, qd, kd, cd)
    return out.reshape(T, MIX_WIDTH)


def _split3(x):
    x1 = x.astype(BF16)
    r1 = x - x1.astype(F32)
    x2 = r1.astype(BF16)
    x3 = (r1 - x2.astype(F32)).astype(BF16)
    return x1, x2, x3


def _mlstm_kernel(qk_ref, v_ref, og_ref, gc_ref, gr_ref, cw_ref, cb_ref, bc_ref, br_ref, gain_ref,
                  tri_ref, trit_ref, o_ref, xbuf_ref, c_ref, n_ref, m_ref):
    for b in range(qk_ref.shape[0]):
        _mlstm_chunk(qk_ref.at[b], v_ref.at[b], og_ref.at[b], gc_ref.at[b], gr_ref.at[b], cw_ref, cb_ref,
                     bc_ref, br_ref, gain_ref, tri_ref, trit_ref, o_ref.at[b],
                     xbuf_ref.at[b], c_ref.at[b], n_ref.at[b], m_ref.at[b])


def _mlstm_chunk(qk_ref, v_ref, og_ref, gc_ref, gr_ref, cw_ref, cb_ref, bc_ref, br_ref, gain_ref,
                 tri_ref, trit_ref, o_ref, xbuf_ref, c_ref, n_ref, m_ref):
    L = MLSTM_CHUNK
    H = MLSTM_HEADS
    P = MLSTM_QK_PAD
    VP = MLSTM_V_PAD
    KT = SUBLANES

    @pl.when(pl.program_id(0) == 0)
    def _():
        xbuf_ref[0:KT, :] = jnp.zeros((KT, xbuf_ref.shape[1]), F32)
        c_ref[...] = jnp.zeros_like(c_ref)
        n_ref[...] = jnp.zeros_like(n_ref)
        m_ref[...] = jnp.zeros_like(m_ref)

    xbuf_ref[KT:KT + L, :] = qk_ref[...]
    acc = cb_ref[...] + cw_ref[MLSTM_CONV - 1:MLSTM_CONV, :] * xbuf_ref[KT:KT + L, :]
    for j in range(MLSTM_CONV - 1):
        sh = MLSTM_CONV - 1 - j
        acc = acc + cw_ref[j:j + 1, :] * xbuf_ref[KT - sh:KT - sh + L, :]
    xbuf_ref[0:KT, :] = qk_ref[L - KT:L, :]
    qk = _silu(acc)

    xc = gc_ref[...] + bc_ref[...]
    xr = gr_ref[...] + br_ref[...]
    lfc = jax.nn.log_sigmoid(xc)
    lfr = jax.nn.log_sigmoid(xr)
    tri = tri_ref[...]
    trit = trit_ref[...]
    bc = sum(_dot(tri, t) for t in _split3(lfc))
    br = sum(_dot(t, trit) for t in _split3(lfr))

    rowi = lax.broadcasted_iota(I32, (L, L), 0)
    coli = lax.broadcasted_iota(I32, (L, L), 1)
    causal = rowi >= coli
    outs = []
    for h in range(H):
        q = qk[:, h * P:(h + 1) * P].astype(BF16)
        k = qk[:, (H + h) * P:(H + h + 1) * P] * (MLSTM_QK_DIM ** -0.5)
        v = v_ref[:, h * VP:(h + 1) * VP]
        li_c = xc[:, h:h + 1]
        b_c = bc[:, H + h:H + h + 1]
        li_r = xr[h:h + 1, :]
        b_r = br[H + h:H + h + 1, :]
        g = b_c[L - 1:L, :]
        m = m_ref[h:h + 1, 0:1]
        C = c_ref[h]
        nvec = n_ref[h:h + 1, :]

        dmat = jnp.where(causal, b_c - b_r + li_r, -jnp.inf)
        inter = b_c + m
        m_t = jnp.maximum(inter, jnp.max(dmat, axis=-1, keepdims=True))
        w = jnp.exp(dmat - m_t)
        a = jnp.exp(inter - m_t)
        s = _dot_nt(q, k.astype(BF16)) * w
        num = _dot(s.astype(BF16), v) + a * _dot(q, C.astype(BF16))
        qf = qk[:, h * P:(h + 1) * P]
        den = jnp.sum(s, axis=-1, keepdims=True) + a * jnp.sum(qf * nvec, axis=-1, keepdims=True)
        hh = num / jnp.maximum(jnp.abs(den), jnp.exp(-m_t))

        u_c = g - b_c + li_c
        u_r = g - b_r + li_r
        m_new = jnp.maximum(g + m, jnp.max(u_r, axis=-1, keepdims=True))
        wk = jnp.exp(u_c - m_new)
        decay = jnp.exp(g + m - m_new)
        kw = k * wk
        c_ref[h] = decay * C + _dot_tn(kw.astype(BF16), v)
        n_ref[h:h + 1, :] = decay * nvec + jnp.sum(kw, axis=0, keepdims=True)
        m_ref[h:h + 1, :] = jnp.broadcast_to(m_new, (1, m_ref.shape[1]))

        ms = jnp.sum(hh * hh, axis=-1, keepdims=True) * (1.0 / MLSTM_V_DIM)
        hc = hh * lax.rsqrt(ms + RMS_EPS) * gain_ref[:, h * VP:(h + 1) * VP]
        outs.append(jax.nn.sigmoid(og_ref[:, h * VP:(h + 1) * VP]) * hc)
    o_ref[...] = jnp.concatenate(outs, axis=-1).astype(o_ref.dtype)


def mlstm(qk, v, og, gates, gates_t, conv_w, conv_b, bias_c, bias_r, gain, batch):
    T = qk.shape[0]
    L = MLSTM_CHUNK
    nc = T // batch // L
    H, P, VP = MLSTM_HEADS, MLSTM_QK_PAD, MLSTM_V_PAD
    tri = jnp.tril(jnp.ones((L, L), BF16))
    cur = lambda c: (0, c, 0)
    c2 = lambda c: (0, 0)
    per_batch = lambda a: a.reshape(batch, T // batch, a.shape[1])
    out = pl.pallas_call(
        _mlstm_kernel,
        grid=(nc,),
        in_specs=[pl.BlockSpec((batch, L, 2 * H * P), cur), pl.BlockSpec((batch, L, H * VP), cur),
                  pl.BlockSpec((batch, L, H * VP), cur), pl.BlockSpec((batch, L, LANES), cur),
                  pl.BlockSpec((batch, SUBLANES, L), lambda c: (0, 0, c)),
                  pl.BlockSpec((MLSTM_CONV, 2 * H * P), c2), pl.BlockSpec((1, 2 * H * P), c2),
                  pl.BlockSpec((1, LANES), c2), pl.BlockSpec((SUBLANES, L), c2),
                  pl.BlockSpec((1, H * VP), c2),
                  pl.BlockSpec((L, L), c2), pl.BlockSpec((L, L), c2)],
        out_specs=pl.BlockSpec((batch, L, H * VP), cur),
        out_shape=jax.ShapeDtypeStruct((batch, T // batch, H * VP), BF16),
        scratch_shapes=[pltpu.VMEM((batch, SUBLANES + L, 2 * H * P), F32),
                        pltpu.VMEM((batch, H, P, VP), F32),
                        pltpu.VMEM((batch, SUBLANES, P), F32),
                        pltpu.VMEM((batch, SUBLANES, LANES), F32)],
        compiler_params=_cparams(1),
        name="mlstm",
    )(per_batch(qk), per_batch(v), per_batch(og), per_batch(gates), gates_t,
      conv_w, conv_b, bias_c, bias_r, gain, tri, tri.T)
    return out.reshape(T, H * VP)


def _out_proj_kernel(h_ref, mix_ref, q_ref, k_ref, v_ref, w1_ref, w2_ref, o_ref):
    outs = []
    for hd in range(MEM_HEADS):
        sl = slice(hd * MEM_HEAD_DIM, (hd + 1) * MEM_HEAD_DIM)
        s = _dot_nt(k_ref[:, sl], q_ref[:, sl]) * (MEM_HEAD_DIM ** -0.5)
        e = jnp.exp(s - jnp.max(s, axis=0, keepdims=True))
        p = (e * (1.0 / jnp.sum(e, axis=0, keepdims=True))).astype(BF16)
        outs.append(_dot_tn(p, v_ref[:, sl]))
    mem_out = jnp.concatenate(outs, axis=-1).astype(BF16)
    o_ref[...] = h_ref[...] + _dot(mix_ref[...], w1_ref[...]) + _dot(mem_out, w2_ref[...])


def out_proj(h, mix, qm, mk, mv, w_mix, w_mem, batch, tm):
    T, D = h.shape
    Wm = mix.shape[1]
    M = mk.shape[0] // batch
    nt = T // batch // tm
    rows = lambda b, i: (b * nt + i, 0)
    mem = lambda b, i: (b, 0)
    const = lambda b, i: (0, 0)
    return pl.pallas_call(
        _out_proj_kernel,
        grid=(batch, nt),
        in_specs=[pl.BlockSpec((tm, D), rows),
                  pl.BlockSpec((tm, Wm), rows),
                  pl.BlockSpec((tm, MEM_WIDTH), rows),
                  pl.BlockSpec((M, MEM_WIDTH), mem),
                  pl.BlockSpec((M, MEM_WIDTH), mem),
                  pl.BlockSpec((Wm, D), const),
                  pl.BlockSpec((MEM_WIDTH, D), const)],
        out_specs=pl.BlockSpec((tm, D), rows),
        out_shape=jax.ShapeDtypeStruct((T, D), F32),
        compiler_params=_cparams(2),
        name="out_proj",
    )(h, mix, qm, mk, mv, w_mix, w_mem)


def _dense_ffn_kernel(h_ref, g_ref, wg_ref, wu_ref, wd_ref, o_ref, xn_ref):
    @pl.when(pl.program_id(1) == 0)
    def _():
        h = h_ref[...]
        xn_ref[...] = (_rms(h) * g_ref[...]).astype(BF16)
        o_ref[...] = h

    xn = xn_ref[...]
    a = _silu(_dot(xn, wg_ref[...])) * _dot(xn, wu_ref[...])
    o_ref[...] += _dot(a.astype(BF16), wd_ref[...])


def dense_ffn(h, gain, w_gate, w_up, w_down, tm, tf):
    T, D = h.shape
    Fd = w_gate.shape[1]
    return pl.pallas_call(
        _dense_ffn_kernel,
        grid=(T // tm, Fd // tf),
        in_specs=[pl.BlockSpec((tm, D), lambda i, j: (i, 0)),
                  pl.BlockSpec((1, D), lambda i, j: (0, 0)),
                  pl.BlockSpec((D, tf), lambda i, j: (0, j)),
                  pl.BlockSpec((D, tf), lambda i, j: (0, j)),
                  pl.BlockSpec((tf, D), lambda i, j: (j, 0))],
        out_specs=pl.BlockSpec((tm, D), lambda i, j: (i, 0)),
        out_shape=jax.ShapeDtypeStruct((T, D), F32),
        scratch_shapes=[pltpu.VMEM((tm, D), BF16)],
        compiler_params=_cparams(2),
        name="dense_ffn",
    )(h, gain.reshape(1, D), w_gate, w_up, w_down)


def _router_kernel(h_ref, g_ref, wr_ref, tri_ref, o_ref, ot_ref, cnt_ref, carry_ref):
    i = pl.program_id(0)

    @pl.when(i == 0)
    def _():
        carry_ref[...] = jnp.zeros_like(carry_ref)

    tm = h_ref.shape[0]
    xn = _rms(h_ref[...]) * g_ref[...]
    x_hi = xn.astype(BF16)
    x_lo = (xn - x_hi.astype(F32)).astype(BF16)
    wr = wr_ref[...]
    w_hi = wr.astype(BF16)
    w_lo = (wr - w_hi.astype(F32)).astype(BF16)
    logits = _dot(x_hi, w_hi) + (_dot(x_hi, w_lo) + _dot(x_lo, w_hi))
    lane = lax.broadcasted_iota(I32, (tm, LANES), 1)
    logits = jnp.where(lane < N_EXPERTS, logits, -jnp.inf)
    t1 = jnp.max(logits, axis=-1, keepdims=True)
    e1 = jnp.min(jnp.where(logits == t1, lane, LANES), axis=-1, keepdims=True)
    rest = jnp.where(lane == e1, -jnp.inf, logits)
    t2 = jnp.max(rest, axis=-1, keepdims=True)
    e2 = jnp.min(jnp.where(rest == t2, lane, LANES), axis=-1, keepdims=True)
    x2 = jnp.exp(t2 - t1)
    w1 = 1.0 / (1.0 + x2)
    w2 = x2 / (1.0 + x2)

    oh1 = lane == e1
    oh2 = lane == e2
    cnt = jnp.where(oh1 | oh2, 1.0, 0.0)
    before = _dot(tri_ref[...], cnt.astype(BF16)) + carry_ref[...]
    r1 = jnp.sum(jnp.where(oh1, before, 0.0), axis=-1, keepdims=True)
    r2 = jnp.sum(jnp.where(oh2, before, 0.0), axis=-1, keepdims=True)
    carry_ref[...] += jnp.sum(cnt, axis=0, keepdims=True)

    cols = (e1.astype(F32), e2.astype(F32), w1, w2, r1, r2)
    out = jnp.zeros((tm, LANES), F32)
    for c, val in enumerate(cols):
        out = jnp.where(lane == c, val, out)
    o_ref[...] = out
    ot_ref[...] = out.T[:SUBLANES, :]
    cnt_ref[...] = jnp.broadcast_to(carry_ref[...], cnt_ref.shape)


def moe_router(h, gain, w_router, tm):
    T, D = h.shape
    wr = jnp.zeros((D, LANES), F32).at[:, :N_EXPERTS].set(w_router)
    tri = jnp.tril(jnp.ones((tm, tm), BF16), k=-1)
    return pl.pallas_call(
        _router_kernel,
        grid=(T // tm,),
        in_specs=[pl.BlockSpec((tm, D), lambda i: (i, 0)),
                  pl.BlockSpec((1, D), lambda i: (0, 0)),
                  pl.BlockSpec((D, LANES), lambda i: (0, 0)),
                  pl.BlockSpec((tm, tm), lambda i: (0, 0))],
        out_specs=[pl.BlockSpec((tm, LANES), lambda i: (i, 0)),
                   pl.BlockSpec((SUBLANES, tm), lambda i: (0, i)),
                   pl.BlockSpec((SUBLANES, LANES), lambda i: (0, 0))],
        out_shape=[jax.ShapeDtypeStruct((T, LANES), F32),
                   jax.ShapeDtypeStruct((SUBLANES, T), F32),
                   jax.ShapeDtypeStruct((SUBLANES, LANES), F32)],
        scratch_shapes=[pltpu.VMEM((1, LANES), F32)],
        compiler_params=_cparams(1),
        name="moe_router",
    )(h, gain.reshape(1, D), wr, tri)


def _dispatch_kernel(ze_ref, zn_ref, p1_ref, p2_ref, h_ref, xs_ref, zero_ref, sem, zsem, *, pad_max, tail_max):
    tb = p1_ref.shape[0]
    zr = zero_ref.shape[0]

    @pl.when(pl.program_id(0) == 0)
    def _():
        zero_ref[...] = jnp.zeros_like(zero_ref)
        chunks = [(e, c) for e in range(N_EXPERTS) for c in range(pad_max // zr)]
        chunks += [(N_EXPERTS, c) for c in range(tail_max // zr)]

        def zcopy(e, c):
            start = pl.multiple_of(ze_ref[e] - (c + 1) * zr, zr)
            return pltpu.make_async_copy(zero_ref, xs_ref.at[pl.ds(start, zr), :], zsem)

        def needed(e, c):
            return c * zr < zn_ref[e]

        for e, c in chunks:
            @pl.when(needed(e, c))
            def _():
                zcopy(e, c).start()

        for e, c in chunks:
            @pl.when(needed(e, c))
            def _():
                zcopy(e, c).wait()

    def copy(t, pos):
        return pltpu.make_async_copy(h_ref.at[pl.ds(t, 1), :], xs_ref.at[pl.ds(pos, 1), :], sem)

    for t in range(tb):
        copy(t, p1_ref[t]).start(priority=0)
        copy(t, p2_ref[t]).start(priority=1)
    for _ in range(TOP_K):
        pltpu.make_async_copy(h_ref, xs_ref.at[pl.ds(0, tb), :], sem).wait()


def moe_dispatch(h, pos1, pos2, pad_end, pad_len, used_rows, n_rows, pad_max, tb, zr=256):
    T, D = h.shape
    tail_max = n_rows - TOP_K * T
    assert tail_max % zr == 0 and pad_max % zr == 0
    zero_end = jnp.concatenate([pad_end, jnp.full((1,), n_rows, I32)])
    zero_len = jnp.concatenate([pad_len, n_rows - used_rows[None]])
    grid_spec = pltpu.PrefetchScalarGridSpec(
        num_scalar_prefetch=2,
        grid=(T // tb,),
        in_specs=[pl.BlockSpec((tb,), lambda i, ps, pn: (i,), memory_space=pltpu.SMEM),
                  pl.BlockSpec((tb,), lambda i, ps, pn: (i,), memory_space=pltpu.SMEM),
                  pl.BlockSpec((tb, D), lambda i, ps, pn: (i, 0))],
        out_specs=pl.BlockSpec(memory_space=pl.ANY),
        scratch_shapes=[pltpu.VMEM((zr, D), h.dtype), pltpu.SemaphoreType.DMA(()), pltpu.SemaphoreType.DMA(())],
    )
    return pl.pallas_call(
        functools.partial(_dispatch_kernel, pad_max=pad_max, tail_max=tail_max),
        grid_spec=grid_spec,
        out_shape=jax.ShapeDtypeStruct((n_rows, D), h.dtype),
        compiler_params=_cparams(1),
        name="moe_dispatch",
    )(zero_end, zero_len, pos1, pos2, h)


def _expert_ffn_kernel(te_ref, tv_ref, x_ref, g_ref, wg_ref, wu_ref, wd_ref, wdl_ref, o_ref, xn_ref, a_ref,
                       *, ts, nf):
    i = pl.program_id(0)
    j = pl.program_id(1)
    valid = tv_ref[i]
    n_sub = x_ref.shape[0] // ts

    n_occ = (valid + ts - 1) // ts
    for k in range(1, n_sub + 1):
        rows = pl.ds(0, k * ts)

        def gate_up(slot):
            xn = xn_ref[rows, :]
            a = _silu(_dot(xn, wg_ref[...].astype(BF16))) * _dot(xn, wu_ref[...].astype(BF16))
            a_ref[slot, rows, :] = a.astype(BF16)

        def down(slot, first, w_ref=wd_ref):
            y = _dot(a_ref[slot, rows, :], w_ref[...].astype(BF16))
            if first:
                o_ref[rows, :] = y
                if k < n_sub:
                    o_ref[pl.ds(k * ts, (n_sub - k) * ts), :] = jnp.zeros(((n_sub - k) * ts, o_ref.shape[1]), F32)
            else:
                o_ref[rows, :] += y

        @pl.when(n_occ == k)
        def _():
            @pl.when(j == 0)
            def _():
                xn_ref[rows, :] = (_rms(x_ref[rows, :]) * g_ref[...]).astype(BF16)
                gate_up(0)

            @pl.when(j == 1)
            def _():
                gate_up(1)
                down(0, first=True)

            @pl.when((j > 1) & (j < nf - 1))
            def _():
                gate_up(j % 2)
                down((j - 1) % 2, first=False)

            @pl.when(j == nf - 1)
            def _():
                gate_up((nf - 1) % 2)
                down((nf - 2) % 2, first=False)
                down((nf - 1) % 2, first=False, w_ref=wdl_ref)

    @pl.when((n_occ == 0) & (j == 0))
    def _():
        o_ref[...] = jnp.zeros_like(o_ref)


def expert_ffn(xs, gain, w_gate, w_up, w_down, layer, tile_expert, tile_valid, n_tiles, tm, ts, tf):
    D = xs.shape[1]
    R = n_tiles * tm
    Fd = w_gate.shape[3]
    nf = Fd // tf

    assert nf >= 3
    def up_tile(i, j, tv):
        return jnp.where(tv[i] > 0, j, nf - 1)

    def down_tile(i, j, tv):
        return jnp.where(tv[i] > 0, jnp.maximum(j - 1, 0), nf - 2)

    grid_spec = pltpu.PrefetchScalarGridSpec(
        num_scalar_prefetch=2,
        grid=(n_tiles, nf),
        in_specs=[pl.BlockSpec((tm, D), lambda i, j, te, tv: (jnp.where(tv[i] > 0, i, 0), 0)),
                  pl.BlockSpec((1, D), lambda i, j, te, tv: (0, 0)),
                  pl.BlockSpec((None, None, D, tf), lambda i, j, te, tv: (layer, te[i], 0, up_tile(i, j, tv))),
                  pl.BlockSpec((None, None, D, tf), lambda i, j, te, tv: (layer, te[i], 0, up_tile(i, j, tv))),
                  pl.BlockSpec((None, None, tf, D), lambda i, j, te, tv: (layer, te[i], down_tile(i, j, tv), 0)),
                  pl.BlockSpec((None, None, tf, D), lambda i, j, te, tv: (layer, te[i], nf - 1, 0))],
        out_specs=pl.BlockSpec((tm, D), lambda i, j, te, tv: (i, 0)),
        scratch_shapes=[pltpu.VMEM((tm, D), BF16), pltpu.VMEM((2, tm, tf), BF16)],
    )
    return pl.pallas_call(
        functools.partial(_expert_ffn_kernel, ts=ts, nf=nf),
        grid_spec=grid_spec,
        out_shape=jax.ShapeDtypeStruct((R, D), F32),
        compiler_params=_cparams(2),
        name="expert_ffn",
    )(tile_expert, tile_valid, xs, gain.reshape(1, D), w_gate, w_up, w_down, w_down)


def _combine_kernel(p1_ref, p2_ref, route_ref, h_ref, y_ref, o_ref, b1_ref, b2_ref, sem):
    tb = p1_ref.shape[0]

    def copies(t):
        dst = pl.ds(t, 1)
        return (pltpu.make_async_copy(y_ref.at[pl.ds(p1_ref[t], 1), :], b1_ref.at[dst, :], sem),
                pltpu.make_async_copy(y_ref.at[pl.ds(p2_ref[t], 1), :], b2_ref.at[dst, :], sem))

    for t in range(tb):
        for prio, cp in enumerate(copies(t)):
            cp.start(priority=prio)
    for b_ref in (b1_ref, b2_ref):
        pltpu.make_async_copy(y_ref.at[pl.ds(0, tb), :], b_ref, sem).wait()
    w1 = route_ref[:, 2:3]
    w2 = route_ref[:, 3:4]
    o_ref[...] = h_ref[...] + w1 * b1_ref[...] + w2 * b2_ref[...]


def moe_combine(h, y, route, pos1, pos2, tb):
    T, D = h.shape
    smem = lambda: pl.BlockSpec((tb,), lambda i: (i,), memory_space=pltpu.SMEM)
    return pl.pallas_call(
        _combine_kernel,
        grid=(T // tb,),
        in_specs=[smem(), smem(),
                  pl.BlockSpec((tb, LANES), lambda i: (i, 0)),
                  pl.BlockSpec((tb, D), lambda i: (i, 0)),
                  pl.BlockSpec(memory_space=pl.ANY)],
        out_specs=pl.BlockSpec((tb, D), lambda i: (i, 0)),
        out_shape=jax.ShapeDtypeStruct((T, D), F32),
        scratch_shapes=[pltpu.VMEM((tb, D), F32), pltpu.VMEM((tb, D), F32),
                        pltpu.SemaphoreType.DMA(())],
        compiler_params=_cparams(1),
        name="moe_combine",
    )(pos1, pos2, route, h, y)


def moe_ffn(h, gain, w_router, w_gate, w_up, w_down, layer, tm=1024, ts=256, tf=512, tb=512):
    T = h.shape[0]
    route, route_t, counts = moe_router(h, gain, w_router, tm=512)
    e1 = route_t[0].astype(I32)
    e2 = route_t[1].astype(I32)
    counts = counts[0, :N_EXPERTS].astype(I32)
    tiles_per = (counts + tm - 1) // tm
    tile_end = jnp.cumsum(tiles_per)
    tile_start = tile_end - tiles_per
    row_start = tile_start * tm
    pos1 = row_start[e1] + route_t[4].astype(I32)
    pos2 = row_start[e2] + route_t[5].astype(I32)
    n_tiles = (TOP_K * T) // tm + N_EXPERTS
    tile_ids = jnp.arange(n_tiles, dtype=I32)
    tile_expert = jnp.sum(jnp.minimum(tile_ids, tile_end[-1] - 1)[:, None] >= tile_end[None, :], axis=1).astype(I32)
    tile_valid = jnp.clip(counts[tile_expert] - (tile_ids - tile_start[tile_expert]) * tm, 0, tm)
    tile_valid = jnp.where(tile_ids < tile_end[-1], tile_valid, 0).astype(I32)
    pad_len = tiles_per * tm - counts

    xs = moe_dispatch(h, pos1, pos2, tile_end * tm, pad_len, tile_end[-1] * tm, n_tiles * tm, tm, tb)
    y = expert_ffn(xs, gain, w_gate, w_up, w_down, layer, tile_expert, tile_valid, n_tiles, tm, ts, tf)
    return moe_combine(h, y, route, pos1, pos2, tb)


def _pad_heads(a, n_heads, width, pad_to, axis):
    shape = list(a.shape)
    a = a.reshape(shape[:axis] + [n_heads, width] + shape[axis + 1:])
    pads = [(0, 0)] * a.ndim
    pads[axis + 1] = (0, pad_to - width)
    a = jnp.pad(a, pads)
    return a.reshape(shape[:axis] + [n_heads * pad_to] + shape[axis + 1:])


def kernel(x, mem, ln_mix, ln_mem, w_mem_kv, mem_q_gain, mem_k_gain, w_out, ln_ffn, swa_w_in, swa_q_gain, swa_k_gain, swa_sink, ret_w_in, mlstm_w_in, mlstm_conv_w, mlstm_conv_b, mlstm_i_bias, mlstm_f_bias, mlstm_out_gain, ffn_w_gate, ffn_w_up, ffn_w_down, moe_router, moe_w_gate, moe_w_up, moe_w_down):
    B, S, D = x.shape
    M = mem.shape[1]
    T = B * S
    depth = ln_mix.shape[0]
    h = x.reshape(T, D)
    mem2 = mem.reshape(B * M, D)

    for layer in range(depth):
        kind = layer % N_MIXERS
        idx = layer // N_MIXERS
        mk, mv = rms_proj(mem2, ln_mem[layer], w_mem_kv[layer].astype(BF16),
                          ((MEM_WIDTH, BF16, mem_k_gain[layer]), (MEM_WIDTH, BF16, None)),
                          tm=B * M, name="mem_kv_proj")
        w_o = w_out[layer].astype(BF16)
        w_o_mix, w_o_mem = w_o[:MIX_WIDTH], w_o[MIX_WIDTH:]
        qm_out = (MEM_WIDTH, BF16, mem_q_gain[layer])

        if kind == 0:
            kvw = SWA_KV_HEADS * HEAD_DIM
            q, k, v, qm = rms_proj(h, ln_mix[layer], swa_w_in[idx].astype(BF16),
                                   ((MIX_WIDTH, BF16, swa_q_gain[idx]), (kvw, BF16, swa_k_gain[idx]),
                                    (kvw, BF16, None), qm_out), tm=1024, name="swa_in_proj")
            mix = swa_attention(q, k, v, swa_sink[idx], B)
        elif kind == 1:
            qkw = RET_HEADS * RET_QK_DIM
            q, k, v, g, qm = rms_proj(h, ln_mix[layer], ret_w_in[idx].astype(BF16),
                                      ((qkw, BF16, None), (qkw, F32, None), (MIX_WIDTH, BF16, None),
                                       (MIX_WIDTH, F32, None), qm_out), tm=1024, name="ret_in_proj")
            mix = retention(q, k, v, g, B)
        else:
            H, P, VP = MLSTM_HEADS, MLSTM_QK_PAD, MLSTM_V_PAD
            w = mlstm_w_in[idx]
            qkw = 2 * H * MLSTM_QK_DIM
            o_v, o_og, o_ig = qkw, qkw + MIX_WIDTH, qkw + 2 * MIX_WIDTH
            o_fg, o_qm = o_ig + H, o_ig + 2 * H
            w_gates = jnp.zeros((D, LANES), F32).at[:, :2 * H].set(w[:, o_ig:o_qm])
            w_pad = jnp.concatenate([
                _pad_heads(w[:, :qkw], 2 * H, MLSTM_QK_DIM, P, 1),
                _pad_heads(w[:, o_v:o_og], H, MLSTM_V_DIM, VP, 1),
                _pad_heads(w[:, o_og:o_ig], H, MLSTM_V_DIM, VP, 1),
                w_gates, w[:, o_qm:]], axis=1).astype(BF16)
            qk, v, og, gates, qm = rms_proj(h, ln_mix[layer], w_pad,
                                            ((2 * H * P, F32, None), (H * VP, BF16, None), (H * VP, F32, None),
                                             (LANES, F32, None), qm_out), tm=1024, name="mlstm_in_proj")
            gates_t = gates[:, :SUBLANES].reshape(B, S, SUBLANES).transpose(0, 2, 1)
            bias = jnp.concatenate([mlstm_i_bias[idx], mlstm_f_bias[idx]])
            bias_c = jnp.zeros((1, LANES), F32).at[0, :2 * H].set(bias)
            bias_r = jnp.broadcast_to(bias[:, None], (SUBLANES, MLSTM_CHUNK))
            mix = mlstm(qk, v, og, gates, gates_t,
                        _pad_heads(mlstm_conv_w[idx], 2 * H, MLSTM_QK_DIM, P, 1),
                        _pad_heads(mlstm_conv_b[idx][None], 2 * H, MLSTM_QK_DIM, P, 1),
                        bias_c, bias_r,
                        _pad_heads(mlstm_out_gain[idx][None], H, MLSTM_V_DIM, VP, 1), B)
            w_o_mix = _pad_heads(w_o_mix, H, MLSTM_V_DIM, VP, 0)

        h = out_proj(h, mix, qm, mk, mv, w_o_mix, w_o_mem, B, tm=1024)
        j = layer // 2
        if layer % 2 == 0:
            h = dense_ffn(h, ln_ffn[layer], ffn_w_gate[j].astype(BF16), ffn_w_up[j].astype(BF16),
                          ffn_w_down[j].astype(BF16), tm=1024, tf=1792)
        else:
            h = moe_ffn(h, ln_ffn[layer], moe_router[j], moe_w_gate, moe_w_up, moe_w_down, j)
    return h.reshape(B, S, D)
```

```python
import functools
import math

import jax
import jax.numpy as jnp
from jax import lax
from jax.experimental import pallas as pl
from jax.experimental.pallas import tpu as pltpu

F32 = jnp.float32
BF16 = jnp.bfloat16
I32 = jnp.int32

D_MODEL = 1024
N_MIXERS = 3
HEAD_DIM = 64
MEM_HEADS = 4
MEM_HEAD_DIM = 64
MEM_WIDTH = MEM_HEADS * MEM_HEAD_DIM
MIX_WIDTH = D_MODEL - MEM_WIDTH
RMS_EPS = 1e-6
NEG_INF = -1e30

SWA_Q_HEADS = MIX_WIDTH // HEAD_DIM
SWA_KV_HEADS = 4
SWA_GROUP = SWA_Q_HEADS // SWA_KV_HEADS
SWA_WINDOW = 128
SWA_BLOCK = 128

RET_HEADS = 6
RET_QK_DIM = 64
RET_V_DIM = MIX_WIDTH // RET_HEADS
RET_CHUNK = 128

MLSTM_HEADS = 4
MLSTM_V_DIM = MIX_WIDTH // MLSTM_HEADS
MLSTM_QK_DIM = MLSTM_V_DIM // 2
MLSTM_CHUNK = 128
MLSTM_CONV = 4
MLSTM_QK_PAD = 128
MLSTM_V_PAD = 256

N_EXPERTS = 8
TOP_K = 2

LANES = 128
SUBLANES = 8
VMEM_LIMIT = 56 * 1024 * 1024


def _cparams(n_axes):
    return pltpu.CompilerParams(dimension_semantics=("arbitrary",) * n_axes,
                                vmem_limit_bytes=VMEM_LIMIT)


def _rms(x, eps=RMS_EPS):
    return x * lax.rsqrt(jnp.mean(x * x, axis=-1, keepdims=True) + eps)


def _dot(a, b):
    return jnp.dot(a, b, preferred_element_type=F32)


def _dot_nt(a, b):
    return lax.dot_general(a, b, (((1,), (1,)), ((), ())), preferred_element_type=F32)


def _dot_tn(a, b):
    return lax.dot_general(a, b, (((0,), (0,)), ((), ())), preferred_element_type=F32)


def _silu(x):
    return x * jax.nn.sigmoid(x)


NORM_CHUNK = 256


def _head_rms(y, seg_ref, hg):
    sq = y * y
    hi = sq.astype(BF16)
    lo = (sq - hi.astype(F32)).astype(BF16)
    ms = _dot(hi, seg_ref[...]) + _dot(lo, seg_ref[...])
    return y * lax.rsqrt(ms + RMS_EPS) * hg


def _rms_proj_kernel(x_ref, g_ref, w_ref, seg_ref, *refs, widths, normed):
    n_gain = sum(normed)
    gain_refs, o_refs = refs[:n_gain], refs[n_gain:]
    xn = (_rms(x_ref[...]) * g_ref[...]).astype(BF16)
    off = 0
    gi = 0
    for o_ref, wd, nrm in zip(o_refs, widths, normed):
        if nrm:
            hg = gain_refs[gi][...]
            gi += 1
            for c in range(0, wd, NORM_CHUNK):
                y = _dot(xn, w_ref[:, off + c:off + c + NORM_CHUNK])
                o_ref[:, c:c + NORM_CHUNK] = _head_rms(y, seg_ref, hg).astype(o_ref.dtype)
        else:
            o_ref[...] = _dot(xn, w_ref[:, off:off + wd]).astype(o_ref.dtype)
        off += wd


def rms_proj(x, gain, w, outs, tm, name):
    T, D = x.shape
    N = w.shape[1]
    widths = tuple(o[0] for o in outs)
    normed = tuple(o[2] is not None for o in outs)
    head_gains = [jnp.tile(o[2], NORM_CHUNK // HEAD_DIM).reshape(1, NORM_CHUNK) for o in outs if o[2] is not None]
    assert sum(widths) == N and T % tm == 0
    assert all(wd % NORM_CHUNK == 0 for wd, nrm in zip(widths, normed) if nrm)
    head_of = jnp.arange(NORM_CHUNK) // HEAD_DIM
    seg = jnp.where(head_of[:, None] == head_of[None, :], 1.0 / HEAD_DIM, 0.0).astype(BF16)
    return pl.pallas_call(
        functools.partial(_rms_proj_kernel, widths=widths, normed=normed),
        grid=(T // tm,),
        in_specs=[pl.BlockSpec((tm, D), lambda i: (i, 0)),
                  pl.BlockSpec((1, D), lambda i: (0, 0)),
                  pl.BlockSpec((D, N), lambda i: (0, 0)),
                  pl.BlockSpec((NORM_CHUNK, NORM_CHUNK), lambda i: (0, 0))]
                 + [pl.BlockSpec((1, NORM_CHUNK), lambda i: (0, 0))] * len(head_gains),
        out_specs=[pl.BlockSpec((tm, wd), lambda i: (i, 0)) for wd in widths],
        out_shape=[jax.ShapeDtypeStruct((T, o[0]), o[1]) for o in outs],
        compiler_params=_cparams(1),
        name=name,
    )(x, gain.reshape(1, D), w, seg, *head_gains)


def _swa_kernel(sink_ref, q_ref, kc_ref, kp_ref, vc_ref, vp_ref, bias_ref, o_ref):
    n = pl.program_id(1)
    L = SWA_BLOCK
    n_blk = q_ref.shape[0] // L
    key = lax.broadcasted_iota(I32, (L, L), 0)
    qry = lax.broadcasted_iota(I32, (L, L), 1)
    from_prev = key > qry
    own = key <= qry
    zero = jnp.zeros((L, L), BF16)
    for blk in range(n_blk):
        rows = slice(blk * L, (blk + 1) * L)
        prows = slice((blk - 1) * L, blk * L)
        outs = []
        for g in range(SWA_KV_HEADS):
            ksl = slice(g * HEAD_DIM, (g + 1) * HEAD_DIM)
            k_prev = kp_ref[:, ksl] if blk == 0 else kc_ref[prows, ksl]
            v_prev = vp_ref[:, ksl] if blk == 0 else vc_ref[prows, ksl]
            k = jnp.concatenate([k_prev, kc_ref[rows, ksl]], axis=0)
            v = jnp.concatenate([v_prev, vc_ref[rows, ksl]], axis=0)
            for h in range(g * SWA_GROUP, (g + 1) * SWA_GROUP):
                sink = sink_ref[h]
                s2 = _dot_nt(k, q_ref[rows, h * HEAD_DIM:(h + 1) * HEAD_DIM])
                s = jnp.where(from_prev, s2[:L], s2[L:]) * (HEAD_DIM ** -0.5) - bias_ref[h]
                if blk == 0:
                    s = jnp.where(jnp.logical_or(n > 0, own), s, NEG_INF)
                m = jnp.maximum(jnp.max(s, axis=0, keepdims=True), sink)
                e = jnp.exp(s - m)
                denom = jnp.sum(e, axis=0, keepdims=True) + jnp.exp(sink - m)
                p = (e * (1.0 / denom)).astype(BF16)
                p2 = jnp.concatenate([jnp.where(from_prev, p, zero), jnp.where(from_prev, zero, p)], axis=0)
                outs.append(_dot_tn(p2, v))
        o_ref[rows, :] = jnp.concatenate(outs, axis=-1).astype(o_ref.dtype)


def swa_attention(q, k, v, sink, batch, n_blk=4):
    T = q.shape[0]
    L = SWA_BLOCK
    nb = T // batch // L
    kvw = SWA_KV_HEADS * HEAD_DIM
    slopes = jnp.exp2(-8.0 * jnp.arange(1, SWA_Q_HEADS + 1, dtype=F32) / SWA_Q_HEADS)
    assert SWA_WINDOW == L
    qi, kj = jnp.arange(L)[:, None], jnp.arange(L)[None, :]
    dist = jnp.where(kj > qi, qi + L - kj, qi - kj).astype(F32)
    bias = slopes[:, None, None] * dist.T
    ns = nb // n_blk
    slab = n_blk * L
    cur = lambda b, n: (b * ns + n, 0)
    prev = lambda b, n: (b * nb + jnp.maximum(n * n_blk - 1, 0), 0)
    return pl.pallas_call(
        _swa_kernel,
        grid=(batch, ns),
        in_specs=[pl.BlockSpec(memory_space=pltpu.SMEM),
                  pl.BlockSpec((slab, MIX_WIDTH), cur),
                  pl.BlockSpec((slab, kvw), cur), pl.BlockSpec((L, kvw), prev),
                  pl.BlockSpec((slab, kvw), cur), pl.BlockSpec((L, kvw), prev),
                  pl.BlockSpec((SWA_Q_HEADS, L, L), lambda b, n: (0, 0, 0))],
        out_specs=pl.BlockSpec((slab, MIX_WIDTH), cur),
        out_shape=jax.ShapeDtypeStruct((T, MIX_WIDTH), BF16),
        compiler_params=_cparams(2),
        name="swa_attention",
    )(sink, q, k, k, v, v, bias)


def _retention_kernel(q_ref, k_ref, v_ref, g_ref, dec_ref, qd_ref, kd_ref, cd_ref, o_ref, state_ref):
    @pl.when(pl.program_id(0) == 0)
    def _():
        state_ref[...] = jnp.zeros_like(state_ref)

    L = RET_CHUNK
    n_ch = q_ref.shape[1] // L
    for b in range(q_ref.shape[0]):
        outs = [[] for _ in range(n_ch)]
        for h in range(RET_HEADS):
            qsl = slice(h * RET_QK_DIM, (h + 1) * RET_QK_DIM)
            vsl = slice(h * RET_V_DIM, (h + 1) * RET_V_DIM)
            state = state_ref[b, h]
            for c in range(n_ch):
                rows = slice(c * L, (c + 1) * L)
                q = q_ref[b, rows, qsl]
                k = k_ref[b, rows, qsl] * (RET_QK_DIM ** -0.5)
                v = v_ref[b, rows, vsl]
                s = _dot_nt(q, k.astype(BF16)) * dec_ref[h]
                o = _dot(s.astype(BF16), v) + _dot(q, state.astype(BF16)) * qd_ref[h]
                state = state * cd_ref[h] + _dot_tn((k * kd_ref[h]).astype(BF16), v)
                outs[c].append(_silu(g_ref[b, rows, vsl]) * _rms(o))
            state_ref[b, h] = state
        for c in range(n_ch):
            o_ref[b, c * L:(c + 1) * L, :] = jnp.concatenate(outs[c], axis=-1).astype(o_ref.dtype)


def _retention_consts():
    H, L = RET_HEADS, RET_CHUNK
    log_gamma = jnp.log1p(-jnp.exp2(-5.0 - jnp.arange(H, dtype=F32)))
    pos = jnp.arange(L, dtype=F32)
    rel = pos[:, None] - pos[None, :]
    intra = jnp.exp(jnp.where(rel >= 0, log_gamma[:, None, None] * rel, -jnp.inf))
    q_decay = jnp.exp(log_gamma[:, None] * (pos + 1.0))[..., None]
    k_decay = jnp.exp(log_gamma[:, None] * (L - 1.0 - pos))[..., None]
    chunk_decay = jnp.exp(log_gamma * L)[:, None, None]
    return (intra,
            jnp.broadcast_to(q_decay, (H, L, RET_V_DIM)),
            jnp.broadcast_to(k_decay, (H, L, RET_QK_DIM)),
            jnp.broadcast_to(chunk_decay, (H, RET_QK_DIM, RET_V_DIM)))


def retention(q, k, v, g, batch, n_ch=4):
    T = q.shape[0]
    L = RET_CHUNK
    nc = T // batch // L
    H = RET_HEADS
    intra, qd, kd, cd = _retention_consts()
    cur = lambda c: (0, c, 0)
    const = lambda c: (0, 0, 0)
    qkw = H * RET_QK_DIM
    per_batch = lambda a: a.reshape(batch, T // batch, a.shape[1])
    slab = n_ch * L
    out = pl.pallas_call(
        _retention_kernel,
        grid=(nc // n_ch,),
        in_specs=[pl.BlockSpec((batch, slab, qkw), cur), pl.BlockSpec((batch, slab, qkw), cur),
                  pl.BlockSpec((batch, slab, MIX_WIDTH), cur), pl.BlockSpec((batch, slab, MIX_WIDTH), cur),
                  pl.BlockSpec((H, L, L), const), pl.BlockSpec((H, L, RET_V_DIM), const),
                  pl.BlockSpec((H, L, RET_QK_DIM), const), pl.BlockSpec((H, RET_QK_DIM, RET_V_DIM), const)],
        out_specs=pl.BlockSpec((batch, slab, MIX_WIDTH), cur),
        out_shape=jax.ShapeDtypeStruct((batch, T // batch, MIX_WIDTH), BF16),
        scratch_shapes=[pltpu.VMEM((batch, H, RET_QK_DIM, RET_V_DIM), F32)],
        compiler_params=_cparams(1),
        name="retention",
    )(per_batch(q), per_batch(k), per_batch(v), per_batch(g), intra, qd, kd, cd)
    return out.reshape(T, MIX_WIDTH)


def _split3(x):
    x1 = x.astype(BF16)
    r1 = x - x1.astype(F32)
    x2 = r1.astype(BF16)
    x3 = (r1 - x2.astype(F32)).astype(BF16)
    return x1, x2, x3


def _mlstm_kernel(qk_ref, v_ref, og_ref, gc_ref, gr_ref, cw_ref, cb_ref, bc_ref, br_ref, gain_ref,
                  tri_ref, trit_ref, o_ref, xbuf_ref, c_ref, n_ref, m_ref):
    for b in range(qk_ref.shape[0]):
        _mlstm_chunk(qk_ref.at[b], v_ref.at[b], og_ref.at[b], gc_ref.at[b], gr_ref.at[b], cw_ref, cb_ref,
                     bc_ref, br_ref, gain_ref, tri_ref, trit_ref, o_ref.at[b],
                     xbuf_ref.at[b], c_ref.at[b], n_ref.at[b], m_ref.at[b])


def _mlstm_chunk(qk_ref, v_ref, og_ref, gc_ref, gr_ref, cw_ref, cb_ref, bc_ref, br_ref, gain_ref,
                 tri_ref, trit_ref, o_ref, xbuf_ref, c_ref, n_ref, m_ref):
    L = MLSTM_CHUNK
    H = MLSTM_HEADS
    P = MLSTM_QK_PAD
    VP = MLSTM_V_PAD
    KT = SUBLANES

    @pl.when(pl.program_id(0) == 0)
    def _():
        xbuf_ref[0:KT, :] = jnp.zeros((KT, xbuf_ref.shape[1]), F32)
        c_ref[...] = jnp.zeros_like(c_ref)
        n_ref[...] = jnp.zeros_like(n_ref)
        m_ref[...] = jnp.zeros_like(m_ref)

    xbuf_ref[KT:KT + L, :] = qk_ref[...]
    acc = cb_ref[...] + cw_ref[MLSTM_CONV - 1:MLSTM_CONV, :] * xbuf_ref[KT:KT + L, :]
    for j in range(MLSTM_CONV - 1):
        sh = MLSTM_CONV - 1 - j
        acc = acc + cw_ref[j:j + 1, :] * xbuf_ref[KT - sh:KT - sh + L, :]
    xbuf_ref[0:KT, :] = qk_ref[L - KT:L, :]
    qk = _silu(acc)

    xc = gc_ref[...] + bc_ref[...]
    xr = gr_ref[...] + br_ref[...]
    lfc = jax.nn.log_sigmoid(xc)
    lfr = jax.nn.log_sigmoid(xr)
    tri = tri_ref[...]
    trit = trit_ref[...]
    bc = sum(_dot(tri, t) for t in _split3(lfc))
    br = sum(_dot(t, trit) for t in _split3(lfr))

    rowi = lax.broadcasted_iota(I32, (L, L), 0)
    coli = lax.broadcasted_iota(I32, (L, L), 1)
    causal = rowi >= coli
    outs = []
    for h in range(H):
        q = qk[:, h * P:(h + 1) * P].astype(BF16)
        k = qk[:, (H + h) * P:(H + h + 1) * P] * (MLSTM_QK_DIM ** -0.5)
        v = v_ref[:, h * VP:(h + 1) * VP]
        li_c = xc[:, h:h + 1]
        b_c = bc[:, H + h:H + h + 1]
        li_r = xr[h:h + 1, :]
        b_r = br[H + h:H + h + 1, :]
        g = b_c[L - 1:L, :]
        m = m_ref[h:h + 1, 0:1]
        C = c_ref[h]
        nvec = n_ref[h:h + 1, :]

        dmat = jnp.where(causal, b_c - b_r + li_r, -jnp.inf)
        inter = b_c + m
        m_t = jnp.maximum(inter, jnp.max(dmat, axis=-1, keepdims=True))
        w = jnp.exp(dmat - m_t)
        a = jnp.exp(inter - m_t)
        s = _dot_nt(q, k.astype(BF16)) * w
        num = _dot(s.astype(BF16), v) + a * _dot(q, C.astype(BF16))
        qf = qk[:, h * P:(h + 1) * P]
        den = jnp.sum(s, axis=-1, keepdims=True) + a * jnp.sum(qf * nvec, axis=-1, keepdims=True)
        hh = num / jnp.maximum(jnp.abs(den), jnp.exp(-m_t))

        u_c = g - b_c + li_c
        u_r = g - b_r + li_r
        m_new = jnp.maximum(g + m, jnp.max(u_r, axis=-1, keepdims=True))
        wk = jnp.exp(u_c - m_new)
        decay = jnp.exp(g + m - m_new)
        kw = k * wk
        c_ref[h] = decay * C + _dot_tn(kw.astype(BF16), v)
        n_ref[h:h + 1, :] = decay * nvec + jnp.sum(kw, axis=0, keepdims=True)
        m_ref[h:h + 1, :] = jnp.broadcast_to(m_new, (1, m_ref.shape[1]))

        ms = jnp.sum(hh * hh, axis=-1, keepdims=True) * (1.0 / MLSTM_V_DIM)
        hc = hh * lax.rsqrt(ms + RMS_EPS) * gain_ref[:, h * VP:(h + 1) * VP]
        outs.append(jax.nn.sigmoid(og_ref[:, h * VP:(h + 1) * VP]) * hc)
    o_ref[...] = jnp.concatenate(outs, axis=-1).astype(o_ref.dtype)


def mlstm(qk, v, og, gates, gates_t, conv_w, conv_b, bias_c, bias_r, gain, batch):
    T = qk.shape[0]
    L = MLSTM_CHUNK
    nc = T // batch // L
    H, P, VP = MLSTM_HEADS, MLSTM_QK_PAD, MLSTM_V_PAD
    tri = jnp.tril(jnp.ones((L, L), BF16))
    cur = lambda c: (0, c, 0)
    c2 = lambda c: (0, 0)
    per_batch = lambda a: a.reshape(batch, T // batch, a.shape[1])
    out = pl.pallas_call(
        _mlstm_kernel,
        grid=(nc,),
        in_specs=[pl.BlockSpec((batch, L, 2 * H * P), cur), pl.BlockSpec((batch, L, H * VP), cur),
                  pl.BlockSpec((batch, L, H * VP), cur), pl.BlockSpec((batch, L, LANES), cur),
                  pl.BlockSpec((batch, SUBLANES, L), lambda c: (0, 0, c)),
                  pl.BlockSpec((MLSTM_CONV, 2 * H * P), c2), pl.BlockSpec((1, 2 * H * P), c2),
                  pl.BlockSpec((1, LANES), c2), pl.BlockSpec((SUBLANES, L), c2),
                  pl.BlockSpec((1, H * VP), c2),
                  pl.BlockSpec((L, L), c2), pl.BlockSpec((L, L), c2)],
        out_specs=pl.BlockSpec((batch, L, H * VP), cur),
        out_shape=jax.ShapeDtypeStruct((batch, T // batch, H * VP), BF16),
        scratch_shapes=[pltpu.VMEM((batch, SUBLANES + L, 2 * H * P), F32),
                        pltpu.VMEM((batch, H, P, VP), F32),
                        pltpu.VMEM((batch, SUBLANES, P), F32),
                        pltpu.VMEM((batch, SUBLANES, LANES), F32)],
        compiler_params=_cparams(1),
        name="mlstm",
    )(per_batch(qk), per_batch(v), per_batch(og), per_batch(gates), gates_t,
      conv_w, conv_b, bias_c, bias_r, gain, tri, tri.T)
    return out.reshape(T, H * VP)


def _out_proj_kernel(h_ref, mix_ref, q_ref, k_ref, v_ref, w1_ref, w2_ref, o_ref):
    outs = []
    for hd in range(MEM_HEADS):
        sl = slice(hd * MEM_HEAD_DIM, (hd + 1) * MEM_HEAD_DIM)
        s = _dot_nt(k_ref[:, sl], q_ref[:, sl]) * (MEM_HEAD_DIM ** -0.5)
        e = jnp.exp(s - jnp.max(s, axis=0, keepdims=True))
        p = (e * (1.0 / jnp.sum(e, axis=0, keepdims=True))).astype(BF16)
        outs.append(_dot_tn(p, v_ref[:, sl]))
    mem_out = jnp.concatenate(outs, axis=-1).astype(BF16)
    o_ref[...] = h_ref[...] + _dot(mix_ref[...], w1_ref[...]) + _dot(mem_out, w2_ref[...])


def out_proj(h, mix, qm, mk, mv, w_mix, w_mem, batch, tm):
    T, D = h.shape
    Wm = mix.shape[1]
    M = mk.shape[0] // batch
    nt = T // batch // tm
    rows = lambda b, i: (b * nt + i, 0)
    mem = lambda b, i: (b, 0)
    const = lambda b, i: (0, 0)
    return pl.pallas_call(
        _out_proj_kernel,
        grid=(batch, nt),
        in_specs=[pl.BlockSpec((tm, D), rows),
                  pl.BlockSpec((tm, Wm), rows),
                  pl.BlockSpec((tm, MEM_WIDTH), rows),
                  pl.BlockSpec((M, MEM_WIDTH), mem),
                  pl.BlockSpec((M, MEM_WIDTH), mem),
                  pl.BlockSpec((Wm, D), const),
                  pl.BlockSpec((MEM_WIDTH, D), const)],
        out_specs=pl.BlockSpec((tm, D), rows),
        out_shape=jax.ShapeDtypeStruct((T, D), F32),
        compiler_params=_cparams(2),
        name="out_proj",
    )(h, mix, qm, mk, mv, w_mix, w_mem)


def _dense_ffn_kernel(h_ref, g_ref, wg_ref, wu_ref, wd_ref, o_ref, xn_ref):
    @pl.when(pl.program_id(1) == 0)
    def _():
        h = h_ref[...]
        xn_ref[...] = (_rms(h) * g_ref[...]).astype(BF16)
        o_ref[...] = h

    xn = xn_ref[...]
    a = _silu(_dot(xn, wg_ref[...])) * _dot(xn, wu_ref[...])
    o_ref[...] += _dot(a.astype(BF16), wd_ref[...])


def dense_ffn(h, gain, w_gate, w_up, w_down, tm, tf):
    T, D = h.shape
    Fd = w_gate.shape[1]
    return pl.pallas_call(
        _dense_ffn_kernel,
        grid=(T // tm, Fd // tf),
        in_specs=[pl.BlockSpec((tm, D), lambda i, j: (i, 0)),
                  pl.BlockSpec((1, D), lambda i, j: (0, 0)),
                  pl.BlockSpec((D, tf), lambda i, j: (0, j)),
                  pl.BlockSpec((D, tf), lambda i, j: (0, j)),
                  pl.BlockSpec((tf, D), lambda i, j: (j, 0))],
        out_specs=pl.BlockSpec((tm, D), lambda i, j: (i, 0)),
        out_shape=jax.ShapeDtypeStruct((T, D), F32),
        scratch_shapes=[pltpu.VMEM((tm, D), BF16)],
        compiler_params=_cparams(2),
        name="dense_ffn",
    )(h, gain.reshape(1, D), w_gate, w_up, w_down)


def _router_kernel(h_ref, g_ref, wr_ref, tri_ref, o_ref, ot_ref, cnt_ref, carry_ref):
    i = pl.program_id(0)

    @pl.when(i == 0)
    def _():
        carry_ref[...] = jnp.zeros_like(carry_ref)

    tm = h_ref.shape[0]
    xn = _rms(h_ref[...]) * g_ref[...]
    x_hi = xn.astype(BF16)
    x_lo = (xn - x_hi.astype(F32)).astype(BF16)
    wr = wr_ref[...]
    w_hi = wr.astype(BF16)
    w_lo = (wr - w_hi.astype(F32)).astype(BF16)
    logits = _dot(x_hi, w_hi) + (_dot(x_hi, w_lo) + _dot(x_lo, w_hi))
    lane = lax.broadcasted_iota(I32, (tm, LANES), 1)
    logits = jnp.where(lane < N_EXPERTS, logits, -jnp.inf)
    t1 = jnp.max(logits, axis=-1, keepdims=True)
    e1 = jnp.min(jnp.where(logits == t1, lane, LANES), axis=-1, keepdims=True)
    rest = jnp.where(lane == e1, -jnp.inf, logits)
    t2 = jnp.max(rest, axis=-1, keepdims=True)
    e2 = jnp.min(jnp.where(rest == t2, lane, LANES), axis=-1, keepdims=True)
    x2 = jnp.exp(t2 - t1)
    w1 = 1.0 / (1.0 + x2)
    w2 = x2 / (1.0 + x2)

    oh1 = lane == e1
    oh2 = lane == e2
    cnt = jnp.where(oh1 | oh2, 1.0, 0.0)
    before = _dot(tri_ref[...], cnt.astype(BF16)) + carry_ref[...]
    r1 = jnp.sum(jnp.where(oh1, before, 0.0), axis=-1, keepdims=True)
    r2 = jnp.sum(jnp.where(oh2, before, 0.0), axis=-1, keepdims=True)
    carry_ref[...] += jnp.sum(cnt, axis=0, keepdims=True)

    cols = (e1.astype(F32), e2.astype(F32), w1, w2, r1, r2)
    out = jnp.zeros((tm, LANES), F32)
    for c, val in enumerate(cols):
        out = jnp.where(lane == c, val, out)
    o_ref[...] = out
    ot_ref[...] = out.T[:SUBLANES, :]
    cnt_ref[...] = jnp.broadcast_to(carry_ref[...], cnt_ref.shape)


def moe_router(h, gain, w_router, tm):
    T, D = h.shape
    wr = jnp.zeros((D, LANES), F32).at[:, :N_EXPERTS].set(w_router)
    tri = jnp.tril(jnp.ones((tm, tm), BF16), k=-1)
    return pl.pallas_call(
        _router_kernel,
        grid=(T // tm,),
        in_specs=[pl.BlockSpec((tm, D), lambda i: (i, 0)),
                  pl.BlockSpec((1, D), lambda i: (0, 0)),
                  pl.BlockSpec((D, LANES), lambda i: (0, 0)),
                  pl.BlockSpec((tm, tm), lambda i: (0, 0))],
        out_specs=[pl.BlockSpec((tm, LANES), lambda i: (i, 0)),
                   pl.BlockSpec((SUBLANES, tm), lambda i: (0, i)),
                   pl.BlockSpec((SUBLANES, LANES), lambda i: (0, 0))],
        out_shape=[jax.ShapeDtypeStruct((T, LANES), F32),
                   jax.ShapeDtypeStruct((SUBLANES, T), F32),
                   jax.ShapeDtypeStruct((SUBLANES, LANES), F32)],
        scratch_shapes=[pltpu.VMEM((1, LANES), F32)],
        compiler_params=_cparams(1),
        name="moe_router",
    )(h, gain.reshape(1, D), wr, tri)


def _dispatch_kernel(ze_ref, zn_ref, p1_ref, p2_ref, h_ref, xs_ref, zero_ref, sem, zsem, *, pad_max, tail_max):
    tb = p1_ref.shape[0]
    zr = zero_ref.shape[0]

    @pl.when(pl.program_id(0) == 0)
    def _():
        zero_ref[...] = jnp.zeros_like(zero_ref)
        chunks = [(e, c) for e in range(N_EXPERTS) for c in range(pad_max // zr)]
        chunks += [(N_EXPERTS, c) for c in range(tail_max // zr)]

        def zcopy(e, c):
            start = pl.multiple_of(ze_ref[e] - (c + 1) * zr, zr)
            return pltpu.make_async_copy(zero_ref, xs_ref.at[pl.ds(start, zr), :], zsem)

        def needed(e, c):
            return c * zr < zn_ref[e]

        for e, c in chunks:
            @pl.when(needed(e, c))
            def _():
                zcopy(e, c).start()

        for e, c in chunks:
            @pl.when(needed(e, c))
            def _():
                zcopy(e, c).wait()

    def copy(t, pos):
        return pltpu.make_async_copy(h_ref.at[pl.ds(t, 1), :], xs_ref.at[pl.ds(pos, 1), :], sem)

    for t in range(tb):
        copy(t, p1_ref[t]).start(priority=0)
        copy(t, p2_ref[t]).start(priority=1)
    for _ in range(TOP_K):
        pltpu.make_async_copy(h_ref, xs_ref.at[pl.ds(0, tb), :], sem).wait()


def moe_dispatch(h, pos1, pos2, pad_end, pad_len, used_rows, n_rows, pad_max, tb, zr=256):
    T, D = h.shape
    tail_max = n_rows - TOP_K * T
    assert tail_max % zr == 0 and pad_max % zr == 0
    zero_end = jnp.concatenate([pad_end, jnp.full((1,), n_rows, I32)])
    zero_len = jnp.concatenate([pad_len, n_rows - used_rows[None]])
    grid_spec = pltpu.PrefetchScalarGridSpec(
        num_scalar_prefetch=2,
        grid=(T // tb,),
        in_specs=[pl.BlockSpec((tb,), lambda i, ps, pn: (i,), memory_space=pltpu.SMEM),
                  pl.BlockSpec((tb,), lambda i, ps, pn: (i,), memory_space=pltpu.SMEM),
                  pl.BlockSpec((tb, D), lambda i, ps, pn: (i, 0))],
        out_specs=pl.BlockSpec(memory_space=pl.ANY),
        scratch_shapes=[pltpu.VMEM((zr, D), h.dtype), pltpu.SemaphoreType.DMA(()), pltpu.SemaphoreType.DMA(())],
    )
    return pl.pallas_call(
        functools.partial(_dispatch_kernel, pad_max=pad_max, tail_max=tail_max),
        grid_spec=grid_spec,
        out_shape=jax.ShapeDtypeStruct((n_rows, D), h.dtype),
        compiler_params=_cparams(1),
        name="moe_dispatch",
    )(zero_end, zero_len, pos1, pos2, h)


def _expert_ffn_kernel(te_ref, tv_ref, x_ref, g_ref, wg_ref, wu_ref, wd_ref, wdl_ref, o_ref, xn_ref, a_ref,
                       *, ts, nf):
    i = pl.program_id(0)
    j = pl.program_id(1)
    valid = tv_ref[i]
    n_sub = x_ref.shape[0] // ts

    n_occ = (valid + ts - 1) // ts
    for k in range(1, n_sub + 1):
        rows = pl.ds(0, k * ts)

        def gate_up(slot):
            xn = xn_ref[rows, :]
            a = _silu(_dot(xn, wg_ref[...].astype(BF16))) * _dot(xn, wu_ref[...].astype(BF16))
            a_ref[slot, rows, :] = a.astype(BF16)

        def down(slot, first, w_ref=wd_ref):
            y = _dot(a_ref[slot, rows, :], w_ref[...].astype(BF16))
            if first:
                o_ref[rows, :] = y
                if k < n_sub:
                    o_ref[pl.ds(k * ts, (n_sub - k) * ts), :] = jnp.zeros(((n_sub - k) * ts, o_ref.shape[1]), F32)
            else:
                o_ref[rows, :] += y

        @pl.when(n_occ == k)
        def _():
            @pl.when(j == 0)
            def _():
                xn_ref[rows, :] = (_rms(x_ref[rows, :]) * g_ref[...]).astype(BF16)
                gate_up(0)

            @pl.when(j == 1)
            def _():
                gate_up(1)
                down(0, first=True)

            @pl.when((j > 1) & (j < nf - 1))
            def _():
                gate_up(j % 2)
                down((j - 1) % 2, first=False)

            @pl.when(j == nf - 1)
            def _():
                gate_up((nf - 1) % 2)
                down((nf - 2) % 2, first=False)
                down((nf - 1) % 2, first=False, w_ref=wdl_ref)

    @pl.when((n_occ == 0) & (j == 0))
    def _():
        o_ref[...] = jnp.zeros_like(o_ref)


def expert_ffn(xs, gain, w_gate, w_up, w_down, layer, tile_expert, tile_valid, n_tiles, tm, ts, tf):
    D = xs.shape[1]
    R = n_tiles * tm
    Fd = w_gate.shape[3]
    nf = Fd // tf

    assert nf >= 3
    def up_tile(i, j, tv):
        return jnp.where(tv[i] > 0, j, nf - 1)

    def down_tile(i, j, tv):
        return jnp.where(tv[i] > 0, jnp.maximum(j - 1, 0), nf - 2)

    grid_spec = pltpu.PrefetchScalarGridSpec(
        num_scalar_prefetch=2,
        grid=(n_tiles, nf),
        in_specs=[pl.BlockSpec((tm, D), lambda i, j, te, tv: (jnp.where(tv[i] > 0, i, 0), 0)),
                  pl.BlockSpec((1, D), lambda i, j, te, tv: (0, 0)),
                  pl.BlockSpec((None, None, D, tf), lambda i, j, te, tv: (layer, te[i], 0, up_tile(i, j, tv))),
                  pl.BlockSpec((None, None, D, tf), lambda i, j, te, tv: (layer, te[i], 0, up_tile(i, j, tv))),
                  pl.BlockSpec((None, None, tf, D), lambda i, j, te, tv: (layer, te[i], down_tile(i, j, tv), 0)),
                  pl.BlockSpec((None, None, tf, D), lambda i, j, te, tv: (layer, te[i], nf - 1, 0))],
        out_specs=pl.BlockSpec((tm, D), lambda i, j, te, tv: (i, 0)),
        scratch_shapes=[pltpu.VMEM((tm, D), BF16), pltpu.VMEM((2, tm, tf), BF16)],
    )
    return pl.pallas_call(
        functools.partial(_expert_ffn_kernel, ts=ts, nf=nf),
        grid_spec=grid_spec,
        out_shape=jax.ShapeDtypeStruct((R, D), F32),
        compiler_params=_cparams(2),
        name="expert_ffn",
    )(tile_expert, tile_valid, xs, gain.reshape(1, D), w_gate, w_up, w_down, w_down)


def _combine_kernel(p1_ref, p2_ref, route_ref, h_ref, y_ref, o_ref, b1_ref, b2_ref, sem):
    tb = p1_ref.shape[0]

    def copies(t):
        dst = pl.ds(t, 1)
        return (pltpu.make_async_copy(y_ref.at[pl.ds(p1_ref[t], 1), :], b1_ref.at[dst, :], sem),
                pltpu.make_async_copy(y_ref.at[pl.ds(p2_ref[t], 1), :], b2_ref.at[dst, :], sem))

    for t in range(tb):
        for prio, cp in enumerate(copies(t)):
            cp.start(priority=prio)
    for b_ref in (b1_ref, b2_ref):
        pltpu.make_async_copy(y_ref.at[pl.ds(0, tb), :], b_ref, sem).wait()
    w1 = route_ref[:, 2:3]
    w2 = route_ref[:, 3:4]
    o_ref[...] = h_ref[...] + w1 * b1_ref[...] + w2 * b2_ref[...]


def moe_combine(h, y, route, pos1, pos2, tb):
    T, D = h.shape
    smem = lambda: pl.BlockSpec((tb,), lambda i: (i,), memory_space=pltpu.SMEM)
    return pl.pallas_call(
        _combine_kernel,
        grid=(T // tb,),
        in_specs=[smem(), smem(),
                  pl.BlockSpec((tb, LANES), lambda i: (i, 0)),
                  pl.BlockSpec((tb, D), lambda i: (i, 0)),
                  pl.BlockSpec(memory_space=pl.ANY)],
        out_specs=pl.BlockSpec((tb, D), lambda i: (i, 0)),
        out_shape=jax.ShapeDtypeStruct((T, D), F32),
        scratch_shapes=[pltpu.VMEM((tb, D), F32), pltpu.VMEM((tb, D), F32),
                        pltpu.SemaphoreType.DMA(())],
        compiler_params=_cparams(1),
        name="moe_combine",
    )(pos1, pos2, route, h, y)


def moe_ffn(h, gain, w_router, w_gate, w_up, w_down, layer, tm=1024, ts=256, tf=512, tb=1024):
    T = h.shape[0]
    route, route_t, counts = moe_router(h, gain, w_router, tm=512)
    e1 = route_t[0].astype(I32)
    e2 = route_t[1].astype(I32)
    counts = counts[0, :N_EXPERTS].astype(I32)
    tiles_per = (counts + tm - 1) // tm
    tile_end = jnp.cumsum(tiles_per)
    tile_start = tile_end - tiles_per
    row_start = tile_start * tm
    pos1 = row_start[e1] + route_t[4].astype(I32)
    pos2 = row_start[e2] + route_t[5].astype(I32)
    n_tiles = (TOP_K * T) // tm + N_EXPERTS
    tile_ids = jnp.arange(n_tiles, dtype=I32)
    tile_expert = jnp.sum(jnp.minimum(tile_ids, tile_end[-1] - 1)[:, None] >= tile_end[None, :], axis=1).astype(I32)
    tile_valid = jnp.clip(counts[tile_expert] - (tile_ids - tile_start[tile_expert]) * tm, 0, tm)
    tile_valid = jnp.where(tile_ids < tile_end[-1], tile_valid, 0).astype(I32)
    pad_len = tiles_per * tm - counts

    xs = moe_dispatch(h, pos1, pos2, tile_end * tm, pad_len, tile_end[-1] * tm, n_tiles * tm, tm, tb)
    y = expert_ffn(xs, gain, w_gate, w_up, w_down, layer, tile_expert, tile_valid, n_tiles, tm, ts, tf)
    return moe_combine(h, y, route, pos1, pos2, tb)


def _pad_heads(a, n_heads, width, pad_to, axis):
    shape = list(a.shape)
    a = a.reshape(shape[:axis] + [n_heads, width] + shape[axis + 1:])
    pads = [(0, 0)] * a.ndim
    pads[axis + 1] = (0, pad_to - width)
    a = jnp.pad(a, pads)
    return a.reshape(shape[:axis] + [n_heads * pad_to] + shape[axis + 1:])


def kernel(x, mem, ln_mix, ln_mem, w_mem_kv, mem_q_gain, mem_k_gain, w_out, ln_ffn, swa_w_in, swa_q_gain, swa_k_gain, swa_sink, ret_w_in, mlstm_w_in, mlstm_conv_w, mlstm_conv_b, mlstm_i_bias, mlstm_f_bias, mlstm_out_gain, ffn_w_gate, ffn_w_up, ffn_w_down, moe_router, moe_w_gate, moe_w_up, moe_w_down):
    B, S, D = x.shape
    M = mem.shape[1]
    T = B * S
    depth = ln_mix.shape[0]
    h = x.reshape(T, D)
    mem2 = mem.reshape(B * M, D)

    for layer in range(depth):
        kind = layer % N_MIXERS
        idx = layer // N_MIXERS
        mk, mv = rms_proj(mem2, ln_mem[layer], w_mem_kv[layer].astype(BF16),
                          ((MEM_WIDTH, BF16, mem_k_gain[layer]), (MEM_WIDTH, BF16, None)),
                          tm=B * M, name="mem_kv_proj")
        w_o = w_out[layer].astype(BF16)
        w_o_mix, w_o_mem = w_o[:MIX_WIDTH], w_o[MIX_WIDTH:]
        qm_out = (MEM_WIDTH, BF16, mem_q_gain[layer])

        if kind == 0:
            kvw = SWA_KV_HEADS * HEAD_DIM
            q, k, v, qm = rms_proj(h, ln_mix[layer], swa_w_in[idx].astype(BF16),
                                   ((MIX_WIDTH, BF16, swa_q_gain[idx]), (kvw, BF16, swa_k_gain[idx]),
                                    (kvw, BF16, None), qm_out), tm=1024, name="swa_in_proj")
            mix = swa_attention(q, k, v, swa_sink[idx], B)
        elif kind == 1:
            qkw = RET_HEADS * RET_QK_DIM
            q, k, v, g, qm = rms_proj(h, ln_mix[layer], ret_w_in[idx].astype(BF16),
                                      ((qkw, BF16, None), (qkw, F32, None), (MIX_WIDTH, BF16, None),
                                       (MIX_WIDTH, F32, None), qm_out), tm=1024, name="ret_in_proj")
            mix = retention(q, k, v, g, B)
        else:
            H, P, VP = MLSTM_HEADS, MLSTM_QK_PAD, MLSTM_V_PAD
            w = mlstm_w_in[idx]
            qkw = 2 * H * MLSTM_QK_DIM
            o_v, o_og, o_ig = qkw, qkw + MIX_WIDTH, qkw + 2 * MIX_WIDTH
            o_fg, o_qm = o_ig + H, o_ig + 2 * H
            w_gates = jnp.zeros((D, LANES), F32).at[:, :2 * H].set(w[:, o_ig:o_qm])
            w_pad = jnp.concatenate([
                _pad_heads(w[:, :qkw], 2 * H, MLSTM_QK_DIM, P, 1),
                _pad_heads(w[:, o_v:o_og], H, MLSTM_V_DIM, VP, 1),
                _pad_heads(w[:, o_og:o_ig], H, MLSTM_V_DIM, VP, 1),
                w_gates, w[:, o_qm:]], axis=1).astype(BF16)
            qk, v, og, gates, qm = rms_proj(h, ln_mix[layer], w_pad,
                                            ((2 * H * P, F32, None), (H * VP, BF16, None), (H * VP, F32, None),
                                             (LANES, F32, None), qm_out), tm=1024, name="mlstm_in_proj")
            gates_t = gates[:, :SUBLANES].reshape(B, S, SUBLANES).transpose(0, 2, 1)
            bias = jnp.concatenate([mlstm_i_bias[idx], mlstm_f_bias[idx]])
            bias_c = jnp.zeros((1, LANES), F32).at[0, :2 * H].set(bias)
            bias_r = jnp.broadcast_to(bias[:, None], (SUBLANES, MLSTM_CHUNK))
            mix = mlstm(qk, v, og, gates, gates_t,
                        _pad_heads(mlstm_conv_w[idx], 2 * H, MLSTM_QK_DIM, P, 1),
                        _pad_heads(mlstm_conv_b[idx][None], 2 * H, MLSTM_QK_DIM, P, 1),
                        bias_c, bias_r,
                        _pad_heads(mlstm_out_gain[idx][None], H, MLSTM_V_DIM, VP, 1), B)
            w_o_mix = _pad_heads(w_o_mix, H, MLSTM_V_DIM, VP, 0)

        h = out_proj(h, mix, qm, mk, mv, w_o_mix, w_o_mem, B, tm=1024)
        j = layer // 2
        if layer % 2 == 0:
            h = dense_ffn(h, ln_ffn[layer], ffn_w_gate[j].astype(BF16), ffn_w_up[j].astype(BF16),
                          ffn_w_down[j].astype(BF16), tm=1024, tf=1792)
        else:
            h = moe_ffn(h, ln_ffn[layer], moe_router[j], moe_w_gate, moe_w_up, moe_w_down, j)
    return h.reshape(B, S, D)
```

```python
import functools
import math

import jax
import jax.numpy as jnp
from jax import lax
from jax.experimental import pallas as pl
from jax.experimental.pallas import tpu as pltpu

F32 = jnp.float32
BF16 = jnp.bfloat16
I32 = jnp.int32

D_MODEL = 1024
N_MIXERS = 3
HEAD_DIM = 64
MEM_HEADS = 4
MEM_HEAD_DIM = 64
MEM_WIDTH = MEM_HEADS * MEM_HEAD_DIM
MIX_WIDTH = D_MODEL - MEM_WIDTH
RMS_EPS = 1e-6
NEG_INF = -1e30

SWA_Q_HEADS = MIX_WIDTH // HEAD_DIM
SWA_KV_HEADS = 4
SWA_GROUP = SWA_Q_HEADS // SWA_KV_HEADS
SWA_WINDOW = 128
SWA_BLOCK = 128

RET_HEADS = 6
RET_QK_DIM = 64
RET_V_DIM = MIX_WIDTH // RET_HEADS
RET_CHUNK = 128

MLSTM_HEADS = 4
MLSTM_V_DIM = MIX_WIDTH // MLSTM_HEADS
MLSTM_QK_DIM = MLSTM_V_DIM // 2
MLSTM_CHUNK = 128
MLSTM_CONV = 4
MLSTM_QK_PAD = 128
MLSTM_V_PAD = 256

N_EXPERTS = 8
TOP_K = 2

LANES = 128
SUBLANES = 8
VMEM_LIMIT = 56 * 1024 * 1024


def _cparams(n_axes):
    return pltpu.CompilerParams(dimension_semantics=("arbitrary",) * n_axes,
                                vmem_limit_bytes=VMEM_LIMIT)


def _rms(x, eps=RMS_EPS):
    return x * lax.rsqrt(jnp.mean(x * x, axis=-1, keepdims=True) + eps)


def _dot(a, b):
    return jnp.dot(a, b, preferred_element_type=F32)


def _dot_nt(a, b):
    return lax.dot_general(a, b, (((1,), (1,)), ((), ())), preferred_element_type=F32)


def _dot_tn(a, b):
    return lax.dot_general(a, b, (((0,), (0,)), ((), ())), preferred_element_type=F32)


def _silu(x):
    return x * jax.nn.sigmoid(x)


NORM_CHUNK = 256


def _head_rms(y, seg_ref, hg):
    sq = y * y
    hi = sq.astype(BF16)
    lo = (sq - hi.astype(F32)).astype(BF16)
    ms = _dot(hi, seg_ref[...]) + _dot(lo, seg_ref[...])
    return y * lax.rsqrt(ms + RMS_EPS) * hg


def _rms_proj_kernel(x_ref, g_ref, w_ref, seg_ref, *refs, widths, normed):
    n_gain = sum(normed)
    gain_refs, o_refs = refs[:n_gain], refs[n_gain:]
    xn = (_rms(x_ref[...]) * g_ref[...]).astype(BF16)
    off = 0
    gi = 0
    for o_ref, wd, nrm in zip(o_refs, widths, normed):
        if nrm:
            hg = gain_refs[gi][...]
            gi += 1
            for c in range(0, wd, NORM_CHUNK):
                y = _dot(xn, w_ref[:, off + c:off + c + NORM_CHUNK])
                o_ref[:, c:c + NORM_CHUNK] = _head_rms(y, seg_ref, hg).astype(o_ref.dtype)
        else:
            o_ref[...] = _dot(xn, w_ref[:, off:off + wd]).astype(o_ref.dtype)
        off += wd


def rms_proj(x, gain, w, outs, tm, name):
    T, D = x.shape
    N = w.shape[1]
    widths = tuple(o[0] for o in outs)
    normed = tuple(o[2] is not None for o in outs)
    head_gains = [jnp.tile(o[2], NORM_CHUNK // HEAD_DIM).reshape(1, NORM_CHUNK) for o in outs if o[2] is not None]
    assert sum(widths) == N and T % tm == 0
    assert all(wd % NORM_CHUNK == 0 for wd, nrm in zip(widths, normed) if nrm)
    head_of = jnp.arange(NORM_CHUNK) // HEAD_DIM
    seg = jnp.where(head_of[:, None] == head_of[None, :], 1.0 / HEAD_DIM, 0.0).astype(BF16)
    return pl.pallas_call(
        functools.partial(_rms_proj_kernel, widths=widths, normed=normed),
        grid=(T // tm,),
        in_specs=[pl.BlockSpec((tm, D), lambda i: (i, 0)),
                  pl.BlockSpec((1, D), lambda i: (0, 0)),
                  pl.BlockSpec((D, N), lambda i: (0, 0)),
                  pl.BlockSpec((NORM_CHUNK, NORM_CHUNK), lambda i: (0, 0))]
                 + [pl.BlockSpec((1, NORM_CHUNK), lambda i: (0, 0))] * len(head_gains),
        out_specs=[pl.BlockSpec((tm, wd), lambda i: (i, 0)) for wd in widths],
        out_shape=[jax.ShapeDtypeStruct((T, o[0]), o[1]) for o in outs],
        compiler_params=_cparams(1),
        name=name,
    )(x, gain.reshape(1, D), w, seg, *head_gains)


def _swa_kernel(sink_ref, q_ref, kc_ref, kp_ref, vc_ref, vp_ref, bias_ref, o_ref):
    n = pl.program_id(1)
    L = SWA_BLOCK
    n_blk = q_ref.shape[0] // L
    key = lax.broadcasted_iota(I32, (L, L), 0)
    qry = lax.broadcasted_iota(I32, (L, L), 1)
    from_prev = key > qry
    own = key <= qry
    zero = jnp.zeros((L, L), BF16)
    for blk in range(n_blk):
        rows = slice(blk * L, (blk + 1) * L)
        prows = slice((blk - 1) * L, blk * L)
        outs = []
        for g in range(SWA_KV_HEADS):
            ksl = slice(g * HEAD_DIM, (g + 1) * HEAD_DIM)
            k_prev = kp_ref[:, ksl] if blk == 0 else kc_ref[prows, ksl]
            v_prev = vp_ref[:, ksl] if blk == 0 else vc_ref[prows, ksl]
            k = jnp.concatenate([k_prev, kc_ref[rows, ksl]], axis=0)
            v = jnp.concatenate([v_prev, vc_ref[rows, ksl]], axis=0)
            for h in range(g * SWA_GROUP, (g + 1) * SWA_GROUP):
                sink = sink_ref[h]
                s2 = _dot_nt(k, q_ref[rows, h * HEAD_DIM:(h + 1) * HEAD_DIM])
                s = jnp.where(from_prev, s2[:L], s2[L:]) * (HEAD_DIM ** -0.5) - bias_ref[h]
                if blk == 0:
                    s = jnp.where(jnp.logical_or(n > 0, own), s, NEG_INF)
                m = jnp.maximum(jnp.max(s, axis=0, keepdims=True), sink)
                e = jnp.exp(s - m)
                denom = jnp.sum(e, axis=0, keepdims=True) + jnp.exp(sink - m)
                p = (e * (1.0 / denom)).astype(BF16)
                p2 = jnp.concatenate([jnp.where(from_prev, p, zero), jnp.where(from_prev, zero, p)], axis=0)
                outs.append(_dot_tn(p2, v))
        o_ref[rows, :] = jnp.concatenate(outs, axis=-1).astype(o_ref.dtype)


def swa_attention(q, k, v, sink, batch, n_blk=4):
    T = q.shape[0]
    L = SWA_BLOCK
    nb = T // batch // L
    kvw = SWA_KV_HEADS * HEAD_DIM
    slopes = jnp.exp2(-8.0 * jnp.arange(1, SWA_Q_HEADS + 1, dtype=F32) / SWA_Q_HEADS)
    assert SWA_WINDOW == L
    qi, kj = jnp.arange(L)[:, None], jnp.arange(L)[None, :]
    dist = jnp.where(kj > qi, qi + L - kj, qi - kj).astype(F32)
    bias = slopes[:, None, None] * dist.T
    ns = nb // n_blk
    slab = n_blk * L
    cur = lambda b, n: (b * ns + n, 0)
    prev = lambda b, n: (b * nb + jnp.maximum(n * n_blk - 1, 0), 0)
    return pl.pallas_call(
        _swa_kernel,
        grid=(batch, ns),
        in_specs=[pl.BlockSpec(memory_space=pltpu.SMEM),
                  pl.BlockSpec((slab, MIX_WIDTH), cur),
                  pl.BlockSpec((slab, kvw), cur), pl.BlockSpec((L, kvw), prev),
                  pl.BlockSpec((slab, kvw), cur), pl.BlockSpec((L, kvw), prev),
                  pl.BlockSpec((SWA_Q_HEADS, L, L), lambda b, n: (0, 0, 0))],
        out_specs=pl.BlockSpec((slab, MIX_WIDTH), cur),
        out_shape=jax.ShapeDtypeStruct((T, MIX_WIDTH), BF16),
        compiler_params=_cparams(2),
        name="swa_attention",
    )(sink, q, k, k, v, v, bias)


def _retention_kernel(q_ref, k_ref, v_ref, g_ref, dec_ref, qd_ref, kd_ref, cd_ref, o_ref, state_ref):
    @pl.when(pl.program_id(0) == 0)
    def _():
        state_ref[...] = jnp.zeros_like(state_ref)

    L = RET_CHUNK
    n_ch = q_ref.shape[1] // L
    for b in range(q_ref.shape[0]):
        outs = [[] for _ in range(n_ch)]
        for h in range(RET_HEADS):
            qsl = slice(h * RET_QK_DIM, (h + 1) * RET_QK_DIM)
            vsl = slice(h * RET_V_DIM, (h + 1) * RET_V_DIM)
            state = state_ref[b, h]
            for c in range(n_ch):
                rows = slice(c * L, (c + 1) * L)
                q = q_ref[b, rows, qsl]
                k = k_ref[b, rows, qsl] * (RET_QK_DIM ** -0.5)
                v = v_ref[b, rows, vsl]
                s = _dot_nt(q, k.astype(BF16)) * dec_ref[h]
                o = _dot(s.astype(BF16), v) + _dot(q, state.astype(BF16)) * qd_ref[h]
                state = state * cd_ref[h] + _dot_tn((k * kd_ref[h]).astype(BF16), v)
                outs[c].append(_silu(g_ref[b, rows, vsl]) * _rms(o))
            state_ref[b, h] = state
        for c in range(n_ch):
            o_ref[b, c * L:(c + 1) * L, :] = jnp.concatenate(outs[c], axis=-1).astype(o_ref.dtype)


def _retention_consts():
    H, L = RET_HEADS, RET_CHUNK
    log_gamma = jnp.log1p(-jnp.exp2(-5.0 - jnp.arange(H, dtype=F32)))
    pos = jnp.arange(L, dtype=F32)
    rel = pos[:, None] - pos[None, :]
    intra = jnp.exp(jnp.where(rel >= 0, log_gamma[:, None, None] * rel, -jnp.inf))
    q_decay = jnp.exp(log_gamma[:, None] * (pos + 1.0))[..., None]
    k_decay = jnp.exp(log_gamma[:, None] * (L - 1.0 - pos))[..., None]
    chunk_decay = jnp.exp(log_gamma * L)[:, None, None]
    return (intra,
            jnp.broadcast_to(q_decay, (H, L, RET_V_DIM)),
            jnp.broadcast_to(k_decay, (H, L, RET_QK_DIM)),
            jnp.broadcast_to(chunk_decay, (H, RET_QK_DIM, RET_V_DIM)))


def retention(q, k, v, g, batch, n_ch=4):
    T = q.shape[0]
    L = RET_CHUNK
    nc = T // batch // L
    H = RET_HEADS
    intra, qd, kd, cd = _retention_consts()
    cur = lambda c: (0, c, 0)
    const = lambda c: (0, 0, 0)
    qkw = H * RET_QK_DIM
    per_batch = lambda a: a.reshape(batch, T // batch, a.shape[1])
    slab = n_ch * L
    out = pl.pallas_call(
        _retention_kernel,
        grid=(nc // n_ch,),
        in_specs=[pl.BlockSpec((batch, slab, qkw), cur), pl.BlockSpec((batch, slab, qkw), cur),
                  pl.BlockSpec((batch, slab, MIX_WIDTH), cur), pl.BlockSpec((batch, slab, MIX_WIDTH), cur),
                  pl.BlockSpec((H, L, L), const), pl.BlockSpec((H, L, RET_V_DIM), const),
                  pl.BlockSpec((H, L, RET_QK_DIM), const), pl.BlockSpec((H, RET_QK_DIM, RET_V_DIM), const)],
        out_specs=pl.BlockSpec((batch, slab, MIX_WIDTH), cur),
        out_shape=jax.ShapeDtypeStruct((batch, T // batch, MIX_WIDTH), BF16),
        scratch_shapes=[pltpu.VMEM((batch, H, RET_QK_DIM, RET_V_DIM), F32)],
        compiler_params=_cparams(1),
        name="retention",
    )(per_batch(q), per_batch(k), per_batch(v), per_batch(g), intra, qd, kd, cd)
    return out.reshape(T, MIX_WIDTH)


def _split3(x):
    x1 = x.astype(BF16)
    r1 = x - x1.astype(F32)
    x2 = r1.astype(BF16)
    x3 = (r1 - x2.astype(F32)).astype(BF16)
    return x1, x2, x3


def _mlstm_kernel(qk_ref, v_ref, og_ref, gc_ref, gr_ref, cw_ref, cb_ref, bc_ref, br_ref, gain_ref,
                  tri_ref, trit_ref, o_ref, xbuf_ref, c_ref, n_ref, m_ref):
    for b in range(qk_ref.shape[0]):
        _mlstm_chunk(qk_ref.at[b], v_ref.at[b], og_ref.at[b], gc_ref.at[b], gr_ref.at[b], cw_ref, cb_ref,
                     bc_ref, br_ref, gain_ref, tri_ref, trit_ref, o_ref.at[b],
                     xbuf_ref.at[b], c_ref.at[b], n_ref.at[b], m_ref.at[b])


def _mlstm_chunk(qk_ref, v_ref, og_ref, gc_ref, gr_ref, cw_ref, cb_ref, bc_ref, br_ref, gain_ref,
                 tri_ref, trit_ref, o_ref, xbuf_ref, c_ref, n_ref, m_ref):
    L = MLSTM_CHUNK
    H = MLSTM_HEADS
    P = MLSTM_QK_PAD
    VP = MLSTM_V_PAD
    KT = SUBLANES

    @pl.when(pl.program_id(0) == 0)
    def _():
        xbuf_ref[0:KT, :] = jnp.zeros((KT, xbuf_ref.shape[1]), F32)
        c_ref[...] = jnp.zeros_like(c_ref)
        n_ref[...] = jnp.zeros_like(n_ref)
        m_ref[...] = jnp.zeros_like(m_ref)

    xbuf_ref[KT:KT + L, :] = qk_ref[...]
    acc = cb_ref[...] + cw_ref[MLSTM_CONV - 1:MLSTM_CONV, :] * xbuf_ref[KT:KT + L, :]
    for j in range(MLSTM_CONV - 1):
        sh = MLSTM_CONV - 1 - j
        acc = acc + cw_ref[j:j + 1, :] * xbuf_ref[KT - sh:KT - sh + L, :]
    xbuf_ref[0:KT, :] = qk_ref[L - KT:L, :]
    qk = _silu(acc)

    xc = gc_ref[...] + bc_ref[...]
    xr = gr_ref[...] + br_ref[...]
    lfc = jax.nn.log_sigmoid(xc)
    lfr = jax.nn.log_sigmoid(xr)
    tri = tri_ref[...]
    trit = trit_ref[...]
    bc = sum(_dot(tri, t) for t in _split3(lfc))
    br = sum(_dot(t, trit) for t in _split3(lfr))

    rowi = lax.broadcasted_iota(I32, (L, L), 0)
    coli = lax.broadcasted_iota(I32, (L, L), 1)
    causal = rowi >= coli
    outs = []
    for h in range(H):
        q = qk[:, h * P:(h + 1) * P].astype(BF16)
        k = qk[:, (H + h) * P:(H + h + 1) * P] * (MLSTM_QK_DIM ** -0.5)
        v = v_ref[:, h * VP:(h + 1) * VP]
        li_c = xc[:, h:h + 1]
        b_c = bc[:, H + h:H + h + 1]
        li_r = xr[h:h + 1, :]
        b_r = br[H + h:H + h + 1, :]
        g = b_c[L - 1:L, :]
        m = m_ref[h:h + 1, 0:1]
        C = c_ref[h]
        nvec = n_ref[h:h + 1, :]

        dmat = jnp.where(causal, b_c - b_r + li_r, -jnp.inf)
        inter = b_c + m
        m_t = jnp.maximum(inter, jnp.max(dmat, axis=-1, keepdims=True))
        w = jnp.exp(dmat - m_t)
        a = jnp.exp(inter - m_t)
        s = _dot_nt(q, k.astype(BF16)) * w
        num = _dot(s.astype(BF16), v) + a * _dot(q, C.astype(BF16))
        qf = qk[:, h * P:(h + 1) * P]
        den = jnp.sum(s, axis=-1, keepdims=True) + a * jnp.sum(qf * nvec, axis=-1, keepdims=True)
        hh = num / jnp.maximum(jnp.abs(den), jnp.exp(-m_t))

        u_c = g - b_c + li_c
        u_r = g - b_r + li_r
        m_new = jnp.maximum(g + m, jnp.max(u_r, axis=-1, keepdims=True))
        wk = jnp.exp(u_c - m_new)
        decay = jnp.exp(g + m - m_new)
        kw = k * wk
        c_ref[h] = decay * C + _dot_tn(kw.astype(BF16), v)
        n_ref[h:h + 1, :] = decay * nvec + jnp.sum(kw, axis=0, keepdims=True)
        m_ref[h:h + 1, :] = jnp.broadcast_to(m_new, (1, m_ref.shape[1]))

        ms = jnp.sum(hh * hh, axis=-1, keepdims=True) * (1.0 / MLSTM_V_DIM)
        hc = hh * lax.rsqrt(ms + RMS_EPS) * gain_ref[:, h * VP:(h + 1) * VP]
        outs.append(jax.nn.sigmoid(og_ref[:, h * VP:(h + 1) * VP]) * hc)
    o_ref[...] = jnp.concatenate(outs, axis=-1).astype(o_ref.dtype)


def mlstm(qk, v, og, gates, gates_t, conv_w, conv_b, bias_c, bias_r, gain, batch):
    T = qk.shape[0]
    L = MLSTM_CHUNK
    nc = T // batch // L
    H, P, VP = MLSTM_HEADS, MLSTM_QK_PAD, MLSTM_V_PAD
    tri = jnp.tril(jnp.ones((L, L), BF16))
    cur = lambda c: (0, c, 0)
    c2 = lambda c: (0, 0)
    per_batch = lambda a: a.reshape(batch, T // batch, a.shape[1])
    out = pl.pallas_call(
        _mlstm_kernel,
        grid=(nc,),
        in_specs=[pl.BlockSpec((batch, L, 2 * H * P), cur), pl.BlockSpec((batch, L, H * VP), cur),
                  pl.BlockSpec((batch, L, H * VP), cur), pl.BlockSpec((batch, L, LANES), cur),
                  pl.BlockSpec((batch, SUBLANES, L), lambda c: (0, 0, c)),
                  pl.BlockSpec((MLSTM_CONV, 2 * H * P), c2), pl.BlockSpec((1, 2 * H * P), c2),
                  pl.BlockSpec((1, LANES), c2), pl.BlockSpec((SUBLANES, L), c2),
                  pl.BlockSpec((1, H * VP), c2),
                  pl.BlockSpec((L, L), c2), pl.BlockSpec((L, L), c2)],
        out_specs=pl.BlockSpec((batch, L, H * VP), cur),
        out_shape=jax.ShapeDtypeStruct((batch, T // batch, H * VP), BF16),
        scratch_shapes=[pltpu.VMEM((batch, SUBLANES + L, 2 * H * P), F32),
                        pltpu.VMEM((batch, H, P, VP), F32),
                        pltpu.VMEM((batch, SUBLANES, P), F32),
                        pltpu.VMEM((batch, SUBLANES, LANES), F32)],
        compiler_params=_cparams(1),
        name="mlstm",
    )(per_batch(qk), per_batch(v), per_batch(og), per_batch(gates), gates_t,
      conv_w, conv_b, bias_c, bias_r, gain, tri, tri.T)
    return out.reshape(T, H * VP)


def _out_proj_kernel(h_ref, mix_ref, q_ref, k_ref, v_ref, w1_ref, w2_ref, o_ref):
    outs = []
    for hd in range(MEM_HEADS):
        sl = slice(hd * MEM_HEAD_DIM, (hd + 1) * MEM_HEAD_DIM)
        s = _dot_nt(k_ref[:, sl], q_ref[:, sl]) * (MEM_HEAD_DIM ** -0.5)
        e = jnp.exp(s - jnp.max(s, axis=0, keepdims=True))
        p = (e * (1.0 / jnp.sum(e, axis=0, keepdims=True))).astype(BF16)
        outs.append(_dot_tn(p, v_ref[:, sl]))
    mem_out = jnp.concatenate(outs, axis=-1).astype(BF16)
    o_ref[...] = h_ref[...] + _dot(mix_ref[...], w1_ref[...]) + _dot(mem_out, w2_ref[...])


def out_proj(h, mix, qm, mk, mv, w_mix, w_mem, batch, tm):
    T, D = h.shape
    Wm = mix.shape[1]
    M = mk.shape[0] // batch
    nt = T // batch // tm
    rows = lambda b, i: (b * nt + i, 0)
    mem = lambda b, i: (b, 0)
    const = lambda b, i: (0, 0)
    return pl.pallas_call(
        _out_proj_kernel,
        grid=(batch, nt),
        in_specs=[pl.BlockSpec((tm, D), rows),
                  pl.BlockSpec((tm, Wm), rows),
                  pl.BlockSpec((tm, MEM_WIDTH), rows),
                  pl.BlockSpec((M, MEM_WIDTH), mem),
                  pl.BlockSpec((M, MEM_WIDTH), mem),
                  pl.BlockSpec((Wm, D), const),
                  pl.BlockSpec((MEM_WIDTH, D), const)],
        out_specs=pl.BlockSpec((tm, D), rows),
        out_shape=jax.ShapeDtypeStruct((T, D), F32),
        compiler_params=_cparams(2),
        name="out_proj",
    )(h, mix, qm, mk, mv, w_mix, w_mem)


def _dense_ffn_kernel(h_ref, g_ref, wg_ref, wu_ref, wd_ref, o_ref, xn_ref):
    @pl.when(pl.program_id(1) == 0)
    def _():
        h = h_ref[...]
        xn_ref[...] = (_rms(h) * g_ref[...]).astype(BF16)
        o_ref[...] = h

    xn = xn_ref[...]
    a = _silu(_dot(xn, wg_ref[...])) * _dot(xn, wu_ref[...])
    o_ref[...] += _dot(a.astype(BF16), wd_ref[...])


def dense_ffn(h, gain, w_gate, w_up, w_down, tm, tf):
    T, D = h.shape
    Fd = w_gate.shape[1]
    return pl.pallas_call(
        _dense_ffn_kernel,
        grid=(T // tm, Fd // tf),
        in_specs=[pl.BlockSpec((tm, D), lambda i, j: (i, 0)),
                  pl.BlockSpec((1, D), lambda i, j: (0, 0)),
                  pl.BlockSpec((D, tf), lambda i, j: (0, j)),
                  pl.BlockSpec((D, tf), lambda i, j: (0, j)),
                  pl.BlockSpec((tf, D), lambda i, j: (j, 0))],
        out_specs=pl.BlockSpec((tm, D), lambda i, j: (i, 0)),
        out_shape=jax.ShapeDtypeStruct((T, D), F32),
        scratch_shapes=[pltpu.VMEM((tm, D), BF16)],
        compiler_params=_cparams(2),
        name="dense_ffn",
    )(h, gain.reshape(1, D), w_gate, w_up, w_down)


def _router_kernel(h_ref, g_ref, wr_ref, tri_ref, o_ref, ot_ref, cnt_ref, carry_ref):
    i = pl.program_id(0)

    @pl.when(i == 0)
    def _():
        carry_ref[...] = jnp.zeros_like(carry_ref)

    tm = h_ref.shape[0]
    xn = _rms(h_ref[...]) * g_ref[...]
    x_hi = xn.astype(BF16)
    x_lo = (xn - x_hi.astype(F32)).astype(BF16)
    wr = wr_ref[...]
    w_hi = wr.astype(BF16)
    w_lo = (wr - w_hi.astype(F32)).astype(BF16)
    logits = _dot(x_hi, w_hi) + (_dot(x_hi, w_lo) + _dot(x_lo, w_hi))
    lane = lax.broadcasted_iota(I32, (tm, LANES), 1)
    logits = jnp.where(lane < N_EXPERTS, logits, -jnp.inf)
    t1 = jnp.max(logits, axis=-1, keepdims=True)
    e1 = jnp.min(jnp.where(logits == t1, lane, LANES), axis=-1, keepdims=True)
    rest = jnp.where(lane == e1, -jnp.inf, logits)
    t2 = jnp.max(rest, axis=-1, keepdims=True)
    e2 = jnp.min(jnp.where(rest == t2, lane, LANES), axis=-1, keepdims=True)
    x2 = jnp.exp(t2 - t1)
    w1 = 1.0 / (1.0 + x2)
    w2 = x2 / (1.0 + x2)

    oh1 = lane == e1
    oh2 = lane == e2
    cnt = jnp.where(oh1 | oh2, 1.0, 0.0)
    before = _dot(tri_ref[...], cnt.astype(BF16)) + carry_ref[...]
    r1 = jnp.sum(jnp.where(oh1, before, 0.0), axis=-1, keepdims=True)
    r2 = jnp.sum(jnp.where(oh2, before, 0.0), axis=-1, keepdims=True)
    carry_ref[...] += jnp.sum(cnt, axis=0, keepdims=True)

    cols = (e1.astype(F32), e2.astype(F32), w1, w2, r1, r2)
    out = jnp.zeros((tm, LANES), F32)
    for c, val in enumerate(cols):
        out = jnp.where(lane == c, val, out)
    o_ref[...] = out
    ot_ref[...] = out.T[:SUBLANES, :]
    cnt_ref[...] = jnp.broadcast_to(carry_ref[...], cnt_ref.shape)


def moe_router(h, gain, w_router, tm):
    T, D = h.shape
    wr = jnp.zeros((D, LANES), F32).at[:, :N_EXPERTS].set(w_router)
    tri = jnp.tril(jnp.ones((tm, tm), BF16), k=-1)
    return pl.pallas_call(
        _router_kernel,
        grid=(T // tm,),
        in_specs=[pl.BlockSpec((tm, D), lambda i: (i, 0)),
                  pl.BlockSpec((1, D), lambda i: (0, 0)),
                  pl.BlockSpec((D, LANES), lambda i: (0, 0)),
                  pl.BlockSpec((tm, tm), lambda i: (0, 0))],
        out_specs=[pl.BlockSpec((tm, LANES), lambda i: (i, 0)),
                   pl.BlockSpec((SUBLANES, tm), lambda i: (0, i)),
                   pl.BlockSpec((SUBLANES, LANES), lambda i: (0, 0))],
        out_shape=[jax.ShapeDtypeStruct((T, LANES), F32),
                   jax.ShapeDtypeStruct((SUBLANES, T), F32),
                   jax.ShapeDtypeStruct((SUBLANES, LANES), F32)],
        scratch_shapes=[pltpu.VMEM((1, LANES), F32)],
        compiler_params=_cparams(1),
        name="moe_router",
    )(h, gain.reshape(1, D), wr, tri)


def _dispatch_kernel(ze_ref, zn_ref, p1_ref, p2_ref, h_ref, xs_ref, zero_ref, sem, zsem, *, pad_max, tail_max):
    tb = p1_ref.shape[0]
    zr = zero_ref.shape[0]

    @pl.when(pl.program_id(0) == 0)
    def _():
        zero_ref[...] = jnp.zeros_like(zero_ref)
        chunks = [(e, c) for e in range(N_EXPERTS) for c in range(pad_max // zr)]
        chunks += [(N_EXPERTS, c) for c in range(tail_max // zr)]

        def zcopy(e, c):
            start = pl.multiple_of(ze_ref[e] - (c + 1) * zr, zr)
            return pltpu.make_async_copy(zero_ref, xs_ref.at[pl.ds(start, zr), :], zsem)

        def needed(e, c):
            return c * zr < zn_ref[e]

        for e, c in chunks:
            @pl.when(needed(e, c))
            def _():
                zcopy(e, c).start()

        for e, c in chunks:
            @pl.when(needed(e, c))
            def _():
                zcopy(e, c).wait()

    def copy(t, pos):
        return pltpu.make_async_copy(h_ref.at[pl.ds(t, 1), :], xs_ref.at[pl.ds(pos, 1), :], sem)

    for t in range(tb):
        copy(t, p1_ref[t]).start(priority=0)
        copy(t, p2_ref[t]).start(priority=1)
    for _ in range(TOP_K):
        pltpu.make_async_copy(h_ref, xs_ref.at[pl.ds(0, tb), :], sem).wait()


def moe_dispatch(h, pos1, pos2, pad_end, pad_len, used_rows, n_rows, pad_max, tb, zr=256):
    T, D = h.shape
    tail_max = n_rows - TOP_K * T
    assert tail_max % zr == 0 and pad_max % zr == 0
    zero_end = jnp.concatenate([pad_end, jnp.full((1,), n_rows, I32)])
    zero_len = jnp.concatenate([pad_len, n_rows - used_rows[None]])
    grid_spec = pltpu.PrefetchScalarGridSpec(
        num_scalar_prefetch=2,
        grid=(T // tb,),
        in_specs=[pl.BlockSpec((tb,), lambda i, ps, pn: (i,), memory_space=pltpu.SMEM),
                  pl.BlockSpec((tb,), lambda i, ps, pn: (i,), memory_space=pltpu.SMEM),
                  pl.BlockSpec((tb, D), lambda i, ps, pn: (i, 0))],
        out_specs=pl.BlockSpec(memory_space=pl.ANY),
        scratch_shapes=[pltpu.VMEM((zr, D), h.dtype), pltpu.SemaphoreType.DMA(()), pltpu.SemaphoreType.DMA(())],
    )
    return pl.pallas_call(
        functools.partial(_dispatch_kernel, pad_max=pad_max, tail_max=tail_max),
        grid_spec=grid_spec,
        out_shape=jax.ShapeDtypeStruct((n_rows, D), h.dtype),
        compiler_params=_cparams(1),
        name="moe_dispatch",
    )(zero_end, zero_len, pos1, pos2, h)


def _expert_ffn_kernel(te_ref, tv_ref, x_ref, g_ref, wg_ref, wu_ref, wd_ref, wdl_ref, o_ref, xn_ref, a_ref,
                       *, ts, nf):
    i = pl.program_id(0)
    j = pl.program_id(1)
    valid = tv_ref[i]
    n_sub = x_ref.shape[0] // ts

    n_occ = (valid + ts - 1) // ts
    for k in range(1, n_sub + 1):
        rows = pl.ds(0, k * ts)

        def gate_up(slot):
            xn = xn_ref[rows, :]
            a = _silu(_dot(xn, wg_ref[...].astype(BF16))) * _dot(xn, wu_ref[...].astype(BF16))
            a_ref[slot, rows, :] = a.astype(BF16)

        def down(slot, first, w_ref=wd_ref):
            y = _dot(a_ref[slot, rows, :], w_ref[...].astype(BF16))
            if first:
                o_ref[rows, :] = y
                if k < n_sub:
                    o_ref[pl.ds(k * ts, (n_sub - k) * ts), :] = jnp.zeros(((n_sub - k) * ts, o_ref.shape[1]), F32)
            else:
                o_ref[rows, :] += y

        @pl.when(n_occ == k)
        def _():
            @pl.when(j == 0)
            def _():
                xn_ref[rows, :] = (_rms(x_ref[rows, :]) * g_ref[...]).astype(BF16)
                gate_up(0)

            @pl.when(j == 1)
            def _():
                gate_up(1)
                down(0, first=True)

            @pl.when((j > 1) & (j < nf - 1))
            def _():
                gate_up(j % 2)
                down((j - 1) % 2, first=False)

            @pl.when(j == nf - 1)
            def _():
                gate_up((nf - 1) % 2)
                down((nf - 2) % 2, first=False)
                down((nf - 1) % 2, first=False, w_ref=wdl_ref)

    @pl.when((n_occ == 0) & (j == 0))
    def _():
        o_ref[...] = jnp.zeros_like(o_ref)


def expert_ffn(xs, gain, w_gate, w_up, w_down, layer, tile_expert, tile_valid, n_tiles, tm, ts, tf):
    D = xs.shape[1]
    R = n_tiles * tm
    Fd = w_gate.shape[3]
    nf = Fd // tf

    assert nf >= 3
    def up_tile(i, j, tv):
        return jnp.where(tv[i] > 0, j, nf - 1)

    def down_tile(i, j, tv):
        return jnp.where(tv[i] > 0, jnp.maximum(j - 1, 0), nf - 2)

    grid_spec = pltpu.PrefetchScalarGridSpec(
        num_scalar_prefetch=2,
        grid=(n_tiles, nf),
        in_specs=[pl.BlockSpec((tm, D), lambda i, j, te, tv: (jnp.where(tv[i] > 0, i, 0), 0)),
                  pl.BlockSpec((1, D), lambda i, j, te, tv: (0, 0)),
                  pl.BlockSpec((None, None, D, tf), lambda i, j, te, tv: (layer, te[i], 0, up_tile(i, j, tv))),
                  pl.BlockSpec((None, None, D, tf), lambda i, j, te, tv: (layer, te[i], 0, up_tile(i, j, tv))),
                  pl.BlockSpec((None, None, tf, D), lambda i, j, te, tv: (layer, te[i], down_tile(i, j, tv), 0)),
                  pl.BlockSpec((None, None, tf, D), lambda i, j, te, tv: (layer, te[i], nf - 1, 0))],
        out_specs=pl.BlockSpec((tm, D), lambda i, j, te, tv: (i, 0)),
        scratch_shapes=[pltpu.VMEM((tm, D), BF16), pltpu.VMEM((2, tm, tf), BF16)],
    )
    return pl.pallas_call(
        functools.partial(_expert_ffn_kernel, ts=ts, nf=nf),
        grid_spec=grid_spec,
        out_shape=jax.ShapeDtypeStruct((R, D), F32),
        compiler_params=_cparams(2),
        name="expert_ffn",
    )(tile_expert, tile_valid, xs, gain.reshape(1, D), w_gate, w_up, w_down, w_down)


def _combine_kernel(p1_ref, p2_ref, route_ref, h_ref, y_ref, o_ref, b1_ref, b2_ref, sem):
    tb = p1_ref.shape[0]

    def copies(t):
        dst = pl.ds(t, 1)
        return (pltpu.make_async_copy(y_ref.at[pl.ds(p1_ref[t], 1), :], b1_ref.at[dst, :], sem),
                pltpu.make_async_copy(y_ref.at[pl.ds(p2_ref[t], 1), :], b2_ref.at[dst, :], sem))

    for t in range(tb):
        for prio, cp in enumerate(copies(t)):
            cp.start(priority=prio)
    for b_ref in (b1_ref, b2_ref):
        pltpu.make_async_copy(y_ref.at[pl.ds(0, tb), :], b_ref, sem).wait()
    w1 = route_ref[:, 2:3]
    w2 = route_ref[:, 3:4]
    o_ref[...] = h_ref[...] + w1 * b1_ref[...] + w2 * b2_ref[...]


def moe_combine(h, y, route, pos1, pos2, tb):
    T, D = h.shape
    smem = lambda: pl.BlockSpec((tb,), lambda i: (i,), memory_space=pltpu.SMEM)
    return pl.pallas_call(
        _combine_kernel,
        grid=(T // tb,),
        in_specs=[smem(), smem(),
                  pl.BlockSpec((tb, LANES), lambda i: (i, 0)),
                  pl.BlockSpec((tb, D), lambda i: (i, 0)),
                  pl.BlockSpec(memory_space=pl.ANY)],
        out_specs=pl.BlockSpec((tb, D), lambda i: (i, 0)),
        out_shape=jax.ShapeDtypeStruct((T, D), F32),
        scratch_shapes=[pltpu.VMEM((tb, D), F32), pltpu.VMEM((tb, D), F32),
                        pltpu.SemaphoreType.DMA(())],
        compiler_params=_cparams(1),
        name="moe_combine",
    )(pos1, pos2, route, h, y)


def moe_ffn(h, gain, w_router, w_gate, w_up, w_down, layer, tm=1024, ts=256, tf=512, tb_dispatch=1024, tb_combine=512):
    T = h.shape[0]
    route, route_t, counts = moe_router(h, gain, w_router, tm=512)
    e1 = route_t[0].astype(I32)
    e2 = route_t[1].astype(I32)
    counts = counts[0, :N_EXPERTS].astype(I32)
    tiles_per = (counts + tm - 1) // tm
    tile_end = jnp.cumsum(tiles_per)
    tile_start = tile_end - tiles_per
    row_start = tile_start * tm
    pos1 = row_start[e1] + route_t[4].astype(I32)
    pos2 = row_start[e2] + route_t[5].astype(I32)
    n_tiles = (TOP_K * T) // tm + N_EXPERTS
    tile_ids = jnp.arange(n_tiles, dtype=I32)
    tile_expert = jnp.sum(jnp.minimum(tile_ids, tile_end[-1] - 1)[:, None] >= tile_end[None, :], axis=1).astype(I32)
    tile_valid = jnp.clip(counts[tile_expert] - (tile_ids - tile_start[tile_expert]) * tm, 0, tm)
    tile_valid = jnp.where(tile_ids < tile_end[-1], tile_valid, 0).astype(I32)
    pad_len = tiles_per * tm - counts

    xs = moe_dispatch(h, pos1, pos2, tile_end * tm, pad_len, tile_end[-1] * tm, n_tiles * tm, tm, tb_dispatch)
    y = expert_ffn(xs, gain, w_gate, w_up, w_down, layer, tile_expert, tile_valid, n_tiles, tm, ts, tf)
    return moe_combine(h, y, route, pos1, pos2, tb_combine)


def _pad_heads(a, n_heads, width, pad_to, axis):
    shape = list(a.shape)
    a = a.reshape(shape[:axis] + [n_heads, width] + shape[axis + 1:])
    pads = [(0, 0)] * a.ndim
    pads[axis + 1] = (0, pad_to - width)
    a = jnp.pad(a, pads)
    return a.reshape(shape[:axis] + [n_heads * pad_to] + shape[axis + 1:])


def kernel(x, mem, ln_mix, ln_mem, w_mem_kv, mem_q_gain, mem_k_gain, w_out, ln_ffn, swa_w_in, swa_q_gain, swa_k_gain, swa_sink, ret_w_in, mlstm_w_in, mlstm_conv_w, mlstm_conv_b, mlstm_i_bias, mlstm_f_bias, mlstm_out_gain, ffn_w_gate, ffn_w_up, ffn_w_down, moe_router, moe_w_gate, moe_w_up, moe_w_down):
    B, S, D = x.shape
    M = mem.shape[1]
    T = B * S
    depth = ln_mix.shape[0]
    h = x.reshape(T, D)
    mem2 = mem.reshape(B * M, D)

    for layer in range(depth):
        kind = layer % N_MIXERS
        idx = layer // N_MIXERS
        mk, mv = rms_proj(mem2, ln_mem[layer], w_mem_kv[layer].astype(BF16),
                          ((MEM_WIDTH, BF16, mem_k_gain[layer]), (MEM_WIDTH, BF16, None)),
                          tm=B * M, name="mem_kv_proj")
        w_o = w_out[layer].astype(BF16)
        w_o_mix, w_o_mem = w_o[:MIX_WIDTH], w_o[MIX_WIDTH:]
        qm_out = (MEM_WIDTH, BF16, mem_q_gain[layer])

        if kind == 0:
            kvw = SWA_KV_HEADS * HEAD_DIM
            q, k, v, qm = rms_proj(h, ln_mix[layer], swa_w_in[idx].astype(BF16),
                                   ((MIX_WIDTH, BF16, swa_q_gain[idx]), (kvw, BF16, swa_k_gain[idx]),
                                    (kvw, BF16, None), qm_out), tm=1024, name="swa_in_proj")
            mix = swa_attention(q, k, v, swa_sink[idx], B)
        elif kind == 1:
            qkw = RET_HEADS * RET_QK_DIM
            q, k, v, g, qm = rms_proj(h, ln_mix[layer], ret_w_in[idx].astype(BF16),
                                      ((qkw, BF16, None), (qkw, F32, None), (MIX_WIDTH, BF16, None),
                                       (MIX_WIDTH, F32, None), qm_out), tm=1024, name="ret_in_proj")
            mix = retention(q, k, v, g, B)
        else:
            H, P, VP = MLSTM_HEADS, MLSTM_QK_PAD, MLSTM_V_PAD
            w = mlstm_w_in[idx]
            qkw = 2 * H * MLSTM_QK_DIM
            o_v, o_og, o_ig = qkw, qkw + MIX_WIDTH, qkw + 2 * MIX_WIDTH
            o_fg, o_qm = o_ig + H, o_ig + 2 * H
            w_gates = jnp.zeros((D, LANES), F32).at[:, :2 * H].set(w[:, o_ig:o_qm])
            w_pad = jnp.concatenate([
                _pad_heads(w[:, :qkw], 2 * H, MLSTM_QK_DIM, P, 1),
                _pad_heads(w[:, o_v:o_og], H, MLSTM_V_DIM, VP, 1),
                _pad_heads(w[:, o_og:o_ig], H, MLSTM_V_DIM, VP, 1),
                w_gates, w[:, o_qm:]], axis=1).astype(BF16)
            qk, v, og, gates, qm = rms_proj(h, ln_mix[layer], w_pad,
                                            ((2 * H * P, F32, None), (H * VP, BF16, None), (H * VP, F32, None),
                                             (LANES, F32, None), qm_out), tm=1024, name="mlstm_in_proj")
            gates_t = gates[:, :SUBLANES].reshape(B, S, SUBLANES).transpose(0, 2, 1)
            bias = jnp.concatenate([mlstm_i_bias[idx], mlstm_f_bias[idx]])
            bias_c = jnp.zeros((1, LANES), F32).at[0, :2 * H].set(bias)
            bias_r = jnp.broadcast_to(bias[:, None], (SUBLANES, MLSTM_CHUNK))
            mix = mlstm(qk, v, og, gates, gates_t,
                        _pad_heads(mlstm_conv_w[idx], 2 * H, MLSTM_QK_DIM, P, 1),
                        _pad_heads(mlstm_conv_b[idx][None], 2 * H, MLSTM_QK_DIM, P, 1),
                        bias_c, bias_r,
                        _pad_heads(mlstm_out_gain[idx][None], H, MLSTM_V_DIM, VP, 1), B)
            w_o_mix = _pad_heads(w_o_mix, H, MLSTM_V_DIM, VP, 0)

        h = out_proj(h, mix, qm, mk, mv, w_o_mix, w_o_mem, B, tm=1024)
        j = layer // 2
        if layer % 2 == 0:
            h = dense_ffn(h, ln_ffn[layer], ffn_w_gate[j].astype(BF16), ffn_w_up[j].astype(BF16),
                          ffn_w_down[j].astype(BF16), tm=1024, tf=1792)
        else:
            h = moe_ffn(h, ln_ffn[layer], moe_router[j], moe_w_gate, moe_w_up, moe_w_down, j)
    return h.reshape(B, S, D)
```

```python
import functools
import math

import jax
import jax.numpy as jnp
from jax import lax
from jax.experimental import pallas as pl
from jax.experimental.pallas import tpu as pltpu

F32 = jnp.float32
BF16 = jnp.bfloat16
I32 = jnp.int32

D_MODEL = 1024
N_MIXERS = 3
HEAD_DIM = 64
MEM_HEADS = 4
MEM_HEAD_DIM = 64
MEM_WIDTH = MEM_HEADS * MEM_HEAD_DIM
MIX_WIDTH = D_MODEL - MEM_WIDTH
RMS_EPS = 1e-6
NEG_INF = -1e30

SWA_Q_HEADS = MIX_WIDTH // HEAD_DIM
SWA_KV_HEADS = 4
SWA_GROUP = SWA_Q_HEADS // SWA_KV_HEADS
SWA_WINDOW = 128
SWA_BLOCK = 128

RET_HEADS = 6
RET_QK_DIM = 64
RET_V_DIM = MIX_WIDTH // RET_HEADS
RET_CHUNK = 128

MLSTM_HEADS = 4
MLSTM_V_DIM = MIX_WIDTH // MLSTM_HEADS
MLSTM_QK_DIM = MLSTM_V_DIM // 2
MLSTM_CHUNK = 128
MLSTM_CONV = 4
MLSTM_QK_PAD = 128
MLSTM_V_PAD = 256

N_EXPERTS = 8
TOP_K = 2

LANES = 128
SUBLANES = 8
VMEM_LIMIT = 56 * 1024 * 1024


def _cparams(n_axes):
    return pltpu.CompilerParams(dimension_semantics=("arbitrary",) * n_axes,
                                vmem_limit_bytes=VMEM_LIMIT)


def _rms(x, eps=RMS_EPS):
    return x * lax.rsqrt(jnp.mean(x * x, axis=-1, keepdims=True) + eps)


def _dot(a, b):
    return jnp.dot(a, b, preferred_element_type=F32)


def _dot_nt(a, b):
    return lax.dot_general(a, b, (((1,), (1,)), ((), ())), preferred_element_type=F32)


def _dot_tn(a, b):
    return lax.dot_general(a, b, (((0,), (0,)), ((), ())), preferred_element_type=F32)


def _silu(x):
    return x * jax.nn.sigmoid(x)


NORM_CHUNK = 256


def _head_rms(y, seg_ref, hg):
    sq = y * y
    hi = sq.astype(BF16)
    lo = (sq - hi.astype(F32)).astype(BF16)
    ms = _dot(hi, seg_ref[...]) + _dot(lo, seg_ref[...])
    return y * lax.rsqrt(ms + RMS_EPS) * hg


def _rms_proj_kernel(x_ref, g_ref, w_ref, seg_ref, *refs, widths, normed):
    n_gain = sum(normed)
    gain_refs, o_refs = refs[:n_gain], refs[n_gain:]
    xn = (_rms(x_ref[...]) * g_ref[...]).astype(BF16)
    off = 0
    gi = 0
    for o_ref, wd, nrm in zip(o_refs, widths, normed):
        if nrm:
            hg = gain_refs[gi][...]
            gi += 1
            for c in range(0, wd, NORM_CHUNK):
                y = _dot(xn, w_ref[:, off + c:off + c + NORM_CHUNK])
                o_ref[:, c:c + NORM_CHUNK] = _head_rms(y, seg_ref, hg).astype(o_ref.dtype)
        else:
            o_ref[...] = _dot(xn, w_ref[:, off:off + wd]).astype(o_ref.dtype)
        off += wd


def rms_proj(x, gain, w, outs, tm, name):
    T, D = x.shape
    N = w.shape[1]
    widths = tuple(o[0] for o in outs)
    normed = tuple(o[2] is not None for o in outs)
    head_gains = [jnp.tile(o[2], NORM_CHUNK // HEAD_DIM).reshape(1, NORM_CHUNK) for o in outs if o[2] is not None]
    assert sum(widths) == N and T % tm == 0
    assert all(wd % NORM_CHUNK == 0 for wd, nrm in zip(widths, normed) if nrm)
    head_of = jnp.arange(NORM_CHUNK) // HEAD_DIM
    seg = jnp.where(head_of[:, None] == head_of[None, :], 1.0 / HEAD_DIM, 0.0).astype(BF16)
    return pl.pallas_call(
        functools.partial(_rms_proj_kernel, widths=widths, normed=normed),
        grid=(T // tm,),
        in_specs=[pl.BlockSpec((tm, D), lambda i: (i, 0)),
                  pl.BlockSpec((1, D), lambda i: (0, 0)),
                  pl.BlockSpec((D, N), lambda i: (0, 0)),
                  pl.BlockSpec((NORM_CHUNK, NORM_CHUNK), lambda i: (0, 0))]
                 + [pl.BlockSpec((1, NORM_CHUNK), lambda i: (0, 0))] * len(head_gains),
        out_specs=[pl.BlockSpec((tm, wd), lambda i: (i, 0)) for wd in widths],
        out_shape=[jax.ShapeDtypeStruct((T, o[0]), o[1]) for o in outs],
        compiler_params=_cparams(1),
        name=name,
    )(x, gain.reshape(1, D), w, seg, *head_gains)


def _swa_kernel(sink_ref, q_ref, kc_ref, kp_ref, vc_ref, vp_ref, bias_ref, o_ref):
    n = pl.program_id(1)
    L = SWA_BLOCK
    n_blk = q_ref.shape[0] // L
    key = lax.broadcasted_iota(I32, (L, L), 0)
    qry = lax.broadcasted_iota(I32, (L, L), 1)
    from_prev = key > qry
    own = key <= qry
    zero = jnp.zeros((L, L), BF16)
    for blk in range(n_blk):
        rows = slice(blk * L, (blk + 1) * L)
        prows = slice((blk - 1) * L, blk * L)
        outs = []
        for g in range(SWA_KV_HEADS):
            ksl = slice(g * HEAD_DIM, (g + 1) * HEAD_DIM)
            k_prev = kp_ref[:, ksl] if blk == 0 else kc_ref[prows, ksl]
            v_prev = vp_ref[:, ksl] if blk == 0 else vc_ref[prows, ksl]
            k = jnp.concatenate([k_prev, kc_ref[rows, ksl]], axis=0)
            v = jnp.concatenate([v_prev, vc_ref[rows, ksl]], axis=0)
            for h in range(g * SWA_GROUP, (g + 1) * SWA_GROUP):
                sink = sink_ref[h]
                s2 = _dot_nt(k, q_ref[rows, h * HEAD_DIM:(h + 1) * HEAD_DIM])
                s = jnp.where(from_prev, s2[:L], s2[L:]) * (HEAD_DIM ** -0.5) - bias_ref[h]
                if blk == 0:
                    s = jnp.where(jnp.logical_or(n > 0, own), s, NEG_INF)
                m = jnp.maximum(jnp.max(s, axis=0, keepdims=True), sink)
                e = jnp.exp(s - m)
                denom = jnp.sum(e, axis=0, keepdims=True) + jnp.exp(sink - m)
                p = (e * (1.0 / denom)).astype(BF16)
                p2 = jnp.concatenate([jnp.where(from_prev, p, zero), jnp.where(from_prev, zero, p)], axis=0)
                outs.append(_dot_tn(p2, v))
        o_ref[rows, :] = jnp.concatenate(outs, axis=-1).astype(o_ref.dtype)


def swa_attention(q, k, v, sink, batch, n_blk=4):
    T = q.shape[0]
    L = SWA_BLOCK
    nb = T // batch // L
    kvw = SWA_KV_HEADS * HEAD_DIM
    slopes = jnp.exp2(-8.0 * jnp.arange(1, SWA_Q_HEADS + 1, dtype=F32) / SWA_Q_HEADS)
    assert SWA_WINDOW == L
    qi, kj = jnp.arange(L)[:, None], jnp.arange(L)[None, :]
    dist = jnp.where(kj > qi, qi + L - kj, qi - kj).astype(F32)
    bias = slopes[:, None, None] * dist.T
    ns = nb // n_blk
    slab = n_blk * L
    cur = lambda b, n: (b * ns + n, 0)
    prev = lambda b, n: (b * nb + jnp.maximum(n * n_blk - 1, 0), 0)
    return pl.pallas_call(
        _swa_kernel,
        grid=(batch, ns),
        in_specs=[pl.BlockSpec(memory_space=pltpu.SMEM),
                  pl.BlockSpec((slab, MIX_WIDTH), cur),
                  pl.BlockSpec((slab, kvw), cur), pl.BlockSpec((L, kvw), prev),
                  pl.BlockSpec((slab, kvw), cur), pl.BlockSpec((L, kvw), prev),
                  pl.BlockSpec((SWA_Q_HEADS, L, L), lambda b, n: (0, 0, 0))],
        out_specs=pl.BlockSpec((slab, MIX_WIDTH), cur),
        out_shape=jax.ShapeDtypeStruct((T, MIX_WIDTH), BF16),
        compiler_params=_cparams(2),
        name="swa_attention",
    )(sink, q, k, k, v, v, bias)


def _retention_kernel(q_ref, k_ref, v_ref, g_ref, dec_ref, qd_ref, kd_ref, cd_ref, o_ref, state_ref):
    @pl.when(pl.program_id(0) == 0)
    def _():
        state_ref[...] = jnp.zeros_like(state_ref)

    L = RET_CHUNK
    n_ch = q_ref.shape[1] // L
    for b in range(q_ref.shape[0]):
        outs = [[] for _ in range(n_ch)]
        for h in range(RET_HEADS):
            qsl = slice(h * RET_QK_DIM, (h + 1) * RET_QK_DIM)
            vsl = slice(h * RET_V_DIM, (h + 1) * RET_V_DIM)
            state = state_ref[b, h]
            for c in range(n_ch):
                rows = slice(c * L, (c + 1) * L)
                q = q_ref[b, rows, qsl]
                k = k_ref[b, rows, qsl] * (RET_QK_DIM ** -0.5)
                v = v_ref[b, rows, vsl]
                s = _dot_nt(q, k.astype(BF16)) * dec_ref[h]
                o = _dot(s.astype(BF16), v) + _dot(q, state.astype(BF16)) * qd_ref[h]
                state = state * cd_ref[h] + _dot_tn((k * kd_ref[h]).astype(BF16), v)
                outs[c].append(_silu(g_ref[b, rows, vsl]) * _rms(o))
            state_ref[b, h] = state
        for c in range(n_ch):
            o_ref[b, c * L:(c + 1) * L, :] = jnp.concatenate(outs[c], axis=-1).astype(o_ref.dtype)


def _retention_consts():
    H, L = RET_HEADS, RET_CHUNK
    log_gamma = jnp.log1p(-jnp.exp2(-5.0 - jnp.arange(H, dtype=F32)))
    pos = jnp.arange(L, dtype=F32)
    rel = pos[:, None] - pos[None, :]
    intra = jnp.exp(jnp.where(rel >= 0, log_gamma[:, None, None] * rel, -jnp.inf))
    q_decay = jnp.exp(log_gamma[:, None] * (pos + 1.0))[..., None]
    k_decay = jnp.exp(log_gamma[:, None] * (L - 1.0 - pos))[..., None]
    chunk_decay = jnp.exp(log_gamma * L)[:, None, None]
    return (intra,
            jnp.broadcast_to(q_decay, (H, L, RET_V_DIM)),
            jnp.broadcast_to(k_decay, (H, L, RET_QK_DIM)),
            jnp.broadcast_to(chunk_decay, (H, RET_QK_DIM, RET_V_DIM)))


def retention(q, k, v, g, batch, n_ch=4):
    T = q.shape[0]
    L = RET_CHUNK
    nc = T // batch // L
    H = RET_HEADS
    intra, qd, kd, cd = _retention_consts()
    cur = lambda c: (0, c, 0)
    const = lambda c: (0, 0, 0)
    qkw = H * RET_QK_DIM
    per_batch = lambda a: a.reshape(batch, T // batch, a.shape[1])
    slab = n_ch * L
    out = pl.pallas_call(
        _retention_kernel,
        grid=(nc // n_ch,),
        in_specs=[pl.BlockSpec((batch, slab, qkw), cur), pl.BlockSpec((batch, slab, qkw), cur),
                  pl.BlockSpec((batch, slab, MIX_WIDTH), cur), pl.BlockSpec((batch, slab, MIX_WIDTH), cur),
                  pl.BlockSpec((H, L, L), const), pl.BlockSpec((H, L, RET_V_DIM), const),
                  pl.BlockSpec((H, L, RET_QK_DIM), const), pl.BlockSpec((H, RET_QK_DIM, RET_V_DIM), const)],
        out_specs=pl.BlockSpec((batch, slab, MIX_WIDTH), cur),
        out_shape=jax.ShapeDtypeStruct((batch, T // batch, MIX_WIDTH), BF16),
        scratch_shapes=[pltpu.VMEM((batch, H, RET_QK_DIM, RET_V_DIM), F32)],
        compiler_params=_cparams(1),
        name="retention",
    )(per_batch(q), per_batch(k), per_batch(v), per_batch(g), intra, qd, kd, cd)
    return out.reshape(T, MIX_WIDTH)


def _split3(x):
    x1 = x.astype(BF16)
    r1 = x - x1.astype(F32)
    x2 = r1.astype(BF16)
    x3 = (r1 - x2.astype(F32)).astype(BF16)
    return x1, x2, x3


def _mlstm_kernel(qk_ref, v_ref, og_ref, gc_ref, gr_ref, cw_ref, cb_ref, bc_ref, br_ref, gain_ref,
                  tri_ref, trit_ref, o_ref, xbuf_ref, c_ref, n_ref, m_ref):
    for b in range(qk_ref.shape[0]):
        _mlstm_chunk(qk_ref.at[b], v_ref.at[b], og_ref.at[b], gc_ref.at[b], gr_ref.at[b], cw_ref, cb_ref,
                     bc_ref, br_ref, gain_ref, tri_ref, trit_ref, o_ref.at[b],
                     xbuf_ref.at[b], c_ref.at[b], n_ref.at[b], m_ref.at[b])


def _mlstm_chunk(qk_ref, v_ref, og_ref, gc_ref, gr_ref, cw_ref, cb_ref, bc_ref, br_ref, gain_ref,
                 tri_ref, trit_ref, o_ref, xbuf_ref, c_ref, n_ref, m_ref):
    L = MLSTM_CHUNK
    H = MLSTM_HEADS
    P = MLSTM_QK_PAD
    VP = MLSTM_V_PAD
    KT = SUBLANES

    @pl.when(pl.program_id(0) == 0)
    def _():
        xbuf_ref[0:KT, :] = jnp.zeros((KT, xbuf_ref.shape[1]), F32)
        c_ref[...] = jnp.zeros_like(c_ref)
        n_ref[...] = jnp.zeros_like(n_ref)
        m_ref[...] = jnp.zeros_like(m_ref)

    xbuf_ref[KT:KT + L, :] = qk_ref[...]
    acc = cb_ref[...] + cw_ref[MLSTM_CONV - 1:MLSTM_CONV, :] * xbuf_ref[KT:KT + L, :]
    for j in range(MLSTM_CONV - 1):
        sh = MLSTM_CONV - 1 - j
        acc = acc + cw_ref[j:j + 1, :] * xbuf_ref[KT - sh:KT - sh + L, :]
    xbuf_ref[0:KT, :] = qk_ref[L - KT:L, :]
    qk = _silu(acc)

    xc = gc_ref[...] + bc_ref[...]
    xr = gr_ref[...] + br_ref[...]
    lfc = jax.nn.log_sigmoid(xc)
    lfr = jax.nn.log_sigmoid(xr)
    tri = tri_ref[...]
    trit = trit_ref[...]
    bc = sum(_dot(tri, t) for t in _split3(lfc))
    br = sum(_dot(t, trit) for t in _split3(lfr))

    rowi = lax.broadcasted_iota(I32, (L, L), 0)
    coli = lax.broadcasted_iota(I32, (L, L), 1)
    causal = rowi >= coli
    outs = []
    for h in range(H):
        q = qk[:, h * P:(h + 1) * P].astype(BF16)
        k = qk[:, (H + h) * P:(H + h + 1) * P] * (MLSTM_QK_DIM ** -0.5)
        v = v_ref[:, h * VP:(h + 1) * VP]
        li_c = xc[:, h:h + 1]
        b_c = bc[:, H + h:H + h + 1]
        li_r = xr[h:h + 1, :]
        b_r = br[H + h:H + h + 1, :]
        g = b_c[L - 1:L, :]
        m = m_ref[h:h + 1, 0:1]
        C = c_ref[h]
        nvec = n_ref[h:h + 1, :]

        dmat = jnp.where(causal, b_c - b_r + li_r, -jnp.inf)
        inter = b_c + m
        m_t = jnp.maximum(inter, jnp.max(dmat, axis=-1, keepdims=True))
        w = jnp.exp(dmat - m_t)
        a = jnp.exp(inter - m_t)
        s = _dot_nt(q, k.astype(BF16)) * w
        num = _dot(s.astype(BF16), v) + a * _dot(q, C.astype(BF16))
        qf = qk[:, h * P:(h + 1) * P]
        den = jnp.sum(s, axis=-1, keepdims=True) + a * jnp.sum(qf * nvec, axis=-1, keepdims=True)
        hh = num / jnp.maximum(jnp.abs(den), jnp.exp(-m_t))

        u_c = g - b_c + li_c
        u_r = g - b_r + li_r
        m_new = jnp.maximum(g + m, jnp.max(u_r, axis=-1, keepdims=True))
        wk = jnp.exp(u_c - m_new)
        decay = jnp.exp(g + m - m_new)
        kw = k * wk
        c_ref[h] = decay * C + _dot_tn(kw.astype(BF16), v)
        n_ref[h:h + 1, :] = decay * nvec + jnp.sum(kw, axis=0, keepdims=True)
        m_ref[h:h + 1, :] = jnp.broadcast_to(m_new, (1, m_ref.shape[1]))

        ms = jnp.sum(hh * hh, axis=-1, keepdims=True) * (1.0 / MLSTM_V_DIM)
        hc = hh * lax.rsqrt(ms + RMS_EPS) * gain_ref[:, h * VP:(h + 1) * VP]
        outs.append(jax.nn.sigmoid(og_ref[:, h * VP:(h + 1) * VP]) * hc)
    o_ref[...] = jnp.concatenate(outs, axis=-1).astype(o_ref.dtype)


def mlstm(qk, v, og, gates, gates_t, conv_w, conv_b, bias_c, bias_r, gain, batch):
    T = qk.shape[0]
    L = MLSTM_CHUNK
    nc = T // batch // L
    H, P, VP = MLSTM_HEADS, MLSTM_QK_PAD, MLSTM_V_PAD
    tri = jnp.tril(jnp.ones((L, L), BF16))
    cur = lambda c: (0, c, 0)
    c2 = lambda c: (0, 0)
    per_batch = lambda a: a.reshape(batch, T // batch, a.shape[1])
    out = pl.pallas_call(
        _mlstm_kernel,
        grid=(nc,),
        in_specs=[pl.BlockSpec((batch, L, 2 * H * P), cur), pl.BlockSpec((batch, L, H * VP), cur),
                  pl.BlockSpec((batch, L, H * VP), cur), pl.BlockSpec((batch, L, LANES), cur),
                  pl.BlockSpec((batch, SUBLANES, L), lambda c: (0, 0, c)),
                  pl.BlockSpec((MLSTM_CONV, 2 * H * P), c2), pl.BlockSpec((1, 2 * H * P), c2),
                  pl.BlockSpec((1, LANES), c2), pl.BlockSpec((SUBLANES, L), c2),
                  pl.BlockSpec((1, H * VP), c2),
                  pl.BlockSpec((L, L), c2), pl.BlockSpec((L, L), c2)],
        out_specs=pl.BlockSpec((batch, L, H * VP), cur),
        out_shape=jax.ShapeDtypeStruct((batch, T // batch, H * VP), BF16),
        scratch_shapes=[pltpu.VMEM((batch, SUBLANES + L, 2 * H * P), F32),
                        pltpu.VMEM((batch, H, P, VP), F32),
                        pltpu.VMEM((batch, SUBLANES, P), F32),
                        pltpu.VMEM((batch, SUBLANES, LANES), F32)],
        compiler_params=_cparams(1),
        name="mlstm",
    )(per_batch(qk), per_batch(v), per_batch(og), per_batch(gates), gates_t,
      conv_w, conv_b, bias_c, bias_r, gain, tri, tri.T)
    return out.reshape(T, H * VP)


def _out_proj_kernel(h_ref, mix_ref, q_ref, k_ref, v_ref, w1_ref, w2_ref, o_ref):
    outs = []
    for hd in range(MEM_HEADS):
        sl = slice(hd * MEM_HEAD_DIM, (hd + 1) * MEM_HEAD_DIM)
        s = _dot_nt(k_ref[:, sl], q_ref[:, sl]) * (MEM_HEAD_DIM ** -0.5)
        e = jnp.exp(s - jnp.max(s, axis=0, keepdims=True))
        p = (e * (1.0 / jnp.sum(e, axis=0, keepdims=True))).astype(BF16)
        outs.append(_dot_tn(p, v_ref[:, sl]))
    mem_out = jnp.concatenate(outs, axis=-1).astype(BF16)
    o_ref[...] = h_ref[...] + _dot(mix_ref[...], w1_ref[...]) + _dot(mem_out, w2_ref[...])


def out_proj(h, mix, qm, mk, mv, w_mix, w_mem, batch, tm):
    T, D = h.shape
    Wm = mix.shape[1]
    M = mk.shape[0] // batch
    nt = T // batch // tm
    rows = lambda b, i: (b * nt + i, 0)
    mem = lambda b, i: (b, 0)
    const = lambda b, i: (0, 0)
    return pl.pallas_call(
        _out_proj_kernel,
        grid=(batch, nt),
        in_specs=[pl.BlockSpec((tm, D), rows),
                  pl.BlockSpec((tm, Wm), rows),
                  pl.BlockSpec((tm, MEM_WIDTH), rows),
                  pl.BlockSpec((M, MEM_WIDTH), mem),
                  pl.BlockSpec((M, MEM_WIDTH), mem),
                  pl.BlockSpec((Wm, D), const),
                  pl.BlockSpec((MEM_WIDTH, D), const)],
        out_specs=pl.BlockSpec((tm, D), rows),
        out_shape=jax.ShapeDtypeStruct((T, D), F32),
        compiler_params=_cparams(2),
        name="out_proj",
    )(h, mix, qm, mk, mv, w_mix, w_mem)


def _dense_ffn_kernel(h_ref, g_ref, wg_ref, wu_ref, wd_ref, o_ref, xn_ref):
    @pl.when(pl.program_id(1) == 0)
    def _():
        h = h_ref[...]
        xn_ref[...] = (_rms(h) * g_ref[...]).astype(BF16)
        o_ref[...] = h

    xn = xn_ref[...]
    a = _silu(_dot(xn, wg_ref[...])) * _dot(xn, wu_ref[...])
    o_ref[...] += _dot(a.astype(BF16), wd_ref[...])


def dense_ffn(h, gain, w_gate, w_up, w_down, tm, tf):
    T, D = h.shape
    Fd = w_gate.shape[1]
    return pl.pallas_call(
        _dense_ffn_kernel,
        grid=(T // tm, Fd // tf),
        in_specs=[pl.BlockSpec((tm, D), lambda i, j: (i, 0)),
                  pl.BlockSpec((1, D), lambda i, j: (0, 0)),
                  pl.BlockSpec((D, tf), lambda i, j: (0, j)),
                  pl.BlockSpec((D, tf), lambda i, j: (0, j)),
                  pl.BlockSpec((tf, D), lambda i, j: (j, 0))],
        out_specs=pl.BlockSpec((tm, D), lambda i, j: (i, 0)),
        out_shape=jax.ShapeDtypeStruct((T, D), F32),
        scratch_shapes=[pltpu.VMEM((tm, D), BF16)],
        compiler_params=_cparams(2),
        name="dense_ffn",
    )(h, gain.reshape(1, D), w_gate, w_up, w_down)


def _router_kernel(h_ref, g_ref, wr_ref, tri_ref, o_ref, ot_ref, cnt_ref, carry_ref):
    i = pl.program_id(0)

    @pl.when(i == 0)
    def _():
        carry_ref[...] = jnp.zeros_like(carry_ref)

    tm = h_ref.shape[0]
    xn = _rms(h_ref[...]) * g_ref[...]
    x_hi = xn.astype(BF16)
    x_lo = (xn - x_hi.astype(F32)).astype(BF16)
    wr = wr_ref[...]
    w_hi = wr.astype(BF16)
    w_lo = (wr - w_hi.astype(F32)).astype(BF16)
    logits = _dot(x_hi, w_hi) + (_dot(x_hi, w_lo) + _dot(x_lo, w_hi))
    lane = lax.broadcasted_iota(I32, (tm, LANES), 1)
    logits = jnp.where(lane < N_EXPERTS, logits, -jnp.inf)
    t1 = jnp.max(logits, axis=-1, keepdims=True)
    e1 = jnp.min(jnp.where(logits == t1, lane, LANES), axis=-1, keepdims=True)
    rest = jnp.where(lane == e1, -jnp.inf, logits)
    t2 = jnp.max(rest, axis=-1, keepdims=True)
    e2 = jnp.min(jnp.where(rest == t2, lane, LANES), axis=-1, keepdims=True)
    x2 = jnp.exp(t2 - t1)
    w1 = 1.0 / (1.0 + x2)
    w2 = x2 / (1.0 + x2)

    oh1 = lane == e1
    oh2 = lane == e2
    cnt = jnp.where(oh1 | oh2, 1.0, 0.0)
    before = _dot(tri_ref[...], cnt.astype(BF16)) + carry_ref[...]
    r1 = jnp.sum(jnp.where(oh1, before, 0.0), axis=-1, keepdims=True)
    r2 = jnp.sum(jnp.where(oh2, before, 0.0), axis=-1, keepdims=True)
    carry_ref[...] += jnp.sum(cnt, axis=0, keepdims=True)

    cols = (e1.astype(F32), e2.astype(F32), w1, w2, r1, r2)
    out = jnp.zeros((tm, LANES), F32)
    for c, val in enumerate(cols):
        out = jnp.where(lane == c, val, out)
    o_ref[...] = out
    ot_ref[...] = out.T[:SUBLANES, :]
    cnt_ref[...] = jnp.broadcast_to(carry_ref[...], cnt_ref.shape)


def moe_router(h, gain, w_router, tm):
    T, D = h.shape
    wr = jnp.zeros((D, LANES), F32).at[:, :N_EXPERTS].set(w_router)
    tri = jnp.tril(jnp.ones((tm, tm), BF16), k=-1)
    return pl.pallas_call(
        _router_kernel,
        grid=(T // tm,),
        in_specs=[pl.BlockSpec((tm, D), lambda i: (i, 0)),
                  pl.BlockSpec((1, D), lambda i: (0, 0)),
                  pl.BlockSpec((D, LANES), lambda i: (0, 0)),
                  pl.BlockSpec((tm, tm), lambda i: (0, 0))],
        out_specs=[pl.BlockSpec((tm, LANES), lambda i: (i, 0)),
                   pl.BlockSpec((SUBLANES, tm), lambda i: (0, i)),
                   pl.BlockSpec((SUBLANES, LANES), lambda i: (0, 0))],
        out_shape=[jax.ShapeDtypeStruct((T, LANES), F32),
                   jax.ShapeDtypeStruct((SUBLANES, T), F32),
                   jax.ShapeDtypeStruct((SUBLANES, LANES), F32)],
        scratch_shapes=[pltpu.VMEM((1, LANES), F32)],
        compiler_params=_cparams(1),
        name="moe_router",
    )(h, gain.reshape(1, D), wr, tri)


def _dispatch_kernel(ze_ref, zn_ref, p1_ref, p2_ref, h_ref, xs_ref, zero_ref, sem, zsem, *, pad_max, tail_max):
    tb = p1_ref.shape[0]
    zr = zero_ref.shape[0]

    @pl.when(pl.program_id(0) == 0)
    def _():
        zero_ref[...] = jnp.zeros_like(zero_ref)
        chunks = [(e, c) for e in range(N_EXPERTS) for c in range(pad_max // zr)]
        chunks += [(N_EXPERTS, c) for c in range(tail_max // zr)]

        def zcopy(e, c):
            start = pl.multiple_of(ze_ref[e] - (c + 1) * zr, zr)
            return pltpu.make_async_copy(zero_ref, xs_ref.at[pl.ds(start, zr), :], zsem)

        def needed(e, c):
            return c * zr < zn_ref[e]

        for e, c in chunks:
            @pl.when(needed(e, c))
            def _():
                zcopy(e, c).start()

        for e, c in chunks:
            @pl.when(needed(e, c))
            def _():
                zcopy(e, c).wait()

    def copy(t, pos):
        return pltpu.make_async_copy(h_ref.at[pl.ds(t, 1), :], xs_ref.at[pl.ds(pos, 1), :], sem)

    for t in range(tb):
        copy(t, p1_ref[t]).start(priority=0)
        copy(t, p2_ref[t]).start(priority=1)
    for _ in range(TOP_K):
        pltpu.make_async_copy(h_ref, xs_ref.at[pl.ds(0, tb), :], sem).wait()


def moe_dispatch(h, pos1, pos2, pad_end, pad_len, used_rows, n_rows, pad_max, tb, zr=256):
    T, D = h.shape
    tail_max = n_rows - TOP_K * T
    assert tail_max % zr == 0 and pad_max % zr == 0
    zero_end = jnp.concatenate([pad_end, jnp.full((1,), n_rows, I32)])
    zero_len = jnp.concatenate([pad_len, n_rows - used_rows[None]])
    grid_spec = pltpu.PrefetchScalarGridSpec(
        num_scalar_prefetch=2,
        grid=(T // tb,),
        in_specs=[pl.BlockSpec((tb,), lambda i, ps, pn: (i,), memory_space=pltpu.SMEM),
                  pl.BlockSpec((tb,), lambda i, ps, pn: (i,), memory_space=pltpu.SMEM),
                  pl.BlockSpec((tb, D), lambda i, ps, pn: (i, 0))],
        out_specs=pl.BlockSpec(memory_space=pl.ANY),
        scratch_shapes=[pltpu.VMEM((zr, D), h.dtype), pltpu.SemaphoreType.DMA(()), pltpu.SemaphoreType.DMA(())],
    )
    return pl.pallas_call(
        functools.partial(_dispatch_kernel, pad_max=pad_max, tail_max=tail_max),
        grid_spec=grid_spec,
        out_shape=jax.ShapeDtypeStruct((n_rows, D), h.dtype),
        compiler_params=_cparams(1),
        name="moe_dispatch",
    )(zero_end, zero_len, pos1, pos2, h)


def _expert_ffn_kernel(te_ref, tv_ref, x_ref, g_ref, wg_ref, wu_ref, wd_ref, wdl_ref, o_ref, xn_ref, a_ref,
                       *, ts, nf):
    i = pl.program_id(0)
    j = pl.program_id(1)
    valid = tv_ref[i]
    n_sub = x_ref.shape[0] // ts

    n_occ = (valid + ts - 1) // ts
    for k in range(1, n_sub + 1):
        rows = pl.ds(0, k * ts)

        def gate_up(slot):
            xn = xn_ref[rows, :]
            a = _silu(_dot(xn, wg_ref[...].astype(BF16))) * _dot(xn, wu_ref[...].astype(BF16))
            a_ref[slot, rows, :] = a.astype(BF16)

        def down(slot, first, w_ref=wd_ref):
            y = _dot(a_ref[slot, rows, :], w_ref[...].astype(BF16))
            if first:
                o_ref[rows, :] = y
                if k < n_sub:
                    o_ref[pl.ds(k * ts, (n_sub - k) * ts), :] = jnp.zeros(((n_sub - k) * ts, o_ref.shape[1]), F32)
            else:
                o_ref[rows, :] += y

        @pl.when(n_occ == k)
        def _():
            @pl.when(j == 0)
            def _():
                xn_ref[rows, :] = (_rms(x_ref[rows, :]) * g_ref[...]).astype(BF16)
                gate_up(0)

            @pl.when(j == 1)
            def _():
                gate_up(1)
                down(0, first=True)

            @pl.when((j > 1) & (j < nf - 1))
            def _():
                gate_up(j % 2)
                down((j - 1) % 2, first=False)

            @pl.when(j == nf - 1)
            def _():
                gate_up((nf - 1) % 2)
                down((nf - 2) % 2, first=False)
                down((nf - 1) % 2, first=False, w_ref=wdl_ref)

    @pl.when((n_occ == 0) & (j == 0))
    def _():
        o_ref[...] = jnp.zeros_like(o_ref)


def expert_ffn(xs, gain, w_gate, w_up, w_down, layer, tile_expert, tile_valid, n_tiles, tm, ts, tf):
    D = xs.shape[1]
    R = n_tiles * tm
    Fd = w_gate.shape[3]
    nf = Fd // tf

    assert nf >= 3
    def up_tile(i, j, tv):
        return jnp.where(tv[i] > 0, j, nf - 1)

    def down_tile(i, j, tv):
        return jnp.where((tv[i] > 0) & (j > 0), j - 1, nf - 2)

    grid_spec = pltpu.PrefetchScalarGridSpec(
        num_scalar_prefetch=2,
        grid=(n_tiles, nf),
        in_specs=[pl.BlockSpec((tm, D), lambda i, j, te, tv: (jnp.where(tv[i] > 0, i, 0), 0)),
                  pl.BlockSpec((1, D), lambda i, j, te, tv: (0, 0)),
                  pl.BlockSpec((None, None, D, tf), lambda i, j, te, tv: (layer, te[i], 0, up_tile(i, j, tv))),
                  pl.BlockSpec((None, None, D, tf), lambda i, j, te, tv: (layer, te[i], 0, up_tile(i, j, tv))),
                  pl.BlockSpec((None, None, tf, D), lambda i, j, te, tv: (layer, te[i], down_tile(i, j, tv), 0)),
                  pl.BlockSpec((None, None, tf, D), lambda i, j, te, tv: (layer, te[i], nf - 1, 0))],
        out_specs=pl.BlockSpec((tm, D), lambda i, j, te, tv: (i, 0)),
        scratch_shapes=[pltpu.VMEM((tm, D), BF16), pltpu.VMEM((2, tm, tf), BF16)],
    )
    return pl.pallas_call(
        functools.partial(_expert_ffn_kernel, ts=ts, nf=nf),
        grid_spec=grid_spec,
        out_shape=jax.ShapeDtypeStruct((R, D), F32),
        compiler_params=_cparams(2),
        name="expert_ffn",
    )(tile_expert, tile_valid, xs, gain.reshape(1, D), w_gate, w_up, w_down, w_down)


def _combine_kernel(p1_ref, p2_ref, route_ref, h_ref, y_ref, o_ref, b1_ref, b2_ref, sem):
    tb = p1_ref.shape[0]

    def copies(t):
        dst = pl.ds(t, 1)
        return (pltpu.make_async_copy(y_ref.at[pl.ds(p1_ref[t], 1), :], b1_ref.at[dst, :], sem),
                pltpu.make_async_copy(y_ref.at[pl.ds(p2_ref[t], 1), :], b2_ref.at[dst, :], sem))

    for t in range(tb):
        for prio, cp in enumerate(copies(t)):
            cp.start(priority=prio)
    for b_ref in (b1_ref, b2_ref):
        pltpu.make_async_copy(y_ref.at[pl.ds(0, tb), :], b_ref, sem).wait()
    w1 = route_ref[:, 2:3]
    w2 = route_ref[:, 3:4]
    o_ref[...] = h_ref[...] + w1 * b1_ref[...] + w2 * b2_ref[...]


def moe_combine(h, y, route, pos1, pos2, tb):
    T, D = h.shape
    smem = lambda: pl.BlockSpec((tb,), lambda i: (i,), memory_space=pltpu.SMEM)
    return pl.pallas_call(
        _combine_kernel,
        grid=(T // tb,),
        in_specs=[smem(), smem(),
                  pl.BlockSpec((tb, LANES), lambda i: (i, 0)),
                  pl.BlockSpec((tb, D), lambda i: (i, 0)),
                  pl.BlockSpec(memory_space=pl.ANY)],
        out_specs=pl.BlockSpec((tb, D), lambda i: (i, 0)),
        out_shape=jax.ShapeDtypeStruct((T, D), F32),
        scratch_shapes=[pltpu.VMEM((tb, D), F32), pltpu.VMEM((tb, D), F32),
                        pltpu.SemaphoreType.DMA(())],
        compiler_params=_cparams(1),
        name="moe_combine",
    )(pos1, pos2, route, h, y)


def moe_ffn(h, gain, w_router, w_gate, w_up, w_down, layer, tm=1024, ts=256, tf=512, tb_dispatch=1024, tb_combine=512):
    T = h.shape[0]
    route, route_t, counts = moe_router(h, gain, w_router, tm=512)
    e1 = route_t[0].astype(I32)
    e2 = route_t[1].astype(I32)
    counts = counts[0, :N_EXPERTS].astype(I32)
    tiles_per = (counts + tm - 1) // tm
    tile_end = jnp.cumsum(tiles_per)
    tile_start = tile_end - tiles_per
    row_start = tile_start * tm
    pos1 = row_start[e1] + route_t[4].astype(I32)
    pos2 = row_start[e2] + route_t[5].astype(I32)
    n_tiles = (TOP_K * T) // tm + N_EXPERTS
    tile_ids = jnp.arange(n_tiles, dtype=I32)
    tile_expert = jnp.sum(jnp.minimum(tile_ids, tile_end[-1] - 1)[:, None] >= tile_end[None, :], axis=1).astype(I32)
    tile_valid = jnp.clip(counts[tile_expert] - (tile_ids - tile_start[tile_expert]) * tm, 0, tm)
    tile_valid = jnp.where(tile_ids < tile_end[-1], tile_valid, 0).astype(I32)
    pad_len = tiles_per * tm - counts

    xs = moe_dispatch(h, pos1, pos2, tile_end * tm, pad_len, tile_end[-1] * tm, n_tiles * tm, tm, tb_dispatch)
    y = expert_ffn(xs, gain, w_gate, w_up, w_down, layer, tile_expert, tile_valid, n_tiles, tm, ts, tf)
    return moe_combine(h, y, route, pos1, pos2, tb_combine)


def _pad_heads(a, n_heads, width, pad_to, axis):
    shape = list(a.shape)
    a = a.reshape(shape[:axis] + [n_heads, width] + shape[axis + 1:])
    pads = [(0, 0)] * a.ndim
    pads[axis + 1] = (0, pad_to - width)
    a = jnp.pad(a, pads)
    return a.reshape(shape[:axis] + [n_heads * pad_to] + shape[axis + 1:])


def kernel(x, mem, ln_mix, ln_mem, w_mem_kv, mem_q_gain, mem_k_gain, w_out, ln_ffn, swa_w_in, swa_q_gain, swa_k_gain, swa_sink, ret_w_in, mlstm_w_in, mlstm_conv_w, mlstm_conv_b, mlstm_i_bias, mlstm_f_bias, mlstm_out_gain, ffn_w_gate, ffn_w_up, ffn_w_down, moe_router, moe_w_gate, moe_w_up, moe_w_down):
    B, S, D = x.shape
    M = mem.shape[1]
    T = B * S
    depth = ln_mix.shape[0]
    h = x.reshape(T, D)
    mem2 = mem.reshape(B * M, D)

    for layer in range(depth):
        kind = layer % N_MIXERS
        idx = layer // N_MIXERS
        mk, mv = rms_proj(mem2, ln_mem[layer], w_mem_kv[layer].astype(BF16),
                          ((MEM_WIDTH, BF16, mem_k_gain[layer]), (MEM_WIDTH, BF16, None)),
                          tm=B * M, name="mem_kv_proj")
        w_o = w_out[layer].astype(BF16)
        w_o_mix, w_o_mem = w_o[:MIX_WIDTH], w_o[MIX_WIDTH:]
        qm_out = (MEM_WIDTH, BF16, mem_q_gain[layer])

        if kind == 0:
            kvw = SWA_KV_HEADS * HEAD_DIM
            q, k, v, qm = rms_proj(h, ln_mix[layer], swa_w_in[idx].astype(BF16),
                                   ((MIX_WIDTH, BF16, swa_q_gain[idx]), (kvw, BF16, swa_k_gain[idx]),
                                    (kvw, BF16, None), qm_out), tm=1024, name="swa_in_proj")
            mix = swa_attention(q, k, v, swa_sink[idx], B)
        elif kind == 1:
            qkw = RET_HEADS * RET_QK_DIM
            q, k, v, g, qm = rms_proj(h, ln_mix[layer], ret_w_in[idx].astype(BF16),
                                      ((qkw, BF16, None), (qkw, F32, None), (MIX_WIDTH, BF16, None),
                                       (MIX_WIDTH, F32, None), qm_out), tm=1024, name="ret_in_proj")
            mix = retention(q, k, v, g, B)
        else:
            H, P, VP = MLSTM_HEADS, MLSTM_QK_PAD, MLSTM_V_PAD
            w = mlstm_w_in[idx]
            qkw = 2 * H * MLSTM_QK_DIM
            o_v, o_og, o_ig = qkw, qkw + MIX_WIDTH, qkw + 2 * MIX_WIDTH
            o_fg, o_qm = o_ig + H, o_ig + 2 * H
            w_gates = jnp.zeros((D, LANES), F32).at[:, :2 * H].set(w[:, o_ig:o_qm])
            w_pad = jnp.concatenate([
                _pad_heads(w[:, :qkw], 2 * H, MLSTM_QK_DIM, P, 1),
                _pad_heads(w[:, o_v:o_og], H, MLSTM_V_DIM, VP, 1),
                _pad_heads(w[:, o_og:o_ig], H, MLSTM_V_DIM, VP, 1),
                w_gates, w[:, o_qm:]], axis=1).astype(BF16)
            qk, v, og, gates, qm = rms_proj(h, ln_mix[layer], w_pad,
                                            ((2 * H * P, F32, None), (H * VP, BF16, None), (H * VP, F32, None),
                                             (LANES, F32, None), qm_out), tm=1024, name="mlstm_in_proj")
            gates_t = gates[:, :SUBLANES].reshape(B, S, SUBLANES).transpose(0, 2, 1)
            bias = jnp.concatenate([mlstm_i_bias[idx], mlstm_f_bias[idx]])
            bias_c = jnp.zeros((1, LANES), F32).at[0, :2 * H].set(bias)
            bias_r = jnp.broadcast_to(bias[:, None], (SUBLANES, MLSTM_CHUNK))
            mix = mlstm(qk, v, og, gates, gates_t,
                        _pad_heads(mlstm_conv_w[idx], 2 * H, MLSTM_QK_DIM, P, 1),
                        _pad_heads(mlstm_conv_b[idx][None], 2 * H, MLSTM_QK_DIM, P, 1),
                        bias_c, bias_r,
                        _pad_heads(mlstm_out_gain[idx][None], H, MLSTM_V_DIM, VP, 1), B)
            w_o_mix = _pad_heads(w_o_mix, H, MLSTM_V_DIM, VP, 0)

        h = out_proj(h, mix, qm, mk, mv, w_o_mix, w_o_mem, B, tm=1024)
        j = layer // 2
        if layer % 2 == 0:
            h = dense_ffn(h, ln_ffn[layer], ffn_w_gate[j].astype(BF16), ffn_w_up[j].astype(BF16),
                          ffn_w_down[j].astype(BF16), tm=1024, tf=1792)
        else:
            h = moe_ffn(h, ln_ffn[layer], moe_router[j], moe_w_gate, moe_w_up, moe_w_down, j)
    return h.reshape(B, S, D)
```
